```python
import math
import jax, jax.numpy as jnp
from jax import lax
import numpy as np

D_MODEL = 1024
BATCH = 8
SEQ = 2048
DEPTH = 1

RET_HEADS = 4
RET_DK = 128
RET_DV = 128
RET_CHUNK = 128
RET_FWD_DECAY_OFFSET = 5.0
RET_BWD_DECAY_OFFSET = 5.5
RET_THETA_BASE = 10000.0
DIFF_HEADS = 4
DIFF_DH = 64
DIFF_DV = 2 * DIFF_DH
Q_BLOCK = 128
N_BUCKETS = 32
MAX_DISTANCE = 128
N_EXPERTS = 16
EXPERT_FF = 1024
CAPACITY_FACTOR = 2
NORM_EPS = 1e-6

RET_QK_W = RET_HEADS * RET_DK
RET_V_W = RET_HEADS * RET_DV
DIFF_QK_W = DIFF_HEADS * DIFF_DH
DIFF_V_W = DIFF_HEADS * DIFF_DV
N_BRANCHES = 2
IN_SIZES = (RET_QK_W, RET_QK_W, RET_V_W, RET_V_W,
            DIFF_QK_W, DIFF_QK_W, DIFF_QK_W, DIFF_QK_W, DIFF_V_W,
            D_MODEL, D_MODEL)
IN_COLS = sum(IN_SIZES)

kernel_name = 'hybrid_retention_diffattn_ec_moe'


def _rmsnorm(x, g):
    xf = x.astype(jnp.float32)
    y = xf * lax.rsqrt(jnp.mean(xf * xf, axis=-1, keepdims=True) + NORM_EPS)
    return (y * g.astype(jnp.float32)).astype(x.dtype)


def _rotate(x, pos):
    d = x.shape[-1]
    inv = 1.0 / (RET_THETA_BASE ** jnp.linspace(0.0, 1.0, d // 2, dtype=jnp.float32))
    ang = pos.astype(jnp.float32)[:, None] * inv[None, :]
    cos = jnp.cos(ang)[None, :, None, :].astype(x.dtype)
    sin = jnp.sin(ang)[None, :, None, :].astype(x.dtype)
    x1 = x[..., 0::2]
    x2 = x[..., 1::2]
    return jnp.stack([x1 * cos - x2 * sin, x1 * sin + x2 * cos], axis=-1).reshape(x.shape)


def _retention_one_dir(q, k, v, log_g, include_diag):
    B, H, S, dk = q.shape
    dv = v.shape[-1]
    C = RET_CHUNK
    N = S // C
    qc = q.reshape(B, H, N, C, dk)
    kc = k.reshape(B, H, N, C, dk)
    vc = v.reshape(B, H, N, C, dv)
    idx = jnp.arange(C, dtype=jnp.float32)
    diff = idx[:, None] - idx[None, :]
    mask = (diff >= 0) if include_diag else (diff > 0)
    decay_in = jnp.where(mask[None], jnp.exp(jnp.maximum(diff, 0.0)[None] * log_g[:, None, None]), 0.0)
    decay_in = decay_in.astype(q.dtype)
    scores = jnp.einsum('bhncd,bhnmd->bhncm', qc, kc) * decay_in[None, :, None]
    inner = jnp.einsum('bhncm,bhnme->bhnce', scores, vc)
    k_dec = jnp.exp((C - 1 - idx)[None, :] * log_g[:, None]).astype(q.dtype)
    q_dec = jnp.exp((idx + 1)[None, :] * log_g[:, None]).astype(q.dtype)
    chunk_dec = jnp.exp(C * log_g)
    kv = jnp.einsum('bhncd,hc,bhnce->nbhde', kc, k_dec, vc).astype(jnp.float32)

    def step(state, kv_n):
        return chunk_dec[None, :, None, None] * state + kv_n, state

    _, prev = lax.scan(step, jnp.zeros((B, H, dk, dv), jnp.float32), kv)
    cross = jnp.einsum('bhncd,hc,nbhde->bhnce', qc, q_dec, prev.astype(q.dtype))
    return (inner + cross).reshape(B, H, S, dv)


def _t5_bucket(rel):
    nb = N_BUCKETS // 2
    max_exact = nb // 2
    ret = (rel > 0).astype(jnp.int32) * nb
    n = jnp.abs(rel)
    large = max_exact + (jnp.log(jnp.maximum(n, 1).astype(jnp.float32) / max_exact)
                         / math.log(MAX_DISTANCE / max_exact) * (nb - max_exact)).astype(jnp.int32)
    large = jnp.minimum(large, nb - 1)
    return ret + jnp.where(n < max_exact, n, large)


def _diff_attention(q1, q2, k1, k2, v, rel_bias, lam):
    B, H, S, dh = q1.shape
    dv = v.shape[-1]
    NB = S // Q_BLOCK
    scale = dh ** -0.5
    kpos = jnp.arange(S, dtype=jnp.int32)

    def to_blocks(t):
        return jnp.moveaxis(t.reshape(B, H, NB, Q_BLOCK, dh), 2, 0)

    def block(args):
        q1b, q2b, start = args
        qpos = start + jnp.arange(Q_BLOCK, dtype=jnp.int32)
        bias = jnp.transpose(rel_bias[_t5_bucket(kpos[None, :] - qpos[:, None])], (2, 0, 1))
        bias = bias.astype(jnp.float32)
        s1 = jnp.einsum('bhqd,bhkd->bhqk', q1b, k1).astype(jnp.float32) * scale + bias
        s2 = jnp.einsum('bhqd,bhkd->bhqk', q2b, k2).astype(jnp.float32) * scale + bias
        attn = jax.nn.softmax(s1, axis=-1) - lam * jax.nn.softmax(s2, axis=-1)
        return jnp.einsum('bhqk,bhke->bhqe', attn.astype(v.dtype), v)

    starts = jnp.arange(NB, dtype=jnp.int32) * Q_BLOCK
    out = lax.map(block, (to_blocks(q1), to_blocks(q2), starts))
    return jnp.transpose(out, (1, 0, 3, 2, 4)).reshape(B, S, H, dv)


def _expert_choice(h, w_router, w_gate, w_up, w_down):
    B, S, D = h.shape
    cap = CAPACITY_FACTOR * S // N_EXPERTS
    aff = jax.nn.softmax((h @ w_router).astype(jnp.float32), axis=-1)
    g, idx = lax.top_k(jnp.swapaxes(aff, 1, 2), cap)
    bidx = jnp.arange(B)[:, None, None]
    xin = h[bidx, idx]
    a = jnp.einsum('becd,edf->becf', xin, w_gate)
    u = jnp.einsum('becd,edf->becf', xin, w_up)
    y = jnp.einsum('becf,efd->becd', jax.nn.silu(a) * u, w_down) * g[..., None].astype(h.dtype)
    return jnp.zeros_like(h).at[bidx, idx].add(y)


def setup_inputs(seed: int = 0) -> dict:
    key = jax.random.key(seed)
    ks = jax.random.split(key, 24)
    f32 = jnp.float32
    nrm = lambda k, shape, s: jax.random.normal(k, shape, f32) * s
    gain = lambda k, shape: 1.0 + 0.02 * jax.random.normal(k, shape, f32)
    L, D = DEPTH, D_MODEL
    return {
        'x': nrm(ks[0], (BATCH, SEQ, D), 1.0),
        'c': nrm(ks[1], (BATCH, D), 1.0),
        'w_ada': nrm(ks[2], (L, D, 6 * D), D ** -0.5),
        'b_ada': nrm(ks[3], (L, 6 * D), 0.01),
        'norm_mix_g': gain(ks[4], (L, D)),
        'w_in': nrm(ks[5], (L, D, IN_COLS), D ** -0.5),
        'ret_gn_g': gain(ks[6], (L, RET_V_W)),
        'diff_subln_g': gain(ks[7], (L, DIFF_DV)),
        'lambda_q1': nrm(ks[8], (L, DIFF_DH), 0.1),
        'lambda_k1': nrm(ks[9], (L, DIFF_DH), 0.1),
        'lambda_q2': nrm(ks[10], (L, DIFF_DH), 0.1),
        'lambda_k2': nrm(ks[11], (L, DIFF_DH), 0.1),
        'w_ret_out': nrm(ks[12], (L, RET_V_W, D), RET_V_W ** -0.5),
        'w_diff_out': nrm(ks[13], (L, DIFF_V_W, D), DIFF_V_W ** -0.5),
        'w_o': nrm(ks[14], (L, D, D), D ** -0.5),
        'rel_bias': nrm(ks[15], (N_BUCKETS, DIFF_HEADS), 0.5),
        'norm_ffn_g': gain(ks[16], (L, D)),
        'w_router': nrm(ks[17], (L, D, N_EXPERTS), D ** -0.5),
        'w_exp_gate': nrm(ks[18], (L, N_EXPERTS, D, EXPERT_FF), D ** -0.5),
        'w_exp_up': nrm(ks[19], (L, N_EXPERTS, D, EXPERT_FF), D ** -0.5),
        'w_exp_down': nrm(ks[20], (L, N_EXPERTS, EXPERT_FF, D), EXPERT_FF ** -0.5),
        'final_g': gain(ks[21], (D,)),
    }


def reference(x, c, w_ada, b_ada, norm_mix_g, w_in, ret_gn_g, diff_subln_g,
              lambda_q1, lambda_k1, lambda_q2, lambda_k2, w_ret_out, w_diff_out,
              w_o, rel_bias, norm_ffn_g, w_router, w_exp_gate, w_exp_up,
              w_exp_down, final_g):
    B, S, D = x.shape
    f32 = jnp.float32
    pos = jnp.arange(S, dtype=jnp.int32)
    heads = jnp.arange(RET_HEADS, dtype=f32)
    log_g_fwd = jnp.log1p(-jnp.exp2(-RET_FWD_DECAY_OFFSET - heads))
    log_g_bwd = jnp.log1p(-jnp.exp2(-RET_BWD_DECAY_OFFSET - heads))
    split_points = np.cumsum(IN_SIZES)[:-1].tolist()
    to_bhsd = lambda t: jnp.transpose(t, (0, 2, 1, 3))

    for l in range(DEPTH):
        mod = jax.nn.silu(c) @ w_ada[l] + b_ada[l]
        sh1, sc1, ga1, sh2, sc2, ga2 = [m[:, None, :] for m in jnp.split(mod, 6, axis=-1)]

        h = _rmsnorm(x, norm_mix_g[l]) * (1.0 + sc1) + sh1
        proj = h @ w_in[l]
        rq, rk, rv, rg, dq1, dq2, dk1, dk2, dvv, gr, gd = jnp.split(proj, split_points, axis=-1)

        rq = _rotate(rq.reshape(B, S, RET_HEADS, RET_DK), pos)
        rk = _rotate(rk.reshape(B, S, RET_HEADS, RET_DK), pos) * (RET_DK ** -0.5)
        rq, rk = to_bhsd(rq), to_bhsd(rk)
        rv = to_bhsd(rv.reshape(B, S, RET_HEADS, RET_DV))
        fwd = _retention_one_dir(rq, rk, rv, log_g_fwd, True)
        bwd = jnp.flip(_retention_one_dir(jnp.flip(rq, 2), jnp.flip(rk, 2), jnp.flip(rv, 2),
                                          log_g_bwd, False), 2)
        yr = to_bhsd(fwd + bwd).astype(f32)
        mu = jnp.mean(yr, axis=-1, keepdims=True)
        var = jnp.mean(jnp.square(yr - mu), axis=-1, keepdims=True)
        yr = ((yr - mu) * lax.rsqrt(var + NORM_EPS)).reshape(B, S, RET_V_W)
        yr = (yr * ret_gn_g[l].astype(f32)).astype(x.dtype)
        y_ret = jax.nn.silu(rg) * yr

        lam_init = 0.8 - 0.6 * math.exp(-0.3 * l)
        lam = (jnp.exp(jnp.sum(lambda_q1[l].astype(f32) * lambda_k1[l].astype(f32)))
               - jnp.exp(jnp.sum(lambda_q2[l].astype(f32) * lambda_k2[l].astype(f32))) + lam_init)
        hd = lambda t, d: to_bhsd(t.reshape(B, S, DIFF_HEADS, d))
        yd = _diff_attention(hd(dq1, DIFF_DH), hd(dq2, DIFF_DH), hd(dk1, DIFF_DH),
                             hd(dk2, DIFF_DH), hd(dvv, DIFF_DV), rel_bias, lam)
        y_diff = (_rmsnorm(yd, diff_subln_g[l]) * (1.0 - lam_init)).reshape(B, S, DIFF_V_W)

        merged = (jax.nn.sigmoid(gr) * (y_ret @ w_ret_out[l])
                  + jax.nn.sigmoid(gd) * (y_diff @ w_diff_out[l]))
        x = x + ga1 * (merged @ w_o[l])

        h = _rmsnorm(x, norm_ffn_g[l]) * (1.0 + sc2) + sh2
        x = x + ga2 * _expert_choice(h, w_router[l], w_exp_gate[l], w_exp_up[l], w_exp_down[l])

    return _rmsnorm(x, final_g)
```

```python
import functools
import math

import numpy as np
import jax
import jax.numpy as jnp
from jax import lax
from jax.experimental import pallas as pl
from jax.experimental.pallas import tpu as pltpu

F32 = jnp.float32
BF16 = jnp.bfloat16

RET_HEADS = 4
RET_DK = 128
RET_DV = 128
RET_FWD_DECAY_OFFSET = 5.0
RET_BWD_DECAY_OFFSET = 5.5
RET_THETA_BASE = 10000.0
DIFF_HEADS = 4
DIFF_DH = 64
DIFF_DV = 2 * DIFF_DH
N_BUCKETS = 32
MAX_DISTANCE = 128
N_EXPERTS = 16
CAPACITY_FACTOR = 2
NORM_EPS = 1e-6
LAM_INIT = 0.8 - 0.6 * math.exp(-0.3 * 0)

LANES = 128
VMEM_LIMIT_BYTES = 56 * 2**20

TOKEN_TILE = 512
RET_CHUNK = 256
Q_TILE = 256

COL_GR, COL_GD = 0, 8
COL_RQ, COL_RK, COL_RV, COL_RG = 16, 20, 24, 28
COL_DQ, COL_DK, COL_DV = 32, 36, 40
PROJ_COLS = 44 * LANES


def _cparams(sem):
    return pltpu.CompilerParams(dimension_semantics=sem, vmem_limit_bytes=VMEM_LIMIT_BYTES)


def _resident(shape, index_map):
    return pl.BlockSpec(shape, index_map, pipeline_mode=pl.Buffered(1))


def _ada_kernel(c_ref, w_ref, b_ref, o_ref):
    c = c_ref[...]
    a = c * jax.nn.sigmoid(c)
    o_ref[...] = jnp.dot(a, w_ref[...], preferred_element_type=F32,
                         precision=lax.Precision.HIGHEST) + b_ref[...]


def _ada(c, w, b):
    B, D = c.shape
    n = w.shape[1] // D
    return pl.pallas_call(
        _ada_kernel,
        grid=(n,),
        in_specs=[pl.BlockSpec((B, D), lambda j: (0, 0)),
                  pl.BlockSpec((D, D), lambda j: (0, j)),
                  pl.BlockSpec((1, D), lambda j: (0, j))],
        out_specs=pl.BlockSpec((B, D), lambda j: (0, j)),
        out_shape=jax.ShapeDtypeStruct((B, n * D), F32),
        compiler_params=_cparams(("arbitrary",)),
        name="ada",
    )(c, w, b.reshape(1, -1))


def _rms(x, g):
    ms = jnp.mean(x * x, axis=-1, keepdims=True)
    return x * lax.rsqrt(ms + NORM_EPS) * g


def _rot_half(x, cos, sin):
    return x * cos + pltpu.roll(x, LANES // 2, axis=1) * sin


def _inproj_kernel(x_ref, g_ref, mod_ref, w_ref, rot_ref, o_ref):
    x = x_ref[0]
    h = _rms(x, g_ref[...]) * (1.0 + mod_ref[0, 1:2, :]) + mod_ref[0, 0:1, :]
    hb = h.astype(BF16)

    def mm(col, width):
        return jnp.dot(hb, w_ref[:, col * LANES:(col + width) * LANES], preferred_element_type=F32)

    def put(col, val):
        o_ref[0, :, col * LANES:col * LANES + val.shape[1]] = val.astype(BF16)

    for col in range(COL_GR, COL_RQ, 4):
        put(col, jax.nn.sigmoid(mm(col, 4)))
    for base, t in ((COL_RQ, 0), (COL_RK, 2)):
        cos, sin = rot_ref[t], rot_ref[t + 1]
        r = mm(base, RET_HEADS)
        for hh in range(RET_HEADS):
            put(base + hh, _rot_half(r[:, hh * LANES:(hh + 1) * LANES], cos, sin))
    put(COL_RV, mm(COL_RV, 4))
    r = mm(COL_RG, 4)
    put(COL_RG, r * jax.nn.sigmoid(r))
    put(COL_DQ, mm(COL_DQ, 4) * (DIFF_DH ** -0.5))
    put(COL_DK, mm(COL_DK, 4))
    put(COL_DV, mm(COL_DV, 4))


def _inproj(x, g, mod, w, rot):
    B, S, D = x.shape
    tm = TOKEN_TILE
    return pl.pallas_call(
        _inproj_kernel,
        grid=(B, S // tm),
        in_specs=[pl.BlockSpec((1, tm, D), lambda b, i: (b, i, 0)),
                  _resident((1, D), lambda b, i: (0, 0)),
                  pl.BlockSpec((1, 6, D), lambda b, i: (b, 0, 0)),
                  _resident((D, PROJ_COLS), lambda b, i: (0, 0)),
                  pl.BlockSpec((4, tm, LANES), lambda b, i: (0, i, 0))],
        out_specs=pl.BlockSpec((1, tm, PROJ_COLS), lambda b, i: (b, i, 0)),
        out_shape=jax.ShapeDtypeStruct((B, S, PROJ_COLS), BF16),
        compiler_params=_cparams(("arbitrary", "arbitrary")),
        name="inproj",
    )(x, g, mod, w, rot)


def _ret_kernel(cd_ref, q_ref, k_ref, v_ref, rg_ref, dec_ref, dm_ref, gn_ref, o_ref, acc_ref):
    hh = pl.program_id(1)
    S = q_ref.shape[1]
    C = RET_CHUNK
    nc = S // C
    dmat = dm_ref[0]
    qdf, kdf, qdb, kdb = dec_ref[0, 0], dec_ref[0, 1], dec_ref[0, 2], dec_ref[0, 3]
    nt = (((1,), (1,)), ((), ()))
    tn = (((0,), (0,)), ((), ()))

    def chunk(n):
        sl = pl.ds(n * C, C)
        return sl, q_ref[0, sl, :], k_ref[0, sl, :], v_ref[0, sl, :]

    def scaled(t, dec):
        return (t.astype(F32) * dec).astype(BF16)

    state = jnp.zeros((RET_DK, RET_DV), F32)
    for n in range(nc):
        sl, q, k, v = chunk(n)
        s = lax.dot_general(q, k, nt, preferred_element_type=F32) * dmat
        inner = jnp.dot(s.astype(BF16), v, preferred_element_type=F32)
        cross = jnp.dot(scaled(q, qdf), state.astype(BF16), preferred_element_type=F32)
        acc_ref[sl, :] = inner + cross
        kv = lax.dot_general(scaled(k, kdf), v, tn, preferred_element_type=F32)
        state = cd_ref[hh, 0] * state + kv
    state = jnp.zeros((RET_DK, RET_DV), F32)
    for n in reversed(range(nc)):
        sl, q, k, v = chunk(n)
        cross = jnp.dot(scaled(q, qdb), state.astype(BF16), preferred_element_type=F32)
        acc_ref[sl, :] = acc_ref[sl, :] + cross
        kv = lax.dot_general(scaled(k, kdb), v, tn, preferred_element_type=F32)
        state = cd_ref[hh, 1] * state + kv

    y = acc_ref[...]
    mu = jnp.mean(y, axis=-1, keepdims=True)
    yc = y - mu
    var = jnp.mean(yc * yc, axis=-1, keepdims=True)
    yn = yc * lax.rsqrt(var + NORM_EPS) * gn_ref[...]
    o_ref[0] = (rg_ref[0].astype(F32) * yn).astype(BF16)


def _ret(proj, cd, dec, dmat, gn):
    B, S, _ = proj.shape
    H = RET_HEADS
    C = RET_CHUNK

    def col(base):
        return pl.BlockSpec((1, S, LANES), lambda b, h: (b, 0, base + h))

    return pl.pallas_call(
        _ret_kernel,
        grid=(B, H),
        in_specs=[pl.BlockSpec(memory_space=pltpu.SMEM),
                  col(COL_RQ), col(COL_RK), col(COL_RV), col(COL_RG),
                  pl.BlockSpec((1, 4, C, LANES), lambda b, h: (h, 0, 0, 0)),
                  pl.BlockSpec((1, C, C), lambda b, h: (h, 0, 0)),
                  pl.BlockSpec((1, LANES), lambda b, h: (0, h))],
        out_specs=pl.BlockSpec((1, S, LANES), lambda b, h: (b, 0, h)),
        out_shape=jax.ShapeDtypeStruct((B, S, H * RET_DV), BF16),
        scratch_shapes=[pltpu.VMEM((S, RET_DV), F32)],
        compiler_params=_cparams(("arbitrary", "arbitrary")),
        name="ret",
    )(cd, proj, proj, proj, proj, dec, dmat, gn)


def _diff_kernel(lam_ref, q_ref, k_ref, v_ref, t_ref, g_ref, o_ref):
    S = k_ref.shape[1]
    TQ = Q_TILE
    lv = lam_ref[...]
    lam = (jnp.exp(jnp.sum(lv[0:1] * lv[1:2], axis=1, keepdims=True))
           - jnp.exp(jnp.sum(lv[2:3] * lv[3:4], axis=1, keepdims=True)) + LAM_INIT)
    kk = k_ref[0]
    vv = v_ref[0]
    first_half = lax.broadcasted_iota(jnp.int32, (TQ, LANES), 1) < DIFF_DH
    nt = (((1,), (1,)), ((), ()))

    def softmax_parts(qz, bias):
        s = lax.dot_general(qz, kk, nt, preferred_element_type=F32) + bias
        p = jnp.exp(s - jnp.max(s, axis=-1, keepdims=True))
        return p, jnp.sum(p, axis=-1, keepdims=True)

    def body(qb, carry):
        q0 = pl.multiple_of(qb * TQ, TQ)
        w0 = pl.multiple_of((S - TQ) - qb * TQ, TQ)
        q = q_ref[0, pl.ds(q0, TQ), :]
        bias = t_ref[0, :, pl.ds(w0, S)]
        zero = jnp.zeros_like(q)
        p1, l1 = softmax_parts(jnp.where(first_half, q, zero), bias)
        p2, l2 = softmax_parts(jnp.where(first_half, zero, q), bias)
        attn = p1 * (1.0 / l1) - p2 * (lam / l2)
        o = jnp.dot(attn.astype(BF16), vv, preferred_element_type=F32)
        y = _rms(o, g_ref[...]) * (1.0 - LAM_INIT)
        o_ref[0, pl.ds(q0, TQ), :] = y.astype(BF16)
        return carry

    lax.fori_loop(0, S // TQ, body, 0)


def _diff(proj, lam_vecs, ttab, g):
    B, S, _ = proj.shape
    H = DIFF_HEADS

    def col(base):
        return pl.BlockSpec((1, S, LANES), lambda h, b: (b, 0, base + h))

    return pl.pallas_call(
        _diff_kernel,
        grid=(H, B),
        in_specs=[pl.BlockSpec((4, DIFF_DH), lambda h, b: (0, 0)),
                  col(COL_DQ), col(COL_DK), col(COL_DV),
                  pl.BlockSpec((1, Q_TILE, 2 * S - Q_TILE), lambda h, b: (h, 0, 0)),
                  pl.BlockSpec((1, DIFF_DV), lambda h, b: (0, 0))],
        out_specs=pl.BlockSpec((1, S, LANES), lambda h, b: (b, 0, h)),
        out_shape=jax.ShapeDtypeStruct((B, S, H * DIFF_DV), BF16),
        compiler_params=_cparams(("arbitrary", "arbitrary")),
        name="diff",
    )(lam_vecs, proj, proj, proj, ttab, g)


def _merge_kernel(x_ref, yr_ref, yd_ref, gr_ref, gd_ref, mod_ref, wr_ref, wd_ref, wo_ref,
                  g_ref, wrt_ref, x1_ref, h2_ref, lg_ref):
    a = jnp.dot(yr_ref[0], wr_ref[...], preferred_element_type=F32)
    d = jnp.dot(yd_ref[0], wd_ref[...], preferred_element_type=F32)
    merged = gr_ref[0].astype(F32) * a + gd_ref[0].astype(F32) * d
    o = jnp.dot(merged.astype(BF16), wo_ref[...], preferred_element_type=F32)
    x1 = x_ref[0] + mod_ref[0, 2:3, :] * o
    x1_ref[0] = x1
    h2 = (_rms(x1, g_ref[...]) * (1.0 + mod_ref[0, 4:5, :]) + mod_ref[0, 3:4, :]).astype(BF16)
    h2_ref[0] = h2
    lg_ref[0] = lax.dot_general(wrt_ref[...], h2, (((1,), (1,)), ((), ())),
                                preferred_element_type=F32)


def _merge(x, yr, yd, proj, mod, wr, wd, wo, g, wrt):
    B, S, D = x.shape
    tm = TOKEN_TILE
    E = wrt.shape[0]
    gate_w = COL_GD - COL_GR
    return pl.pallas_call(
        _merge_kernel,
        grid=(B, S // tm),
        in_specs=[pl.BlockSpec((1, tm, D), lambda b, i: (b, i, 0)),
                  pl.BlockSpec((1, tm, yr.shape[2]), lambda b, i: (b, i, 0)),
                  pl.BlockSpec((1, tm, yd.shape[2]), lambda b, i: (b, i, 0)),
                  pl.BlockSpec((1, tm, gate_w * LANES), lambda b, i: (b, i, COL_GR // gate_w)),
                  pl.BlockSpec((1, tm, gate_w * LANES), lambda b, i: (b, i, COL_GD // gate_w)),
                  pl.BlockSpec((1, 6, D), lambda b, i: (b, 0, 0)),
                  _resident(wr.shape, lambda b, i: (0, 0)),
                  _resident(wd.shape, lambda b, i: (0, 0)),
                  _resident(wo.shape, lambda b, i: (0, 0)),
                  _resident((1, D), lambda b, i: (0, 0)),
                  _resident(wrt.shape, lambda b, i: (0, 0))],
        out_specs=[pl.BlockSpec((1, tm, D), lambda b, i: (b, i, 0)),
                   pl.BlockSpec((1, tm, D), lambda b, i: (b, i, 0)),
                   pl.BlockSpec((1, E, tm), lambda b, i: (b, 0, i))],
        out_shape=[jax.ShapeDtypeStruct((B, S, D), F32),
                   jax.ShapeDtypeStruct((B, S, D), BF16),
                   jax.ShapeDtypeStruct((B, E, S), F32)],
        compiler_params=_cparams(("arbitrary", "arbitrary")),
        name="merge",
    )(x, yr, yd, proj, proj, mod, wr, wd, wo, g, wrt)


def _lane_prefix(m, tri):
    E, S = m.shape
    off = jnp.zeros((E, 1), F32)
    parts = []
    for j in range(S // LANES):
        blk = m[:, j * LANES:(j + 1) * LANES]
        parts.append(jnp.dot(blk.astype(BF16), tri, preferred_element_type=F32) + off)
        off = off + jnp.sum(blk, axis=1, keepdims=True)
    return jnp.concatenate(parts, axis=1)


def _route_kernel(lg_ref, rank_ref, gate_ref, *, cap):
    lg = lg_ref[0]
    e = jnp.exp(lg - jnp.max(lg, axis=0, keepdims=True))
    aff = e / jnp.sum(e, axis=0, keepdims=True)
    bits = pltpu.bitcast(aff, jnp.int32)
    E = lg.shape[0]

    def count(mask):
        return jnp.sum(jnp.where(mask, 1.0, 0.0), axis=1, keepdims=True)

    thr = jnp.zeros((E, 1), jnp.int32)
    for bit in range(30, -1, -1):
        cand = thr | (1 << bit)
        thr = jnp.where(count(bits >= cand) >= cap, cand, thr)
    gt = bits > thr
    eq = bits == thr
    need = cap - count(gt)
    r = lax.broadcasted_iota(jnp.int32, (LANES, LANES), 0)
    c = lax.broadcasted_iota(jnp.int32, (LANES, LANES), 1)
    tri = jnp.where(r < c, 1.0, 0.0).astype(BF16)
    eq_before = _lane_prefix(jnp.where(eq, 1.0, 0.0), tri)
    sel = gt | (eq & (eq_before < need))
    slot = _lane_prefix(jnp.where(sel, 1.0, 0.0), tri)
    rank_ref[0] = jnp.where(sel, slot, -1.0).astype(jnp.int32)
    gate_ref[0] = jnp.where(sel, aff, 0.0)


def _route(logits, cap):
    B, E, S = logits.shape
    spec = pl.BlockSpec((1, E, S), lambda b: (b, 0, 0))
    return pl.pallas_call(
        functools.partial(_route_kernel, cap=cap),
        grid=(B,),
        in_specs=[spec],
        out_specs=[spec, spec],
        out_shape=[jax.ShapeDtypeStruct((B, E, S), jnp.int32),
                   jax.ShapeDtypeStruct((B, E, S), F32)],
        compiler_params=_cparams(("arbitrary",)),
        name="route",
    )(logits)


def _moe_kernel(rank_ref, gate_ref, h_ref, wg_ref, wu_ref, wd_ref, o_ref, wg_s, wu_s, wd_s, *, cap):
    @pl.when(pl.program_id(1) == 0)
    def _():
        wg_s[...] = wg_ref[0].astype(BF16)
        wu_s[...] = wu_ref[0].astype(BF16)
        wd_s[...] = wd_ref[0].astype(BF16)

    rank = rank_ref[0, 0]
    S = rank.shape[1]
    pick = lax.broadcasted_iota(jnp.int32, (cap, S), 0) == rank
    xin = jnp.dot(jnp.where(pick, 1.0, 0.0).astype(BF16), h_ref[0],
                  preferred_element_type=F32).astype(BF16)
    a = jnp.dot(xin, wg_s[...], preferred_element_type=F32)
    u = jnp.dot(xin, wu_s[...], preferred_element_type=F32)
    act = (a * jax.nn.sigmoid(a) * u).astype(BF16)
    y = jnp.dot(act, wd_s[...], preferred_element_type=F32)
    g = jnp.sum(jnp.where(pick, gate_ref[0, 0], 0.0), axis=1, keepdims=True)
    o_ref[0, 0] = (y * g).astype(BF16)


def _moe(rank, gate, h2, wg, wu, wd, cap):
    B, E, S = rank.shape
    D = h2.shape[2]
    Fd = wg.shape[2]
    row = pl.BlockSpec((1, 1, 1, S), lambda e, b: (b, e, 0, 0))
    return pl.pallas_call(
        functools.partial(_moe_kernel, cap=cap),
        grid=(E, B),
        in_specs=[row, row,
                  pl.BlockSpec((1, S, D), lambda e, b: (b, 0, 0)),
                  pl.BlockSpec((1, D, Fd), lambda e, b: (e, 0, 0)),
                  pl.BlockSpec((1, D, Fd), lambda e, b: (e, 0, 0)),
                  pl.BlockSpec((1, Fd, D), lambda e, b: (e, 0, 0))],
        out_specs=pl.BlockSpec((1, 1, cap, D), lambda e, b: (e, b, 0, 0)),
        out_shape=jax.ShapeDtypeStruct((E, B, cap, D), BF16),
        scratch_shapes=[pltpu.VMEM((D, Fd), BF16), pltpu.VMEM((D, Fd), BF16),
                        pltpu.VMEM((Fd, D), BF16)],
        compiler_params=_cparams(("arbitrary", "arbitrary")),
        name="moe",
    )(rank.reshape(B, E, 1, S), gate.reshape(B, E, 1, S), h2, wg, wu, wd)


def _scat_kernel(rt_ref, y_ref, x1_ref, mod_ref, g_ref, o_ref, *, cap):
    rt = rt_ref[0]
    tm, E = rt.shape
    slot = lax.broadcasted_iota(jnp.int32, (tm, cap), 1)
    place = jnp.concatenate(
        [jnp.where(rt[:, e:e + 1] == slot, 1.0, 0.0).astype(BF16) for e in range(E)], axis=1)
    y = y_ref[:, 0].reshape(E * cap, y_ref.shape[3])
    moe = jnp.dot(place, y, preferred_element_type=F32)
    x2 = x1_ref[0] + mod_ref[0, 5:6, :] * moe
    o_ref[0] = _rms(x2, g_ref[...])


def _scat(rank_t, ybuf, x1, mod, g, cap):
    B, S, D = x1.shape
    E = rank_t.shape[2]
    tm = TOKEN_TILE
    return pl.pallas_call(
        functools.partial(_scat_kernel, cap=cap),
        grid=(B, S // tm),
        in_specs=[pl.BlockSpec((1, tm, E), lambda b, i: (b, i, 0)),
                  pl.BlockSpec((E, 1, cap, D), lambda b, i: (0, b, 0, 0)),
                  pl.BlockSpec((1, tm, D), lambda b, i: (b, i, 0)),
                  pl.BlockSpec((1, 6, D), lambda b, i: (b, 0, 0)),
                  _resident((1, D), lambda b, i: (0, 0))],
        out_specs=pl.BlockSpec((1, tm, D), lambda b, i: (b, i, 0)),
        out_shape=jax.ShapeDtypeStruct((B, S, D), F32),
        compiler_params=_cparams(("arbitrary", "arbitrary")),
        name="scat",
    )(rank_t, ybuf, x1, mod, g)


def _permute_w_in(w):
    D = w.shape[0]
    sizes = (512, 512, 512, 512, 256, 256, 256, 256, 512, 1024, 1024)
    rq, rk, rv, rg, dq1, dq2, dk1, dk2, dv, gr, gd = jnp.split(w, np.cumsum(sizes)[:-1].tolist(), axis=1)

    def halves(t):
        return t.reshape(D, RET_HEADS, RET_DK // 2, 2).transpose(0, 1, 3, 2).reshape(D, -1)

    def pair(a, b):
        return jnp.stack([a.reshape(D, DIFF_HEADS, DIFF_DH), b.reshape(D, DIFF_HEADS, DIFF_DH)],
                         axis=2).reshape(D, -1)

    return jnp.concatenate([gr, gd, halves(rq), halves(rk), rv, rg, pair(dq1, dq2), pair(dk1, dk2), dv],
                           axis=1).astype(BF16)


def _rot_tables(S):
    half = RET_DK // 2
    inv = 1.0 / (RET_THETA_BASE ** jnp.linspace(0.0, 1.0, half, dtype=F32))
    ang = jnp.arange(S, dtype=jnp.int32).astype(F32)[:, None] * inv[None, :]
    cos = jnp.concatenate([jnp.cos(ang), jnp.cos(ang)], axis=1)
    sin = jnp.concatenate([-jnp.sin(ang), jnp.sin(ang)], axis=1)
    sc = RET_DK ** -0.5
    return jnp.stack([cos, sin, cos * sc, sin * sc])


def _ret_tables():
    C = RET_CHUNK
    heads = jnp.arange(RET_HEADS, dtype=F32)
    lgf = jnp.log1p(-jnp.exp2(-RET_FWD_DECAY_OFFSET - heads))[:, None]
    lgb = jnp.log1p(-jnp.exp2(-RET_BWD_DECAY_OFFSET - heads))[:, None]
    idx = jnp.arange(C, dtype=F32)
    diff = idx[:, None] - idx[None, :]
    dmat = jnp.where(diff >= 0,
                     jnp.exp(jnp.maximum(diff, 0.0)[None] * lgf[:, :, None]),
                     jnp.exp(jnp.maximum(-diff, 0.0)[None] * lgb[:, :, None]))
    dec = jnp.stack([jnp.exp((idx + 1)[None, :] * lgf),
                     jnp.exp((C - 1 - idx)[None, :] * lgf),
                     jnp.exp((C - idx)[None, :] * lgb),
                     jnp.exp(idx[None, :] * lgb)], axis=1)
    dec = jnp.broadcast_to(dec[..., None], dec.shape + (LANES,))
    cd = jnp.concatenate([jnp.exp(C * lgf), jnp.exp(C * lgb)], axis=1)
    return cd, dec, dmat


def _t5_bucket(rel):
    nb = N_BUCKETS // 2
    max_exact = nb // 2
    ret = (rel > 0).astype(jnp.int32) * nb
    n = jnp.abs(rel)
    large = max_exact + (jnp.log(jnp.maximum(n, 1).astype(F32) / max_exact)
                         / math.log(MAX_DISTANCE / max_exact) * (nb - max_exact)).astype(jnp.int32)
    large = jnp.minimum(large, nb - 1)
    return ret + jnp.where(n < max_exact, n, large)


def _bias_table(rel_bias, S):
    TQ = Q_TILE
    rel = jnp.arange(-(S - 1), S, dtype=jnp.int32)
    fvec = rel_bias[_t5_bucket(rel)].astype(F32).T
    H, L = fvec.shape
    u = jnp.concatenate([fvec, jnp.zeros((H, 1), F32)], axis=1)
    shifted = jnp.tile(u, (1, TQ))[:, :TQ * L].reshape(H, TQ, L)
    return shifted[:, :, TQ - 1:TQ - 1 + 2 * S - TQ]


def kernel(x, c, w_ada, b_ada, norm_mix_g, w_in, ret_gn_g, diff_subln_g, lambda_q1, lambda_k1, lambda_q2, lambda_k2, w_ret_out, w_diff_out, w_o, rel_bias, norm_ffn_g, w_router, w_exp_gate, w_exp_up, w_exp_down, final_g):
    B, S, D = x.shape
    cap = CAPACITY_FACTOR * S // N_EXPERTS
    l = 0

    mod = _ada(c, w_ada[l], b_ada[l]).reshape(B, 6, D)
    proj = _inproj(x, norm_mix_g[l].reshape(1, D), mod, _permute_w_in(w_in[l]), _rot_tables(S))

    cd, dec, dmat = _ret_tables()
    y_ret = _ret(proj, cd, dec, dmat, ret_gn_g[l].reshape(1, -1))

    lam_vecs = jnp.stack([lambda_q1[l], lambda_k1[l], lambda_q2[l], lambda_k2[l]]).astype(F32)
    y_diff = _diff(proj, lam_vecs, _bias_table(rel_bias, S), diff_subln_g[l].reshape(1, -1))

    x1, h2, logits = _merge(x, y_ret, y_diff, proj, mod,
                            w_ret_out[l].astype(BF16), w_diff_out[l].astype(BF16), w_o[l].astype(BF16),
                            norm_ffn_g[l].reshape(1, D), w_router[l].T.astype(BF16))

    rank, gate = _route(logits, cap)
    ybuf = _moe(rank, gate, h2, w_exp_gate[l], w_exp_up[l], w_exp_down[l], cap)
    return _scat(jnp.swapaxes(rank, 1, 2), ybuf, x1, mod, final_g.reshape(1, D), cap)
```

```python
import functools
import math

import numpy as np
import jax
import jax.numpy as jnp
from jax import lax
from jax.experimental import pallas as pl
from jax.experimental.pallas import tpu as pltpu

F32 = jnp.float32
BF16 = jnp.bfloat16

RET_HEADS = 4
RET_DK = 128
RET_DV = 128
RET_FWD_DECAY_OFFSET = 5.0
RET_BWD_DECAY_OFFSET = 5.5
RET_THETA_BASE = 10000.0
DIFF_HEADS = 4
DIFF_DH = 64
DIFF_DV = 2 * DIFF_DH
N_BUCKETS = 32
MAX_DISTANCE = 128
N_EXPERTS = 16
CAPACITY_FACTOR = 2
NORM_EPS = 1e-6
LAM_INIT = 0.8 - 0.6 * math.exp(-0.3 * 0)
LOG2E = math.log2(math.e)

LANES = 128
VMEM_LIMIT_BYTES = 56 * 2**20

TOKEN_TILE = 512
RET_CHUNK = 256
Q_TILE = 256
KEY_CHUNK = 512

COL_GR, COL_GD = 0, 8
COL_RQ, COL_RK, COL_RV, COL_RG = 16, 20, 24, 28
COL_DQ, COL_DK, COL_DV = 32, 36, 40
PROJ_COLS = 44 * LANES


def _cparams(sem):
    return pltpu.CompilerParams(dimension_semantics=sem, vmem_limit_bytes=VMEM_LIMIT_BYTES)


def _resident(shape, index_map):
    return pl.BlockSpec(shape, index_map, pipeline_mode=pl.Buffered(1))


def _ada_kernel(c_ref, w_ref, b_ref, o_ref):
    c = c_ref[...]
    a = c * jax.nn.sigmoid(c)
    o_ref[...] = jnp.dot(a, w_ref[...], preferred_element_type=F32,
                         precision=lax.Precision.HIGHEST) + b_ref[...]


def _ada(c, w, b):
    B, D = c.shape
    n = w.shape[1] // D
    return pl.pallas_call(
        _ada_kernel,
        grid=(n,),
        in_specs=[pl.BlockSpec((B, D), lambda j: (0, 0)),
                  pl.BlockSpec((D, D), lambda j: (0, j)),
                  pl.BlockSpec((1, D), lambda j: (0, j))],
        out_specs=pl.BlockSpec((B, D), lambda j: (0, j)),
        out_shape=jax.ShapeDtypeStruct((B, n * D), F32),
        compiler_params=_cparams(("arbitrary",)),
        name="ada",
    )(c, w, b.reshape(1, -1))


def _rms(x, g):
    ms = jnp.mean(x * x, axis=-1, keepdims=True)
    return x * lax.rsqrt(ms + NORM_EPS) * g


def _rot_half(x, cos, sin):
    return x * cos + pltpu.roll(x, LANES // 2, axis=1) * sin


def _inproj_kernel(x_ref, g_ref, mod_ref, w_ref, rot_ref, o_ref):
    x = x_ref[0]
    h = _rms(x, g_ref[...]) * (1.0 + mod_ref[0, 1:2, :]) + mod_ref[0, 0:1, :]
    hb = h.astype(BF16)

    def mm(col, width):
        return jnp.dot(hb, w_ref[:, col * LANES:(col + width) * LANES], preferred_element_type=F32)

    def put(col, val):
        o_ref[0, :, col * LANES:col * LANES + val.shape[1]] = val.astype(BF16)

    for col in range(COL_GR, COL_RQ, 4):
        put(col, jax.nn.sigmoid(mm(col, 4)))
    for base, t in ((COL_RQ, 0), (COL_RK, 2)):
        cos, sin = rot_ref[t], rot_ref[t + 1]
        r = mm(base, RET_HEADS)
        for hh in range(RET_HEADS):
            put(base + hh, _rot_half(r[:, hh * LANES:(hh + 1) * LANES], cos, sin))
    put(COL_RV, mm(COL_RV, 4))
    r = mm(COL_RG, 4)
    put(COL_RG, r * jax.nn.sigmoid(r))
    put(COL_DQ, mm(COL_DQ, 4) * (DIFF_DH ** -0.5 * LOG2E))
    put(COL_DK, mm(COL_DK, 4))
    put(COL_DV, mm(COL_DV, 4))


def _inproj(x, g, mod, w, rot):
    B, S, D = x.shape
    tm = TOKEN_TILE
    return pl.pallas_call(
        _inproj_kernel,
        grid=(B, S // tm),
        in_specs=[pl.BlockSpec((1, tm, D), lambda b, i: (b, i, 0)),
                  _resident((1, D), lambda b, i: (0, 0)),
                  pl.BlockSpec((1, 6, D), lambda b, i: (b, 0, 0)),
                  _resident((D, PROJ_COLS), lambda b, i: (0, 0)),
                  pl.BlockSpec((4, tm, LANES), lambda b, i: (0, i, 0))],
        out_specs=pl.BlockSpec((1, tm, PROJ_COLS), lambda b, i: (b, i, 0)),
        out_shape=jax.ShapeDtypeStruct((B, S, PROJ_COLS), BF16),
        compiler_params=_cparams(("arbitrary", "arbitrary")),
        name="inproj",
    )(x, g, mod, w, rot)


def _ret_kernel(cd_ref, q_ref, k_ref, v_ref, rg_ref, dec_ref, dm_ref, gn_ref, o_ref, acc_ref):
    hh = pl.program_id(1)
    S = q_ref.shape[1]
    C = RET_CHUNK
    nc = S // C
    dmat = dm_ref[0]
    qdf, kdf, qdb, kdb = dec_ref[0, 0], dec_ref[0, 1], dec_ref[0, 2], dec_ref[0, 3]
    nt = (((1,), (1,)), ((), ()))
    tn = (((0,), (0,)), ((), ()))

    def chunk(n):
        sl = pl.ds(n * C, C)
        return sl, q_ref[0, sl, :], k_ref[0, sl, :], v_ref[0, sl, :]

    def scaled(t, dec):
        return (t.astype(F32) * dec).astype(BF16)

    state = jnp.zeros((RET_DK, RET_DV), F32)
    for n in range(nc):
        sl, q, k, v = chunk(n)
        s = lax.dot_general(q, k, nt, preferred_element_type=F32) * dmat
        inner = jnp.dot(s.astype(BF16), v, preferred_element_type=F32)
        cross = jnp.dot(scaled(q, qdf), state.astype(BF16), preferred_element_type=F32)
        acc_ref[sl, :] = inner + cross
        kv = lax.dot_general(scaled(k, kdf), v, tn, preferred_element_type=F32)
        state = cd_ref[hh, 0] * state + kv
    state = jnp.zeros((RET_DK, RET_DV), F32)
    for n in reversed(range(nc)):
        sl, q, k, v = chunk(n)
        cross = jnp.dot(scaled(q, qdb), state.astype(BF16), preferred_element_type=F32)
        acc_ref[sl, :] = acc_ref[sl, :] + cross
        kv = lax.dot_general(scaled(k, kdb), v, tn, preferred_element_type=F32)
        state = cd_ref[hh, 1] * state + kv

    y = acc_ref[...]
    mu = jnp.mean(y, axis=-1, keepdims=True)
    yc = y - mu
    var = jnp.mean(yc * yc, axis=-1, keepdims=True)
    yn = yc * lax.rsqrt(var + NORM_EPS) * gn_ref[...]
    o_ref[0] = (rg_ref[0].astype(F32) * yn).astype(BF16)


def _ret(proj, cd, dec, dmat, gn):
    B, S, _ = proj.shape
    H = RET_HEADS
    C = RET_CHUNK

    def col(base):
        return pl.BlockSpec((1, S, LANES), lambda b, h: (b, 0, base + h))

    return pl.pallas_call(
        _ret_kernel,
        grid=(B, H),
        in_specs=[pl.BlockSpec(memory_space=pltpu.SMEM),
                  col(COL_RQ), col(COL_RK), col(COL_RV), col(COL_RG),
                  pl.BlockSpec((1, 4, C, LANES), lambda b, h: (h, 0, 0, 0)),
                  pl.BlockSpec((1, C, C), lambda b, h: (h, 0, 0)),
                  pl.BlockSpec((1, LANES), lambda b, h: (0, h))],
        out_specs=pl.BlockSpec((1, S, LANES), lambda b, h: (b, 0, h)),
        out_shape=jax.ShapeDtypeStruct((B, S, H * RET_DV), BF16),
        scratch_shapes=[pltpu.VMEM((S, RET_DV), F32)],
        compiler_params=_cparams(("arbitrary", "arbitrary")),
        name="ret",
    )(cd, proj, proj, proj, proj, dec, dmat, gn)


def _diff_kernel(lam_ref, q_ref, k_ref, v_ref, t_ref, g_ref, o_ref, s_ref, p_ref):
    S = k_ref.shape[1]
    TQ, KC = Q_TILE, KEY_CHUNK
    nk, nl = S // KC, KC // LANES
    lv = lam_ref[...]
    lam = (jnp.exp(jnp.sum(lv[0:1] * lv[1:2], axis=1, keepdims=True))
           - jnp.exp(jnp.sum(lv[2:3] * lv[3:4], axis=1, keepdims=True)) + LAM_INIT)
    first_half = lax.broadcasted_iota(jnp.int32, (TQ, LANES), 1) < DIFF_DH
    nt = (((1,), (1,)), ((), ()))

    def lane_tiles(a):
        return [a[:, j * LANES:(j + 1) * LANES] for j in range(nl)]

    def scores(qb, slot):
        w0 = pl.multiple_of((S - TQ) - qb * TQ, TQ)
        q = q_ref[0, pl.ds(pl.multiple_of(qb * TQ, TQ), TQ), :]
        zero = jnp.zeros_like(q)
        qz = (jnp.where(first_half, q, zero), jnp.where(first_half, zero, q))
        m = [jnp.full((TQ, LANES), -jnp.inf, F32) for _ in range(2)]
        for c in range(nk):
            cols = slice(c * KC, (c + 1) * KC)
            kc = k_ref[0, cols, :]
            bias = t_ref[0, :, pl.ds(w0 + c * KC, KC)]
            for i in range(2):
                s = lax.dot_general(qz[i], kc, nt, preferred_element_type=F32) + bias
                s_ref[slot, i, :, cols] = s
                for t in lane_tiles(s):
                    m[i] = jnp.maximum(m[i], t)
        return [jnp.max(m[i], axis=1, keepdims=True) for i in range(2)]

    def exponentials(slot, mr):
        l = [jnp.zeros((TQ, LANES), F32) for _ in range(2)]
        for c in range(nk):
            cols = slice(c * KC, (c + 1) * KC)
            for i in range(2):
                p = jnp.exp2(s_ref[slot, i, :, cols] - mr[i])
                for t in lane_tiles(p):
                    l[i] = l[i] + t
                p_ref[slot, i, :, cols] = p.astype(BF16)
        return [jnp.sum(l[i], axis=1, keepdims=True) for i in range(2)]

    def values(qb, slot, lr):
        ratio = jnp.broadcast_to(lam * lr[0] / lr[1], (TQ, LANES)).astype(BF16)
        o = jnp.zeros((TQ, DIFF_DV), F32)
        for c in range(nk):
            cols = slice(c * KC, (c + 1) * KC)
            p1, p2 = lane_tiles(p_ref[slot, 0, :, cols]), lane_tiles(p_ref[slot, 1, :, cols])
            a = jnp.concatenate([p1[j] - ratio * p2[j] for j in range(nl)], axis=1)
            o = o + jnp.dot(a, v_ref[0, cols, :], preferred_element_type=F32)
        y = _rms(o * (1.0 / lr[0]), g_ref[...]) * (1.0 - LAM_INIT)
        o_ref[0, pl.ds(pl.multiple_of(qb * TQ, TQ), TQ), :] = y.astype(BF16)

    def body(j, carry):
        qa, qb = 2 * j, 2 * j + 1
        ma = scores(qa, 0)
        la = exponentials(0, ma)
        mb = scores(qb, 1)
        values(qa, 0, la)
        lb = exponentials(1, mb)
        values(qb, 1, lb)
        return carry

    lax.fori_loop(0, S // (2 * TQ), body, 0)


def _diff(proj, lam_vecs, ttab, g):
    B, S, _ = proj.shape
    H = DIFF_HEADS

    def col(base):
        return pl.BlockSpec((1, S, LANES), lambda h, b: (b, 0, base + h))

    return pl.pallas_call(
        _diff_kernel,
        grid=(H, B),
        in_specs=[pl.BlockSpec((4, DIFF_DH), lambda h, b: (0, 0)),
                  col(COL_DQ), col(COL_DK), col(COL_DV),
                  pl.BlockSpec((1, Q_TILE, 2 * S - Q_TILE), lambda h, b: (h, 0, 0)),
                  pl.BlockSpec((1, DIFF_DV), lambda h, b: (0, 0))],
        out_specs=pl.BlockSpec((1, S, LANES), lambda h, b: (b, 0, h)),
        out_shape=jax.ShapeDtypeStruct((B, S, H * DIFF_DV), BF16),
        scratch_shapes=[pltpu.VMEM((2, 2, Q_TILE, S), F32), pltpu.VMEM((2, 2, Q_TILE, S), BF16)],
        compiler_params=_cparams(("arbitrary", "arbitrary")),
        name="diff",
    )(lam_vecs, proj, proj, proj, ttab, g)


def _merge_kernel(x_ref, yr_ref, yd_ref, gr_ref, gd_ref, mod_ref, wr_ref, wd_ref, wo_ref,
                  g_ref, wrt_ref, x1_ref, h2_ref, lg_ref):
    a = jnp.dot(yr_ref[0], wr_ref[...], preferred_element_type=F32)
    d = jnp.dot(yd_ref[0], wd_ref[...], preferred_element_type=F32)
    merged = gr_ref[0].astype(F32) * a + gd_ref[0].astype(F32) * d
    o = jnp.dot(merged.astype(BF16), wo_ref[...], preferred_element_type=F32)
    x1 = x_ref[0] + mod_ref[0, 2:3, :] * o
    x1_ref[0] = x1
    h2 = (_rms(x1, g_ref[...]) * (1.0 + mod_ref[0, 4:5, :]) + mod_ref[0, 3:4, :]).astype(BF16)
    h2_ref[0] = h2
    lg_ref[0] = lax.dot_general(wrt_ref[...], h2, (((1,), (1,)), ((), ())),
                                preferred_element_type=F32)


def _merge(x, yr, yd, proj, mod, wr, wd, wo, g, wrt):
    B, S, D = x.shape
    tm = TOKEN_TILE
    E = wrt.shape[0]
    gate_w = COL_GD - COL_GR
    return pl.pallas_call(
        _merge_kernel,
        grid=(B, S // tm),
        in_specs=[pl.BlockSpec((1, tm, D), lambda b, i: (b, i, 0)),
                  pl.BlockSpec((1, tm, yr.shape[2]), lambda b, i: (b, i, 0)),
                  pl.BlockSpec((1, tm, yd.shape[2]), lambda b, i: (b, i, 0)),
                  pl.BlockSpec((1, tm, gate_w * LANES), lambda b, i: (b, i, COL_GR // gate_w)),
                  pl.BlockSpec((1, tm, gate_w * LANES), lambda b, i: (b, i, COL_GD // gate_w)),
                  pl.BlockSpec((1, 6, D), lambda b, i: (b, 0, 0)),
                  _resident(wr.shape, lambda b, i: (0, 0)),
                  _resident(wd.shape, lambda b, i: (0, 0)),
                  _resident(wo.shape, lambda b, i: (0, 0)),
                  _resident((1, D), lambda b, i: (0, 0)),
                  _resident(wrt.shape, lambda b, i: (0, 0))],
        out_specs=[pl.BlockSpec((1, tm, D), lambda b, i: (b, i, 0)),
                   pl.BlockSpec((1, tm, D), lambda b, i: (b, i, 0)),
                   pl.BlockSpec((1, E, tm), lambda b, i: (b, 0, i))],
        out_shape=[jax.ShapeDtypeStruct((B, S, D), F32),
                   jax.ShapeDtypeStruct((B, S, D), BF16),
                   jax.ShapeDtypeStruct((B, E, S), F32)],
        compiler_params=_cparams(("arbitrary", "arbitrary")),
        name="merge",
    )(x, yr, yd, proj, proj, mod, wr, wd, wo, g, wrt)


def _lane_prefix(m, tri):
    E, S = m.shape
    off = jnp.zeros((E, 1), F32)
    parts = []
    for j in range(S // LANES):
        blk = m[:, j * LANES:(j + 1) * LANES]
        parts.append(jnp.dot(blk.astype(BF16), tri, preferred_element_type=F32) + off)
        off = off + jnp.sum(blk, axis=1, keepdims=True)
    return jnp.concatenate(parts, axis=1)


def _route_kernel(lg_ref, rank_ref, gate_ref, *, cap):
    lg = lg_ref[0]
    e = jnp.exp(lg - jnp.max(lg, axis=0, keepdims=True))
    aff = e / jnp.sum(e, axis=0, keepdims=True)
    bits = pltpu.bitcast(aff, jnp.int32)
    E = lg.shape[0]

    def count(mask):
        return jnp.sum(jnp.where(mask, 1.0, 0.0), axis=1, keepdims=True)

    thr = jnp.zeros((E, 1), jnp.int32)
    for bit in range(30, -1, -1):
        cand = thr | (1 << bit)
        thr = jnp.where(count(bits >= cand) >= cap, cand, thr)
    gt = bits > thr
    eq = bits == thr
    need = cap - count(gt)
    r = lax.broadcasted_iota(jnp.int32, (LANES, LANES), 0)
    c = lax.broadcasted_iota(jnp.int32, (LANES, LANES), 1)
    tri = jnp.where(r < c, 1.0, 0.0).astype(BF16)
    eq_before = _lane_prefix(jnp.where(eq, 1.0, 0.0), tri)
    sel = gt | (eq & (eq_before < need))
    slot = _lane_prefix(jnp.where(sel, 1.0, 0.0), tri)
    rank_ref[0] = jnp.where(sel, slot, -1.0).astype(jnp.int32)
    gate_ref[0] = jnp.where(sel, aff, 0.0)


def _route(logits, cap):
    B, E, S = logits.shape
    spec = pl.BlockSpec((1, E, S), lambda b: (b, 0, 0))
    return pl.pallas_call(
        functools.partial(_route_kernel, cap=cap),
        grid=(B,),
        in_specs=[spec],
        out_specs=[spec, spec],
        out_shape=[jax.ShapeDtypeStruct((B, E, S), jnp.int32),
                   jax.ShapeDtypeStruct((B, E, S), F32)],
        compiler_params=_cparams(("arbitrary",)),
        name="route",
    )(logits)


def _moe_kernel(rank_ref, gate_ref, h_ref, wg_ref, wu_ref, wd_ref, o_ref, wg_s, wu_s, wd_s, *, cap):
    @pl.when(pl.program_id(1) == 0)
    def _():
        wg_s[...] = wg_ref[0].astype(BF16)
        wu_s[...] = wu_ref[0].astype(BF16)
        wd_s[...] = wd_ref[0].astype(BF16)

    rank = rank_ref[0, 0]
    S = rank.shape[1]
    pick = lax.broadcasted_iota(jnp.int32, (cap, S), 0) == rank
    xin = jnp.dot(jnp.where(pick, 1.0, 0.0).astype(BF16), h_ref[0],
                  preferred_element_type=F32).astype(BF16)
    a = jnp.dot(xin, wg_s[...], preferred_element_type=F32)
    u = jnp.dot(xin, wu_s[...], preferred_element_type=F32)
    act = (a * jax.nn.sigmoid(a) * u).astype(BF16)
    y = jnp.dot(act, wd_s[...], preferred_element_type=F32)
    g = jnp.sum(jnp.where(pick, gate_ref[0, 0], 0.0), axis=1, keepdims=True)
    o_ref[0, 0] = (y * g).astype(BF16)


def _moe(rank, gate, h2, wg, wu, wd, cap):
    B, E, S = rank.shape
    D = h2.shape[2]
    Fd = wg.shape[2]
    row = pl.BlockSpec((1, 1, 1, S), lambda e, b: (b, e, 0, 0))
    return pl.pallas_call(
        functools.partial(_moe_kernel, cap=cap),
        grid=(E, B),
        in_specs=[row, row,
                  pl.BlockSpec((1, S, D), lambda e, b: (b, 0, 0)),
                  pl.BlockSpec((1, D, Fd), lambda e, b: (e, 0, 0)),
                  pl.BlockSpec((1, D, Fd), lambda e, b: (e, 0, 0)),
                  pl.BlockSpec((1, Fd, D), lambda e, b: (e, 0, 0))],
        out_specs=pl.BlockSpec((1, 1, cap, D), lambda e, b: (e, b, 0, 0)),
        out_shape=jax.ShapeDtypeStruct((E, B, cap, D), BF16),
        scratch_shapes=[pltpu.VMEM((D, Fd), BF16), pltpu.VMEM((D, Fd), BF16),
                        pltpu.VMEM((Fd, D), BF16)],
        compiler_params=_cparams(("arbitrary", "arbitrary")),
        name="moe",
    )(rank.reshape(B, E, 1, S), gate.reshape(B, E, 1, S), h2, wg, wu, wd)


def _scat_kernel(rt_ref, y_ref, x1_ref, mod_ref, g_ref, o_ref, *, cap):
    rt = rt_ref[0]
    tm, E = rt.shape
    slot = lax.broadcasted_iota(jnp.int32, (tm, cap), 1)
    place = jnp.concatenate(
        [jnp.where(rt[:, e:e + 1] == slot, 1.0, 0.0).astype(BF16) for e in range(E)], axis=1)
    y = y_ref[:, 0].reshape(E * cap, y_ref.shape[3])
    moe = jnp.dot(place, y, preferred_element_type=F32)
    x2 = x1_ref[0] + mod_ref[0, 5:6, :] * moe
    o_ref[0] = _rms(x2, g_ref[...])


def _scat(rank_t, ybuf, x1, mod, g, cap):
    B, S, D = x1.shape
    E = rank_t.shape[2]
    tm = TOKEN_TILE
    return pl.pallas_call(
        functools.partial(_scat_kernel, cap=cap),
        grid=(B, S // tm),
        in_specs=[pl.BlockSpec((1, tm, E), lambda b, i: (b, i, 0)),
                  pl.BlockSpec((E, 1, cap, D), lambda b, i: (0, b, 0, 0)),
                  pl.BlockSpec((1, tm, D), lambda b, i: (b, i, 0)),
                  pl.BlockSpec((1, 6, D), lambda b, i: (b, 0, 0)),
                  _resident((1, D), lambda b, i: (0, 0))],
        out_specs=pl.BlockSpec((1, tm, D), lambda b, i: (b, i, 0)),
        out_shape=jax.ShapeDtypeStruct((B, S, D), F32),
        compiler_params=_cparams(("arbitrary", "arbitrary")),
        name="scat",
    )(rank_t, ybuf, x1, mod, g)


def _permute_w_in(w):
    D = w.shape[0]
    w = w.astype(BF16)
    sizes = (512, 512, 512, 512, 256, 256, 256, 256, 512, 1024, 1024)
    rq, rk, rv, rg, dq1, dq2, dk1, dk2, dv, gr, gd = jnp.split(w, np.cumsum(sizes)[:-1].tolist(), axis=1)

    def halves(t):
        return t.reshape(D, RET_HEADS, RET_DK // 2, 2).transpose(0, 1, 3, 2).reshape(D, -1)

    def pair(a, b):
        return jnp.stack([a.reshape(D, DIFF_HEADS, DIFF_DH), b.reshape(D, DIFF_HEADS, DIFF_DH)],
                         axis=2).reshape(D, -1)

    return jnp.concatenate([gr, gd, halves(rq), halves(rk), rv, rg, pair(dq1, dq2), pair(dk1, dk2), dv],
                           axis=1)


def _rot_tables(S):
    half = RET_DK // 2
    inv = 1.0 / (RET_THETA_BASE ** np.linspace(0.0, 1.0, half))
    ang = np.arange(S, dtype=np.float64)[:, None] * inv[None, :]
    cos = np.concatenate([np.cos(ang), np.cos(ang)], axis=1)
    sin = np.concatenate([-np.sin(ang), np.sin(ang)], axis=1)
    sc = RET_DK ** -0.5
    return jnp.asarray(np.stack([cos, sin, cos * sc, sin * sc]), F32)


def _ret_tables():
    C = RET_CHUNK
    heads = np.arange(RET_HEADS, dtype=np.float64)
    lgf = np.log1p(-np.exp2(-RET_FWD_DECAY_OFFSET - heads))[:, None]
    lgb = np.log1p(-np.exp2(-RET_BWD_DECAY_OFFSET - heads))[:, None]
    idx = np.arange(C, dtype=np.float64)
    diff = idx[:, None] - idx[None, :]
    dmat = np.where(diff >= 0,
                    np.exp(np.maximum(diff, 0.0)[None] * lgf[:, :, None]),
                    np.exp(np.maximum(-diff, 0.0)[None] * lgb[:, :, None]))
    dec = np.stack([np.exp((idx + 1)[None, :] * lgf),
                    np.exp((C - 1 - idx)[None, :] * lgf),
                    np.exp((C - idx)[None, :] * lgb),
                    np.exp(idx[None, :] * lgb)], axis=1)
    dec = np.broadcast_to(dec[..., None], dec.shape + (LANES,))
    cd = np.concatenate([np.exp(C * lgf), np.exp(C * lgb)], axis=1)
    return jnp.asarray(cd, F32), jnp.asarray(dec, F32), jnp.asarray(dmat, F32)


def _t5_bucket(rel):
    nb = N_BUCKETS // 2
    max_exact = nb // 2
    ret = (rel > 0).astype(jnp.int32) * nb
    n = jnp.abs(rel)
    large = max_exact + (jnp.log(jnp.maximum(n, 1).astype(F32) / max_exact)
                         / math.log(MAX_DISTANCE / max_exact) * (nb - max_exact)).astype(jnp.int32)
    large = jnp.minimum(large, nb - 1)
    return ret + jnp.where(n < max_exact, n, large)


def _bias_table(rel_bias, S):
    TQ, M = Q_TILE, MAX_DISTANCE
    reach = TQ + M - 1
    rel = jnp.concatenate([jnp.arange(-reach, reach + 1, dtype=jnp.int32),
                           jnp.array([-(S - 1), S - 1], jnp.int32)])
    f = rel_bias[_t5_bucket(rel)].astype(F32).T * LOG2E
    H = f.shape[0]
    lo, hi = f[:, -2], f[:, -1]
    L = 2 * reach + 1
    u = jnp.concatenate([f[:, :L], jnp.zeros((H, 1), F32)], axis=1)
    shifted = jnp.tile(u, (1, TQ))[:, :TQ * L].reshape(H, TQ, L)
    band = shifted[:, :, TQ - 1:TQ - 1 + TQ + 2 * M]
    side = S - TQ - M
    return jnp.concatenate([jnp.broadcast_to(lo[:, None, None], (H, TQ, side)), band,
                            jnp.broadcast_to(hi[:, None, None], (H, TQ, side))], axis=2)


def kernel(x, c, w_ada, b_ada, norm_mix_g, w_in, ret_gn_g, diff_subln_g, lambda_q1, lambda_k1, lambda_q2, lambda_k2, w_ret_out, w_diff_out, w_o, rel_bias, norm_ffn_g, w_router, w_exp_gate, w_exp_up, w_exp_down, final_g):
    B, S, D = x.shape
    cap = CAPACITY_FACTOR * S // N_EXPERTS
    l = 0

    mod = _ada(c, w_ada[l], b_ada[l]).reshape(B, 6, D)
    proj = _inproj(x, norm_mix_g[l].reshape(1, D), mod, _permute_w_in(w_in[l]), _rot_tables(S))

    cd, dec, dmat = _ret_tables()
    y_ret = _ret(proj, cd, dec, dmat, ret_gn_g[l].reshape(1, -1))

    lam_vecs = jnp.stack([lambda_q1[l], lambda_k1[l], lambda_q2[l], lambda_k2[l]]).astype(F32)
    y_diff = _diff(proj, lam_vecs, _bias_table(rel_bias, S), diff_subln_g[l].reshape(1, -1))

    x1, h2, logits = _merge(x, y_ret, y_diff, proj, mod,
                            w_ret_out[l].astype(BF16), w_diff_out[l].astype(BF16), w_o[l].astype(BF16),
                            norm_ffn_g[l].reshape(1, D), w_router[l].T.astype(BF16))

    rank, gate = _route(logits, cap)
    ybuf = _moe(rank, gate, h2, w_exp_gate[l], w_exp_up[l], w_exp_down[l], cap)
    return _scat(jnp.swapaxes(rank, 1, 2), ybuf, x1, mod, final_g.reshape(1, D), cap)
```

```python
import functools
import math

import numpy as np
import jax
import jax.numpy as jnp
from jax import lax
from jax.experimental import pallas as pl
from jax.experimental.pallas import tpu as pltpu

F32 = jnp.float32
BF16 = jnp.bfloat16

RET_HEADS = 4
RET_DK = 128
RET_DV = 128
RET_FWD_DECAY_OFFSET = 5.0
RET_BWD_DECAY_OFFSET = 5.5
RET_THETA_BASE = 10000.0
DIFF_HEADS = 4
DIFF_DH = 64
DIFF_DV = 2 * DIFF_DH
N_BUCKETS = 32
MAX_DISTANCE = 128
N_EXPERTS = 16
CAPACITY_FACTOR = 2
NORM_EPS = 1e-6
LAM_INIT = 0.8 - 0.6 * math.exp(-0.3 * 0)
LOG2E = math.log2(math.e)

LANES = 128
VMEM_LIMIT_BYTES = 56 * 2**20

TOKEN_TILE = 512
RET_CHUNK = 256
Q_TILE = 256
KEY_CHUNK = 512
L_MIN = 2.0 ** -60

COL_GR, COL_GD = 0, 8
COL_RQ, COL_RK, COL_RV, COL_RG = 16, 20, 24, 28
COL_DQ, COL_DK, COL_DV = 32, 36, 40
PROJ_COLS = 44 * LANES


def _cparams(sem):
    return pltpu.CompilerParams(dimension_semantics=sem, vmem_limit_bytes=VMEM_LIMIT_BYTES)


def _resident(shape, index_map):
    return pl.BlockSpec(shape, index_map, pipeline_mode=pl.Buffered(1))


def _ada_kernel(c_ref, w_ref, b_ref, o_ref):
    c = c_ref[...]
    a = c * jax.nn.sigmoid(c)
    o_ref[...] = jnp.dot(a, w_ref[...], preferred_element_type=F32,
                         precision=lax.Precision.HIGHEST) + b_ref[...]


def _ada(c, w, b):
    B, D = c.shape
    n = w.shape[1] // D
    return pl.pallas_call(
        _ada_kernel,
        grid=(n,),
        in_specs=[pl.BlockSpec((B, D), lambda j: (0, 0)),
                  pl.BlockSpec((D, D), lambda j: (0, j)),
                  pl.BlockSpec((1, D), lambda j: (0, j))],
        out_specs=pl.BlockSpec((B, D), lambda j: (0, j)),
        out_shape=jax.ShapeDtypeStruct((B, n * D), F32),
        compiler_params=_cparams(("arbitrary",)),
        name="ada",
    )(c, w, b.reshape(1, -1))


def _rms(x, g):
    ms = jnp.mean(x * x, axis=-1, keepdims=True)
    return x * lax.rsqrt(ms + NORM_EPS) * g


def _rot_half(x, cos, sin):
    return x * cos + pltpu.roll(x, LANES // 2, axis=1) * sin


def _inproj_kernel(x_ref, g_ref, mod_ref, w_ref, rot_ref, o_ref):
    x = x_ref[0]
    h = _rms(x, g_ref[...]) * (1.0 + mod_ref[0, 1:2, :]) + mod_ref[0, 0:1, :]
    hb = h.astype(BF16)

    def mm(col, width):
        return jnp.dot(hb, w_ref[:, col * LANES:(col + width) * LANES], preferred_element_type=F32)

    def put(col, val):
        o_ref[0, :, col * LANES:col * LANES + val.shape[1]] = val.astype(BF16)

    for col in range(COL_GR, COL_RQ, 4):
        put(col, jax.nn.sigmoid(mm(col, 4)))
    for base, t in ((COL_RQ, 0), (COL_RK, 2)):
        cos, sin = rot_ref[t], rot_ref[t + 1]
        r = mm(base, RET_HEADS)
        for hh in range(RET_HEADS):
            put(base + hh, _rot_half(r[:, hh * LANES:(hh + 1) * LANES], cos, sin))
    put(COL_RV, mm(COL_RV, 4))
    r = mm(COL_RG, 4)
    put(COL_RG, r * jax.nn.sigmoid(r))
    put(COL_DQ, mm(COL_DQ, 4) * (DIFF_DH ** -0.5 * LOG2E))
    put(COL_DK, mm(COL_DK, 4))
    put(COL_DV, mm(COL_DV, 4))


def _inproj(x, g, mod, w, rot):
    B, S, D = x.shape
    tm = TOKEN_TILE
    return pl.pallas_call(
        _inproj_kernel,
        grid=(B, S // tm),
        in_specs=[pl.BlockSpec((1, tm, D), lambda b, i: (b, i, 0)),
                  _resident((1, D), lambda b, i: (0, 0)),
                  pl.BlockSpec((1, 6, D), lambda b, i: (b, 0, 0)),
                  _resident((D, PROJ_COLS), lambda b, i: (0, 0)),
                  pl.BlockSpec((4, tm, LANES), lambda b, i: (0, i, 0))],
        out_specs=pl.BlockSpec((1, tm, PROJ_COLS), lambda b, i: (b, i, 0)),
        out_shape=jax.ShapeDtypeStruct((B, S, PROJ_COLS), BF16),
        compiler_params=_cparams(("arbitrary", "arbitrary")),
        name="inproj",
    )(x, g, mod, w, rot)


def _ret_kernel(cd_ref, q_ref, k_ref, v_ref, rg_ref, dec_ref, dm_ref, gn_ref, o_ref, acc_ref):
    hh = pl.program_id(1)
    S = q_ref.shape[1]
    C = RET_CHUNK
    nc = S // C
    dmat = dm_ref[0]
    qdf, kdf, qdb, kdb = dec_ref[0, 0], dec_ref[0, 1], dec_ref[0, 2], dec_ref[0, 3]
    nt = (((1,), (1,)), ((), ()))
    tn = (((0,), (0,)), ((), ()))

    def chunk(n):
        sl = pl.ds(n * C, C)
        return sl, q_ref[0, sl, :], k_ref[0, sl, :], v_ref[0, sl, :]

    def scaled(t, dec):
        return (t.astype(F32) * dec).astype(BF16)

    state = jnp.zeros((RET_DK, RET_DV), F32)
    for n in range(nc):
        sl, q, k, v = chunk(n)
        s = lax.dot_general(q, k, nt, preferred_element_type=F32) * dmat
        inner = jnp.dot(s.astype(BF16), v, preferred_element_type=F32)
        cross = jnp.dot(scaled(q, qdf), state.astype(BF16), preferred_element_type=F32)
        acc_ref[sl, :] = inner + cross
        kv = lax.dot_general(scaled(k, kdf), v, tn, preferred_element_type=F32)
        state = cd_ref[hh, 0] * state + kv
    state = jnp.zeros((RET_DK, RET_DV), F32)
    for n in reversed(range(nc)):
        sl, q, k, v = chunk(n)
        cross = jnp.dot(scaled(q, qdb), state.astype(BF16), preferred_element_type=F32)
        acc_ref[sl, :] = acc_ref[sl, :] + cross
        kv = lax.dot_general(scaled(k, kdb), v, tn, preferred_element_type=F32)
        state = cd_ref[hh, 1] * state + kv

    y = acc_ref[...]
    mu = jnp.mean(y, axis=-1, keepdims=True)
    yc = y - mu
    var = jnp.mean(yc * yc, axis=-1, keepdims=True)
    yn = yc * lax.rsqrt(var + NORM_EPS) * gn_ref[...]
    o_ref[0] = (rg_ref[0].astype(F32) * yn).astype(BF16)


def _ret(proj, cd, dec, dmat, gn):
    B, S, _ = proj.shape
    H = RET_HEADS
    C = RET_CHUNK

    def col(base):
        return pl.BlockSpec((1, S, LANES), lambda b, h: (b, 0, base + h))

    return pl.pallas_call(
        _ret_kernel,
        grid=(B, H),
        in_specs=[pl.BlockSpec(memory_space=pltpu.SMEM),
                  col(COL_RQ), col(COL_RK), col(COL_RV), col(COL_RG),
                  pl.BlockSpec((1, 4, C, LANES), lambda b, h: (h, 0, 0, 0)),
                  pl.BlockSpec((1, C, C), lambda b, h: (h, 0, 0)),
                  pl.BlockSpec((1, LANES), lambda b, h: (0, h))],
        out_specs=pl.BlockSpec((1, S, LANES), lambda b, h: (b, 0, h)),
        out_shape=jax.ShapeDtypeStruct((B, S, H * RET_DV), BF16),
        scratch_shapes=[pltpu.VMEM((S, RET_DV), F32)],
        compiler_params=_cparams(("arbitrary", "arbitrary")),
        name="ret",
    )(cd, proj, proj, proj, proj, dec, dmat, gn)


def _diff_kernel(bmax_ref, lam_ref, q_ref, k_ref, v_ref, t_ref, g_ref, o_ref, p_ref):
    hh = pl.program_id(0)
    S = k_ref.shape[1]
    TQ, KC = Q_TILE, KEY_CHUNK
    nk, nl = S // KC, KC // LANES
    lv = lam_ref[...]
    lam = (jnp.exp(jnp.sum(lv[0:1] * lv[1:2], axis=1, keepdims=True))
           - jnp.exp(jnp.sum(lv[2:3] * lv[3:4], axis=1, keepdims=True)) + LAM_INIT)
    first_half = lax.broadcasted_iota(jnp.int32, (TQ, LANES), 1) < DIFF_DH
    nt = (((1,), (1,)), ((), ()))
    bmax = bmax_ref[hh]

    def half_norms(t):
        sq = t.astype(F32) * t.astype(F32)
        lane = lax.broadcasted_iota(jnp.int32, (1, LANES), 1) < DIFF_DH
        zero = jnp.zeros_like(sq)
        return [jnp.sqrt(jnp.sum(jnp.where(lane, sq, zero), axis=1, keepdims=True)),
                jnp.sqrt(jnp.sum(jnp.where(lane, zero, sq), axis=1, keepdims=True))]

    kmax = [jnp.max(n, axis=0, keepdims=True) for n in half_norms(k_ref[0])]

    def lane_tiles(a):
        return [a[:, j * LANES:(j + 1) * LANES] for j in range(nl)]

    def finish(o, q0):
        y = _rms(o, g_ref[...]) * (1.0 - LAM_INIT)
        o_ref[0, pl.ds(q0, TQ), :] = y.astype(BF16)

    def masked_q(qb):
        q = q_ref[0, pl.ds(pl.multiple_of(qb * TQ, TQ), TQ), :]
        zero = jnp.zeros_like(q)
        return q, (jnp.where(first_half, q, zero), jnp.where(first_half, zero, q))

    def exponentials(qb, slot):
        w0 = pl.multiple_of((S - TQ) - qb * TQ, TQ)
        q, qz = masked_q(qb)
        m = [qn * kn + bmax for qn, kn in zip(half_norms(q), kmax)]
        l = [jnp.zeros((TQ, LANES), F32) for _ in range(2)]
        for c in range(nk):
            cols = slice(c * KC, (c + 1) * KC)
            kc = k_ref[0, cols, :]
            bias = t_ref[0, :, pl.ds(w0 + c * KC, KC)]
            for i in range(2):
                s = lax.dot_general(qz[i], kc, nt, preferred_element_type=F32)
                p = jnp.exp2((s - m[i]) + bias)
                for t in lane_tiles(p):
                    l[i] = l[i] + t
                p_ref[slot, i, :, cols] = p.astype(BF16)
        return [jnp.sum(l[i], axis=1, keepdims=True) for i in range(2)]

    def values(qb, slot, lr):
        ratio = jnp.broadcast_to(lam * lr[0] / lr[1], (TQ, LANES)).astype(BF16)
        o = jnp.zeros((TQ, DIFF_DV), F32)
        for c in range(nk):
            cols = slice(c * KC, (c + 1) * KC)
            p1, p2 = lane_tiles(p_ref[slot, 0, :, cols]), lane_tiles(p_ref[slot, 1, :, cols])
            a = jnp.concatenate([p1[j] - ratio * p2[j] for j in range(nl)], axis=1)
            o = o + jnp.dot(a, v_ref[0, cols, :], preferred_element_type=F32)
        finish(o * (1.0 / lr[0]), pl.multiple_of(qb * TQ, TQ))

    def fast_pair(j, lmin):
        la = exponentials(2 * j, 0)
        lb = exponentials(2 * j + 1, 1)
        values(2 * j, 0, la)
        values(2 * j + 1, 1, lb)
        return jnp.minimum(lmin, jnp.minimum(jnp.minimum(la[0], la[1]), jnp.minimum(lb[0], lb[1])))

    lmin = lax.fori_loop(0, S // (2 * TQ), fast_pair, jnp.full((TQ, 1), jnp.inf, F32))

    @pl.when(jnp.min(lmin) < L_MIN)
    def _():
        def exact_tile(qb, carry):
            w0 = pl.multiple_of((S - TQ) - qb * TQ, TQ)
            _, qz = masked_q(qb)
            bias = t_ref[0, :, pl.ds(w0, S)]
            attn = None
            for i, scale in ((0, 1.0), (1, lam)):
                s = lax.dot_general(qz[i], k_ref[0], nt, preferred_element_type=F32) + bias
                p = jnp.exp2(s - jnp.max(s, axis=-1, keepdims=True))
                part = p * (scale / jnp.sum(p, axis=-1, keepdims=True))
                attn = part if attn is None else attn - part
            finish(jnp.dot(attn.astype(BF16), v_ref[0], preferred_element_type=F32),
                   pl.multiple_of(qb * TQ, TQ))
            return carry

        lax.fori_loop(0, S // TQ, exact_tile, 0)


def _diff(proj, lam_vecs, ttab, bmax, g):
    B, S, _ = proj.shape
    H = DIFF_HEADS

    def col(base):
        return pl.BlockSpec((1, S, LANES), lambda h, b: (b, 0, base + h))

    return pl.pallas_call(
        _diff_kernel,
        grid=(H, B),
        in_specs=[pl.BlockSpec(memory_space=pltpu.SMEM),
                  pl.BlockSpec((4, DIFF_DH), lambda h, b: (0, 0)),
                  col(COL_DQ), col(COL_DK), col(COL_DV),
                  pl.BlockSpec((1, Q_TILE, 2 * S - Q_TILE), lambda h, b: (h, 0, 0)),
                  pl.BlockSpec((1, DIFF_DV), lambda h, b: (0, 0))],
        out_specs=pl.BlockSpec((1, S, LANES), lambda h, b: (b, 0, h)),
        out_shape=jax.ShapeDtypeStruct((B, S, H * DIFF_DV), BF16),
        scratch_shapes=[pltpu.VMEM((2, 2, Q_TILE, S), BF16)],
        compiler_params=_cparams(("arbitrary", "arbitrary")),
        name="diff",
    )(bmax, lam_vecs, proj, proj, proj, ttab, g)


def _merge_kernel(x_ref, yr_ref, yd_ref, gr_ref, gd_ref, mod_ref, wr_ref, wd_ref, wo_ref,
                  g_ref, wrt_ref, x1_ref, h2_ref, lg_ref):
    a = jnp.dot(yr_ref[0], wr_ref[...], preferred_element_type=F32)
    d = jnp.dot(yd_ref[0], wd_ref[...], preferred_element_type=F32)
    merged = gr_ref[0].astype(F32) * a + gd_ref[0].astype(F32) * d
    o = jnp.dot(merged.astype(BF16), wo_ref[...], preferred_element_type=F32)
    x1 = x_ref[0] + mod_ref[0, 2:3, :] * o
    x1_ref[0] = x1
    h2 = (_rms(x1, g_ref[...]) * (1.0 + mod_ref[0, 4:5, :]) + mod_ref[0, 3:4, :]).astype(BF16)
    h2_ref[0] = h2
    lg_ref[0] = lax.dot_general(wrt_ref[...], h2, (((1,), (1,)), ((), ())),
                                preferred_element_type=F32)


def _merge(x, yr, yd, proj, mod, wr, wd, wo, g, wrt):
    B, S, D = x.shape
    tm = TOKEN_TILE
    E = wrt.shape[0]
    gate_w = COL_GD - COL_GR
    return pl.pallas_call(
        _merge_kernel,
        grid=(B, S // tm),
        in_specs=[pl.BlockSpec((1, tm, D), lambda b, i: (b, i, 0)),
                  pl.BlockSpec((1, tm, yr.shape[2]), lambda b, i: (b, i, 0)),
                  pl.BlockSpec((1, tm, yd.shape[2]), lambda b, i: (b, i, 0)),
                  pl.BlockSpec((1, tm, gate_w * LANES), lambda b, i: (b, i, COL_GR // gate_w)),
                  pl.BlockSpec((1, tm, gate_w * LANES), lambda b, i: (b, i, COL_GD // gate_w)),
                  pl.BlockSpec((1, 6, D), lambda b, i: (b, 0, 0)),
                  _resident(wr.shape, lambda b, i: (0, 0)),
                  _resident(wd.shape, lambda b, i: (0, 0)),
                  _resident(wo.shape, lambda b, i: (0, 0)),
                  _resident((1, D), lambda b, i: (0, 0)),
                  _resident(wrt.shape, lambda b, i: (0, 0))],
        out_specs=[pl.BlockSpec((1, tm, D), lambda b, i: (b, i, 0)),
                   pl.BlockSpec((1, tm, D), lambda b, i: (b, i, 0)),
                   pl.BlockSpec((1, E, tm), lambda b, i: (b, 0, i))],
        out_shape=[jax.ShapeDtypeStruct((B, S, D), F32),
                   jax.ShapeDtypeStruct((B, S, D), BF16),
                   jax.ShapeDtypeStruct((B, E, S), F32)],
        compiler_params=_cparams(("arbitrary", "arbitrary")),
        name="merge",
    )(x, yr, yd, proj, proj, mod, wr, wd, wo, g, wrt)


def _lane_prefix(m, tri):
    E, S = m.shape
    off = jnp.zeros((E, 1), F32)
    parts = []
    for j in range(S // LANES):
        blk = m[:, j * LANES:(j + 1) * LANES]
        parts.append(jnp.dot(blk.astype(BF16), tri, preferred_element_type=F32) + off)
        off = off + jnp.sum(blk, axis=1, keepdims=True)
    return jnp.concatenate(parts, axis=1)


def _route_kernel(lg_ref, rank_ref, gate_ref, *, cap):
    lg = lg_ref[0]
    e = jnp.exp(lg - jnp.max(lg, axis=0, keepdims=True))
    aff = e / jnp.sum(e, axis=0, keepdims=True)
    bits = pltpu.bitcast(aff, jnp.int32)
    E = lg.shape[0]

    def count(mask):
        return jnp.sum(jnp.where(mask, 1.0, 0.0), axis=1, keepdims=True)

    thr = jnp.zeros((E, 1), jnp.int32)
    for bit in range(30, -1, -1):
        cand = thr | (1 << bit)
        thr = jnp.where(count(bits >= cand) >= cap, cand, thr)
    gt = bits > thr
    eq = bits == thr
    need = cap - count(gt)
    r = lax.broadcasted_iota(jnp.int32, (LANES, LANES), 0)
    c = lax.broadcasted_iota(jnp.int32, (LANES, LANES), 1)
    tri = jnp.where(r < c, 1.0, 0.0).astype(BF16)
    eq_before = _lane_prefix(jnp.where(eq, 1.0, 0.0), tri)
    sel = gt | (eq & (eq_before < need))
    slot = _lane_prefix(jnp.where(sel, 1.0, 0.0), tri)
    rank_ref[0] = jnp.where(sel, slot, -1.0).astype(jnp.int32)
    gate_ref[0] = jnp.where(sel, aff, 0.0)


def _route(logits, cap):
    B, E, S = logits.shape
    spec = pl.BlockSpec((1, E, S), lambda b: (b, 0, 0))
    return pl.pallas_call(
        functools.partial(_route_kernel, cap=cap),
        grid=(B,),
        in_specs=[spec],
        out_specs=[spec, spec],
        out_shape=[jax.ShapeDtypeStruct((B, E, S), jnp.int32),
                   jax.ShapeDtypeStruct((B, E, S), F32)],
        compiler_params=_cparams(("arbitrary",)),
        name="route",
    )(logits)


def _moe_kernel(rank_ref, gate_ref, h_ref, wg_ref, wu_ref, wd_ref, o_ref, wg_s, wu_s, wd_s, *, cap):
    @pl.when(pl.program_id(1) == 0)
    def _():
        wg_s[...] = wg_ref[0].astype(BF16)
        wu_s[...] = wu_ref[0].astype(BF16)
        wd_s[...] = wd_ref[0].astype(BF16)

    rank = rank_ref[0, 0]
    S = rank.shape[1]
    pick = lax.broadcasted_iota(jnp.int32, (cap, S), 0) == rank
    xin = jnp.dot(jnp.where(pick, 1.0, 0.0).astype(BF16), h_ref[0],
                  preferred_element_type=F32).astype(BF16)
    a = jnp.dot(xin, wg_s[...], preferred_element_type=F32)
    u = jnp.dot(xin, wu_s[...], preferred_element_type=F32)
    act = (a * jax.nn.sigmoid(a) * u).astype(BF16)
    y = jnp.dot(act, wd_s[...], preferred_element_type=F32)
    g = jnp.sum(jnp.where(pick, gate_ref[0, 0], 0.0), axis=1, keepdims=True)
    o_ref[0, 0] = (y * g).astype(BF16)


def _moe(rank, gate, h2, wg, wu, wd, cap):
    B, E, S = rank.shape
    D = h2.shape[2]
    Fd = wg.shape[2]
    row = pl.BlockSpec((1, 1, 1, S), lambda e, b: (b, e, 0, 0))
    return pl.pallas_call(
        functools.partial(_moe_kernel, cap=cap),
        grid=(E, B),
        in_specs=[row, row,
                  pl.BlockSpec((1, S, D), lambda e, b: (b, 0, 0)),
                  pl.BlockSpec((1, D, Fd), lambda e, b: (e, 0, 0)),
                  pl.BlockSpec((1, D, Fd), lambda e, b: (e, 0, 0)),
                  pl.BlockSpec((1, Fd, D), lambda e, b: (e, 0, 0))],
        out_specs=pl.BlockSpec((1, 1, cap, D), lambda e, b: (e, b, 0, 0)),
        out_shape=jax.ShapeDtypeStruct((E, B, cap, D), BF16),
        scratch_shapes=[pltpu.VMEM((D, Fd), BF16), pltpu.VMEM((D, Fd), BF16),
                        pltpu.VMEM((Fd, D), BF16)],
        compiler_params=_cparams(("arbitrary", "arbitrary")),
        name="moe",
    )(rank.reshape(B, E, 1, S), gate.reshape(B, E, 1, S), h2, wg, wu, wd)


def _scat_kernel(rt_ref, y_ref, x1_ref, mod_ref, g_ref, o_ref, *, cap):
    rt = rt_ref[0]
    tm, E = rt.shape
    slot = lax.broadcasted_iota(jnp.int32, (tm, cap), 1)
    place = jnp.concatenate(
        [jnp.where(rt[:, e:e + 1] == slot, 1.0, 0.0).astype(BF16) for e in range(E)], axis=1)
    y = y_ref[:, 0].reshape(E * cap, y_ref.shape[3])
    moe = jnp.dot(place, y, preferred_element_type=F32)
    x2 = x1_ref[0] + mod_ref[0, 5:6, :] * moe
    o_ref[0] = _rms(x2, g_ref[...])


def _scat(rank_t, ybuf, x1, mod, g, cap):
    B, S, D = x1.shape
    E = rank_t.shape[2]
    tm = TOKEN_TILE
    return pl.pallas_call(
        functools.partial(_scat_kernel, cap=cap),
        grid=(B, S // tm),
        in_specs=[pl.BlockSpec((1, tm, E), lambda b, i: (b, i, 0)),
                  pl.BlockSpec((E, 1, cap, D), lambda b, i: (0, b, 0, 0)),
                  pl.BlockSpec((1, tm, D), lambda b, i: (b, i, 0)),
                  pl.BlockSpec((1, 6, D), lambda b, i: (b, 0, 0)),
                  _resident((1, D), lambda b, i: (0, 0))],
        out_specs=pl.BlockSpec((1, tm, D), lambda b, i: (b, i, 0)),
        out_shape=jax.ShapeDtypeStruct((B, S, D), F32),
        compiler_params=_cparams(("arbitrary", "arbitrary")),
        name="scat",
    )(rank_t, ybuf, x1, mod, g)


def _permute_w_in(w):
    D = w.shape[0]
    w = w.astype(BF16)
    sizes = (512, 512, 512, 512, 256, 256, 256, 256, 512, 1024, 1024)
    rq, rk, rv, rg, dq1, dq2, dk1, dk2, dv, gr, gd = jnp.split(w, np.cumsum(sizes)[:-1].tolist(), axis=1)

    def halves(t):
        return t.reshape(D, RET_HEADS, RET_DK // 2, 2).transpose(0, 1, 3, 2).reshape(D, -1)

    def pair(a, b):
        return jnp.stack([a.reshape(D, DIFF_HEADS, DIFF_DH), b.reshape(D, DIFF_HEADS, DIFF_DH)],
                         axis=2).reshape(D, -1)

    return jnp.concatenate([gr, gd, halves(rq), halves(rk), rv, rg, pair(dq1, dq2), pair(dk1, dk2), dv],
                           axis=1)


def _rot_tables(S):
    half = RET_DK // 2
    inv = 1.0 / (RET_THETA_BASE ** np.linspace(0.0, 1.0, half))
    ang = np.arange(S, dtype=np.float64)[:, None] * inv[None, :]
    cos = np.concatenate([np.cos(ang), np.cos(ang)], axis=1)
    sin = np.concatenate([-np.sin(ang), np.sin(ang)], axis=1)
    sc = RET_DK ** -0.5
    return jnp.asarray(np.stack([cos, sin, cos * sc, sin * sc]), F32)


def _ret_tables():
    C = RET_CHUNK
    heads = np.arange(RET_HEADS, dtype=np.float64)
    lgf = np.log1p(-np.exp2(-RET_FWD_DECAY_OFFSET - heads))[:, None]
    lgb = np.log1p(-np.exp2(-RET_BWD_DECAY_OFFSET - heads))[:, None]
    idx = np.arange(C, dtype=np.float64)
    diff = idx[:, None] - idx[None, :]
    dmat = np.where(diff >= 0,
                    np.exp(np.maximum(diff, 0.0)[None] * lgf[:, :, None]),
                    np.exp(np.maximum(-diff, 0.0)[None] * lgb[:, :, None]))
    dec = np.stack([np.exp((idx + 1)[None, :] * lgf),
                    np.exp((C - 1 - idx)[None, :] * lgf),
                    np.exp((C - idx)[None, :] * lgb),
                    np.exp(idx[None, :] * lgb)], axis=1)
    dec = np.broadcast_to(dec[..., None], dec.shape + (LANES,))
    cd = np.concatenate([np.exp(C * lgf), np.exp(C * lgb)], axis=1)
    return jnp.asarray(cd, F32), jnp.asarray(dec, F32), jnp.asarray(dmat, F32)


def _t5_bucket(rel):
    nb = N_BUCKETS // 2
    max_exact = nb // 2
    ret = (rel > 0).astype(jnp.int32) * nb
    n = jnp.abs(rel)
    large = max_exact + (jnp.log(jnp.maximum(n, 1).astype(F32) / max_exact)
                         / math.log(MAX_DISTANCE / max_exact) * (nb - max_exact)).astype(jnp.int32)
    large = jnp.minimum(large, nb - 1)
    return ret + jnp.where(n < max_exact, n, large)


def _bias_table(rel_bias, S):
    TQ, M = Q_TILE, MAX_DISTANCE
    reach = TQ + M - 1
    rel = jnp.concatenate([jnp.arange(-reach, reach + 1, dtype=jnp.int32),
                           jnp.array([-(S - 1), S - 1], jnp.int32)])
    f = rel_bias[_t5_bucket(rel)].astype(F32).T * LOG2E
    H = f.shape[0]
    lo, hi = f[:, -2], f[:, -1]
    L = 2 * reach + 1
    u = jnp.concatenate([f[:, :L], jnp.zeros((H, 1), F32)], axis=1)
    shifted = jnp.tile(u, (1, TQ))[:, :TQ * L].reshape(H, TQ, L)
    band = shifted[:, :, TQ - 1:TQ - 1 + TQ + 2 * M]
    side = S - TQ - M
    table = jnp.concatenate([jnp.broadcast_to(lo[:, None, None], (H, TQ, side)), band,
                             jnp.broadcast_to(hi[:, None, None], (H, TQ, side))], axis=2)
    return table, jnp.max(f, axis=1)


def kernel(x, c, w_ada, b_ada, norm_mix_g, w_in, ret_gn_g, diff_subln_g, lambda_q1, lambda_k1, lambda_q2, lambda_k2, w_ret_out, w_diff_out, w_o, rel_bias, norm_ffn_g, w_router, w_exp_gate, w_exp_up, w_exp_down, final_g):
    B, S, D = x.shape
    cap = CAPACITY_FACTOR * S // N_EXPERTS
    l = 0

    mod = _ada(c, w_ada[l], b_ada[l]).reshape(B, 6, D)
    proj = _inproj(x, norm_mix_g[l].reshape(1, D), mod, _permute_w_in(w_in[l]), _rot_tables(S))

    cd, dec, dmat = _ret_tables()
    y_ret = _ret(proj, cd, dec, dmat, ret_gn_g[l].reshape(1, -1))

    lam_vecs = jnp.stack([lambda_q1[l], lambda_k1[l], lambda_q2[l], lambda_k2[l]]).astype(F32)
    ttab, bmax = _bias_table(rel_bias, S)
    y_diff = _diff(proj, lam_vecs, ttab, bmax, diff_subln_g[l].reshape(1, -1))

    x1, h2, logits = _merge(x, y_ret, y_diff, proj, mod,
                            w_ret_out[l].astype(BF16), w_diff_out[l].astype(BF16), w_o[l].astype(BF16),
                            norm_ffn_g[l].reshape(1, D), w_router[l].T.astype(BF16))

    rank, gate = _route(logits, cap)
    ybuf = _moe(rank, gate, h2, w_exp_gate[l], w_exp_up[l], w_exp_down[l], cap)
    return _scat(jnp.swapaxes(rank, 1, 2), ybuf, x1, mod, final_g.reshape(1, D), cap)
```

```python
import functools
import math

import numpy as np
import jax
import jax.numpy as jnp
from jax import lax
from jax.experimental import pallas as pl
from jax.experimental.pallas import tpu as pltpu
from jax.experimental.pallas import tpu_sc as plsc

F32 = jnp.float32
BF16 = jnp.bfloat16

RET_HEADS = 4
RET_DK = 128
RET_DV = 128
RET_FWD_DECAY_OFFSET = 5.0
RET_BWD_DECAY_OFFSET = 5.5
RET_THETA_BASE = 10000.0
DIFF_HEADS = 4
DIFF_DH = 64
DIFF_DV = 2 * DIFF_DH
N_BUCKETS = 32
MAX_DISTANCE = 128
N_EXPERTS = 16
CAPACITY_FACTOR = 2
NORM_EPS = 1e-6
LAM_INIT = 0.8 - 0.6 * math.exp(-0.3 * 0)
LOG2E = math.log2(math.e)

LANES = 128
VMEM_LIMIT_BYTES = 56 * 2**20

SC_CORES = 2
SC_SUBCORES = 16
SC_LANES = 16
SC_GATHER_ROWS = 64

TOKEN_TILE = 512
PROJ_TILE = 1024
RET_CHUNK = 256
Q_TILE = 256
KEY_CHUNK = 512
L_MIN = 2.0 ** -60

COL_GR, COL_GD = 0, 8
COL_RQ, COL_RK, COL_RV, COL_RG = 16, 20, 24, 28
COL_DQ, COL_DK, COL_DV = 32, 36, 40
PROJ_COLS = 44 * LANES


def _cparams(sem):
    return pltpu.CompilerParams(dimension_semantics=sem, vmem_limit_bytes=VMEM_LIMIT_BYTES)


def _resident(shape, index_map):
    return pl.BlockSpec(shape, index_map, pipeline_mode=pl.Buffered(1))


def _ada_kernel(c_ref, w_ref, b_ref, o_ref):
    c = c_ref[...]
    a = c * jax.nn.sigmoid(c)
    o_ref[...] = jnp.dot(a, w_ref[...], preferred_element_type=F32,
                         precision=lax.Precision.HIGHEST) + b_ref[...]


def _ada(c, w, b):
    B, D = c.shape
    n = w.shape[1] // D
    return pl.pallas_call(
        _ada_kernel,
        grid=(n,),
        in_specs=[pl.BlockSpec((B, D), lambda j: (0, 0)),
                  pl.BlockSpec((D, D), lambda j: (0, j)),
                  pl.BlockSpec((1, D), lambda j: (0, j))],
        out_specs=pl.BlockSpec((B, D), lambda j: (0, j)),
        out_shape=jax.ShapeDtypeStruct((B, n * D), F32),
        compiler_params=_cparams(("arbitrary",)),
        name="ada",
    )(c, w, b.reshape(1, -1))


def _rms(x, g):
    ms = jnp.mean(x * x, axis=-1, keepdims=True)
    return x * lax.rsqrt(ms + NORM_EPS) * g


def _rot_half(x, cos, sin):
    return x * cos + pltpu.roll(x, LANES // 2, axis=1) * sin


def _inproj_kernel(x_ref, g_ref, mod_ref, w_ref, rot_ref, o_ref):
    x = x_ref[0]
    h = _rms(x, g_ref[...]) * (1.0 + mod_ref[0, 1:2, :]) + mod_ref[0, 0:1, :]
    hb = h.astype(BF16)

    def mm(col, width):
        return jnp.dot(hb, w_ref[:, col * LANES:(col + width) * LANES], preferred_element_type=F32)

    def put(col, val):
        o_ref[0, :, col * LANES:col * LANES + val.shape[1]] = val.astype(BF16)

    for col in range(COL_GR, COL_RQ, 4):
        put(col, jax.nn.sigmoid(mm(col, 4)))
    for base, t in ((COL_RQ, 0), (COL_RK, 2)):
        cos, sin = rot_ref[t], rot_ref[t + 1]
        r = mm(base, RET_HEADS)
        for hh in range(RET_HEADS):
            put(base + hh, _rot_half(r[:, hh * LANES:(hh + 1) * LANES], cos, sin))
    put(COL_RV, mm(COL_RV, 4))
    r = mm(COL_RG, 4)
    put(COL_RG, r * jax.nn.sigmoid(r))
    put(COL_DQ, mm(COL_DQ, 4) * (DIFF_DH ** -0.5 * LOG2E))
    put(COL_DK, mm(COL_DK, 4))
    put(COL_DV, mm(COL_DV, 4))


def _inproj(x, g, mod, w, rot):
    B, S, D = x.shape
    tm = PROJ_TILE
    return pl.pallas_call(
        _inproj_kernel,
        grid=(B, S // tm),
        in_specs=[pl.BlockSpec((1, tm, D), lambda b, i: (b, i, 0)),
                  _resident((1, D), lambda b, i: (0, 0)),
                  pl.BlockSpec((1, 6, D), lambda b, i: (b, 0, 0)),
                  _resident((D, PROJ_COLS), lambda b, i: (0, 0)),
                  pl.BlockSpec((4, tm, LANES), lambda b, i: (0, i, 0))],
        out_specs=pl.BlockSpec((1, tm, PROJ_COLS), lambda b, i: (b, i, 0)),
        out_shape=jax.ShapeDtypeStruct((B, S, PROJ_COLS), BF16),
        compiler_params=_cparams(("arbitrary", "arbitrary")),
        name="inproj",
    )(x, g, mod, w, rot)


def _ret_kernel(cd_ref, q_ref, k_ref, v_ref, rg_ref, dec_ref, dm_ref, gn_ref, o_ref, acc_ref):
    hh = pl.program_id(1)
    S = q_ref.shape[1]
    C = RET_CHUNK
    nc = S // C
    dmat = dm_ref[0]
    qdf, kdf, qdb, kdb = dec_ref[0, 0], dec_ref[0, 1], dec_ref[0, 2], dec_ref[0, 3]
    nt = (((1,), (1,)), ((), ()))
    tn = (((0,), (0,)), ((), ()))

    def chunk(n):
        sl = pl.ds(n * C, C)
        return sl, q_ref[0, sl, :], k_ref[0, sl, :], v_ref[0, sl, :]

    def scaled(t, dec):
        return (t.astype(F32) * dec).astype(BF16)

    state = jnp.zeros((RET_DK, RET_DV), F32)
    for n in range(nc):
        sl, q, k, v = chunk(n)
        s = lax.dot_general(q, k, nt, preferred_element_type=F32) * dmat
        inner = jnp.dot(s.astype(BF16), v, preferred_element_type=F32)
        cross = jnp.dot(scaled(q, qdf), state.astype(BF16), preferred_element_type=F32)
        acc_ref[sl, :] = inner + cross
        kv = lax.dot_general(scaled(k, kdf), v, tn, preferred_element_type=F32)
        state = cd_ref[hh, 0] * state + kv
    state = jnp.zeros((RET_DK, RET_DV), F32)
    for n in reversed(range(nc)):
        sl, q, k, v = chunk(n)
        cross = jnp.dot(scaled(q, qdb), state.astype(BF16), preferred_element_type=F32)
        acc_ref[sl, :] = acc_ref[sl, :] + cross
        kv = lax.dot_general(scaled(k, kdb), v, tn, preferred_element_type=F32)
        state = cd_ref[hh, 1] * state + kv

    y = acc_ref[...]
    mu = jnp.mean(y, axis=-1, keepdims=True)
    yc = y - mu
    var = jnp.mean(yc * yc, axis=-1, keepdims=True)
    yn = yc * lax.rsqrt(var + NORM_EPS) * gn_ref[...]
    o_ref[0] = (rg_ref[0].astype(F32) * yn).astype(BF16)


def _ret(proj, cd, dec, dmat, gn):
    B, S, _ = proj.shape
    H = RET_HEADS
    C = RET_CHUNK

    def col(base):
        return pl.BlockSpec((1, S, LANES), lambda b, h: (b, 0, base + h))

    return pl.pallas_call(
        _ret_kernel,
        grid=(B, H),
        in_specs=[pl.BlockSpec(memory_space=pltpu.SMEM),
                  col(COL_RQ), col(COL_RK), col(COL_RV), col(COL_RG),
                  pl.BlockSpec((1, 4, C, LANES), lambda b, h: (h, 0, 0, 0)),
                  pl.BlockSpec((1, C, C), lambda b, h: (h, 0, 0)),
                  pl.BlockSpec((1, LANES), lambda b, h: (0, h))],
        out_specs=pl.BlockSpec((1, S, LANES), lambda b, h: (b, 0, h)),
        out_shape=jax.ShapeDtypeStruct((B, S, H * RET_DV), BF16),
        scratch_shapes=[pltpu.VMEM((S, RET_DV), F32)],
        compiler_params=_cparams(("arbitrary", "arbitrary")),
        name="ret",
    )(cd, proj, proj, proj, proj, dec, dmat, gn)


def _diff_kernel(bmax_ref, lam_ref, q_ref, k_ref, v_ref, t_ref, g_ref, o_ref, p_ref):
    hh = pl.program_id(0)
    S = k_ref.shape[1]
    TQ, KC = Q_TILE, KEY_CHUNK
    nk, nl = S // KC, KC // LANES
    lv = lam_ref[...]
    lam = (jnp.exp(jnp.sum(lv[0:1] * lv[1:2], axis=1, keepdims=True))
           - jnp.exp(jnp.sum(lv[2:3] * lv[3:4], axis=1, keepdims=True)) + LAM_INIT)
    first_half = lax.broadcasted_iota(jnp.int32, (TQ, LANES), 1) < DIFF_DH
    nt = (((1,), (1,)), ((), ()))
    bmax = bmax_ref[hh]

    def half_norms(t):
        sq = t.astype(F32) * t.astype(F32)
        lane = lax.broadcasted_iota(jnp.int32, (1, LANES), 1) < DIFF_DH
        zero = jnp.zeros_like(sq)
        return [jnp.sqrt(jnp.sum(jnp.where(lane, sq, zero), axis=1, keepdims=True)),
                jnp.sqrt(jnp.sum(jnp.where(lane, zero, sq), axis=1, keepdims=True))]

    kmax = [jnp.max(n, axis=0, keepdims=True) for n in half_norms(k_ref[0])]

    def lane_tiles(a):
        return [a[:, j * LANES:(j + 1) * LANES] for j in range(nl)]

    def finish(o, q0):
        y = _rms(o, g_ref[...]) * (1.0 - LAM_INIT)
        o_ref[0, pl.ds(q0, TQ), :] = y.astype(BF16)

    def masked_q(qb):
        q = q_ref[0, pl.ds(pl.multiple_of(qb * TQ, TQ), TQ), :]
        zero = jnp.zeros_like(q)
        return q, (jnp.where(first_half, q, zero), jnp.where(first_half, zero, q))

    def exponentials(qb, slot):
        w0 = pl.multiple_of((S - TQ) - qb * TQ, TQ)
        q, qz = masked_q(qb)
        m = [qn * kn + bmax for qn, kn in zip(half_norms(q), kmax)]
        l = [jnp.zeros((TQ, LANES), F32) for _ in range(2)]
        for c in range(nk):
            cols = slice(c * KC, (c + 1) * KC)
            kc = k_ref[0, cols, :]
            bias = t_ref[0, :, pl.ds(w0 + c * KC, KC)]
            for i in range(2):
                s = lax.dot_general(qz[i], kc, nt, preferred_element_type=F32)
                p = jnp.exp2((s - m[i]) + bias)
                for t in lane_tiles(p):
                    l[i] = l[i] + t
                p_ref[slot, i, :, cols] = p.astype(BF16)
        return [jnp.sum(l[i], axis=1, keepdims=True) for i in range(2)]

    def values(qb, slot, lr):
        ratio = jnp.broadcast_to(lam * lr[0] / lr[1], (TQ, LANES)).astype(BF16)
        o = jnp.zeros((TQ, DIFF_DV), F32)
        for c in range(nk):
            cols = slice(c * KC, (c + 1) * KC)
            p1, p2 = lane_tiles(p_ref[slot, 0, :, cols]), lane_tiles(p_ref[slot, 1, :, cols])
            a = jnp.concatenate([p1[j] - ratio * p2[j] for j in range(nl)], axis=1)
            o = o + jnp.dot(a, v_ref[0, cols, :], preferred_element_type=F32)
        finish(o * (1.0 / lr[0]), pl.multiple_of(qb * TQ, TQ))

    def fast_pair(j, lmin):
        la = exponentials(2 * j, 0)
        lb = exponentials(2 * j + 1, 1)
        values(2 * j, 0, la)
        values(2 * j + 1, 1, lb)
        return jnp.minimum(lmin, jnp.minimum(jnp.minimum(la[0], la[1]), jnp.minimum(lb[0], lb[1])))

    lmin = lax.fori_loop(0, S // (2 * TQ), fast_pair, jnp.full((TQ, 1), jnp.inf, F32))

    @pl.when(jnp.min(lmin) < L_MIN)
    def _():
        def exact_tile(qb, carry):
            w0 = pl.multiple_of((S - TQ) - qb * TQ, TQ)
            _, qz = masked_q(qb)
            bias = t_ref[0, :, pl.ds(w0, S)]
            attn = None
            for i, scale in ((0, 1.0), (1, lam)):
                s = lax.dot_general(qz[i], k_ref[0], nt, preferred_element_type=F32) + bias
                p = jnp.exp2(s - jnp.max(s, axis=-1, keepdims=True))
                part = p * (scale / jnp.sum(p, axis=-1, keepdims=True))
                attn = part if attn is None else attn - part
            finish(jnp.dot(attn.astype(BF16), v_ref[0], preferred_element_type=F32),
                   pl.multiple_of(qb * TQ, TQ))
            return carry

        lax.fori_loop(0, S // TQ, exact_tile, 0)


def _diff(proj, lam_vecs, ttab, bmax, g):
    B, S, _ = proj.shape
    H = DIFF_HEADS

    def col(base):
        return pl.BlockSpec((1, S, LANES), lambda h, b: (b, 0, base + h))

    return pl.pallas_call(
        _diff_kernel,
        grid=(H, B),
        in_specs=[pl.BlockSpec(memory_space=pltpu.SMEM),
                  pl.BlockSpec((4, DIFF_DH), lambda h, b: (0, 0)),
                  col(COL_DQ), col(COL_DK), col(COL_DV),
                  pl.BlockSpec((1, Q_TILE, 2 * S - Q_TILE), lambda h, b: (h, 0, 0)),
                  pl.BlockSpec((1, DIFF_DV), lambda h, b: (0, 0))],
        out_specs=pl.BlockSpec((1, S, LANES), lambda h, b: (b, 0, h)),
        out_shape=jax.ShapeDtypeStruct((B, S, H * DIFF_DV), BF16),
        scratch_shapes=[pltpu.VMEM((2, 2, Q_TILE, S), BF16)],
        compiler_params=_cparams(("arbitrary", "arbitrary")),
        name="diff",
    )(bmax, lam_vecs, proj, proj, proj, ttab, g)


def _pack_halves(x):
    w = x.shape[1] // 2
    lo = pltpu.bitcast(x[:, :w].astype(BF16).astype(F32), jnp.int32)
    hi = pltpu.bitcast(x[:, w:].astype(BF16).astype(F32), jnp.int32)
    return lax.shift_right_logical(lo, jnp.full_like(lo, 16)) | (hi & jnp.int32(-65536))


def _unpack_halves(words):
    lo = pltpu.bitcast(lax.shift_left(words, jnp.full_like(words, 16)), F32)
    hi = pltpu.bitcast(words & jnp.int32(-65536), F32)
    return jnp.concatenate([lo, hi], axis=1).astype(BF16)


def _merge_kernel(x_ref, yr_ref, yd_ref, gr_ref, gd_ref, mod_ref, wr_ref, wd_ref, wo_ref,
                  g_ref, wrt_ref, x1_ref, h2_ref, lg_ref):
    a = jnp.dot(yr_ref[0], wr_ref[...], preferred_element_type=F32)
    d = jnp.dot(yd_ref[0], wd_ref[...], preferred_element_type=F32)
    merged = gr_ref[0].astype(F32) * a + gd_ref[0].astype(F32) * d
    o = jnp.dot(merged.astype(BF16), wo_ref[...], preferred_element_type=F32)
    x1 = x_ref[0] + mod_ref[0, 2:3, :] * o
    x1_ref[0] = x1
    h2 = _rms(x1, g_ref[...]) * (1.0 + mod_ref[0, 4:5, :]) + mod_ref[0, 3:4, :]
    h2_ref[0] = _pack_halves(h2)
    lg_ref[0] = lax.dot_general(wrt_ref[...], h2.astype(BF16), (((1,), (1,)), ((), ())),
                                preferred_element_type=F32)


def _merge(x, yr, yd, proj, mod, wr, wd, wo, g, wrt):
    B, S, D = x.shape
    tm = PROJ_TILE
    E = wrt.shape[0]
    gate_w = COL_GD - COL_GR
    return pl.pallas_call(
        _merge_kernel,
        grid=(B, S // tm),
        in_specs=[pl.BlockSpec((1, tm, D), lambda b, i: (b, i, 0)),
                  pl.BlockSpec((1, tm, yr.shape[2]), lambda b, i: (b, i, 0)),
                  pl.BlockSpec((1, tm, yd.shape[2]), lambda b, i: (b, i, 0)),
                  pl.BlockSpec((1, tm, gate_w * LANES), lambda b, i: (b, i, COL_GR // gate_w)),
                  pl.BlockSpec((1, tm, gate_w * LANES), lambda b, i: (b, i, COL_GD // gate_w)),
                  pl.BlockSpec((1, 6, D), lambda b, i: (b, 0, 0)),
                  _resident(wr.shape, lambda b, i: (0, 0)),
                  _resident(wd.shape, lambda b, i: (0, 0)),
                  _resident(wo.shape, lambda b, i: (0, 0)),
                  _resident((1, D), lambda b, i: (0, 0)),
                  _resident(wrt.shape, lambda b, i: (0, 0))],
        out_specs=[pl.BlockSpec((1, tm, D), lambda b, i: (b, i, 0)),
                   pl.BlockSpec((1, tm, D // 2), lambda b, i: (b, i, 0)),
                   pl.BlockSpec((1, E, tm), lambda b, i: (b, 0, i))],
        out_shape=[jax.ShapeDtypeStruct((B, S, D), F32),
                   jax.ShapeDtypeStruct((B, S, D // 2), jnp.int32),
                   jax.ShapeDtypeStruct((B, E, S), F32)],
        compiler_params=_cparams(("arbitrary", "arbitrary")),
        name="merge",
    )(x, yr, yd, proj, proj, mod, wr, wd, wo, g, wrt)


def _lane_prefix(m, tri):
    E, S = m.shape
    off = jnp.zeros((E, 1), F32)
    parts = []
    for j in range(S // LANES):
        blk = m[:, j * LANES:(j + 1) * LANES]
        parts.append(jnp.dot(blk.astype(BF16), tri, preferred_element_type=F32) + off)
        off = off + jnp.sum(blk, axis=1, keepdims=True)
    return jnp.concatenate(parts, axis=1)


def _route_kernel(lg_ref, rank_ref, gate_ref, *, cap):
    lg = lg_ref[0]
    e = jnp.exp(lg - jnp.max(lg, axis=0, keepdims=True))
    aff = e / jnp.sum(e, axis=0, keepdims=True)
    bits = pltpu.bitcast(aff, jnp.int32)
    E = lg.shape[0]

    def count(mask):
        return jnp.sum(jnp.where(mask, 1.0, 0.0), axis=1, keepdims=True)

    thr = jnp.zeros((E, 1), jnp.int32)
    for bit in range(30, -1, -1):
        cand = thr | (1 << bit)
        thr = jnp.where(count(bits >= cand) >= cap, cand, thr)
    gt = bits > thr
    eq = bits == thr
    need = cap - count(gt)
    r = lax.broadcasted_iota(jnp.int32, (LANES, LANES), 0)
    c = lax.broadcasted_iota(jnp.int32, (LANES, LANES), 1)
    tri = jnp.where(r < c, 1.0, 0.0).astype(BF16)
    eq_before = _lane_prefix(jnp.where(eq, 1.0, 0.0), tri)
    sel = gt | (eq & (eq_before < need))
    slot = _lane_prefix(jnp.where(sel, 1.0, 0.0), tri)
    rank_ref[0] = jnp.where(sel, slot, -1.0).astype(jnp.int32)
    gate_ref[0] = jnp.where(sel, aff, 0.0)


def _route(logits, cap):
    B, E, S = logits.shape
    spec = pl.BlockSpec((1, E, S), lambda b: (b, 0, 0))
    return pl.pallas_call(
        functools.partial(_route_kernel, cap=cap),
        grid=(B,),
        in_specs=[spec],
        out_specs=[spec, spec],
        out_shape=[jax.ShapeDtypeStruct((B, E, S), jnp.int32),
                   jax.ShapeDtypeStruct((B, E, S), F32)],
        compiler_params=_cparams(("arbitrary",)),
        name="route",
    )(logits)


def _gather_rows(table, rank, cap):
    B, E, S = rank.shape
    W = table.shape[1]
    workers = SC_CORES * SC_SUBCORES
    pairs = E * B
    per = pairs // workers
    assert per * workers == pairs and B & (B - 1) == 0 and cap % SC_GATHER_ROWS == 0 and S % SC_LANES == 0
    shift = B.bit_length() - 1
    mesh = plsc.VectorSubcoreMesh(core_axis_name="c", subcore_axis_name="s")

    def body(table_hbm, rank_hbm, out_hbm, rank_v, idx_v, buf_v):
        wid = lax.axis_index("s") * SC_CORES + lax.axis_index("c")
        for j in range(per):
            p = wid * per + j
            e = lax.shift_right_logical(p, shift)
            b = p & (B - 1)
            pltpu.sync_copy(rank_hbm.at[b, e], rank_v)
            base = b * S

            @pl.loop(0, S, step=SC_LANES)
            def _(t0):
                r = rank_v[pl.ds(t0, SC_LANES)]
                tok = lax.iota(jnp.int32, SC_LANES) + (base + t0)
                plsc.store_scatter(idx_v, [r], tok, mask=r >= 0)

            for c in range(cap // SC_GATHER_ROWS):
                rows = pl.ds(c * SC_GATHER_ROWS, SC_GATHER_ROWS)
                pltpu.sync_copy(table_hbm.at[idx_v.at[rows]], buf_v)
                pltpu.sync_copy(buf_v, out_hbm.at[pl.ds(p * cap + c * SC_GATHER_ROWS, SC_GATHER_ROWS)])

    return pl.kernel(
        body,
        out_type=jax.ShapeDtypeStruct((pairs * cap, W), jnp.int32),
        mesh=mesh,
        scratch_types=[pltpu.VMEM((S,), jnp.int32), pltpu.VMEM((cap,), jnp.int32),
                       pltpu.VMEM((SC_GATHER_ROWS, W), jnp.int32)],
        compiler_params=pltpu.CompilerParams(needs_layout_passes=False),
    )(table, rank)


def _moe_kernel(rank_ref, gate_ref, x_ref, wg_ref, wu_ref, wd_ref, o_ref, wg_s, wu_s, wd_s, *, cap):
    @pl.when(pl.program_id(1) == 0)
    def _():
        wg_s[...] = wg_ref[0].astype(BF16)
        wu_s[...] = wu_ref[0].astype(BF16)
        wd_s[...] = wd_ref[0].astype(BF16)

    rank = rank_ref[0, 0]
    S = rank.shape[1]
    pick = lax.broadcasted_iota(jnp.int32, (cap, S), 0) == rank
    xin = _unpack_halves(x_ref[0])
    a = jnp.dot(xin, wg_s[...], preferred_element_type=F32)
    u = jnp.dot(xin, wu_s[...], preferred_element_type=F32)
    act = (a * jax.nn.sigmoid(a) * u).astype(BF16)
    y = jnp.dot(act, wd_s[...], preferred_element_type=F32)
    g = jnp.sum(jnp.where(pick, gate_ref[0, 0], 0.0), axis=1, keepdims=True)
    o_ref[0, 0] = (y * g).astype(BF16)


def _moe(rank, gate, xin, wg, wu, wd, cap):
    B, E, S = rank.shape
    D, Fd = wg.shape[1], wg.shape[2]
    row = pl.BlockSpec((1, 1, 1, S), lambda e, b: (b, e, 0, 0))
    return pl.pallas_call(
        functools.partial(_moe_kernel, cap=cap),
        grid=(E, B),
        in_specs=[row, row,
                  pl.BlockSpec((1, cap, D // 2), lambda e, b: (e * B + b, 0, 0)),
                  pl.BlockSpec((1, D, Fd), lambda e, b: (e, 0, 0)),
                  pl.BlockSpec((1, D, Fd), lambda e, b: (e, 0, 0)),
                  pl.BlockSpec((1, Fd, D), lambda e, b: (e, 0, 0))],
        out_specs=pl.BlockSpec((1, 1, cap, D), lambda e, b: (e, b, 0, 0)),
        out_shape=jax.ShapeDtypeStruct((E, B, cap, D), BF16),
        scratch_shapes=[pltpu.VMEM((D, Fd), BF16), pltpu.VMEM((D, Fd), BF16),
                        pltpu.VMEM((Fd, D), BF16)],
        compiler_params=_cparams(("arbitrary", "arbitrary")),
        name="moe",
    )(rank.reshape(B, E, 1, S), gate.reshape(B, E, 1, S), xin.reshape(E * B, cap, D // 2), wg, wu, wd)


def _scat_kernel(rt_ref, y_ref, x1_ref, mod_ref, g_ref, o_ref, *, cap):
    rt = rt_ref[0]
    tm, E = rt.shape
    slot = lax.broadcasted_iota(jnp.int32, (tm, cap), 1)
    place = jnp.concatenate(
        [jnp.where(rt[:, e:e + 1] == slot, 1.0, 0.0).astype(BF16) for e in range(E)], axis=1)
    y = y_ref[:, 0].reshape(E * cap, y_ref.shape[3])
    moe = jnp.dot(place, y, preferred_element_type=F32)
    x2 = x1_ref[0] + mod_ref[0, 5:6, :] * moe
    o_ref[0] = _rms(x2, g_ref[...])


def _scat(rank_t, ybuf, x1, mod, g, cap):
    B, S, D = x1.shape
    E = rank_t.shape[2]
    tm = TOKEN_TILE
    return pl.pallas_call(
        functools.partial(_scat_kernel, cap=cap),
        grid=(B, S // tm),
        in_specs=[pl.BlockSpec((1, tm, E), lambda b, i: (b, i, 0)),
                  pl.BlockSpec((E, 1, cap, D), lambda b, i: (0, b, 0, 0)),
                  pl.BlockSpec((1, tm, D), lambda b, i: (b, i, 0)),
                  pl.BlockSpec((1, 6, D), lambda b, i: (b, 0, 0)),
                  _resident((1, D), lambda b, i: (0, 0))],
        out_specs=pl.BlockSpec((1, tm, D), lambda b, i: (b, i, 0)),
        out_shape=jax.ShapeDtypeStruct((B, S, D), F32),
        compiler_params=_cparams(("arbitrary", "arbitrary")),
        name="scat",
    )(rank_t, ybuf, x1, mod, g)


def _permute_w_in(w):
    D = w.shape[0]
    w = w.astype(BF16)
    sizes = (512, 512, 512, 512, 256, 256, 256, 256, 512, 1024, 1024)
    rq, rk, rv, rg, dq1, dq2, dk1, dk2, dv, gr, gd = jnp.split(w, np.cumsum(sizes)[:-1].tolist(), axis=1)

    def halves(t):
        return t.reshape(D, RET_HEADS, RET_DK // 2, 2).transpose(0, 1, 3, 2).reshape(D, -1)

    def pair(a, b):
        return jnp.stack([a.reshape(D, DIFF_HEADS, DIFF_DH), b.reshape(D, DIFF_HEADS, DIFF_DH)],
                         axis=2).reshape(D, -1)

    return jnp.concatenate([gr, gd, halves(rq), halves(rk), rv, rg, pair(dq1, dq2), pair(dk1, dk2), dv],
                           axis=1)


def _rot_tables(S):
    half = RET_DK // 2
    inv = 1.0 / (RET_THETA_BASE ** np.linspace(0.0, 1.0, half))
    ang = np.arange(S, dtype=np.float64)[:, None] * inv[None, :]
    cos = np.concatenate([np.cos(ang), np.cos(ang)], axis=1)
    sin = np.concatenate([-np.sin(ang), np.sin(ang)], axis=1)
    sc = RET_DK ** -0.5
    return jnp.asarray(np.stack([cos, sin, cos * sc, sin * sc]), F32)


def _ret_tables():
    C = RET_CHUNK
    heads = np.arange(RET_HEADS, dtype=np.float64)
    lgf = np.log1p(-np.exp2(-RET_FWD_DECAY_OFFSET - heads))[:, None]
    lgb = np.log1p(-np.exp2(-RET_BWD_DECAY_OFFSET - heads))[:, None]
    idx = np.arange(C, dtype=np.float64)
    diff = idx[:, None] - idx[None, :]
    dmat = np.where(diff >= 0,
                    np.exp(np.maximum(diff, 0.0)[None] * lgf[:, :, None]),
                    np.exp(np.maximum(-diff, 0.0)[None] * lgb[:, :, None]))
    dec = np.stack([np.exp((idx + 1)[None, :] * lgf),
                    np.exp((C - 1 - idx)[None, :] * lgf),
                    np.exp((C - idx)[None, :] * lgb),
                    np.exp(idx[None, :] * lgb)], axis=1)
    dec = np.broadcast_to(dec[..., None], dec.shape + (LANES,))
    cd = np.concatenate([np.exp(C * lgf), np.exp(C * lgb)], axis=1)
    return jnp.asarray(cd, F32), jnp.asarray(dec, F32), jnp.asarray(dmat, F32)


def _t5_bucket(rel):
    nb = N_BUCKETS // 2
    max_exact = nb // 2
    ret = (rel > 0).astype(jnp.int32) * nb
    n = jnp.abs(rel)
    large = max_exact + (jnp.log(jnp.maximum(n, 1).astype(F32) / max_exact)
                         / math.log(MAX_DISTANCE / max_exact) * (nb - max_exact)).astype(jnp.int32)
    large = jnp.minimum(large, nb - 1)
    return ret + jnp.where(n < max_exact, n, large)


def _bias_table(rel_bias, S):
    TQ, M = Q_TILE, MAX_DISTANCE
    reach = TQ + M - 1
    rel = jnp.concatenate([jnp.arange(-reach, reach + 1, dtype=jnp.int32),
                           jnp.array([-(S - 1), S - 1], jnp.int32)])
    f = rel_bias[_t5_bucket(rel)].astype(F32).T * LOG2E
    H = f.shape[0]
    lo, hi = f[:, -2], f[:, -1]
    L = 2 * reach + 1
    u = jnp.concatenate([f[:, :L], jnp.zeros((H, 1), F32)], axis=1)
    shifted = jnp.tile(u, (1, TQ))[:, :TQ * L].reshape(H, TQ, L)
    band = shifted[:, :, TQ - 1:TQ - 1 + TQ + 2 * M]
    side = S - TQ - M
    table = jnp.concatenate([jnp.broadcast_to(lo[:, None, None], (H, TQ, side)), band,
                             jnp.broadcast_to(hi[:, None, None], (H, TQ, side))], axis=2)
    return table, jnp.max(f, axis=1)


def kernel(x, c, w_ada, b_ada, norm_mix_g, w_in, ret_gn_g, diff_subln_g, lambda_q1, lambda_k1, lambda_q2, lambda_k2, w_ret_out, w_diff_out, w_o, rel_bias, norm_ffn_g, w_router, w_exp_gate, w_exp_up, w_exp_down, final_g):
    B, S, D = x.shape
    cap = CAPACITY_FACTOR * S // N_EXPERTS
    l = 0

    mod = _ada(c, w_ada[l], b_ada[l]).reshape(B, 6, D)
    proj = _inproj(x, norm_mix_g[l].reshape(1, D), mod, _permute_w_in(w_in[l]), _rot_tables(S))

    cd, dec, dmat = _ret_tables()
    y_ret = _ret(proj, cd, dec, dmat, ret_gn_g[l].reshape(1, -1))

    lam_vecs = jnp.stack([lambda_q1[l], lambda_k1[l], lambda_q2[l], lambda_k2[l]]).astype(F32)
    ttab, bmax = _bias_table(rel_bias, S)
    y_diff = _diff(proj, lam_vecs, ttab, bmax, diff_subln_g[l].reshape(1, -1))

    x1, h2, logits = _merge(x, y_ret, y_diff, proj, mod,
                            w_ret_out[l].astype(BF16), w_diff_out[l].astype(BF16), w_o[l].astype(BF16),
                            norm_ffn_g[l].reshape(1, D), w_router[l].T.astype(BF16))

    rank, gate = _route(logits, cap)
    xin = _gather_rows(h2.reshape(B * S, D // 2), rank, cap)
    ybuf = _moe(rank, gate, xin, w_exp_gate[l], w_exp_up[l], w_exp_down[l], cap)
    return _scat(jnp.swapaxes(rank, 1, 2), ybuf, x1, mod, final_g.reshape(1, D), cap)
```

```python
import functools
import math

import numpy as np
import jax
import jax.numpy as jnp
from jax import lax
from jax.experimental import pallas as pl
from jax.experimental.pallas import tpu as pltpu
from jax.experimental.pallas import tpu_sc as plsc

F32 = jnp.float32
BF16 = jnp.bfloat16

RET_HEADS = 4
RET_DK = 128
RET_DV = 128
RET_FWD_DECAY_OFFSET = 5.0
RET_BWD_DECAY_OFFSET = 5.5
RET_THETA_BASE = 10000.0
DIFF_HEADS = 4
DIFF_DH = 64
DIFF_DV = 2 * DIFF_DH
N_BUCKETS = 32
MAX_DISTANCE = 128
N_EXPERTS = 16
CAPACITY_FACTOR = 2
NORM_EPS = 1e-6
LAM_INIT = 0.8 - 0.6 * math.exp(-0.3 * 0)
LOG2E = math.log2(math.e)

LANES = 128
VMEM_LIMIT_BYTES = 56 * 2**20

SC_CORES = 2
SC_SUBCORES = 16
SC_LANES = 16
SC_GATHER_ROWS = 64

TOKEN_TILE = 512
PROJ_TILE = 1024
RET_CHUNK = 256
Q_TILE = 256
KEY_CHUNK = 512
L_MIN = 2.0 ** -60

COL_GR, COL_GD = 0, 8
COL_RQ, COL_RK, COL_RV, COL_RG = 16, 20, 24, 28
COL_DQ, COL_DK, COL_DV = 32, 36, 40
PROJ_COLS = 44 * LANES


def _cparams(sem):
    return pltpu.CompilerParams(dimension_semantics=sem, vmem_limit_bytes=VMEM_LIMIT_BYTES)


def _resident(shape, index_map):
    return pl.BlockSpec(shape, index_map, pipeline_mode=pl.Buffered(1))


def _ada_kernel(c_ref, w_ref, b_ref, o_ref):
    c = c_ref[...]
    a = c * jax.nn.sigmoid(c)
    o_ref[...] = jnp.dot(a, w_ref[...], preferred_element_type=F32,
                         precision=lax.Precision.HIGHEST) + b_ref[...]


def _ada(c, w, b):
    B, D = c.shape
    n = w.shape[1] // D
    return pl.pallas_call(
        _ada_kernel,
        grid=(n,),
        in_specs=[pl.BlockSpec((B, D), lambda j: (0, 0)),
                  pl.BlockSpec((D, D), lambda j: (0, j)),
                  pl.BlockSpec((1, D), lambda j: (0, j))],
        out_specs=pl.BlockSpec((B, D), lambda j: (0, j)),
        out_shape=jax.ShapeDtypeStruct((B, n * D), F32),
        compiler_params=_cparams(("arbitrary",)),
        name="ada",
    )(c, w, b.reshape(1, -1))


def _rms(x, g):
    ms = jnp.mean(x * x, axis=-1, keepdims=True)
    return x * lax.rsqrt(ms + NORM_EPS) * g


def _rot_half(x, cos, sin):
    return x * cos + pltpu.roll(x, LANES // 2, axis=1) * sin


def _inproj_kernel(x_ref, g_ref, mod_ref, w_ref, rot_ref, o_ref):
    x = x_ref[0]
    h = _rms(x, g_ref[...]) * (1.0 + mod_ref[0, 1:2, :]) + mod_ref[0, 0:1, :]
    hb = h.astype(BF16)

    def mm(col, width):
        return jnp.dot(hb, w_ref[:, col * LANES:(col + width) * LANES], preferred_element_type=F32)

    def put(col, val):
        o_ref[0, :, col * LANES:col * LANES + val.shape[1]] = val.astype(BF16)

    for col in range(COL_GR, COL_RQ, 4):
        put(col, jax.nn.sigmoid(mm(col, 4)))
    for base, t in ((COL_RQ, 0), (COL_RK, 2)):
        cos, sin = rot_ref[t], rot_ref[t + 1]
        r = mm(base, RET_HEADS)
        for hh in range(RET_HEADS):
            put(base + hh, _rot_half(r[:, hh * LANES:(hh + 1) * LANES], cos, sin))
    put(COL_RV, mm(COL_RV, 4))
    r = mm(COL_RG, 4)
    put(COL_RG, r * jax.nn.sigmoid(r))
    put(COL_DQ, mm(COL_DQ, 4) * (DIFF_DH ** -0.5 * LOG2E))
    put(COL_DK, mm(COL_DK, 4))
    put(COL_DV, mm(COL_DV, 4))


def _inproj(x, g, mod, w, rot):
    B, S, D = x.shape
    tm = PROJ_TILE
    return pl.pallas_call(
        _inproj_kernel,
        grid=(B, S // tm),
        in_specs=[pl.BlockSpec((1, tm, D), lambda b, i: (b, i, 0)),
                  _resident((1, D), lambda b, i: (0, 0)),
                  pl.BlockSpec((1, 6, D), lambda b, i: (b, 0, 0)),
                  _resident((D, PROJ_COLS), lambda b, i: (0, 0)),
                  pl.BlockSpec((4, tm, LANES), lambda b, i: (0, i, 0))],
        out_specs=pl.BlockSpec((1, tm, PROJ_COLS), lambda b, i: (b, i, 0)),
        out_shape=jax.ShapeDtypeStruct((B, S, PROJ_COLS), BF16),
        compiler_params=_cparams(("arbitrary", "arbitrary")),
        name="inproj",
    )(x, g, mod, w, rot)


def _ret_kernel(cd_ref, q_ref, k_ref, v_ref, rg_ref, dec_ref, dm_ref, gn_ref, o_ref, acc_ref):
    hh = pl.program_id(1)
    S = q_ref.shape[1]
    C = RET_CHUNK
    nc = S // C
    dmat = dm_ref[0]
    qdf, kdf, qdb, kdb = dec_ref[0, 0], dec_ref[0, 1], dec_ref[0, 2], dec_ref[0, 3]
    nt = (((1,), (1,)), ((), ()))
    tn = (((0,), (0,)), ((), ()))

    def chunk(n):
        sl = pl.ds(n * C, C)
        return sl, q_ref[0, sl, :], k_ref[0, sl, :], v_ref[0, sl, :]

    def scaled(t, dec):
        return (t.astype(F32) * dec).astype(BF16)

    state = jnp.zeros((RET_DK, RET_DV), F32)
    for n in range(nc):
        sl, q, k, v = chunk(n)
        s = lax.dot_general(q, k, nt, preferred_element_type=F32) * dmat
        inner = jnp.dot(s.astype(BF16), v, preferred_element_type=F32)
        cross = jnp.dot(scaled(q, qdf), state.astype(BF16), preferred_element_type=F32)
        acc_ref[sl, :] = inner + cross
        kv = lax.dot_general(scaled(k, kdf), v, tn, preferred_element_type=F32)
        state = cd_ref[hh, 0] * state + kv
    state = jnp.zeros((RET_DK, RET_DV), F32)
    for n in reversed(range(nc)):
        sl, q, k, v = chunk(n)
        cross = jnp.dot(scaled(q, qdb), state.astype(BF16), preferred_element_type=F32)
        acc_ref[sl, :] = acc_ref[sl, :] + cross
        kv = lax.dot_general(scaled(k, kdb), v, tn, preferred_element_type=F32)
        state = cd_ref[hh, 1] * state + kv

    y = acc_ref[...]
    mu = jnp.mean(y, axis=-1, keepdims=True)
    yc = y - mu
    var = jnp.mean(yc * yc, axis=-1, keepdims=True)
    yn = yc * lax.rsqrt(var + NORM_EPS) * gn_ref[...]
    o_ref[0] = (rg_ref[0].astype(F32) * yn).astype(BF16)


def _ret(proj, cd, dec, dmat, gn):
    B, S, _ = proj.shape
    H = RET_HEADS
    C = RET_CHUNK

    def col(base):
        return pl.BlockSpec((1, S, LANES), lambda b, h: (b, 0, base + h))

    return pl.pallas_call(
        _ret_kernel,
        grid=(B, H),
        in_specs=[pl.BlockSpec(memory_space=pltpu.SMEM),
                  col(COL_RQ), col(COL_RK), col(COL_RV), col(COL_RG),
                  pl.BlockSpec((1, 4, C, LANES), lambda b, h: (h, 0, 0, 0)),
                  pl.BlockSpec((1, C, C), lambda b, h: (h, 0, 0)),
                  pl.BlockSpec((1, LANES), lambda b, h: (0, h))],
        out_specs=pl.BlockSpec((1, S, LANES), lambda b, h: (b, 0, h)),
        out_shape=jax.ShapeDtypeStruct((B, S, H * RET_DV), BF16),
        scratch_shapes=[pltpu.VMEM((S, RET_DV), F32)],
        compiler_params=_cparams(("arbitrary", "arbitrary")),
        name="ret",
    )(cd, proj, proj, proj, proj, dec, dmat, gn)


def _diff_kernel(bmax_ref, lam_ref, q_ref, k_ref, v_ref, t_ref, g_ref, o_ref, p_ref):
    hh = pl.program_id(0)
    S = k_ref.shape[1]
    TQ, KC = Q_TILE, KEY_CHUNK
    nk, nl = S // KC, KC // LANES
    lv = lam_ref[...]
    lam = (jnp.exp(jnp.sum(lv[0:1] * lv[1:2], axis=1, keepdims=True))
           - jnp.exp(jnp.sum(lv[2:3] * lv[3:4], axis=1, keepdims=True)) + LAM_INIT)
    first_half = lax.broadcasted_iota(jnp.int32, (TQ, LANES), 1) < DIFF_DH
    nt = (((1,), (1,)), ((), ()))
    bmax = bmax_ref[hh]

    def half_norms(t):
        sq = t.astype(F32) * t.astype(F32)
        lane = lax.broadcasted_iota(jnp.int32, (1, LANES), 1) < DIFF_DH
        zero = jnp.zeros_like(sq)
        return [jnp.sqrt(jnp.sum(jnp.where(lane, sq, zero), axis=1, keepdims=True)),
                jnp.sqrt(jnp.sum(jnp.where(lane, zero, sq), axis=1, keepdims=True))]

    kmax = [jnp.max(n, axis=0, keepdims=True) for n in half_norms(k_ref[0])]

    def lane_tiles(a):
        return [a[:, j * LANES:(j + 1) * LANES] for j in range(nl)]

    def finish(o, q0):
        y = _rms(o, g_ref[...]) * (1.0 - LAM_INIT)
        o_ref[0, pl.ds(q0, TQ), :] = y.astype(BF16)

    def masked_q(qb):
        q = q_ref[0, pl.ds(pl.multiple_of(qb * TQ, TQ), TQ), :]
        zero = jnp.zeros_like(q)
        return q, (jnp.where(first_half, q, zero), jnp.where(first_half, zero, q))

    def exponentials(qb, slot):
        w0 = pl.multiple_of((S - TQ) - qb * TQ, TQ)
        q, qz = masked_q(qb)
        m = [qn * kn + bmax for qn, kn in zip(half_norms(q), kmax)]
        l = [jnp.zeros((TQ, LANES), F32) for _ in range(2)]
        for c in range(nk):
            cols = slice(c * KC, (c + 1) * KC)
            kc = k_ref[0, cols, :]
            bias = t_ref[0, :, pl.ds(w0 + c * KC, KC)]
            for i in range(2):
                s = lax.dot_general(qz[i], kc, nt, preferred_element_type=F32)
                p = jnp.exp2((s - m[i]) + bias)
                for t in lane_tiles(p):
                    l[i] = l[i] + t
                p_ref[slot, i, :, cols] = p.astype(BF16)
        return [jnp.sum(l[i], axis=1, keepdims=True) for i in range(2)]

    def values(qb, slot, lr):
        ratio = jnp.broadcast_to(lam * lr[0] / lr[1], (TQ, LANES)).astype(BF16)
        o = jnp.zeros((TQ, DIFF_DV), F32)
        for c in range(nk):
            cols = slice(c * KC, (c + 1) * KC)
            p1, p2 = lane_tiles(p_ref[slot, 0, :, cols]), lane_tiles(p_ref[slot, 1, :, cols])
            a = jnp.concatenate([p1[j] - ratio * p2[j] for j in range(nl)], axis=1)
            o = o + jnp.dot(a, v_ref[0, cols, :], preferred_element_type=F32)
        finish(o * (1.0 / lr[0]), pl.multiple_of(qb * TQ, TQ))

    def fast_pair(j, lmin):
        la = exponentials(2 * j, 0)
        lb = exponentials(2 * j + 1, 1)
        values(2 * j, 0, la)
        values(2 * j + 1, 1, lb)
        return jnp.minimum(lmin, jnp.minimum(jnp.minimum(la[0], la[1]), jnp.minimum(lb[0], lb[1])))

    lmin = lax.fori_loop(0, S // (2 * TQ), fast_pair, jnp.full((TQ, 1), jnp.inf, F32))

    @pl.when(jnp.min(lmin) < L_MIN)
    def _():
        def exact_tile(qb, carry):
            w0 = pl.multiple_of((S - TQ) - qb * TQ, TQ)
            _, qz = masked_q(qb)
            bias = t_ref[0, :, pl.ds(w0, S)]
            attn = None
            for i, scale in ((0, 1.0), (1, lam)):
                s = lax.dot_general(qz[i], k_ref[0], nt, preferred_element_type=F32) + bias
                p = jnp.exp2(s - jnp.max(s, axis=-1, keepdims=True))
                part = p * (scale / jnp.sum(p, axis=-1, keepdims=True))
                attn = part if attn is None else attn - part
            finish(jnp.dot(attn.astype(BF16), v_ref[0], preferred_element_type=F32),
                   pl.multiple_of(qb * TQ, TQ))
            return carry

        lax.fori_loop(0, S // TQ, exact_tile, 0)


def _diff(proj, lam_vecs, ttab, bmax, g):
    B, S, _ = proj.shape
    H = DIFF_HEADS

    def col(base):
        return pl.BlockSpec((1, S, LANES), lambda h, b: (b, 0, base + h))

    return pl.pallas_call(
        _diff_kernel,
        grid=(H, B),
        in_specs=[pl.BlockSpec(memory_space=pltpu.SMEM),
                  pl.BlockSpec((4, DIFF_DH), lambda h, b: (0, 0)),
                  col(COL_DQ), col(COL_DK), col(COL_DV),
                  pl.BlockSpec((1, Q_TILE, 2 * S - Q_TILE), lambda h, b: (h, 0, 0)),
                  pl.BlockSpec((1, DIFF_DV), lambda h, b: (0, 0))],
        out_specs=pl.BlockSpec((1, S, LANES), lambda h, b: (b, 0, h)),
        out_shape=jax.ShapeDtypeStruct((B, S, H * DIFF_DV), BF16),
        scratch_shapes=[pltpu.VMEM((2, 2, Q_TILE, S), BF16)],
        compiler_params=_cparams(("arbitrary", "arbitrary")),
        name="diff",
    )(bmax, lam_vecs, proj, proj, proj, ttab, g)


def _pack_halves(x):
    w = x.shape[1] // 2
    lo = pltpu.bitcast(x[:, :w].astype(BF16).astype(F32), jnp.int32)
    hi = pltpu.bitcast(x[:, w:].astype(BF16).astype(F32), jnp.int32)
    return lax.shift_right_logical(lo, jnp.full_like(lo, 16)) | (hi & jnp.int32(-65536))


def _unpack_halves(words):
    lo = pltpu.bitcast(lax.shift_left(words, jnp.full_like(words, 16)), F32)
    hi = pltpu.bitcast(words & jnp.int32(-65536), F32)
    return jnp.concatenate([lo, hi], axis=1).astype(BF16)


def _merge_kernel(x_ref, yr_ref, yd_ref, gr_ref, gd_ref, mod_ref, wr_ref, wd_ref, wo_ref,
                  g_ref, wrt_ref, x1_ref, h2_ref, lg_ref):
    a = jnp.dot(yr_ref[0], wr_ref[...], preferred_element_type=F32)
    d = jnp.dot(yd_ref[0], wd_ref[...], preferred_element_type=F32)
    merged = gr_ref[0].astype(F32) * a + gd_ref[0].astype(F32) * d
    o = jnp.dot(merged.astype(BF16), wo_ref[...], preferred_element_type=F32)
    x1 = x_ref[0] + mod_ref[0, 2:3, :] * o
    x1_ref[0] = x1
    h2 = _rms(x1, g_ref[...]) * (1.0 + mod_ref[0, 4:5, :]) + mod_ref[0, 3:4, :]
    h2_ref[0] = _pack_halves(h2)
    lg_ref[0] = lax.dot_general(wrt_ref[...], h2.astype(BF16), (((1,), (1,)), ((), ())),
                                preferred_element_type=F32)


def _merge(x, yr, yd, proj, mod, wr, wd, wo, g, wrt):
    B, S, D = x.shape
    tm = PROJ_TILE
    E = wrt.shape[0]
    gate_w = COL_GD - COL_GR
    return pl.pallas_call(
        _merge_kernel,
        grid=(B, S // tm),
        in_specs=[pl.BlockSpec((1, tm, D), lambda b, i: (b, i, 0)),
                  pl.BlockSpec((1, tm, yr.shape[2]), lambda b, i: (b, i, 0)),
                  pl.BlockSpec((1, tm, yd.shape[2]), lambda b, i: (b, i, 0)),
                  pl.BlockSpec((1, tm, gate_w * LANES), lambda b, i: (b, i, COL_GR // gate_w)),
                  pl.BlockSpec((1, tm, gate_w * LANES), lambda b, i: (b, i, COL_GD // gate_w)),
                  pl.BlockSpec((1, 6, D), lambda b, i: (b, 0, 0)),
                  _resident(wr.shape, lambda b, i: (0, 0)),
                  _resident(wd.shape, lambda b, i: (0, 0)),
                  _resident(wo.shape, lambda b, i: (0, 0)),
                  _resident((1, D), lambda b, i: (0, 0)),
                  _resident(wrt.shape, lambda b, i: (0, 0))],
        out_specs=[pl.BlockSpec((1, tm, D), lambda b, i: (b, i, 0)),
                   pl.BlockSpec((1, tm, D // 2), lambda b, i: (b, i, 0)),
                   pl.BlockSpec((1, E, tm), lambda b, i: (b, 0, i))],
        out_shape=[jax.ShapeDtypeStruct((B, S, D), F32),
                   jax.ShapeDtypeStruct((B, S, D // 2), jnp.int32),
                   jax.ShapeDtypeStruct((B, E, S), F32)],
        compiler_params=_cparams(("arbitrary", "arbitrary")),
        name="merge",
    )(x, yr, yd, proj, proj, mod, wr, wd, wo, g, wrt)


def _lane_prefix(m, tri):
    E, S = m.shape
    off = jnp.zeros((E, 1), F32)
    parts = []
    for j in range(S // LANES):
        blk = m[:, j * LANES:(j + 1) * LANES]
        parts.append(jnp.dot(blk.astype(BF16), tri, preferred_element_type=F32) + off)
        off = off + jnp.sum(blk, axis=1, keepdims=True)
    return jnp.concatenate(parts, axis=1)


def _route_kernel(lg_ref, rank_ref, gate_ref, *, cap):
    lg = lg_ref[0]
    e = jnp.exp(lg - jnp.max(lg, axis=0, keepdims=True))
    aff = e / jnp.sum(e, axis=0, keepdims=True)
    bits = pltpu.bitcast(aff, jnp.int32)
    E = lg.shape[0]

    def count(mask):
        return jnp.sum(jnp.where(mask, 1.0, 0.0), axis=1, keepdims=True)

    thr = jnp.zeros((E, 1), jnp.int32)
    for bit in range(30, -1, -1):
        cand = thr | (1 << bit)
        thr = jnp.where(count(bits >= cand) >= cap, cand, thr)
    gt = bits > thr
    eq = bits == thr
    need = cap - count(gt)
    r = lax.broadcasted_iota(jnp.int32, (LANES, LANES), 0)
    c = lax.broadcasted_iota(jnp.int32, (LANES, LANES), 1)
    tri = jnp.where(r < c, 1.0, 0.0).astype(BF16)
    eq_before = _lane_prefix(jnp.where(eq, 1.0, 0.0), tri)
    sel = gt | (eq & (eq_before < need))
    slot = _lane_prefix(jnp.where(sel, 1.0, 0.0), tri)
    rank_ref[0] = jnp.where(sel, slot, -1.0).astype(jnp.int32)
    gate_ref[0] = jnp.where(sel, aff, 0.0)


def _route(logits, cap):
    B, E, S = logits.shape
    spec = pl.BlockSpec((1, E, S), lambda b: (b, 0, 0))
    return pl.pallas_call(
        functools.partial(_route_kernel, cap=cap),
        grid=(B,),
        in_specs=[spec],
        out_specs=[spec, spec],
        out_shape=[jax.ShapeDtypeStruct((B, E, S), jnp.int32),
                   jax.ShapeDtypeStruct((B, E, S), F32)],
        compiler_params=_cparams(("arbitrary",)),
        name="route",
    )(logits)


def _gather_rows(table, rank, cap):
    B, E, S = rank.shape
    W = table.shape[1]
    workers = SC_CORES * SC_SUBCORES
    pairs = E * B
    per = pairs // workers
    assert per * workers == pairs and B & (B - 1) == 0 and cap % SC_GATHER_ROWS == 0 and S % SC_LANES == 0
    shift = B.bit_length() - 1
    mesh = plsc.VectorSubcoreMesh(core_axis_name="c", subcore_axis_name="s")

    R = SC_GATHER_ROWS
    n_chunks = cap // R

    def body(table_hbm, rank_hbm, out_hbm, rank_v, idx_v, buf_v, gsem, wsem):
        wid = lax.axis_index("s") * SC_CORES + lax.axis_index("c")

        def fetch(c):
            return pltpu.make_async_copy(table_hbm.at[idx_v.at[pl.ds(c * R, R)]], buf_v.at[c % 2], gsem.at[c % 2])

        def flush(p, c):
            return pltpu.make_async_copy(buf_v.at[c % 2], out_hbm.at[pl.ds(p * cap + c * R, R)], wsem.at[c % 2])

        for j in range(per):
            p = wid * per + j
            e = lax.shift_right_logical(p, shift)
            b = p & (B - 1)
            pltpu.sync_copy(rank_hbm.at[b, e], rank_v)
            base = b * S

            @pl.loop(0, S, step=SC_LANES)
            def _(t0):
                r = rank_v[pl.ds(t0, SC_LANES)]
                tok = lax.iota(jnp.int32, SC_LANES) + (base + t0)
                plsc.store_scatter(idx_v, [r], tok, mask=r >= 0)

            fetch(0).start()
            for c in range(n_chunks):
                fetch(c).wait()
                if c >= 1:
                    flush(p, c - 1).wait()
                if c + 1 < n_chunks:
                    fetch(c + 1).start()
                flush(p, c).start()
            flush(p, n_chunks - 1).wait()

    return pl.kernel(
        body,
        out_type=jax.ShapeDtypeStruct((pairs * cap, W), jnp.int32),
        mesh=mesh,
        scratch_types=[pltpu.VMEM((S,), jnp.int32), pltpu.VMEM((cap,), jnp.int32),
                       pltpu.VMEM((2, R, W), jnp.int32),
                       pltpu.SemaphoreType.DMA((2,)), pltpu.SemaphoreType.DMA((2,))],
        compiler_params=pltpu.CompilerParams(needs_layout_passes=False),
    )(table, rank)


def _moe_kernel(rank_ref, gate_ref, x_ref, wg_ref, wu_ref, wd_ref, o_ref, wg_s, wu_s, wd_s, *, cap):
    r, b = pl.program_id(0), pl.program_id(1)
    n_exp = pl.num_programs(0) - 1

    @pl.when(r < n_exp)
    def _():
        slot = r % 2
        rows_in, rows_ff = wg_ref.shape[1], wd_ref.shape[1]
        wg_s[slot, pl.ds(pl.multiple_of(b * rows_in, rows_in), rows_in), :] = wg_ref[0].astype(BF16)
        wu_s[slot, pl.ds(pl.multiple_of(b * rows_in, rows_in), rows_in), :] = wu_ref[0].astype(BF16)
        wd_s[slot, pl.ds(pl.multiple_of(b * rows_ff, rows_ff), rows_ff), :] = wd_ref[0].astype(BF16)

    @pl.when(r == 0)
    def _():
        o_ref[...] = jnp.zeros_like(o_ref)

    @pl.when(r > 0)
    def _():
        slot = (r - 1) % 2
        rank = rank_ref[0, 0]
        S = rank.shape[1]
        pick = lax.broadcasted_iota(jnp.int32, (cap, S), 0) == rank
        xin = _unpack_halves(x_ref[0])
        a = jnp.dot(xin, wg_s[slot], preferred_element_type=F32)
        u = jnp.dot(xin, wu_s[slot], preferred_element_type=F32)
        act = (a * jax.nn.sigmoid(a) * u).astype(BF16)
        y = jnp.dot(act, wd_s[slot], preferred_element_type=F32)
        g = jnp.sum(jnp.where(pick, gate_ref[0, 0], 0.0), axis=1, keepdims=True)
        o_ref[0, 0] = (y * g).astype(BF16)


def _moe(rank, gate, xin, wg, wu, wd, cap):
    B, E, S = rank.shape
    D, Fd = wg.shape[1], wg.shape[2]
    assert D % B == 0 and Fd % B == 0

    def cur(r):
        return jnp.maximum(r - 1, 0)

    def nxt(r):
        return jnp.minimum(r, E - 1)

    row = pl.BlockSpec((1, 1, 1, S), lambda r, b: (b, cur(r), 0, 0))
    return pl.pallas_call(
        functools.partial(_moe_kernel, cap=cap),
        grid=(E + 1, B),
        in_specs=[row, row,
                  pl.BlockSpec((1, cap, D // 2), lambda r, b: (cur(r) * B + b, 0, 0)),
                  pl.BlockSpec((1, D // B, Fd), lambda r, b: (nxt(r), b, 0)),
                  pl.BlockSpec((1, D // B, Fd), lambda r, b: (nxt(r), b, 0)),
                  pl.BlockSpec((1, Fd // B, D), lambda r, b: (nxt(r), b, 0))],
        out_specs=pl.BlockSpec((1, 1, cap, D), lambda r, b: (jnp.where(r == 0, E, r - 1), b, 0, 0)),
        out_shape=jax.ShapeDtypeStruct((E + 1, B, cap, D), BF16),
        scratch_shapes=[pltpu.VMEM((2, D, Fd), BF16), pltpu.VMEM((2, D, Fd), BF16),
                        pltpu.VMEM((2, Fd, D), BF16)],
        compiler_params=_cparams(("arbitrary", "arbitrary")),
        name="moe",
    )(rank.reshape(B, E, 1, S), gate.reshape(B, E, 1, S), xin.reshape(E * B, cap, D // 2), wg, wu, wd)


def _scat_kernel(rt_ref, y_ref, x1_ref, mod_ref, g_ref, o_ref, *, cap):
    rt = rt_ref[0]
    tm, E = rt.shape
    slot = lax.broadcasted_iota(jnp.int32, (tm, cap), 1)
    place = jnp.concatenate(
        [jnp.where(rt[:, e:e + 1] == slot, 1.0, 0.0).astype(BF16) for e in range(E)], axis=1)
    y = y_ref[:, 0].reshape(E * cap, y_ref.shape[3])
    moe = jnp.dot(place, y, preferred_element_type=F32)
    x2 = x1_ref[0] + mod_ref[0, 5:6, :] * moe
    o_ref[0] = _rms(x2, g_ref[...])


def _scat(rank_t, ybuf, x1, mod, g, cap):
    B, S, D = x1.shape
    E = rank_t.shape[2]
    tm = TOKEN_TILE
    return pl.pallas_call(
        functools.partial(_scat_kernel, cap=cap),
        grid=(B, S // tm),
        in_specs=[pl.BlockSpec((1, tm, E), lambda b, i: (b, i, 0)),
                  pl.BlockSpec((E, 1, cap, D), lambda b, i: (0, b, 0, 0)),
                  pl.BlockSpec((1, tm, D), lambda b, i: (b, i, 0)),
                  pl.BlockSpec((1, 6, D), lambda b, i: (b, 0, 0)),
                  _resident((1, D), lambda b, i: (0, 0))],
        out_specs=pl.BlockSpec((1, tm, D), lambda b, i: (b, i, 0)),
        out_shape=jax.ShapeDtypeStruct((B, S, D), F32),
        compiler_params=_cparams(("arbitrary", "arbitrary")),
        name="scat",
    )(rank_t, ybuf, x1, mod, g)


def _permute_w_in(w):
    D = w.shape[0]
    w = w.astype(BF16)
    sizes = (512, 512, 512, 512, 256, 256, 256, 256, 512, 1024, 1024)
    rq, rk, rv, rg, dq1, dq2, dk1, dk2, dv, gr, gd = jnp.split(w, np.cumsum(sizes)[:-1].tolist(), axis=1)

    def halves(t):
        return t.reshape(D, RET_HEADS, RET_DK // 2, 2).transpose(0, 1, 3, 2).reshape(D, -1)

    def pair(a, b):
        return jnp.stack([a.reshape(D, DIFF_HEADS, DIFF_DH), b.reshape(D, DIFF_HEADS, DIFF_DH)],
                         axis=2).reshape(D, -1)

    return jnp.concatenate([gr, gd, halves(rq), halves(rk), rv, rg, pair(dq1, dq2), pair(dk1, dk2), dv],
                           axis=1)


def _rot_tables(S):
    half = RET_DK // 2
    inv = 1.0 / (RET_THETA_BASE ** np.linspace(0.0, 1.0, half))
    ang = np.arange(S, dtype=np.float64)[:, None] * inv[None, :]
    cos = np.concatenate([np.cos(ang), np.cos(ang)], axis=1)
    sin = np.concatenate([-np.sin(ang), np.sin(ang)], axis=1)
    sc = RET_DK ** -0.5
    return jnp.asarray(np.stack([cos, sin, cos * sc, sin * sc]), F32)


def _ret_tables():
    C = RET_CHUNK
    heads = np.arange(RET_HEADS, dtype=np.float64)
    lgf = np.log1p(-np.exp2(-RET_FWD_DECAY_OFFSET - heads))[:, None]
    lgb = np.log1p(-np.exp2(-RET_BWD_DECAY_OFFSET - heads))[:, None]
    idx = np.arange(C, dtype=np.float64)
    diff = idx[:, None] - idx[None, :]
    dmat = np.where(diff >= 0,
                    np.exp(np.maximum(diff, 0.0)[None] * lgf[:, :, None]),
                    np.exp(np.maximum(-diff, 0.0)[None] * lgb[:, :, None]))
    dec = np.stack([np.exp((idx + 1)[None, :] * lgf),
                    np.exp((C - 1 - idx)[None, :] * lgf),
                    np.exp((C - idx)[None, :] * lgb),
                    np.exp(idx[None, :] * lgb)], axis=1)
    dec = np.broadcast_to(dec[..., None], dec.shape + (LANES,))
    cd = np.concatenate([np.exp(C * lgf), np.exp(C * lgb)], axis=1)
    return jnp.asarray(cd, F32), jnp.asarray(dec, F32), jnp.asarray(dmat, F32)


def _t5_bucket(rel):
    nb = N_BUCKETS // 2
    max_exact = nb // 2
    ret = (rel > 0).astype(jnp.int32) * nb
    n = jnp.abs(rel)
    large = max_exact + (jnp.log(jnp.maximum(n, 1).astype(F32) / max_exact)
                         / math.log(MAX_DISTANCE / max_exact) * (nb - max_exact)).astype(jnp.int32)
    large = jnp.minimum(large, nb - 1)
    return ret + jnp.where(n < max_exact, n, large)


def _bias_table(rel_bias, S):
    TQ, M = Q_TILE, MAX_DISTANCE
    reach = TQ + M - 1
    rel = jnp.concatenate([jnp.arange(-reach, reach + 1, dtype=jnp.int32),
                           jnp.array([-(S - 1), S - 1], jnp.int32)])
    f = rel_bias[_t5_bucket(rel)].astype(F32).T * LOG2E
    H = f.shape[0]
    lo, hi = f[:, -2], f[:, -1]
    L = 2 * reach + 1
    u = jnp.concatenate([f[:, :L], jnp.zeros((H, 1), F32)], axis=1)
    shifted = jnp.tile(u, (1, TQ))[:, :TQ * L].reshape(H, TQ, L)
    band = shifted[:, :, TQ - 1:TQ - 1 + TQ + 2 * M]
    side = S - TQ - M
    table = jnp.concatenate([jnp.broadcast_to(lo[:, None, None], (H, TQ, side)), band,
                             jnp.broadcast_to(hi[:, None, None], (H, TQ, side))], axis=2)
    return table, jnp.max(f, axis=1)


def kernel(x, c, w_ada, b_ada, norm_mix_g, w_in, ret_gn_g, diff_subln_g, lambda_q1, lambda_k1, lambda_q2, lambda_k2, w_ret_out, w_diff_out, w_o, rel_bias, norm_ffn_g, w_router, w_exp_gate, w_exp_up, w_exp_down, final_g):
    B, S, D = x.shape
    cap = CAPACITY_FACTOR * S // N_EXPERTS
    l = 0

    mod = _ada(c, w_ada[l], b_ada[l]).reshape(B, 6, D)
    proj = _inproj(x, norm_mix_g[l].reshape(1, D), mod, _permute_w_in(w_in[l]), _rot_tables(S))

    cd, dec, dmat = _ret_tables()
    y_ret = _ret(proj, cd, dec, dmat, ret_gn_g[l].reshape(1, -1))

    lam_vecs = jnp.stack([lambda_q1[l], lambda_k1[l], lambda_q2[l], lambda_k2[l]]).astype(F32)
    ttab, bmax = _bias_table(rel_bias, S)
    y_diff = _diff(proj, lam_vecs, ttab, bmax, diff_subln_g[l].reshape(1, -1))

    x1, h2, logits = _merge(x, y_ret, y_diff, proj, mod,
                            w_ret_out[l].astype(BF16), w_diff_out[l].astype(BF16), w_o[l].astype(BF16),
                            norm_ffn_g[l].reshape(1, D), w_router[l].T.astype(BF16))

    rank, gate = _route(logits, cap)
    xin = _gather_rows(h2.reshape(B * S, D // 2), rank, cap)
    ybuf = _moe(rank, gate, xin, w_exp_gate[l], w_exp_up[l], w_exp_down[l], cap)
    return _scat(jnp.swapaxes(rank, 1, 2), ybuf, x1, mod, final_g.reshape(1, D), cap)
```

```python
import functools
import math

import numpy as np
import jax
import jax.numpy as jnp
from jax import lax
from jax.experimental import pallas as pl
from jax.experimental.pallas import tpu as pltpu
from jax.experimental.pallas import tpu_sc as plsc

F32 = jnp.float32
BF16 = jnp.bfloat16

RET_HEADS = 4
RET_DK = 128
RET_DV = 128
RET_FWD_DECAY_OFFSET = 5.0
RET_BWD_DECAY_OFFSET = 5.5
RET_THETA_BASE = 10000.0
DIFF_HEADS = 4
DIFF_DH = 64
DIFF_DV = 2 * DIFF_DH
N_BUCKETS = 32
MAX_DISTANCE = 128
N_EXPERTS = 16
CAPACITY_FACTOR = 2
NORM_EPS = 1e-6
LAM_INIT = 0.8 - 0.6 * math.exp(-0.3 * 0)
LOG2E = math.log2(math.e)

LANES = 128
VMEM_LIMIT_BYTES = 56 * 2**20

SC_CORES = 2
SC_SUBCORES = 16
SC_LANES = 16
SC_GATHER_ROWS = 64

TOKEN_TILE = 512
PROJ_TILE = 1024
RET_CHUNK = 256
Q_TILE = 256
KEY_CHUNK = 512
L_MIN = 2.0 ** -60

COL_GR, COL_GD = 0, 8
COL_RQ, COL_RK, COL_RV, COL_RG = 16, 20, 24, 28
COL_DQ, COL_DK = 32, 36
PROJ_COLS = 40 * LANES


def _cparams(sem):
    return pltpu.CompilerParams(dimension_semantics=sem, vmem_limit_bytes=VMEM_LIMIT_BYTES)


def _resident(shape, index_map):
    return pl.BlockSpec(shape, index_map, pipeline_mode=pl.Buffered(1))


def _ada_kernel(c_ref, w_ref, b_ref, o_ref):
    c = c_ref[...]
    a = c * jax.nn.sigmoid(c)
    o_ref[...] = jnp.dot(a, w_ref[...], preferred_element_type=F32,
                         precision=lax.Precision.HIGHEST) + b_ref[...]


def _ada(c, w, b):
    B, D = c.shape
    n = w.shape[1] // D
    return pl.pallas_call(
        _ada_kernel,
        grid=(n,),
        in_specs=[pl.BlockSpec((B, D), lambda j: (0, 0)),
                  pl.BlockSpec((D, D), lambda j: (0, j)),
                  pl.BlockSpec((1, D), lambda j: (0, j))],
        out_specs=pl.BlockSpec((B, D), lambda j: (0, j)),
        out_shape=jax.ShapeDtypeStruct((B, n * D), F32),
        compiler_params=_cparams(("arbitrary",)),
        name="ada",
    )(c, w, b.reshape(1, -1))


def _rms(x, g):
    ms = jnp.mean(x * x, axis=-1, keepdims=True)
    return x * lax.rsqrt(ms + NORM_EPS) * g


def _rot_half(x, cos, sin):
    return x * cos + pltpu.roll(x, LANES // 2, axis=1) * sin


def _inproj_kernel(x_ref, g_ref, mod_ref, w_ref, wvt_ref, rot_ref, o_ref, vt_ref):
    x = x_ref[0]
    h = _rms(x, g_ref[...]) * (1.0 + mod_ref[0, 1:2, :]) + mod_ref[0, 0:1, :]
    hb = h.astype(BF16)

    def mm(col, width):
        return jnp.dot(hb, w_ref[:, col * LANES:(col + width) * LANES], preferred_element_type=F32)

    def put(col, val):
        o_ref[0, :, col * LANES:col * LANES + val.shape[1]] = val.astype(BF16)

    for col in range(COL_GR, COL_RQ, 4):
        put(col, jax.nn.sigmoid(mm(col, 4)))
    for base, t in ((COL_RQ, 0), (COL_RK, 2)):
        cos, sin = rot_ref[t], rot_ref[t + 1]
        r = mm(base, RET_HEADS)
        for hh in range(RET_HEADS):
            put(base + hh, _rot_half(r[:, hh * LANES:(hh + 1) * LANES], cos, sin))
    put(COL_RV, mm(COL_RV, 4))
    r = mm(COL_RG, 4)
    put(COL_RG, r * jax.nn.sigmoid(r))
    put(COL_DQ, mm(COL_DQ, 4) * (DIFF_DH ** -0.5 * LOG2E))
    put(COL_DK, mm(COL_DK, 4))
    vt_ref[0] = lax.dot_general(wvt_ref[...], hb, (((1,), (1,)), ((), ())),
                                preferred_element_type=F32).astype(BF16)


def _inproj(x, g, mod, w, wvt, rot):
    B, S, D = x.shape
    tm = PROJ_TILE
    return pl.pallas_call(
        _inproj_kernel,
        grid=(B, S // tm),
        in_specs=[pl.BlockSpec((1, tm, D), lambda b, i: (b, i, 0)),
                  _resident((1, D), lambda b, i: (0, 0)),
                  pl.BlockSpec((1, 6, D), lambda b, i: (b, 0, 0)),
                  _resident((D, PROJ_COLS), lambda b, i: (0, 0)),
                  _resident(wvt.shape, lambda b, i: (0, 0)),
                  pl.BlockSpec((4, tm, LANES), lambda b, i: (0, i, 0))],
        out_specs=[pl.BlockSpec((1, tm, PROJ_COLS), lambda b, i: (b, i, 0)),
                   pl.BlockSpec((1, wvt.shape[0], tm), lambda b, i: (b, 0, i))],
        out_shape=[jax.ShapeDtypeStruct((B, S, PROJ_COLS), BF16),
                   jax.ShapeDtypeStruct((B, wvt.shape[0], S), BF16)],
        compiler_params=_cparams(("arbitrary", "arbitrary")),
        name="inproj",
    )(x, g, mod, w, wvt, rot)


def _ret_kernel(cd_ref, q_ref, k_ref, v_ref, rg_ref, dec_ref, dm_ref, gn_ref, o_ref, acc_ref):
    hh = pl.program_id(1)
    S = q_ref.shape[1]
    C = RET_CHUNK
    nc = S // C
    dmat = dm_ref[0]
    qdf, kdf, qdb, kdb = dec_ref[0, 0], dec_ref[0, 1], dec_ref[0, 2], dec_ref[0, 3]
    nt = (((1,), (1,)), ((), ()))
    tn = (((0,), (0,)), ((), ()))

    def chunk(n):
        sl = pl.ds(n * C, C)
        return sl, q_ref[0, sl, :], k_ref[0, sl, :], v_ref[0, sl, :]

    def scaled(t, dec):
        return (t.astype(F32) * dec).astype(BF16)

    state = jnp.zeros((RET_DK, RET_DV), F32)
    for n in range(nc):
        sl, q, k, v = chunk(n)
        s = lax.dot_general(q, k, nt, preferred_element_type=F32) * dmat
        inner = jnp.dot(s.astype(BF16), v, preferred_element_type=F32)
        cross = jnp.dot(scaled(q, qdf), state.astype(BF16), preferred_element_type=F32)
        acc_ref[sl, :] = inner + cross
        kv = lax.dot_general(scaled(k, kdf), v, tn, preferred_element_type=F32)
        state = cd_ref[hh, 0] * state + kv
    state = jnp.zeros((RET_DK, RET_DV), F32)
    for n in reversed(range(nc)):
        sl, q, k, v = chunk(n)
        cross = jnp.dot(scaled(q, qdb), state.astype(BF16), preferred_element_type=F32)
        acc_ref[sl, :] = acc_ref[sl, :] + cross
        kv = lax.dot_general(scaled(k, kdb), v, tn, preferred_element_type=F32)
        state = cd_ref[hh, 1] * state + kv

    y = acc_ref[...]
    mu = jnp.mean(y, axis=-1, keepdims=True)
    yc = y - mu
    var = jnp.mean(yc * yc, axis=-1, keepdims=True)
    yn = yc * lax.rsqrt(var + NORM_EPS) * gn_ref[...]
    o_ref[0] = (rg_ref[0].astype(F32) * yn).astype(BF16)


def _ret(proj, cd, dec, dmat, gn):
    B, S, _ = proj.shape
    H = RET_HEADS
    C = RET_CHUNK

    def col(base):
        return pl.BlockSpec((1, S, LANES), lambda b, h: (b, 0, base + h))

    return pl.pallas_call(
        _ret_kernel,
        grid=(B, H),
        in_specs=[pl.BlockSpec(memory_space=pltpu.SMEM),
                  col(COL_RQ), col(COL_RK), col(COL_RV), col(COL_RG),
                  pl.BlockSpec((1, 4, C, LANES), lambda b, h: (h, 0, 0, 0)),
                  pl.BlockSpec((1, C, C), lambda b, h: (h, 0, 0)),
                  pl.BlockSpec((1, LANES), lambda b, h: (0, h))],
        out_specs=pl.BlockSpec((1, S, LANES), lambda b, h: (b, 0, h)),
        out_shape=jax.ShapeDtypeStruct((B, S, H * RET_DV), BF16),
        scratch_shapes=[pltpu.VMEM((S, RET_DV), F32)],
        compiler_params=_cparams(("arbitrary", "arbitrary")),
        name="ret",
    )(cd, proj, proj, proj, proj, dec, dmat, gn)


def _diff_kernel(bmax_ref, lam_ref, q_ref, k_ref, vt_ref, t_ref, g_ref, o_ref, p_ref):
    hh = pl.program_id(0)
    S = k_ref.shape[1]
    TQ, KB = Q_TILE, KEY_CHUNK
    nk = S // KB
    lv = lam_ref[...]
    lam = (jnp.exp(jnp.sum(lv[0:1] * lv[1:2], axis=1, keepdims=True))
           - jnp.exp(jnp.sum(lv[2:3] * lv[3:4], axis=1, keepdims=True)) + LAM_INIT)
    first_half = lax.broadcasted_iota(jnp.int32, (TQ, LANES), 1) < DIFF_DH
    nt = (((1,), (1,)), ((), ()))
    bmax = bmax_ref[hh]
    kf = k_ref[0].astype(F32)
    kmax = jnp.sqrt(jnp.max(jnp.sum(kf * kf, axis=1, keepdims=True), axis=0, keepdims=True))
    sel_r = lax.broadcasted_iota(jnp.int32, (8, LANES), 0)
    sel_c = lax.broadcasted_iota(jnp.int32, (8, LANES), 1)
    sel = jnp.where((sel_c < DIFF_DH) == (sel_r == 0), 1.0, 0.0)
    sel = jnp.where(sel_r < 2, sel, 0.0).astype(BF16)

    def masked_q(qb):
        q = q_ref[0, pl.ds(pl.multiple_of(qb * TQ, TQ), TQ), :]
        zero = jnp.zeros_like(q)
        return q, (jnp.where(first_half, q, zero), jnp.where(first_half, zero, q))

    def finish(ot, qb):
        ms = jnp.mean(ot * ot, axis=0, keepdims=True)
        y = ot * lax.rsqrt(ms + NORM_EPS) * g_ref[...] * (1.0 - LAM_INIT)
        o_ref[0, pl.ds(pl.multiple_of(qb * TQ, TQ), TQ), :] = y.T.astype(BF16)

    def exponentials(qb, slot):
        w0 = pl.multiple_of((S - TQ) - qb * TQ, TQ)
        q, qz = masked_q(qb)
        qsq = (q.astype(F32) * q.astype(F32)).astype(BF16)
        n2 = lax.dot_general(sel, qsq, nt, preferred_element_type=F32) * 1.01
        m = [jnp.sqrt(n2[i:i + 1]) * kmax + bmax for i in range(2)]
        l = [jnp.zeros((8, TQ), F32) for _ in range(2)]
        for c in range(nk):
            rows = slice(c * KB, (c + 1) * KB)
            kb = k_ref[0, rows, :]
            bias = t_ref[0, pl.ds(w0 + c * KB, KB), :]
            for i in range(2):
                s = lax.dot_general(kb, qz[i], nt, preferred_element_type=F32)
                p = jnp.exp2((s - m[i]) + bias)
                l[i] = l[i] + jnp.sum(p.reshape(KB // 8, 8, TQ), axis=0)
                p_ref[slot, i, rows, :] = p.astype(BF16)
        return [jnp.sum(l[i], axis=0, keepdims=True) for i in range(2)]

    def values(qb, slot, lr):
        ratio = jnp.broadcast_to(lam * lr[0] / lr[1], (16, TQ)).astype(BF16)
        ot = jnp.zeros((DIFF_DV, TQ), F32)
        for c in range(nk):
            rows = slice(c * KB, (c + 1) * KB)
            p1 = p_ref[slot, 0, rows, :].reshape(KB // 16, 16, TQ)
            p2 = p_ref[slot, 1, rows, :].reshape(KB // 16, 16, TQ)
            a = (p1 - ratio * p2).reshape(KB, TQ)
            ot = ot + jnp.dot(vt_ref[0, :, rows], a, preferred_element_type=F32)
        finish(ot * (1.0 / lr[0]), qb)

    def fast_pair(j, lmin):
        la = exponentials(2 * j, 0)
        lb = exponentials(2 * j + 1, 1)
        values(2 * j, 0, la)
        values(2 * j + 1, 1, lb)
        return jnp.minimum(lmin, jnp.minimum(jnp.minimum(la[0], la[1]), jnp.minimum(lb[0], lb[1])))

    lmin = lax.fori_loop(0, S // (2 * TQ), fast_pair, jnp.full((1, TQ), jnp.inf, F32))

    @pl.when(jnp.min(lmin) < L_MIN)
    def _():
        def exact_tile(qb, carry):
            w0 = pl.multiple_of((S - TQ) - qb * TQ, TQ)
            _, qz = masked_q(qb)
            bias = t_ref[0, pl.ds(w0, S), :]
            attn = None
            for i, scale in ((0, 1.0), (1, lam)):
                s = lax.dot_general(k_ref[0], qz[i], nt, preferred_element_type=F32) + bias
                p = jnp.exp2(s - jnp.max(s, axis=0, keepdims=True))
                part = p * (scale / jnp.sum(p, axis=0, keepdims=True))
                attn = part if attn is None else attn - part
            finish(jnp.dot(vt_ref[0], attn.astype(BF16), preferred_element_type=F32), qb)
            return carry

        lax.fori_loop(0, S // TQ, exact_tile, 0)


def _diff(proj, vt, lam_vecs, ttab, bmax, g):
    B, S, _ = proj.shape
    H = DIFF_HEADS

    def col(base):
        return pl.BlockSpec((1, S, LANES), lambda h, b: (b, 0, base + h))

    return pl.pallas_call(
        _diff_kernel,
        grid=(H, B),
        in_specs=[pl.BlockSpec(memory_space=pltpu.SMEM),
                  pl.BlockSpec((4, DIFF_DH), lambda h, b: (0, 0)),
                  col(COL_DQ), col(COL_DK),
                  pl.BlockSpec((1, DIFF_DV, S), lambda h, b: (b, h, 0)),
                  pl.BlockSpec((1, 2 * S - Q_TILE, Q_TILE), lambda h, b: (h, 0, 0)),
                  pl.BlockSpec((DIFF_DV, Q_TILE), lambda h, b: (0, 0))],
        out_specs=pl.BlockSpec((1, S, LANES), lambda h, b: (b, 0, h)),
        out_shape=jax.ShapeDtypeStruct((B, S, H * DIFF_DV), BF16),
        scratch_shapes=[pltpu.VMEM((2, 2, S, Q_TILE), BF16)],
        compiler_params=_cparams(("arbitrary", "arbitrary")),
        name="diff",
    )(bmax, lam_vecs, proj, proj, vt, ttab, g)


def _pack_halves(x):
    w = x.shape[1] // 2
    lo = pltpu.bitcast(x[:, :w].astype(BF16).astype(F32), jnp.int32)
    hi = pltpu.bitcast(x[:, w:].astype(BF16).astype(F32), jnp.int32)
    return lax.shift_right_logical(lo, jnp.full_like(lo, 16)) | (hi & jnp.int32(-65536))


def _unpack_halves(words):
    lo = pltpu.bitcast(lax.shift_left(words, jnp.full_like(words, 16)), F32)
    hi = pltpu.bitcast(words & jnp.int32(-65536), F32)
    return jnp.concatenate([lo, hi], axis=1).astype(BF16)


def _merge_kernel(x_ref, yr_ref, yd_ref, gr_ref, gd_ref, mod_ref, wr_ref, wd_ref, wo_ref,
                  g_ref, wrt_ref, x1_ref, h2_ref, lg_ref):
    a = jnp.dot(yr_ref[0], wr_ref[...], preferred_element_type=F32)
    d = jnp.dot(yd_ref[0], wd_ref[...], preferred_element_type=F32)
    merged = gr_ref[0].astype(F32) * a + gd_ref[0].astype(F32) * d
    o = jnp.dot(merged.astype(BF16), wo_ref[...], preferred_element_type=F32)
    x1 = x_ref[0] + mod_ref[0, 2:3, :] * o
    x1_ref[0] = x1
    h2 = _rms(x1, g_ref[...]) * (1.0 + mod_ref[0, 4:5, :]) + mod_ref[0, 3:4, :]
    h2_ref[0] = _pack_halves(h2)
    lg_ref[0] = lax.dot_general(wrt_ref[...], h2.astype(BF16), (((1,), (1,)), ((), ())),
                                preferred_element_type=F32)


def _merge(x, yr, yd, proj, mod, wr, wd, wo, g, wrt):
    B, S, D = x.shape
    tm = PROJ_TILE
    E = wrt.shape[0]
    gate_w = COL_GD - COL_GR
    return pl.pallas_call(
        _merge_kernel,
        grid=(B, S // tm),
        in_specs=[pl.BlockSpec((1, tm, D), lambda b, i: (b, i, 0)),
                  pl.BlockSpec((1, tm, yr.shape[2]), lambda b, i: (b, i, 0)),
                  pl.BlockSpec((1, tm, yd.shape[2]), lambda b, i: (b, i, 0)),
                  pl.BlockSpec((1, tm, gate_w * LANES), lambda b, i: (b, i, COL_GR // gate_w)),
                  pl.BlockSpec((1, tm, gate_w * LANES), lambda b, i: (b, i, COL_GD // gate_w)),
                  pl.BlockSpec((1, 6, D), lambda b, i: (b, 0, 0)),
                  _resident(wr.shape, lambda b, i: (0, 0)),
                  _resident(wd.shape, lambda b, i: (0, 0)),
                  _resident(wo.shape, lambda b, i: (0, 0)),
                  _resident((1, D), lambda b, i: (0, 0)),
                  _resident(wrt.shape, lambda b, i: (0, 0))],
        out_specs=[pl.BlockSpec((1, tm, D), lambda b, i: (b, i, 0)),
                   pl.BlockSpec((1, tm, D // 2), lambda b, i: (b, i, 0)),
                   pl.BlockSpec((1, E, tm), lambda b, i: (b, 0, i))],
        out_shape=[jax.ShapeDtypeStruct((B, S, D), F32),
                   jax.ShapeDtypeStruct((B, S, D // 2), jnp.int32),
                   jax.ShapeDtypeStruct((B, E, S), F32)],
        compiler_params=_cparams(("arbitrary", "arbitrary")),
        name="merge",
    )(x, yr, yd, proj, proj, mod, wr, wd, wo, g, wrt)


def _lane_prefix(m, tri):
    E, S = m.shape
    off = jnp.zeros((E, 1), F32)
    parts = []
    for j in range(S // LANES):
        blk = m[:, j * LANES:(j + 1) * LANES]
        parts.append(jnp.dot(blk.astype(BF16), tri, preferred_element_type=F32) + off)
        off = off + jnp.sum(blk, axis=1, keepdims=True)
    return jnp.concatenate(parts, axis=1)


def _route_kernel(lg_ref, rank_ref, gate_ref, *, cap):
    lg = lg_ref[0]
    e = jnp.exp(lg - jnp.max(lg, axis=0, keepdims=True))
    aff = e / jnp.sum(e, axis=0, keepdims=True)
    bits = pltpu.bitcast(aff, jnp.int32)
    E = lg.shape[0]

    def count(mask):
        return jnp.sum(jnp.where(mask, 1.0, 0.0), axis=1, keepdims=True)

    thr = jnp.zeros((E, 1), jnp.int32)
    for bit in range(30, -1, -1):
        cand = thr | (1 << bit)
        thr = jnp.where(count(bits >= cand) >= cap, cand, thr)
    gt = bits > thr
    eq = bits == thr
    need = cap - count(gt)
    r = lax.broadcasted_iota(jnp.int32, (LANES, LANES), 0)
    c = lax.broadcasted_iota(jnp.int32, (LANES, LANES), 1)
    tri = jnp.where(r < c, 1.0, 0.0).astype(BF16)
    eq_before = _lane_prefix(jnp.where(eq, 1.0, 0.0), tri)
    sel = gt | (eq & (eq_before < need))
    slot = _lane_prefix(jnp.where(sel, 1.0, 0.0), tri)
    rank_ref[0] = jnp.where(sel, slot, -1.0).astype(jnp.int32)
    gate_ref[0] = jnp.where(sel, aff, 0.0)


def _route(logits, cap):
    B, E, S = logits.shape
    spec = pl.BlockSpec((1, E, S), lambda b: (b, 0, 0))
    return pl.pallas_call(
        functools.partial(_route_kernel, cap=cap),
        grid=(B,),
        in_specs=[spec],
        out_specs=[spec, spec],
        out_shape=[jax.ShapeDtypeStruct((B, E, S), jnp.int32),
                   jax.ShapeDtypeStruct((B, E, S), F32)],
        compiler_params=_cparams(("arbitrary",)),
        name="route",
    )(logits)


def _gather_rows(table, rank, cap):
    B, E, S = rank.shape
    W = table.shape[1]
    workers = SC_CORES * SC_SUBCORES
    pairs = E * B
    per = pairs // workers
    assert per * workers == pairs and B & (B - 1) == 0 and cap % SC_GATHER_ROWS == 0 and S % SC_LANES == 0
    shift = B.bit_length() - 1
    mesh = plsc.VectorSubcoreMesh(core_axis_name="c", subcore_axis_name="s")

    R = SC_GATHER_ROWS
    n_chunks = cap // R

    def body(table_hbm, rank_hbm, out_hbm, rank_v, idx_v, buf_v, gsem, wsem):
        wid = lax.axis_index("s") * SC_CORES + lax.axis_index("c")

        def fetch(c):
            return pltpu.make_async_copy(table_hbm.at[idx_v.at[pl.ds(c * R, R)]], buf_v.at[c % 2], gsem.at[c % 2])

        def flush(p, c):
            return pltpu.make_async_copy(buf_v.at[c % 2], out_hbm.at[pl.ds(p * cap + c * R, R)], wsem.at[c % 2])

        for j in range(per):
            p = wid * per + j
            e = lax.shift_right_logical(p, shift)
            b = p & (B - 1)
            pltpu.sync_copy(rank_hbm.at[b, e], rank_v)
            base = b * S

            @pl.loop(0, S, step=SC_LANES)
            def _(t0):
                r = rank_v[pl.ds(t0, SC_LANES)]
                tok = lax.iota(jnp.int32, SC_LANES) + (base + t0)
                plsc.store_scatter(idx_v, [r], tok, mask=r >= 0)

            fetch(0).start()
            for c in range(n_chunks):
                fetch(c).wait()
                if c >= 1:
                    flush(p, c - 1).wait()
                if c + 1 < n_chunks:
                    fetch(c + 1).start()
                flush(p, c).start()
            flush(p, n_chunks - 1).wait()

    return pl.kernel(
        body,
        out_type=jax.ShapeDtypeStruct((pairs * cap, W), jnp.int32),
        mesh=mesh,
        scratch_types=[pltpu.VMEM((S,), jnp.int32), pltpu.VMEM((cap,), jnp.int32),
                       pltpu.VMEM((2, R, W), jnp.int32),
                       pltpu.SemaphoreType.DMA((2,)), pltpu.SemaphoreType.DMA((2,))],
        compiler_params=pltpu.CompilerParams(needs_layout_passes=False),
    )(table, rank)


def _moe_kernel(rank_ref, gate_ref, x_ref, wg_ref, wu_ref, wd_ref, o_ref, wg_s, wu_s, wd_s, *, cap):
    r, b = pl.program_id(0), pl.program_id(1)
    n_exp = pl.num_programs(0) - 1

    @pl.when(r < n_exp)
    def _():
        slot = r % 2
        rows_in, rows_ff = wg_ref.shape[1], wd_ref.shape[1]
        wg_s[slot, pl.ds(pl.multiple_of(b * rows_in, rows_in), rows_in), :] = wg_ref[0].astype(BF16)
        wu_s[slot, pl.ds(pl.multiple_of(b * rows_in, rows_in), rows_in), :] = wu_ref[0].astype(BF16)
        wd_s[slot, pl.ds(pl.multiple_of(b * rows_ff, rows_ff), rows_ff), :] = wd_ref[0].astype(BF16)

    @pl.when(r == 0)
    def _():
        o_ref[...] = jnp.zeros_like(o_ref)

    @pl.when(r > 0)
    def _():
        slot = (r - 1) % 2
        rank = rank_ref[0, 0]
        S = rank.shape[1]
        pick = lax.broadcasted_iota(jnp.int32, (cap, S), 0) == rank
        xin = _unpack_halves(x_ref[0])
        a = jnp.dot(xin, wg_s[slot], preferred_element_type=F32)
        u = jnp.dot(xin, wu_s[slot], preferred_element_type=F32)
        act = (a * jax.nn.sigmoid(a) * u).astype(BF16)
        y = jnp.dot(act, wd_s[slot], preferred_element_type=F32)
        g = jnp.sum(jnp.where(pick, gate_ref[0, 0], 0.0), axis=1, keepdims=True)
        o_ref[0, 0] = (y * g).astype(BF16)


def _moe(rank, gate, xin, wg, wu, wd, cap):
    B, E, S = rank.shape
    D, Fd = wg.shape[1], wg.shape[2]
    assert D % B == 0 and Fd % B == 0

    def cur(r):
        return jnp.maximum(r - 1, 0)

    def nxt(r):
        return jnp.minimum(r, E - 1)

    row = pl.BlockSpec((1, 1, 1, S), lambda r, b: (b, cur(r), 0, 0))
    return pl.pallas_call(
        functools.partial(_moe_kernel, cap=cap),
        grid=(E + 1, B),
        in_specs=[row, row,
                  pl.BlockSpec((1, cap, D // 2), lambda r, b: (cur(r) * B + b, 0, 0)),
                  pl.BlockSpec((1, D // B, Fd), lambda r, b: (nxt(r), b, 0)),
                  pl.BlockSpec((1, D // B, Fd), lambda r, b: (nxt(r), b, 0)),
                  pl.BlockSpec((1, Fd // B, D), lambda r, b: (nxt(r), b, 0))],
        out_specs=pl.BlockSpec((1, 1, cap, D), lambda r, b: (jnp.where(r == 0, E, r - 1), b, 0, 0)),
        out_shape=jax.ShapeDtypeStruct((E + 1, B, cap, D), BF16),
        scratch_shapes=[pltpu.VMEM((2, D, Fd), BF16), pltpu.VMEM((2, D, Fd), BF16),
                        pltpu.VMEM((2, Fd, D), BF16)],
        compiler_params=_cparams(("arbitrary", "arbitrary")),
        name="moe",
    )(rank.reshape(B, E, 1, S), gate.reshape(B, E, 1, S), xin.reshape(E * B, cap, D // 2), wg, wu, wd)


def _scat_kernel(rt_ref, y_ref, x1_ref, mod_ref, g_ref, o_ref, *, cap):
    rt = rt_ref[0]
    tm, E = rt.shape
    slot = lax.broadcasted_iota(jnp.int32, (tm, cap), 1)
    place = jnp.concatenate(
        [jnp.where(rt[:, e:e + 1] == slot, 1.0, 0.0).astype(BF16) for e in range(E)], axis=1)
    y = y_ref[:, 0].reshape(E * cap, y_ref.shape[3])
    moe = jnp.dot(place, y, preferred_element_type=F32)
    x2 = x1_ref[0] + mod_ref[0, 5:6, :] * moe
    o_ref[0] = _rms(x2, g_ref[...])


def _scat(rank_t, ybuf, x1, mod, g, cap):
    B, S, D = x1.shape
    E = rank_t.shape[2]
    tm = TOKEN_TILE
    return pl.pallas_call(
        functools.partial(_scat_kernel, cap=cap),
        grid=(B, S // tm),
        in_specs=[pl.BlockSpec((1, tm, E), lambda b, i: (b, i, 0)),
                  pl.BlockSpec((E, 1, cap, D), lambda b, i: (0, b, 0, 0)),
                  pl.BlockSpec((1, tm, D), lambda b, i: (b, i, 0)),
                  pl.BlockSpec((1, 6, D), lambda b, i: (b, 0, 0)),
                  _resident((1, D), lambda b, i: (0, 0))],
        out_specs=pl.BlockSpec((1, tm, D), lambda b, i: (b, i, 0)),
        out_shape=jax.ShapeDtypeStruct((B, S, D), F32),
        compiler_params=_cparams(("arbitrary", "arbitrary")),
        name="scat",
    )(rank_t, ybuf, x1, mod, g)


def _permute_w_in(w):
    D = w.shape[0]
    w = w.astype(BF16)
    sizes = (512, 512, 512, 512, 256, 256, 256, 256, 512, 1024, 1024)
    rq, rk, rv, rg, dq1, dq2, dk1, dk2, dv, gr, gd = jnp.split(w, np.cumsum(sizes)[:-1].tolist(), axis=1)

    def halves(t):
        return t.reshape(D, RET_HEADS, RET_DK // 2, 2).transpose(0, 1, 3, 2).reshape(D, -1)

    def pair(a, b):
        return jnp.stack([a.reshape(D, DIFF_HEADS, DIFF_DH), b.reshape(D, DIFF_HEADS, DIFF_DH)],
                         axis=2).reshape(D, -1)

    main = jnp.concatenate([gr, gd, halves(rq), halves(rk), rv, rg, pair(dq1, dq2), pair(dk1, dk2)], axis=1)
    return main, dv.T


def _rot_tables(S):
    half = RET_DK // 2
    inv = 1.0 / (RET_THETA_BASE ** np.linspace(0.0, 1.0, half))
    ang = np.arange(S, dtype=np.float64)[:, None] * inv[None, :]
    cos = np.concatenate([np.cos(ang), np.cos(ang)], axis=1)
    sin = np.concatenate([-np.sin(ang), np.sin(ang)], axis=1)
    sc = RET_DK ** -0.5
    return jnp.asarray(np.stack([cos, sin, cos * sc, sin * sc]), F32)


def _ret_tables():
    C = RET_CHUNK
    heads = np.arange(RET_HEADS, dtype=np.float64)
    lgf = np.log1p(-np.exp2(-RET_FWD_DECAY_OFFSET - heads))[:, None]
    lgb = np.log1p(-np.exp2(-RET_BWD_DECAY_OFFSET - heads))[:, None]
    idx = np.arange(C, dtype=np.float64)
    diff = idx[:, None] - idx[None, :]
    dmat = np.where(diff >= 0,
                    np.exp(np.maximum(diff, 0.0)[None] * lgf[:, :, None]),
                    np.exp(np.maximum(-diff, 0.0)[None] * lgb[:, :, None]))
    dec = np.stack([np.exp((idx + 1)[None, :] * lgf),
                    np.exp((C - 1 - idx)[None, :] * lgf),
                    np.exp((C - idx)[None, :] * lgb),
                    np.exp(idx[None, :] * lgb)], axis=1)
    dec = np.broadcast_to(dec[..., None], dec.shape + (LANES,))
    cd = np.concatenate([np.exp(C * lgf), np.exp(C * lgb)], axis=1)
    return jnp.asarray(cd, F32), jnp.asarray(dec, F32), jnp.asarray(dmat, F32)


def _t5_bucket(rel):
    nb = N_BUCKETS // 2
    max_exact = nb // 2
    ret = (rel > 0).astype(jnp.int32) * nb
    n = jnp.abs(rel)
    large = max_exact + (jnp.log(jnp.maximum(n, 1).astype(F32) / max_exact)
                         / math.log(MAX_DISTANCE / max_exact) * (nb - max_exact)).astype(jnp.int32)
    large = jnp.minimum(large, nb - 1)
    return ret + jnp.where(n < max_exact, n, large)


def _bias_table(rel_bias, S):
    TQ, M = Q_TILE, MAX_DISTANCE
    reach = TQ + M - 1
    rel = jnp.concatenate([jnp.arange(-reach, reach + 1, dtype=jnp.int32),
                           jnp.array([-(S - 1), S - 1], jnp.int32)])
    f = rel_bias[_t5_bucket(rel)].astype(F32).T * LOG2E
    H = f.shape[0]
    lo, hi = f[:, -2], f[:, -1]
    L = 2 * reach + 1
    u = jnp.concatenate([f[:, :L], jnp.zeros((H, 1), F32)], axis=1)
    shifted = jnp.tile(u, (1, TQ))[:, :TQ * L].reshape(H, TQ, L)
    band = jnp.swapaxes(shifted[:, :, TQ - 1:TQ - 1 + TQ + 2 * M], 1, 2)
    side = S - TQ - M
    table = jnp.concatenate([jnp.broadcast_to(lo[:, None, None], (H, side, TQ)), band,
                             jnp.broadcast_to(hi[:, None, None], (H, side, TQ))], axis=1)
    return table, jnp.max(f, axis=1)


def kernel(x, c, w_ada, b_ada, norm_mix_g, w_in, ret_gn_g, diff_subln_g, lambda_q1, lambda_k1, lambda_q2, lambda_k2, w_ret_out, w_diff_out, w_o, rel_bias, norm_ffn_g, w_router, w_exp_gate, w_exp_up, w_exp_down, final_g):
    B, S, D = x.shape
    cap = CAPACITY_FACTOR * S // N_EXPERTS
    l = 0

    mod = _ada(c, w_ada[l], b_ada[l]).reshape(B, 6, D)
    w_main, w_vt = _permute_w_in(w_in[l])
    proj, vt = _inproj(x, norm_mix_g[l].reshape(1, D), mod, w_main, w_vt, _rot_tables(S))

    cd, dec, dmat = _ret_tables()
    y_ret = _ret(proj, cd, dec, dmat, ret_gn_g[l].reshape(1, -1))

    lam_vecs = jnp.stack([lambda_q1[l], lambda_k1[l], lambda_q2[l], lambda_k2[l]]).astype(F32)
    ttab, bmax = _bias_table(rel_bias, S)
    subln = jnp.broadcast_to(diff_subln_g[l].astype(F32)[:, None], (DIFF_DV, Q_TILE))
    y_diff = _diff(proj, vt, lam_vecs, ttab, bmax, subln)

    x1, h2, logits = _merge(x, y_ret, y_diff, proj, mod,
                            w_ret_out[l].astype(BF16), w_diff_out[l].astype(BF16), w_o[l].astype(BF16),
                            norm_ffn_g[l].reshape(1, D), w_router[l].T.astype(BF16))

    rank, gate = _route(logits, cap)
    xin = _gather_rows(h2.reshape(B * S, D // 2), rank, cap)
    ybuf = _moe(rank, gate, xin, w_exp_gate[l], w_exp_up[l], w_exp_down[l], cap)
    return _scat(jnp.swapaxes(rank, 1, 2), ybuf, x1, mod, final_g.reshape(1, D), cap)
```

```python
import functools
import math

import numpy as np
import jax
import jax.numpy as jnp
from jax import lax
from jax.experimental import pallas as pl
from jax.experimental.pallas import tpu as pltpu
from jax.experimental.pallas import tpu_sc as plsc

F32 = jnp.float32
BF16 = jnp.bfloat16

RET_HEADS = 4
RET_DK = 128
RET_DV = 128
RET_FWD_DECAY_OFFSET = 5.0
RET_BWD_DECAY_OFFSET = 5.5
RET_THETA_BASE = 10000.0
DIFF_HEADS = 4
DIFF_DH = 64
DIFF_DV = 2 * DIFF_DH
N_BUCKETS = 32
MAX_DISTANCE = 128
N_EXPERTS = 16
CAPACITY_FACTOR = 2
NORM_EPS = 1e-6
LAM_INIT = 0.8 - 0.6 * math.exp(-0.3 * 0)
LOG2E = math.log2(math.e)

LANES = 128
VMEM_LIMIT_BYTES = 56 * 2**20

SC_CORES = 2
SC_SUBCORES = 16
SC_LANES = 16
SC_GATHER_ROWS = 64

TOKEN_TILE = 512
PROJ_TILE = 1024
RET_CHUNK = 256
Q_TILE = 256
KEY_CHUNK = 512
L_MIN = 2.0 ** -60

W_RQ, W_RK, W_RV, W_RG = 0, 4, 8, 12
W_DQ1, W_DQ2, W_DK1, W_DK2, W_DV = 16, 18, 20, 22, 24
W_GR, W_GD = 28, 36

COL_GR, COL_GD = 0, 8
COL_RQ, COL_RK, COL_RV, COL_RG = 16, 20, 24, 28
COL_DQ, COL_DK = 32, 36
PROJ_COLS = 40 * LANES


def _cparams(sem):
    return pltpu.CompilerParams(dimension_semantics=sem, vmem_limit_bytes=VMEM_LIMIT_BYTES)


def _resident(shape, index_map):
    return pl.BlockSpec(shape, index_map, pipeline_mode=pl.Buffered(1))


def _ada_kernel(c_ref, w_ref, b_ref, o_ref):
    c = c_ref[...]
    a = c * jax.nn.sigmoid(c)
    o_ref[...] = jnp.dot(a, w_ref[...], preferred_element_type=F32,
                         precision=lax.Precision.HIGHEST) + b_ref[...]


def _ada(c, w, b):
    B, D = c.shape
    n = w.shape[1] // D
    return pl.pallas_call(
        _ada_kernel,
        grid=(n,),
        in_specs=[pl.BlockSpec((B, D), lambda j: (0, 0)),
                  pl.BlockSpec((D, D), lambda j: (0, j)),
                  pl.BlockSpec((1, D), lambda j: (0, j))],
        out_specs=pl.BlockSpec((B, D), lambda j: (0, j)),
        out_shape=jax.ShapeDtypeStruct((B, n * D), F32),
        compiler_params=_cparams(("arbitrary",)),
        name="ada",
    )(c, w, b.reshape(1, -1))


def _rms(x, g):
    ms = jnp.mean(x * x, axis=-1, keepdims=True)
    return x * lax.rsqrt(ms + NORM_EPS) * g


def _rot_pairs(x, cos, sin):
    even = (lax.broadcasted_iota(jnp.int32, x.shape, 1) & 1) == 0
    partner = jnp.where(even, pltpu.roll(x, LANES - 1, axis=1), pltpu.roll(x, 1, axis=1))
    return x * cos + partner * sin


def _pair_maps(a, b, second):
    low = lax.broadcasted_iota(jnp.int32, a.shape, 1) < DIFF_DH
    if second:
        return jnp.where(low, pltpu.roll(a, DIFF_DH, axis=1), b)
    return jnp.where(low, a, pltpu.roll(b, DIFF_DH, axis=1))


def _inproj_kernel(x_ref, g_ref, mod_ref, w_ref, rot_ref, o_ref, vt_ref, wvt_s):
    @pl.when((pl.program_id(0) == 0) & (pl.program_id(1) == 0))
    def _():
        wvt_s[...] = w_ref[:, W_DV * LANES:(W_DV + 4) * LANES].astype(F32).T.astype(BF16)

    x = x_ref[0]
    h = _rms(x, g_ref[...]) * (1.0 + mod_ref[0, 1:2, :]) + mod_ref[0, 0:1, :]
    hb = h.astype(BF16)

    def mm(col, width):
        return jnp.dot(hb, w_ref[:, col * LANES:(col + width) * LANES], preferred_element_type=F32)

    def put(col, val):
        o_ref[0, :, col * LANES:col * LANES + val.shape[1]] = val.astype(BF16)

    for j in range(0, COL_GD - COL_GR, 4):
        put(COL_GR + j, jax.nn.sigmoid(mm(W_GR + j, 4)))
        put(COL_GD + j, jax.nn.sigmoid(mm(W_GD + j, 4)))
    for src, dst, t in ((W_RQ, COL_RQ, 0), (W_RK, COL_RK, 2)):
        cos, sin = rot_ref[t], rot_ref[t + 1]
        r = mm(src, RET_HEADS)
        for hh in range(RET_HEADS):
            put(dst + hh, _rot_pairs(r[:, hh * LANES:(hh + 1) * LANES], cos, sin))
    put(COL_RV, mm(W_RV, 4))
    r = mm(W_RG, 4)
    put(COL_RG, r * jax.nn.sigmoid(r))
    for src1, src2, dst, scale in ((W_DQ1, W_DQ2, COL_DQ, DIFF_DH ** -0.5 * LOG2E), (W_DK1, W_DK2, COL_DK, None)):
        m1, m2 = mm(src1, 2), mm(src2, 2)
        for hh in range(DIFF_HEADS):
            blk = slice((hh // 2) * LANES, (hh // 2 + 1) * LANES)
            val = _pair_maps(m1[:, blk], m2[:, blk], hh % 2 == 1)
            put(dst + hh, val if scale is None else val * scale)
    vt_ref[0] = lax.dot_general(wvt_s[...], hb, (((1,), (1,)), ((), ())),
                                preferred_element_type=F32).astype(BF16)


def _inproj(x, g, mod, w, rot):
    B, S, D = x.shape
    tm = PROJ_TILE
    vw = DIFF_HEADS * DIFF_DV
    return pl.pallas_call(
        _inproj_kernel,
        grid=(B, S // tm),
        in_specs=[pl.BlockSpec((1, tm, D), lambda b, i: (b, i, 0)),
                  _resident((1, D), lambda b, i: (0, 0)),
                  pl.BlockSpec((1, 6, D), lambda b, i: (b, 0, 0)),
                  _resident(w.shape, lambda b, i: (0, 0)),
                  pl.BlockSpec((4, tm, LANES), lambda b, i: (0, i, 0))],
        out_specs=[pl.BlockSpec((1, tm, PROJ_COLS), lambda b, i: (b, i, 0)),
                   pl.BlockSpec((1, vw, tm), lambda b, i: (b, 0, i))],
        out_shape=[jax.ShapeDtypeStruct((B, S, PROJ_COLS), BF16),
                   jax.ShapeDtypeStruct((B, vw, S), BF16)],
        scratch_shapes=[pltpu.VMEM((vw, D), BF16)],
        compiler_params=_cparams(("arbitrary", "arbitrary")),
        name="inproj",
    )(x, g, mod, w, rot)


def _ret_kernel(cd_ref, q_ref, k_ref, v_ref, rg_ref, dec_ref, dm_ref, gn_ref, o_ref, acc_ref):
    hh = pl.program_id(1)
    S = q_ref.shape[1]
    C = RET_CHUNK
    nc = S // C
    dmat = dm_ref[0]
    qdf, kdf, qdb, kdb = dec_ref[0, 0], dec_ref[0, 1], dec_ref[0, 2], dec_ref[0, 3]
    nt = (((1,), (1,)), ((), ()))
    tn = (((0,), (0,)), ((), ()))

    def chunk(n):
        sl = pl.ds(n * C, C)
        return sl, q_ref[0, sl, :], k_ref[0, sl, :], v_ref[0, sl, :]

    def scaled(t, dec):
        return (t.astype(F32) * dec).astype(BF16)

    state = jnp.zeros((RET_DK, RET_DV), F32)
    for n in range(nc):
        sl, q, k, v = chunk(n)
        s = lax.dot_general(q, k, nt, preferred_element_type=F32) * dmat
        inner = jnp.dot(s.astype(BF16), v, preferred_element_type=F32)
        cross = jnp.dot(scaled(q, qdf), state.astype(BF16), preferred_element_type=F32)
        acc_ref[sl, :] = inner + cross
        kv = lax.dot_general(scaled(k, kdf), v, tn, preferred_element_type=F32)
        state = cd_ref[hh, 0] * state + kv
    state = jnp.zeros((RET_DK, RET_DV), F32)
    for n in reversed(range(nc)):
        sl, q, k, v = chunk(n)
        cross = jnp.dot(scaled(q, qdb), state.astype(BF16), preferred_element_type=F32)
        acc_ref[sl, :] = acc_ref[sl, :] + cross
        kv = lax.dot_general(scaled(k, kdb), v, tn, preferred_element_type=F32)
        state = cd_ref[hh, 1] * state + kv

    y = acc_ref[...]
    mu = jnp.mean(y, axis=-1, keepdims=True)
    yc = y - mu
    var = jnp.mean(yc * yc, axis=-1, keepdims=True)
    yn = yc * lax.rsqrt(var + NORM_EPS) * gn_ref[...]
    o_ref[0] = (rg_ref[0].astype(F32) * yn).astype(BF16)


def _ret(proj, cd, dec, dmat, gn):
    B, S, _ = proj.shape
    H = RET_HEADS
    C = RET_CHUNK

    def col(base):
        return pl.BlockSpec((1, S, LANES), lambda b, h: (b, 0, base + h))

    return pl.pallas_call(
        _ret_kernel,
        grid=(B, H),
        in_specs=[pl.BlockSpec(memory_space=pltpu.SMEM),
                  col(COL_RQ), col(COL_RK), col(COL_RV), col(COL_RG),
                  pl.BlockSpec((1, 4, C, LANES), lambda b, h: (h, 0, 0, 0)),
                  pl.BlockSpec((1, C, C), lambda b, h: (h, 0, 0)),
                  pl.BlockSpec((1, LANES), lambda b, h: (0, h))],
        out_specs=pl.BlockSpec((1, S, LANES), lambda b, h: (b, 0, h)),
        out_shape=jax.ShapeDtypeStruct((B, S, H * RET_DV), BF16),
        scratch_shapes=[pltpu.VMEM((S, RET_DV), F32)],
        compiler_params=_cparams(("arbitrary", "arbitrary")),
        name="ret",
    )(cd, proj, proj, proj, proj, dec, dmat, gn)


def _diff_kernel(bmax_ref, lam_ref, q_ref, k_ref, vt_ref, t_ref, g_ref, o_ref, p_ref):
    hh = pl.program_id(0)
    S = k_ref.shape[1]
    TQ, KB = Q_TILE, KEY_CHUNK
    nk = S // KB
    lv = lam_ref[...]
    lam = (jnp.exp(jnp.sum(lv[0:1] * lv[1:2], axis=1, keepdims=True))
           - jnp.exp(jnp.sum(lv[2:3] * lv[3:4], axis=1, keepdims=True)) + LAM_INIT)
    first_half = lax.broadcasted_iota(jnp.int32, (TQ, LANES), 1) < DIFF_DH
    nt = (((1,), (1,)), ((), ()))
    bmax = bmax_ref[hh]
    kf = k_ref[0].astype(F32)
    kmax = jnp.sqrt(jnp.max(jnp.sum(kf * kf, axis=1, keepdims=True), axis=0, keepdims=True))
    sel_r = lax.broadcasted_iota(jnp.int32, (8, LANES), 0)
    sel_c = lax.broadcasted_iota(jnp.int32, (8, LANES), 1)
    sel = jnp.where((sel_c < DIFF_DH) == (sel_r == 0), 1.0, 0.0)
    sel = jnp.where(sel_r < 2, sel, 0.0).astype(BF16)

    def masked_q(qb):
        q = q_ref[0, pl.ds(pl.multiple_of(qb * TQ, TQ), TQ), :]
        zero = jnp.zeros_like(q)
        return q, (jnp.where(first_half, q, zero), jnp.where(first_half, zero, q))

    def finish(ot, qb):
        ms = jnp.mean(ot * ot, axis=0, keepdims=True)
        y = ot * lax.rsqrt(ms + NORM_EPS) * g_ref[...] * (1.0 - LAM_INIT)
        o_ref[0, pl.ds(pl.multiple_of(qb * TQ, TQ), TQ), :] = y.T.astype(BF16)

    def exponentials(qb, slot):
        w0 = pl.multiple_of((S - TQ) - qb * TQ, TQ)
        q, qz = masked_q(qb)
        qsq = (q.astype(F32) * q.astype(F32)).astype(BF16)
        n2 = lax.dot_general(sel, qsq, nt, preferred_element_type=F32) * 1.01
        m = [jnp.sqrt(n2[i:i + 1]) * kmax + bmax for i in range(2)]
        l = [jnp.zeros((8, TQ), F32) for _ in range(2)]
        for c in range(nk):
            rows = slice(c * KB, (c + 1) * KB)
            kb = k_ref[0, rows, :]
            bias = t_ref[0, pl.ds(w0 + c * KB, KB), :]
            for i in range(2):
                s = lax.dot_general(kb, qz[i], nt, preferred_element_type=F32)
                p = jnp.exp2((s - m[i]) + bias)
                l[i] = l[i] + jnp.sum(p.reshape(KB // 8, 8, TQ), axis=0)
                p_ref[slot, i, rows, :] = p.astype(BF16)
        return [jnp.sum(l[i], axis=0, keepdims=True) for i in range(2)]

    def values(qb, slot, lr):
        ratio = jnp.broadcast_to(lam * lr[0] / lr[1], (16, TQ)).astype(BF16)
        ot = jnp.zeros((DIFF_DV, TQ), F32)
        for c in range(nk):
            rows = slice(c * KB, (c + 1) * KB)
            p1 = p_ref[slot, 0, rows, :].reshape(KB // 16, 16, TQ)
            p2 = p_ref[slot, 1, rows, :].reshape(KB // 16, 16, TQ)
            a = (p1 - ratio * p2).reshape(KB, TQ)
            ot = ot + jnp.dot(vt_ref[0, :, rows], a, preferred_element_type=F32)
        finish(ot * (1.0 / lr[0]), qb)

    def fast_pair(j, lmin):
        la = exponentials(2 * j, 0)
        lb = exponentials(2 * j + 1, 1)
        values(2 * j, 0, la)
        values(2 * j + 1, 1, lb)
        return jnp.minimum(lmin, jnp.minimum(jnp.minimum(la[0], la[1]), jnp.minimum(lb[0], lb[1])))

    lmin = lax.fori_loop(0, S // (2 * TQ), fast_pair, jnp.full((1, TQ), jnp.inf, F32))

    @pl.when(jnp.min(lmin) < L_MIN)
    def _():
        def exact_tile(qb, carry):
            w0 = pl.multiple_of((S - TQ) - qb * TQ, TQ)
            _, qz = masked_q(qb)
            bias = t_ref[0, pl.ds(w0, S), :]
            attn = None
            for i, scale in ((0, 1.0), (1, lam)):
                s = lax.dot_general(k_ref[0], qz[i], nt, preferred_element_type=F32) + bias
                p = jnp.exp2(s - jnp.max(s, axis=0, keepdims=True))
                part = p * (scale / jnp.sum(p, axis=0, keepdims=True))
                attn = part if attn is None else attn - part
            finish(jnp.dot(vt_ref[0], attn.astype(BF16), preferred_element_type=F32), qb)
            return carry

        lax.fori_loop(0, S // TQ, exact_tile, 0)


def _diff(proj, vt, lam_vecs, ttab, bmax, g):
    B, S, _ = proj.shape
    H = DIFF_HEADS

    def col(base):
        return pl.BlockSpec((1, S, LANES), lambda h, b: (b, 0, base + h))

    return pl.pallas_call(
        _diff_kernel,
        grid=(H, B),
        in_specs=[pl.BlockSpec(memory_space=pltpu.SMEM),
                  pl.BlockSpec((4, DIFF_DH), lambda h, b: (0, 0)),
                  col(COL_DQ), col(COL_DK),
                  pl.BlockSpec((1, DIFF_DV, S), lambda h, b: (b, h, 0)),
                  pl.BlockSpec((1, 2 * S - Q_TILE, Q_TILE), lambda h, b: (h, 0, 0)),
                  pl.BlockSpec((DIFF_DV, Q_TILE), lambda h, b: (0, 0))],
        out_specs=pl.BlockSpec((1, S, LANES), lambda h, b: (b, 0, h)),
        out_shape=jax.ShapeDtypeStruct((B, S, H * DIFF_DV), BF16),
        scratch_shapes=[pltpu.VMEM((2, 2, S, Q_TILE), BF16)],
        compiler_params=_cparams(("arbitrary", "arbitrary")),
        name="diff",
    )(bmax, lam_vecs, proj, proj, vt, ttab, g)


def _pack_halves(x):
    w = x.shape[1] // 2
    lo = pltpu.bitcast(x[:, :w].astype(BF16).astype(F32), jnp.int32)
    hi = pltpu.bitcast(x[:, w:].astype(BF16).astype(F32), jnp.int32)
    return lax.shift_right_logical(lo, jnp.full_like(lo, 16)) | (hi & jnp.int32(-65536))


def _unpack_halves(words):
    lo = pltpu.bitcast(lax.shift_left(words, jnp.full_like(words, 16)), F32)
    hi = pltpu.bitcast(words & jnp.int32(-65536), F32)
    return jnp.concatenate([lo, hi], axis=1).astype(BF16)


def _merge_kernel(x_ref, yr_ref, yd_ref, gr_ref, gd_ref, mod_ref, wr_ref, wd_ref, wo_ref,
                  g_ref, wrt_ref, x1_ref, h2_ref, lg_ref):
    half = x_ref.shape[1] // 2
    for r in range(2):
        rows = slice(r * half, (r + 1) * half)
        a = jnp.dot(yr_ref[0, rows, :], wr_ref[...], preferred_element_type=F32)
        d = jnp.dot(yd_ref[0, rows, :], wd_ref[...], preferred_element_type=F32)
        merged = gr_ref[0, rows, :].astype(F32) * a + gd_ref[0, rows, :].astype(F32) * d
        o = jnp.dot(merged.astype(BF16), wo_ref[...], preferred_element_type=F32)
        x1 = x_ref[0, rows, :] + mod_ref[0, 2:3, :] * o
        x1_ref[0, rows, :] = x1
        h2 = _rms(x1, g_ref[...]) * (1.0 + mod_ref[0, 4:5, :]) + mod_ref[0, 3:4, :]
        h2_ref[0, rows, :] = _pack_halves(h2)
        lg_ref[0, :, rows] = lax.dot_general(wrt_ref[...], h2.astype(BF16), (((1,), (1,)), ((), ())),
                                             preferred_element_type=F32)


def _merge(x, yr, yd, proj, mod, wr, wd, wo, g, wrt):
    B, S, D = x.shape
    tm = PROJ_TILE
    E = wrt.shape[0]
    gate_w = COL_GD - COL_GR
    return pl.pallas_call(
        _merge_kernel,
        grid=(B, S // tm),
        in_specs=[pl.BlockSpec((1, tm, D), lambda b, i: (b, i, 0)),
                  pl.BlockSpec((1, tm, yr.shape[2]), lambda b, i: (b, i, 0)),
                  pl.BlockSpec((1, tm, yd.shape[2]), lambda b, i: (b, i, 0)),
                  pl.BlockSpec((1, tm, gate_w * LANES), lambda b, i: (b, i, COL_GR // gate_w)),
                  pl.BlockSpec((1, tm, gate_w * LANES), lambda b, i: (b, i, COL_GD // gate_w)),
                  pl.BlockSpec((1, 6, D), lambda b, i: (b, 0, 0)),
                  _resident(wr.shape, lambda b, i: (0, 0)),
                  _resident(wd.shape, lambda b, i: (0, 0)),
                  _resident(wo.shape, lambda b, i: (0, 0)),
                  _resident((1, D), lambda b, i: (0, 0)),
                  _resident(wrt.shape, lambda b, i: (0, 0))],
        out_specs=[pl.BlockSpec((1, tm, D), lambda b, i: (b, i, 0)),
                   pl.BlockSpec((1, tm, D // 2), lambda b, i: (b, i, 0)),
                   pl.BlockSpec((1, E, tm), lambda b, i: (b, 0, i))],
        out_shape=[jax.ShapeDtypeStruct((B, S, D), F32),
                   jax.ShapeDtypeStruct((B, S, D // 2), jnp.int32),
                   jax.ShapeDtypeStruct((B, E, S), F32)],
        compiler_params=_cparams(("arbitrary", "arbitrary")),
        name="merge",
    )(x, yr, yd, proj, proj, mod, wr, wd, wo, g, wrt)


def _lane_prefix(m, tri):
    E, S = m.shape
    off = jnp.zeros((E, 1), F32)
    parts = []
    for j in range(S // LANES):
        blk = m[:, j * LANES:(j + 1) * LANES]
        parts.append(jnp.dot(blk.astype(BF16), tri, preferred_element_type=F32) + off)
        off = off + jnp.sum(blk, axis=1, keepdims=True)
    return jnp.concatenate(parts, axis=1)


def _route_kernel(lg_ref, rank_ref, gate_ref, *, cap):
    lg = lg_ref[0]
    e = jnp.exp(lg - jnp.max(lg, axis=0, keepdims=True))
    aff = e / jnp.sum(e, axis=0, keepdims=True)
    bits = pltpu.bitcast(aff, jnp.int32)
    E = lg.shape[0]

    def count(mask):
        return jnp.sum(jnp.where(mask, 1.0, 0.0), axis=1, keepdims=True)

    def reaches(cand):
        return count(bits >= cand) >= cap

    thr = jnp.zeros((E, 1), jnp.int32)
    for hi in range(30, 0, -2):
        c1, c2 = thr | (1 << hi), thr | (1 << (hi - 1))
        c3 = c1 | (1 << (hi - 1))
        thr = jnp.where(reaches(c3), c3, jnp.where(reaches(c1), c1, jnp.where(reaches(c2), c2, thr)))
    thr = jnp.where(reaches(thr | 1), thr | 1, thr)
    gt = bits > thr
    eq = bits == thr
    need = cap - count(gt)
    r = lax.broadcasted_iota(jnp.int32, (LANES, LANES), 0)
    c = lax.broadcasted_iota(jnp.int32, (LANES, LANES), 1)
    tri = jnp.where(r < c, 1.0, 0.0).astype(BF16)
    eq_before = _lane_prefix(jnp.where(eq, 1.0, 0.0), tri)
    sel = gt | (eq & (eq_before < need))
    slot = _lane_prefix(jnp.where(sel, 1.0, 0.0), tri)
    rank_ref[0] = jnp.where(sel, slot, -1.0).astype(jnp.int32)
    gate_ref[0] = jnp.where(sel, aff, 0.0)


def _route(logits, cap):
    B, E, S = logits.shape
    spec = pl.BlockSpec((1, E, S), lambda b: (b, 0, 0))
    return pl.pallas_call(
        functools.partial(_route_kernel, cap=cap),
        grid=(B,),
        in_specs=[spec],
        out_specs=[spec, spec],
        out_shape=[jax.ShapeDtypeStruct((B, E, S), jnp.int32),
                   jax.ShapeDtypeStruct((B, E, S), F32)],
        compiler_params=_cparams(("arbitrary",)),
        name="route",
    )(logits)


def _gather_rows(table, rank, cap):
    B, E, S = rank.shape
    W = table.shape[1]
    workers = SC_CORES * SC_SUBCORES
    pairs = E * B
    per = pairs // workers
    assert per * workers == pairs and B & (B - 1) == 0 and cap % SC_GATHER_ROWS == 0 and S % SC_LANES == 0
    shift = B.bit_length() - 1
    mesh = plsc.VectorSubcoreMesh(core_axis_name="c", subcore_axis_name="s")

    R = SC_GATHER_ROWS
    n_chunks = cap // R

    def body(table_hbm, rank_hbm, out_hbm, rank_v, idx_v, buf_v, gsem, wsem):
        wid = lax.axis_index("s") * SC_CORES + lax.axis_index("c")

        def fetch(c):
            return pltpu.make_async_copy(table_hbm.at[idx_v.at[pl.ds(c * R, R)]], buf_v.at[c % 2], gsem.at[c % 2])

        def flush(p, c):
            return pltpu.make_async_copy(buf_v.at[c % 2], out_hbm.at[pl.ds(p * cap + c * R, R)], wsem.at[c % 2])

        for j in range(per):
            p = wid * per + j
            e = lax.shift_right_logical(p, shift)
            b = p & (B - 1)
            pltpu.sync_copy(rank_hbm.at[b, e], rank_v)
            base = b * S

            @pl.loop(0, S, step=SC_LANES)
            def _(t0):
                r = rank_v[pl.ds(t0, SC_LANES)]
                tok = lax.iota(jnp.int32, SC_LANES) + (base + t0)
                plsc.store_scatter(idx_v, [r], tok, mask=r >= 0)

            fetch(0).start()
            for c in range(n_chunks):
                fetch(c).wait()
                if c >= 1:
                    flush(p, c - 1).wait()
                if c + 1 < n_chunks:
                    fetch(c + 1).start()
                flush(p, c).start()
            flush(p, n_chunks - 1).wait()

    return pl.kernel(
        body,
        out_type=jax.ShapeDtypeStruct((pairs * cap, W), jnp.int32),
        mesh=mesh,
        scratch_types=[pltpu.VMEM((S,), jnp.int32), pltpu.VMEM((cap,), jnp.int32),
                       pltpu.VMEM((2, R, W), jnp.int32),
                       pltpu.SemaphoreType.DMA((2,)), pltpu.SemaphoreType.DMA((2,))],
        compiler_params=pltpu.CompilerParams(needs_layout_passes=False),
    )(table, rank)


def _moe_kernel(rank_ref, gate_ref, x_ref, wg_ref, wu_ref, wd_ref, o_ref, wg_s, wu_s, wd_s, *, cap):
    r, b = pl.program_id(0), pl.program_id(1)
    n_exp = pl.num_programs(0) - 1

    @pl.when(r < n_exp)
    def _():
        slot = r % 2
        rows_in, rows_ff = wg_ref.shape[1], wd_ref.shape[1]
        wg_s[slot, pl.ds(pl.multiple_of(b * rows_in, rows_in), rows_in), :] = wg_ref[0].astype(BF16)
        wu_s[slot, pl.ds(pl.multiple_of(b * rows_in, rows_in), rows_in), :] = wu_ref[0].astype(BF16)
        wd_s[slot, pl.ds(pl.multiple_of(b * rows_ff, rows_ff), rows_ff), :] = wd_ref[0].astype(BF16)

    @pl.when(r == 0)
    def _():
        o_ref[...] = jnp.zeros_like(o_ref)

    @pl.when(r > 0)
    def _():
        slot = (r - 1) % 2
        rank = rank_ref[0, 0]
        S = rank.shape[1]
        pick = lax.broadcasted_iota(jnp.int32, (cap, S), 0) == rank
        xin = _unpack_halves(x_ref[0])
        a = jnp.dot(xin, wg_s[slot], preferred_element_type=F32)
        u = jnp.dot(xin, wu_s[slot], preferred_element_type=F32)
        act = (a * jax.nn.sigmoid(a) * u).astype(BF16)
        y = jnp.dot(act, wd_s[slot], preferred_element_type=F32)
        g = jnp.sum(jnp.where(pick, gate_ref[0, 0], 0.0), axis=1, keepdims=True)
        o_ref[0, 0] = (y * g).astype(BF16)


def _moe(rank, gate, xin, wg, wu, wd, cap):
    B, E, S = rank.shape
    D, Fd = wg.shape[1], wg.shape[2]
    assert D % B == 0 and Fd % B == 0

    def cur(r):
        return jnp.maximum(r - 1, 0)

    def nxt(r):
        return jnp.minimum(r, E - 1)

    row = pl.BlockSpec((1, 1, 1, S), lambda r, b: (b, cur(r), 0, 0))
    return pl.pallas_call(
        functools.partial(_moe_kernel, cap=cap),
        grid=(E + 1, B),
        in_specs=[row, row,
                  pl.BlockSpec((1, cap, D // 2), lambda r, b: (cur(r) * B + b, 0, 0)),
                  pl.BlockSpec((1, D // B, Fd), lambda r, b: (nxt(r), b, 0)),
                  pl.BlockSpec((1, D // B, Fd), lambda r, b: (nxt(r), b, 0)),
                  pl.BlockSpec((1, Fd // B, D), lambda r, b: (nxt(r), b, 0))],
        out_specs=pl.BlockSpec((1, 1, cap, D), lambda r, b: (jnp.where(r == 0, E, r - 1), b, 0, 0)),
        out_shape=jax.ShapeDtypeStruct((E + 1, B, cap, D), BF16),
        scratch_shapes=[pltpu.VMEM((2, D, Fd), BF16), pltpu.VMEM((2, D, Fd), BF16),
                        pltpu.VMEM((2, Fd, D), BF16)],
        compiler_params=_cparams(("arbitrary", "arbitrary")),
        name="moe",
    )(rank.reshape(B, E, 1, S), gate.reshape(B, E, 1, S), xin.reshape(E * B, cap, D // 2), wg, wu, wd)


def _scat_kernel(rt_ref, y_ref, x1_ref, mod_ref, g_ref, o_ref, *, cap):
    rt = rt_ref[0]
    tm, E = rt.shape
    slot = lax.broadcasted_iota(jnp.int32, (tm, cap), 1)
    place = jnp.concatenate(
        [jnp.where(rt[:, e:e + 1] == slot, 1.0, 0.0).astype(BF16) for e in range(E)], axis=1)
    y = y_ref[:, 0].reshape(E * cap, y_ref.shape[3])
    moe = jnp.dot(place, y, preferred_element_type=F32)
    x2 = x1_ref[0] + mod_ref[0, 5:6, :] * moe
    o_ref[0] = _rms(x2, g_ref[...])


def _scat(rank_t, ybuf, x1, mod, g, cap):
    B, S, D = x1.shape
    E = rank_t.shape[2]
    tm = TOKEN_TILE
    return pl.pallas_call(
        functools.partial(_scat_kernel, cap=cap),
        grid=(B, S // tm),
        in_specs=[pl.BlockSpec((1, tm, E), lambda b, i: (b, i, 0)),
                  pl.BlockSpec((E, 1, cap, D), lambda b, i: (0, b, 0, 0)),
                  pl.BlockSpec((1, tm, D), lambda b, i: (b, i, 0)),
                  pl.BlockSpec((1, 6, D), lambda b, i: (b, 0, 0)),
                  _resident((1, D), lambda b, i: (0, 0))],
        out_specs=pl.BlockSpec((1, tm, D), lambda b, i: (b, i, 0)),
        out_shape=jax.ShapeDtypeStruct((B, S, D), F32),
        compiler_params=_cparams(("arbitrary", "arbitrary")),
        name="scat",
    )(rank_t, ybuf, x1, mod, g)


def _rot_tables(S):
    half = RET_DK // 2
    inv = 1.0 / (RET_THETA_BASE ** np.linspace(0.0, 1.0, half))
    ang = np.arange(S, dtype=np.float64)[:, None] * inv[None, :]
    cos = np.repeat(np.cos(ang), 2, axis=1)
    sin = np.stack([-np.sin(ang), np.sin(ang)], axis=2).reshape(S, RET_DK)
    sc = RET_DK ** -0.5
    return jnp.asarray(np.stack([cos, sin, cos * sc, sin * sc]), F32)


def _ret_tables():
    C = RET_CHUNK
    heads = np.arange(RET_HEADS, dtype=np.float64)
    lgf = np.log1p(-np.exp2(-RET_FWD_DECAY_OFFSET - heads))[:, None]
    lgb = np.log1p(-np.exp2(-RET_BWD_DECAY_OFFSET - heads))[:, None]
    idx = np.arange(C, dtype=np.float64)
    diff = idx[:, None] - idx[None, :]
    dmat = np.where(diff >= 0,
                    np.exp(np.maximum(diff, 0.0)[None] * lgf[:, :, None]),
                    np.exp(np.maximum(-diff, 0.0)[None] * lgb[:, :, None]))
    dec = np.stack([np.exp((idx + 1)[None, :] * lgf),
                    np.exp((C - 1 - idx)[None, :] * lgf),
                    np.exp((C - idx)[None, :] * lgb),
                    np.exp(idx[None, :] * lgb)], axis=1)
    dec = np.broadcast_to(dec[..., None], dec.shape + (LANES,))
    cd = np.concatenate([np.exp(C * lgf), np.exp(C * lgb)], axis=1)
    return jnp.asarray(cd, F32), jnp.asarray(dec, F32), jnp.asarray(dmat, F32)


def _t5_bucket(rel):
    nb = N_BUCKETS // 2
    max_exact = nb // 2
    ret = (rel > 0).astype(jnp.int32) * nb
    n = jnp.abs(rel)
    large = max_exact + (jnp.log(jnp.maximum(n, 1).astype(F32) / max_exact)
                         / math.log(MAX_DISTANCE / max_exact) * (nb - max_exact)).astype(jnp.int32)
    large = jnp.minimum(large, nb - 1)
    return ret + jnp.where(n < max_exact, n, large)


def _bias_table(rel_bias, S):
    TQ, M = Q_TILE, MAX_DISTANCE
    reach = TQ + M - 1
    rel = jnp.concatenate([jnp.arange(-reach, reach + 1, dtype=jnp.int32),
                           jnp.array([-(S - 1), S - 1], jnp.int32)])
    f = rel_bias[_t5_bucket(rel)].astype(F32).T * LOG2E
    H = f.shape[0]
    lo, hi = f[:, -2], f[:, -1]
    L = 2 * reach + 1
    u = jnp.concatenate([f[:, :L], jnp.zeros((H, 1), F32)], axis=1)
    shifted = jnp.tile(u, (1, TQ))[:, :TQ * L].reshape(H, TQ, L)
    band = jnp.swapaxes(shifted[:, :, TQ - 1:TQ - 1 + TQ + 2 * M], 1, 2)
    side = S - TQ - M
    table = jnp.concatenate([jnp.broadcast_to(lo[:, None, None], (H, side, TQ)), band,
                             jnp.broadcast_to(hi[:, None, None], (H, side, TQ))], axis=1)
    return table, jnp.max(f, axis=1)


def kernel(x, c, w_ada, b_ada, norm_mix_g, w_in, ret_gn_g, diff_subln_g, lambda_q1, lambda_k1, lambda_q2, lambda_k2, w_ret_out, w_diff_out, w_o, rel_bias, norm_ffn_g, w_router, w_exp_gate, w_exp_up, w_exp_down, final_g):
    B, S, D = x.shape
    cap = CAPACITY_FACTOR * S // N_EXPERTS
    l = 0

    mod = _ada(c, w_ada[l], b_ada[l]).reshape(B, 6, D)
    proj, vt = _inproj(x, norm_mix_g[l].reshape(1, D), mod, w_in[l].astype(BF16), _rot_tables(S))

    cd, dec, dmat = _ret_tables()
    y_ret = _ret(proj, cd, dec, dmat, ret_gn_g[l].reshape(1, -1))

    lam_vecs = jnp.stack([lambda_q1[l], lambda_k1[l], lambda_q2[l], lambda_k2[l]]).astype(F32)
    ttab, bmax = _bias_table(rel_bias, S)
    subln = jnp.broadcast_to(diff_subln_g[l].astype(F32)[:, None], (DIFF_DV, Q_TILE))
    y_diff = _diff(proj, vt, lam_vecs, ttab, bmax, subln)

    x1, h2, logits = _merge(x, y_ret, y_diff, proj, mod,
                            w_ret_out[l].astype(BF16), w_diff_out[l].astype(BF16), w_o[l].astype(BF16),
                            norm_ffn_g[l].reshape(1, D), w_router[l].T.astype(BF16))

    rank, gate = _route(logits, cap)
    xin = _gather_rows(h2.reshape(B * S, D // 2), rank, cap)
    ybuf = _moe(rank, gate, xin, w_exp_gate[l], w_exp_up[l], w_exp_down[l], cap)
    return _scat(jnp.swapaxes(rank, 1, 2), ybuf, x1, mod, final_g.reshape(1, D), cap)
```

```python
import functools
import math

import numpy as np
import jax
import jax.numpy as jnp
from jax import lax
from jax.experimental import pallas as pl
from jax.experimental.pallas import tpu as pltpu
from jax.experimental.pallas import tpu_sc as plsc

F32 = jnp.float32
BF16 = jnp.bfloat16

RET_HEADS = 4
RET_DK = 128
RET_DV = 128
RET_FWD_DECAY_OFFSET = 5.0
RET_BWD_DECAY_OFFSET = 5.5
RET_THETA_BASE = 10000.0
DIFF_HEADS = 4
DIFF_DH = 64
DIFF_DV = 2 * DIFF_DH
N_BUCKETS = 32
MAX_DISTANCE = 128
N_EXPERTS = 16
CAPACITY_FACTOR = 2
NORM_EPS = 1e-6
LAM_INIT = 0.8 - 0.6 * math.exp(-0.3 * 0)
LOG2E = math.log2(math.e)

LANES = 128
BF16_SUBLANES = 16
VMEM_LIMIT_BYTES = 56 * 2**20

SC_CORES = 2
SC_SUBCORES = 16
SC_LANES = 16
SC_GATHER_ROWS = 64

TOKEN_TILE = 512
PROJ_TILE = 1024
RET_CHUNK = 256
Q_TILE = 256
KEY_CHUNK = 512
L_MIN = 2.0 ** -60
SCAT_WINDOW = 128

W_RQ, W_RK, W_RV, W_RG = 0, 4, 8, 12
W_DQ1, W_DQ2, W_DK1, W_DK2, W_DV = 16, 18, 20, 22, 24
W_GR, W_GD = 28, 36

COL_GR, COL_GD = 0, 8
COL_RQ, COL_RK, COL_RV, COL_RG = 16, 20, 24, 28
COL_DQ, COL_DK = 32, 36
PROJ_COLS = 40 * LANES


def _cparams(sem):
    return pltpu.CompilerParams(dimension_semantics=sem, vmem_limit_bytes=VMEM_LIMIT_BYTES)


def _resident(shape, index_map):
    return pl.BlockSpec(shape, index_map, pipeline_mode=pl.Buffered(1))


def _ada_kernel(c_ref, w_ref, b_ref, o_ref):
    c = c_ref[...]
    a = c * jax.nn.sigmoid(c)
    o_ref[...] = jnp.dot(a, w_ref[...], preferred_element_type=F32,
                         precision=lax.Precision.HIGHEST) + b_ref[...]


def _ada(c, w, b):
    B, D = c.shape
    n = w.shape[1] // D
    return pl.pallas_call(
        _ada_kernel,
        grid=(n,),
        in_specs=[pl.BlockSpec((B, D), lambda j: (0, 0)),
                  pl.BlockSpec((D, D), lambda j: (0, j)),
                  pl.BlockSpec((1, D), lambda j: (0, j))],
        out_specs=pl.BlockSpec((B, D), lambda j: (0, j)),
        out_shape=jax.ShapeDtypeStruct((B, n * D), F32),
        compiler_params=_cparams(("arbitrary",)),
        name="ada",
    )(c, w, b.reshape(1, -1))


def _rms(x, g):
    ms = jnp.mean(x * x, axis=-1, keepdims=True)
    return x * lax.rsqrt(ms + NORM_EPS) * g


def _rot_pairs(x, cos, sin):
    even = (lax.broadcasted_iota(jnp.int32, x.shape, 1) & 1) == 0
    partner = jnp.where(even, pltpu.roll(x, LANES - 1, axis=1), pltpu.roll(x, 1, axis=1))
    return x * cos + partner * sin


def _pair_maps(a, b, second):
    low = lax.broadcasted_iota(jnp.int32, a.shape, 1) < DIFF_DH
    if second:
        return jnp.where(low, pltpu.roll(a, DIFF_DH, axis=1), b)
    return jnp.where(low, a, pltpu.roll(b, DIFF_DH, axis=1))


def _inproj_kernel(x_ref, g_ref, mod_ref, w_ref, rot_ref, o_ref, vt_ref, wvt_s):
    @pl.when((pl.program_id(0) == 0) & (pl.program_id(1) == 0))
    def _():
        wvt_s[...] = w_ref[:, W_DV * LANES:(W_DV + 4) * LANES].astype(F32).T.astype(BF16)

    x = x_ref[0]
    h = _rms(x, g_ref[...]) * (1.0 + mod_ref[0, 1:2, :]) + mod_ref[0, 0:1, :]
    hb = h.astype(BF16)

    def mm(col, width):
        return jnp.dot(hb, w_ref[:, col * LANES:(col + width) * LANES], preferred_element_type=F32)

    def put(col, val):
        o_ref[0, :, col * LANES:col * LANES + val.shape[1]] = val.astype(BF16)

    for j in range(0, COL_GD - COL_GR, 4):
        put(COL_GR + j, jax.nn.sigmoid(mm(W_GR + j, 4)))
        put(COL_GD + j, jax.nn.sigmoid(mm(W_GD + j, 4)))
    for src, dst, t in ((W_RQ, COL_RQ, 0), (W_RK, COL_RK, 2)):
        cos, sin = rot_ref[t], rot_ref[t + 1]
        r = mm(src, RET_HEADS)
        for hh in range(RET_HEADS):
            put(dst + hh, _rot_pairs(r[:, hh * LANES:(hh + 1) * LANES], cos, sin))
    put(COL_RV, mm(W_RV, 4))
    r = mm(W_RG, 4)
    put(COL_RG, r * jax.nn.sigmoid(r))
    for src1, src2, dst, scale in ((W_DQ1, W_DQ2, COL_DQ, DIFF_DH ** -0.5 * LOG2E), (W_DK1, W_DK2, COL_DK, None)):
        m1, m2 = mm(src1, 2), mm(src2, 2)
        for hh in range(DIFF_HEADS):
            blk = slice((hh // 2) * LANES, (hh // 2 + 1) * LANES)
            val = _pair_maps(m1[:, blk], m2[:, blk], hh % 2 == 1)
            put(dst + hh, val if scale is None else val * scale)
    vt_ref[0] = lax.dot_general(wvt_s[...], hb, (((1,), (1,)), ((), ())),
                                preferred_element_type=F32).astype(BF16)


def _inproj(x, g, mod, w, rot):
    B, S, D = x.shape
    tm = PROJ_TILE
    vw = DIFF_HEADS * DIFF_DV
    return pl.pallas_call(
        _inproj_kernel,
        grid=(B, S // tm),
        in_specs=[pl.BlockSpec((1, tm, D), lambda b, i: (b, i, 0)),
                  _resident((1, D), lambda b, i: (0, 0)),
                  pl.BlockSpec((1, 6, D), lambda b, i: (b, 0, 0)),
                  _resident(w.shape, lambda b, i: (0, 0)),
                  pl.BlockSpec((4, tm, LANES), lambda b, i: (0, i, 0))],
        out_specs=[pl.BlockSpec((1, tm, PROJ_COLS), lambda b, i: (b, i, 0)),
                   pl.BlockSpec((1, vw, tm), lambda b, i: (b, 0, i))],
        out_shape=[jax.ShapeDtypeStruct((B, S, PROJ_COLS), BF16),
                   jax.ShapeDtypeStruct((B, vw, S), BF16)],
        scratch_shapes=[pltpu.VMEM((vw, D), BF16)],
        compiler_params=_cparams(("arbitrary", "arbitrary")),
        name="inproj",
    )(x, g, mod, w, rot)


def _ret_kernel(cd_ref, q_ref, k_ref, v_ref, rg_ref, dec_ref, dm_ref, gn_ref, o_ref, acc_ref):
    hh = pl.program_id(1)
    S = q_ref.shape[1]
    C = RET_CHUNK
    nc = S // C
    dmat = dm_ref[0]
    qdf, kdf, qdb, kdb = dec_ref[0, 0], dec_ref[0, 1], dec_ref[0, 2], dec_ref[0, 3]
    nt = (((1,), (1,)), ((), ()))
    tn = (((0,), (0,)), ((), ()))

    def chunk(n):
        sl = pl.ds(n * C, C)
        return sl, q_ref[0, sl, :], k_ref[0, sl, :], v_ref[0, sl, :]

    def scaled(t, dec):
        return (t.astype(F32) * dec).astype(BF16)

    state = jnp.zeros((RET_DK, RET_DV), F32)
    for n in range(nc):
        sl, q, k, v = chunk(n)
        s = lax.dot_general(q, k, nt, preferred_element_type=F32) * dmat
        inner = jnp.dot(s.astype(BF16), v, preferred_element_type=F32)
        cross = jnp.dot(scaled(q, qdf), state.astype(BF16), preferred_element_type=F32)
        acc_ref[sl, :] = inner + cross
        kv = lax.dot_general(scaled(k, kdf), v, tn, preferred_element_type=F32)
        state = cd_ref[hh, 0] * state + kv
    state = jnp.zeros((RET_DK, RET_DV), F32)
    for n in reversed(range(nc)):
        sl, q, k, v = chunk(n)
        cross = jnp.dot(scaled(q, qdb), state.astype(BF16), preferred_element_type=F32)
        acc_ref[sl, :] = acc_ref[sl, :] + cross
        kv = lax.dot_general(scaled(k, kdb), v, tn, preferred_element_type=F32)
        state = cd_ref[hh, 1] * state + kv

    y = acc_ref[...]
    mu = jnp.mean(y, axis=-1, keepdims=True)
    yc = y - mu
    var = jnp.mean(yc * yc, axis=-1, keepdims=True)
    yn = yc * lax.rsqrt(var + NORM_EPS) * gn_ref[...]
    o_ref[0] = (rg_ref[0].astype(F32) * yn).astype(BF16)


def _ret(proj, cd, dec, dmat, gn):
    B, S, _ = proj.shape
    H = RET_HEADS
    C = RET_CHUNK

    def col(base):
        return pl.BlockSpec((1, S, LANES), lambda b, h: (b, 0, base + h))

    return pl.pallas_call(
        _ret_kernel,
        grid=(B, H),
        in_specs=[pl.BlockSpec(memory_space=pltpu.SMEM),
                  col(COL_RQ), col(COL_RK), col(COL_RV), col(COL_RG),
                  pl.BlockSpec((1, 4, C, LANES), lambda b, h: (h, 0, 0, 0)),
                  pl.BlockSpec((1, C, C), lambda b, h: (h, 0, 0)),
                  pl.BlockSpec((1, LANES), lambda b, h: (0, h))],
        out_specs=pl.BlockSpec((1, S, LANES), lambda b, h: (b, 0, h)),
        out_shape=jax.ShapeDtypeStruct((B, S, H * RET_DV), BF16),
        scratch_shapes=[pltpu.VMEM((S, RET_DV), F32)],
        compiler_params=_cparams(("arbitrary", "arbitrary")),
        name="ret",
    )(cd, proj, proj, proj, proj, dec, dmat, gn)


def _diff_kernel(bmax_ref, lam_ref, q_ref, k_ref, vt_ref, t_ref, g_ref, o_ref, p_ref):
    hh = pl.program_id(0)
    S = k_ref.shape[1]
    TQ, KB = Q_TILE, KEY_CHUNK
    nk = S // KB
    lv = lam_ref[...]
    lam = (jnp.exp(jnp.sum(lv[0:1] * lv[1:2], axis=1, keepdims=True))
           - jnp.exp(jnp.sum(lv[2:3] * lv[3:4], axis=1, keepdims=True)) + LAM_INIT)
    first_half = lax.broadcasted_iota(jnp.int32, (TQ, LANES), 1) < DIFF_DH
    nt = (((1,), (1,)), ((), ()))
    bmax = bmax_ref[hh]
    kf = k_ref[0].astype(F32)
    kmax = jnp.sqrt(jnp.max(jnp.sum(kf * kf, axis=1, keepdims=True), axis=0, keepdims=True))
    sel_r = lax.broadcasted_iota(jnp.int32, (8, LANES), 0)
    sel_c = lax.broadcasted_iota(jnp.int32, (8, LANES), 1)
    sel = jnp.where((sel_c < DIFF_DH) == (sel_r == 0), 1.0, 0.0)
    sel = jnp.where(sel_r < 2, sel, 0.0).astype(BF16)

    def masked_q(qb):
        q = q_ref[0, pl.ds(pl.multiple_of(qb * TQ, TQ), TQ), :]
        zero = jnp.zeros_like(q)
        return q, (jnp.where(first_half, q, zero), jnp.where(first_half, zero, q))

    def finish(ot, qb):
        ms = jnp.mean(ot * ot, axis=0, keepdims=True)
        y = ot * lax.rsqrt(ms + NORM_EPS) * g_ref[...] * (1.0 - LAM_INIT)
        o_ref[0, pl.ds(pl.multiple_of(qb * TQ, TQ), TQ), :] = y.T.astype(BF16)

    def exponentials(qb, slot):
        w0 = pl.multiple_of((S - TQ) - qb * TQ, TQ)
        q, qz = masked_q(qb)
        qsq = (q.astype(F32) * q.astype(F32)).astype(BF16)
        n2 = lax.dot_general(sel, qsq, nt, preferred_element_type=F32) * 1.01
        m = [jnp.sqrt(n2[i:i + 1]) * kmax + bmax for i in range(2)]
        l = [jnp.zeros((8, TQ), F32) for _ in range(2)]
        for c in range(nk):
            rows = slice(c * KB, (c + 1) * KB)
            kb = k_ref[0, rows, :]
            bias = t_ref[0, pl.ds(w0 + c * KB, KB), :]
            for i in range(2):
                s = lax.dot_general(kb, qz[i], nt, preferred_element_type=F32)
                p = jnp.exp2((s - m[i]) + bias)
                l[i] = l[i] + jnp.sum(p.reshape(KB // 8, 8, TQ), axis=0)
                p_ref[slot, i, rows, :] = p.astype(BF16)
        return [jnp.sum(l[i], axis=0, keepdims=True) for i in range(2)]

    def values(qb, slot, lr):
        ratio = jnp.broadcast_to(lam * lr[0] / lr[1], (16, TQ)).astype(BF16)
        ot = jnp.zeros((DIFF_DV, TQ), F32)
        for c in range(nk):
            rows = slice(c * KB, (c + 1) * KB)
            p1 = p_ref[slot, 0, rows, :].reshape(KB // 16, 16, TQ)
            p2 = p_ref[slot, 1, rows, :].reshape(KB // 16, 16, TQ)
            a = (p1 - ratio * p2).reshape(KB, TQ)
            ot = ot + jnp.dot(vt_ref[0, :, rows], a, preferred_element_type=F32)
        finish(ot * (1.0 / lr[0]), qb)

    def fast_pair(j, lmin):
        la = exponentials(2 * j, 0)
        lb = exponentials(2 * j + 1, 1)
        values(2 * j, 0, la)
        values(2 * j + 1, 1, lb)
        return jnp.minimum(lmin, jnp.minimum(jnp.minimum(la[0], la[1]), jnp.minimum(lb[0], lb[1])))

    lmin = lax.fori_loop(0, S // (2 * TQ), fast_pair, jnp.full((1, TQ), jnp.inf, F32))

    @pl.when(jnp.min(lmin) < L_MIN)
    def _():
        def exact_tile(qb, carry):
            w0 = pl.multiple_of((S - TQ) - qb * TQ, TQ)
            _, qz = masked_q(qb)
            bias = t_ref[0, pl.ds(w0, S), :]
            attn = None
            for i, scale in ((0, 1.0), (1, lam)):
                s = lax.dot_general(k_ref[0], qz[i], nt, preferred_element_type=F32) + bias
                p = jnp.exp2(s - jnp.max(s, axis=0, keepdims=True))
                part = p * (scale / jnp.sum(p, axis=0, keepdims=True))
                attn = part if attn is None else attn - part
            finish(jnp.dot(vt_ref[0], attn.astype(BF16), preferred_element_type=F32), qb)
            return carry

        lax.fori_loop(0, S // TQ, exact_tile, 0)


def _diff(proj, vt, lam_vecs, ttab, bmax, g):
    B, S, _ = proj.shape
    H = DIFF_HEADS

    def col(base):
        return pl.BlockSpec((1, S, LANES), lambda h, b: (b, 0, base + h))

    return pl.pallas_call(
        _diff_kernel,
        grid=(H, B),
        in_specs=[pl.BlockSpec(memory_space=pltpu.SMEM),
                  pl.BlockSpec((4, DIFF_DH), lambda h, b: (0, 0)),
                  col(COL_DQ), col(COL_DK),
                  pl.BlockSpec((1, DIFF_DV, S), lambda h, b: (b, h, 0)),
                  pl.BlockSpec((1, 2 * S - Q_TILE, Q_TILE), lambda h, b: (h, 0, 0)),
                  pl.BlockSpec((DIFF_DV, Q_TILE), lambda h, b: (0, 0))],
        out_specs=pl.BlockSpec((1, S, LANES), lambda h, b: (b, 0, h)),
        out_shape=jax.ShapeDtypeStruct((B, S, H * DIFF_DV), BF16),
        scratch_shapes=[pltpu.VMEM((2, 2, S, Q_TILE), BF16)],
        compiler_params=_cparams(("arbitrary", "arbitrary")),
        name="diff",
    )(bmax, lam_vecs, proj, proj, vt, ttab, g)


def _pack_halves(x):
    w = x.shape[1] // 2
    lo = pltpu.bitcast(x[:, :w].astype(BF16).astype(F32), jnp.int32)
    hi = pltpu.bitcast(x[:, w:].astype(BF16).astype(F32), jnp.int32)
    return lax.shift_right_logical(lo, jnp.full_like(lo, 16)) | (hi & jnp.int32(-65536))


def _unpack_halves(words):
    lo = pltpu.bitcast(lax.shift_left(words, jnp.full_like(words, 16)), F32)
    hi = pltpu.bitcast(words & jnp.int32(-65536), F32)
    return jnp.concatenate([lo, hi], axis=1).astype(BF16)


def _merge_kernel(x_ref, yr_ref, yd_ref, gr_ref, gd_ref, mod_ref, wr_ref, wd_ref, wo_ref,
                  g_ref, wrt_ref, x1_ref, h2_ref, lg_ref):
    half = x_ref.shape[1] // 2
    for r in range(2):
        rows = slice(r * half, (r + 1) * half)
        a = jnp.dot(yr_ref[0, rows, :], wr_ref[...], preferred_element_type=F32)
        d = jnp.dot(yd_ref[0, rows, :], wd_ref[...], preferred_element_type=F32)
        merged = gr_ref[0, rows, :].astype(F32) * a + gd_ref[0, rows, :].astype(F32) * d
        o = jnp.dot(merged.astype(BF16), wo_ref[...], preferred_element_type=F32)
        x1 = x_ref[0, rows, :] + mod_ref[0, 2:3, :] * o
        x1_ref[0, rows, :] = x1
        h2 = _rms(x1, g_ref[...]) * (1.0 + mod_ref[0, 4:5, :]) + mod_ref[0, 3:4, :]
        h2_ref[0, rows, :] = _pack_halves(h2)
        lg_ref[0, :, rows] = lax.dot_general(wrt_ref[...], h2.astype(BF16), (((1,), (1,)), ((), ())),
                                             preferred_element_type=F32)


def _merge(x, yr, yd, proj, mod, wr, wd, wo, g, wrt):
    B, S, D = x.shape
    tm = PROJ_TILE
    E = wrt.shape[0]
    gate_w = COL_GD - COL_GR
    return pl.pallas_call(
        _merge_kernel,
        grid=(B, S // tm),
        in_specs=[pl.BlockSpec((1, tm, D), lambda b, i: (b, i, 0)),
                  pl.BlockSpec((1, tm, yr.shape[2]), lambda b, i: (b, i, 0)),
                  pl.BlockSpec((1, tm, yd.shape[2]), lambda b, i: (b, i, 0)),
                  pl.BlockSpec((1, tm, gate_w * LANES), lambda b, i: (b, i, COL_GR // gate_w)),
                  pl.BlockSpec((1, tm, gate_w * LANES), lambda b, i: (b, i, COL_GD // gate_w)),
                  pl.BlockSpec((1, 6, D), lambda b, i: (b, 0, 0)),
                  _resident(wr.shape, lambda b, i: (0, 0)),
                  _resident(wd.shape, lambda b, i: (0, 0)),
                  _resident(wo.shape, lambda b, i: (0, 0)),
                  _resident((1, D), lambda b, i: (0, 0)),
                  _resident(wrt.shape, lambda b, i: (0, 0))],
        out_specs=[pl.BlockSpec((1, tm, D), lambda b, i: (b, i, 0)),
                   pl.BlockSpec((1, tm, D // 2), lambda b, i: (b, i, 0)),
                   pl.BlockSpec((1, E, tm), lambda b, i: (b, 0, i))],
        out_shape=[jax.ShapeDtypeStruct((B, S, D), F32),
                   jax.ShapeDtypeStruct((B, S, D // 2), jnp.int32),
                   jax.ShapeDtypeStruct((B, E, S), F32)],
        compiler_params=_cparams(("arbitrary", "arbitrary")),
        name="merge",
    )(x, yr, yd, proj, proj, mod, wr, wd, wo, g, wrt)


def _lane_prefix(m, tri):
    E, S = m.shape
    off = jnp.zeros((E, 1), F32)
    parts = []
    for j in range(S // LANES):
        blk = m[:, j * LANES:(j + 1) * LANES]
        parts.append(jnp.dot(blk.astype(BF16), tri, preferred_element_type=F32) + off)
        off = off + jnp.sum(blk, axis=1, keepdims=True)
    return jnp.concatenate(parts, axis=1)


def _route_kernel(lg_ref, rank_ref, gate_ref, *, cap):
    lg = lg_ref[0]
    e = jnp.exp(lg - jnp.max(lg, axis=0, keepdims=True))
    aff = e / jnp.sum(e, axis=0, keepdims=True)
    bits = pltpu.bitcast(aff, jnp.int32)
    E = lg.shape[0]

    def count(mask):
        return jnp.sum(jnp.where(mask, 1.0, 0.0), axis=1, keepdims=True)

    def reaches(cand):
        return count(bits >= cand) >= cap

    thr = jnp.zeros((E, 1), jnp.int32)
    for hi in range(30, 0, -2):
        c1, c2 = thr | (1 << hi), thr | (1 << (hi - 1))
        c3 = c1 | (1 << (hi - 1))
        thr = jnp.where(reaches(c3), c3, jnp.where(reaches(c1), c1, jnp.where(reaches(c2), c2, thr)))
    thr = jnp.where(reaches(thr | 1), thr | 1, thr)
    gt = bits > thr
    eq = bits == thr
    need = cap - count(gt)
    r = lax.broadcasted_iota(jnp.int32, (LANES, LANES), 0)
    c = lax.broadcasted_iota(jnp.int32, (LANES, LANES), 1)
    tri = jnp.where(r < c, 1.0, 0.0).astype(BF16)
    eq_before = _lane_prefix(jnp.where(eq, 1.0, 0.0), tri)
    sel = gt | (eq & (eq_before < need))
    slot = _lane_prefix(jnp.where(sel, 1.0, 0.0), tri)
    rank_ref[0] = jnp.where(sel, slot, -1.0).astype(jnp.int32)
    gate_ref[0] = jnp.where(sel, aff, 0.0)


def _route(logits, cap):
    B, E, S = logits.shape
    spec = pl.BlockSpec((1, E, S), lambda b: (b, 0, 0))
    return pl.pallas_call(
        functools.partial(_route_kernel, cap=cap),
        grid=(B,),
        in_specs=[spec],
        out_specs=[spec, spec],
        out_shape=[jax.ShapeDtypeStruct((B, E, S), jnp.int32),
                   jax.ShapeDtypeStruct((B, E, S), F32)],
        compiler_params=_cparams(("arbitrary",)),
        name="route",
    )(logits)


def _gather_rows(table, rank, cap):
    B, E, S = rank.shape
    W = table.shape[1]
    workers = SC_CORES * SC_SUBCORES
    pairs = E * B
    per = pairs // workers
    assert per * workers == pairs and B & (B - 1) == 0 and cap % SC_GATHER_ROWS == 0 and S % SC_LANES == 0
    shift = B.bit_length() - 1
    mesh = plsc.VectorSubcoreMesh(core_axis_name="c", subcore_axis_name="s")

    R = SC_GATHER_ROWS
    n_chunks = cap // R

    def body(table_hbm, rank_hbm, out_hbm, rank_v, idx_v, buf_v, gsem, wsem):
        wid = lax.axis_index("s") * SC_CORES + lax.axis_index("c")

        def fetch(c):
            return pltpu.make_async_copy(table_hbm.at[idx_v.at[pl.ds(c * R, R)]], buf_v.at[c % 2], gsem.at[c % 2])

        def flush(p, c):
            return pltpu.make_async_copy(buf_v.at[c % 2], out_hbm.at[pl.ds(p * cap + c * R, R)], wsem.at[c % 2])

        for j in range(per):
            p = wid * per + j
            e = lax.shift_right_logical(p, shift)
            b = p & (B - 1)
            pltpu.sync_copy(rank_hbm.at[b, e], rank_v)
            base = b * S

            @pl.loop(0, S, step=SC_LANES)
            def _(t0):
                r = rank_v[pl.ds(t0, SC_LANES)]
                tok = lax.iota(jnp.int32, SC_LANES) + (base + t0)
                plsc.store_scatter(idx_v, [r], tok, mask=r >= 0)

            fetch(0).start()
            for c in range(n_chunks):
                fetch(c).wait()
                if c >= 1:
                    flush(p, c - 1).wait()
                if c + 1 < n_chunks:
                    fetch(c + 1).start()
                flush(p, c).start()
            flush(p, n_chunks - 1).wait()

    return pl.kernel(
        body,
        out_type=jax.ShapeDtypeStruct((pairs * cap, W), jnp.int32),
        mesh=mesh,
        scratch_types=[pltpu.VMEM((S,), jnp.int32), pltpu.VMEM((cap,), jnp.int32),
                       pltpu.VMEM((2, R, W), jnp.int32),
                       pltpu.SemaphoreType.DMA((2,)), pltpu.SemaphoreType.DMA((2,))],
        compiler_params=pltpu.CompilerParams(needs_layout_passes=False),
    )(table, rank)


def _moe_kernel(rank_ref, gate_ref, x_ref, wg_ref, wu_ref, wd_ref, o_ref, wg_s, wu_s, wd_s, *, cap):
    r, b = pl.program_id(0), pl.program_id(1)
    n_exp = pl.num_programs(0) - 1

    @pl.when(r < n_exp)
    def _():
        slot = r % 2
        rows_in, rows_ff = wg_ref.shape[1], wd_ref.shape[1]
        wg_s[slot, pl.ds(pl.multiple_of(b * rows_in, rows_in), rows_in), :] = wg_ref[0].astype(BF16)
        wu_s[slot, pl.ds(pl.multiple_of(b * rows_in, rows_in), rows_in), :] = wu_ref[0].astype(BF16)
        wd_s[slot, pl.ds(pl.multiple_of(b * rows_ff, rows_ff), rows_ff), :] = wd_ref[0].astype(BF16)

    @pl.when(r == 0)
    def _():
        o_ref[...] = jnp.zeros_like(o_ref)

    @pl.when(r > 0)
    def _():
        slot = (r - 1) % 2
        rank = rank_ref[0, 0]
        S = rank.shape[1]
        pick = lax.broadcasted_iota(jnp.int32, (cap, S), 0) == rank
        xin = _unpack_halves(x_ref[0])
        a = jnp.dot(xin, wg_s[slot], preferred_element_type=F32)
        u = jnp.dot(xin, wu_s[slot], preferred_element_type=F32)
        act = (a * jax.nn.sigmoid(a) * u).astype(BF16)
        y = jnp.dot(act, wd_s[slot], preferred_element_type=F32)
        g = jnp.sum(jnp.where(pick, gate_ref[0, 0], 0.0), axis=1, keepdims=True)
        o_ref[0, 0] = (y * g).astype(BF16)


def _moe(rank, gate, xin, wg, wu, wd, cap):
    B, E, S = rank.shape
    D, Fd = wg.shape[1], wg.shape[2]
    assert D % B == 0 and Fd % B == 0

    def cur(r):
        return jnp.maximum(r - 1, 0)

    def nxt(r):
        return jnp.minimum(r, E - 1)

    row = pl.BlockSpec((1, 1, 1, S), lambda r, b: (b, cur(r), 0, 0))
    return pl.pallas_call(
        functools.partial(_moe_kernel, cap=cap),
        grid=(E + 1, B),
        in_specs=[row, row,
                  pl.BlockSpec((1, cap, D // 2), lambda r, b: (cur(r) * B + b, 0, 0)),
                  pl.BlockSpec((1, D // B, Fd), lambda r, b: (nxt(r), b, 0)),
                  pl.BlockSpec((1, D // B, Fd), lambda r, b: (nxt(r), b, 0)),
                  pl.BlockSpec((1, Fd // B, D), lambda r, b: (nxt(r), b, 0))],
        out_specs=pl.BlockSpec((1, 1, cap, D), lambda r, b: (jnp.where(r == 0, E, r - 1), b, 0, 0)),
        out_shape=jax.ShapeDtypeStruct((E + 1, B, cap, D), BF16),
        scratch_shapes=[pltpu.VMEM((2, D, Fd), BF16), pltpu.VMEM((2, D, Fd), BF16),
                        pltpu.VMEM((2, Fd, D), BF16)],
        compiler_params=_cparams(("arbitrary", "arbitrary")),
        name="moe",
    )(rank.reshape(B, E, 1, S), gate.reshape(B, E, 1, S), xin.reshape(E * B, cap, D // 2), wg, wu, wd)


def _scat_kernel(win_ref, ok_ref, rt_ref, y_ref, x1_ref, mod_ref, g_ref, o_ref, *, cap):
    b, i = pl.program_id(0), pl.program_id(1)
    rt = rt_ref[0]
    tm, E = rt.shape
    D = y_ref.shape[3]

    def finish(moe):
        x2 = x1_ref[0] + mod_ref[0, 5:6, :] * moe
        o_ref[0] = _rms(x2, g_ref[...])

    @pl.when(ok_ref[b, i] != 0)
    def _():
        slot = lax.broadcasted_iota(jnp.int32, (tm, SCAT_WINDOW), 1)
        moe = jnp.zeros((tm, D), F32)
        for e in range(0, E, 2):
            starts = [pl.multiple_of(win_ref[b, i, e + k], BF16_SUBLANES) for k in range(2)]
            place = jnp.concatenate(
                [jnp.where(rt[:, e + k:e + k + 1] - starts[k] == slot, 1.0, 0.0).astype(BF16)
                 for k in range(2)], axis=1)
            y = jnp.concatenate([y_ref[e + k, 0, pl.ds(starts[k], SCAT_WINDOW), :] for k in range(2)], axis=0)
            moe = moe + jnp.dot(place, y, preferred_element_type=F32)
        finish(moe)

    @pl.when(ok_ref[b, i] == 0)
    def _():
        slot = lax.broadcasted_iota(jnp.int32, (tm, cap), 1)
        place = jnp.concatenate(
            [jnp.where(rt[:, e:e + 1] == slot, 1.0, 0.0).astype(BF16) for e in range(E)], axis=1)
        y = y_ref[0:E, 0].reshape(E * cap, D)
        finish(jnp.dot(place, y, preferred_element_type=F32))


def _scat_windows(rank, cap, tm):
    B, E, S = rank.shape
    counts = jnp.sum((rank >= 0).reshape(B, E, S // tm, tm), axis=-1, dtype=jnp.int32)
    ends = jnp.cumsum(counts, axis=-1)
    start = ends - counts
    win = jnp.minimum(start // BF16_SUBLANES * BF16_SUBLANES, cap - SCAT_WINDOW)
    ok = jnp.all(ends - win <= SCAT_WINDOW, axis=1)
    return jnp.swapaxes(win, 1, 2), ok.astype(jnp.int32)


def _scat(rank, ybuf, x1, mod, g, cap):
    B, S, D = x1.shape
    E = rank.shape[1]
    tm = TOKEN_TILE
    win, ok = _scat_windows(rank, cap, tm)
    smem = pl.BlockSpec(memory_space=pltpu.SMEM)
    return pl.pallas_call(
        functools.partial(_scat_kernel, cap=cap),
        grid=(B, S // tm),
        in_specs=[smem, smem,
                  pl.BlockSpec((1, tm, E), lambda b, i: (b, i, 0)),
                  pl.BlockSpec(ybuf.shape[:1] + (1, cap, D), lambda b, i: (0, b, 0, 0)),
                  pl.BlockSpec((1, tm, D), lambda b, i: (b, i, 0)),
                  pl.BlockSpec((1, 6, D), lambda b, i: (b, 0, 0)),
                  _resident((1, D), lambda b, i: (0, 0))],
        out_specs=pl.BlockSpec((1, tm, D), lambda b, i: (b, i, 0)),
        out_shape=jax.ShapeDtypeStruct((B, S, D), F32),
        compiler_params=_cparams(("arbitrary", "arbitrary")),
        name="scat",
    )(win, ok, jnp.swapaxes(rank, 1, 2), ybuf, x1, mod, g)


def _rot_tables(S):
    half = RET_DK // 2
    inv = 1.0 / (RET_THETA_BASE ** np.linspace(0.0, 1.0, half))
    ang = np.arange(S, dtype=np.float64)[:, None] * inv[None, :]
    cos = np.repeat(np.cos(ang), 2, axis=1)
    sin = np.stack([-np.sin(ang), np.sin(ang)], axis=2).reshape(S, RET_DK)
    sc = RET_DK ** -0.5
    return jnp.asarray(np.stack([cos, sin, cos * sc, sin * sc]), F32)


def _ret_tables():
    C = RET_CHUNK
    heads = np.arange(RET_HEADS, dtype=np.float64)
    lgf = np.log1p(-np.exp2(-RET_FWD_DECAY_OFFSET - heads))[:, None]
    lgb = np.log1p(-np.exp2(-RET_BWD_DECAY_OFFSET - heads))[:, None]
    idx = np.arange(C, dtype=np.float64)
    diff = idx[:, None] - idx[None, :]
    dmat = np.where(diff >= 0,
                    np.exp(np.maximum(diff, 0.0)[None] * lgf[:, :, None]),
                    np.exp(np.maximum(-diff, 0.0)[None] * lgb[:, :, None]))
    dec = np.stack([np.exp((idx + 1)[None, :] * lgf),
                    np.exp((C - 1 - idx)[None, :] * lgf),
                    np.exp((C - idx)[None, :] * lgb),
                    np.exp(idx[None, :] * lgb)], axis=1)
    dec = np.broadcast_to(dec[..., None], dec.shape + (LANES,))
    cd = np.concatenate([np.exp(C * lgf), np.exp(C * lgb)], axis=1)
    return jnp.asarray(cd, F32), jnp.asarray(dec, F32), jnp.asarray(dmat, F32)


def _t5_bucket(rel):
    nb = N_BUCKETS // 2
    max_exact = nb // 2
    ret = (rel > 0).astype(jnp.int32) * nb
    n = jnp.abs(rel)
    large = max_exact + (jnp.log(jnp.maximum(n, 1).astype(F32) / max_exact)
                         / math.log(MAX_DISTANCE / max_exact) * (nb - max_exact)).astype(jnp.int32)
    large = jnp.minimum(large, nb - 1)
    return ret + jnp.where(n < max_exact, n, large)


def _bias_table(rel_bias, S):
    TQ, M = Q_TILE, MAX_DISTANCE
    reach = TQ + M - 1
    rel = jnp.concatenate([jnp.arange(-reach, reach + 1, dtype=jnp.int32),
                           jnp.array([-(S - 1), S - 1], jnp.int32)])
    f = rel_bias[_t5_bucket(rel)].astype(F32).T * LOG2E
    H = f.shape[0]
    lo, hi = f[:, -2], f[:, -1]
    L = 2 * reach + 1
    u = jnp.concatenate([f[:, :L], jnp.zeros((H, 1), F32)], axis=1)
    shifted = jnp.tile(u, (1, TQ))[:, :TQ * L].reshape(H, TQ, L)
    band = jnp.swapaxes(shifted[:, :, TQ - 1:TQ - 1 + TQ + 2 * M], 1, 2)
    side = S - TQ - M
    table = jnp.concatenate([jnp.broadcast_to(lo[:, None, None], (H, side, TQ)), band,
                             jnp.broadcast_to(hi[:, None, None], (H, side, TQ))], axis=1)
    return table, jnp.max(f, axis=1)


def kernel(x, c, w_ada, b_ada, norm_mix_g, w_in, ret_gn_g, diff_subln_g, lambda_q1, lambda_k1, lambda_q2, lambda_k2, w_ret_out, w_diff_out, w_o, rel_bias, norm_ffn_g, w_router, w_exp_gate, w_exp_up, w_exp_down, final_g):
    B, S, D = x.shape
    cap = CAPACITY_FACTOR * S // N_EXPERTS
    l = 0

    mod = _ada(c, w_ada[l], b_ada[l]).reshape(B, 6, D)
    proj, vt = _inproj(x, norm_mix_g[l].reshape(1, D), mod, w_in[l].astype(BF16), _rot_tables(S))

    cd, dec, dmat = _ret_tables()
    y_ret = _ret(proj, cd, dec, dmat, ret_gn_g[l].reshape(1, -1))

    lam_vecs = jnp.stack([lambda_q1[l], lambda_k1[l], lambda_q2[l], lambda_k2[l]]).astype(F32)
    ttab, bmax = _bias_table(rel_bias, S)
    subln = jnp.broadcast_to(diff_subln_g[l].astype(F32)[:, None], (DIFF_DV, Q_TILE))
    y_diff = _diff(proj, vt, lam_vecs, ttab, bmax, subln)

    x1, h2, logits = _merge(x, y_ret, y_diff, proj, mod,
                            w_ret_out[l].astype(BF16), w_diff_out[l].astype(BF16), w_o[l].astype(BF16),
                            norm_ffn_g[l].reshape(1, D), w_router[l].T.astype(BF16))

    rank, gate = _route(logits, cap)
    xin = _gather_rows(h2.reshape(B * S, D // 2), rank, cap)
    ybuf = _moe(rank, gate, xin, w_exp_gate[l], w_exp_up[l], w_exp_down[l], cap)
    return _scat(rank, ybuf, x1, mod, final_g.reshape(1, D), cap)
```

```python
import functools
import math

import numpy as np
import jax
import jax.numpy as jnp
from jax import lax
from jax.experimental import pallas as pl
from jax.experimental.pallas import tpu as pltpu
from jax.experimental.pallas import tpu_sc as plsc

F32 = jnp.float32
BF16 = jnp.bfloat16

RET_HEADS = 4
RET_DK = 128
RET_DV = 128
RET_FWD_DECAY_OFFSET = 5.0
RET_BWD_DECAY_OFFSET = 5.5
RET_THETA_BASE = 10000.0
DIFF_HEADS = 4
DIFF_DH = 64
DIFF_DV = 2 * DIFF_DH
N_BUCKETS = 32
MAX_DISTANCE = 128
N_EXPERTS = 16
CAPACITY_FACTOR = 2
NORM_EPS = 1e-6
LAM_INIT = 0.8 - 0.6 * math.exp(-0.3 * 0)
LOG2E = math.log2(math.e)

LANES = 128
MXU_DEPTH = 256
BF16_SUBLANES = 16
VMEM_LIMIT_BYTES = 56 * 2**20

SC_CORES = 2
SC_SUBCORES = 16
SC_LANES = 16
SC_GATHER_ROWS = 64

TOKEN_TILE = 512
PROJ_TILE = 1024
RET_CHUNK = 256
Q_TILE = 256
KEY_CHUNK = 512
L_MIN = 2.0 ** -60
SCAT_TOKENS = 256
SCAT_WINDOW = 64

W_RQ, W_RK, W_RV, W_RG = 0, 4, 8, 12
W_DQ1, W_DQ2, W_DK1, W_DK2, W_DV = 16, 18, 20, 22, 24
W_GR, W_GD = 28, 36

COL_GR, COL_GD = 0, 8
COL_RQ, COL_RK, COL_RV, COL_RG = 16, 20, 24, 28
COL_DQ, COL_DK = 32, 36
PROJ_COLS = 40 * LANES


def _cparams(sem):
    return pltpu.CompilerParams(dimension_semantics=sem, vmem_limit_bytes=VMEM_LIMIT_BYTES)


def _resident(shape, index_map):
    return pl.BlockSpec(shape, index_map, pipeline_mode=pl.Buffered(1))


def _ada_kernel(c_ref, w_ref, b_ref, o_ref):
    c = c_ref[...]
    a = c * jax.nn.sigmoid(c)
    o_ref[...] = jnp.dot(a, w_ref[...], preferred_element_type=F32,
                         precision=lax.Precision.HIGHEST) + b_ref[...]


def _ada(c, w, b):
    B, D = c.shape
    n = w.shape[1] // D
    return pl.pallas_call(
        _ada_kernel,
        grid=(n,),
        in_specs=[pl.BlockSpec((B, D), lambda j: (0, 0)),
                  pl.BlockSpec((D, D), lambda j: (0, j)),
                  pl.BlockSpec((1, D), lambda j: (0, j))],
        out_specs=pl.BlockSpec((B, D), lambda j: (0, j)),
        out_shape=jax.ShapeDtypeStruct((B, n * D), F32),
        compiler_params=_cparams(("arbitrary",)),
        name="ada",
    )(c, w, b.reshape(1, -1))


def _rms(x, g):
    ms = jnp.mean(x * x, axis=-1, keepdims=True)
    return x * lax.rsqrt(ms + NORM_EPS) * g


def _rot_pairs(x, cos, sin):
    even = (lax.broadcasted_iota(jnp.int32, x.shape, 1) & 1) == 0
    partner = jnp.where(even, pltpu.roll(x, LANES - 1, axis=1), pltpu.roll(x, 1, axis=1))
    return x * cos + partner * sin


def _pair_maps(a, b, second):
    low = lax.broadcasted_iota(jnp.int32, a.shape, 1) < DIFF_DH
    if second:
        return jnp.where(low, pltpu.roll(a, DIFF_DH, axis=1), b)
    return jnp.where(low, a, pltpu.roll(b, DIFF_DH, axis=1))


def _inproj_kernel(x_ref, g_ref, mod_ref, w_ref, rot_ref, o_ref, vt_ref, wvt_s):
    @pl.when((pl.program_id(0) == 0) & (pl.program_id(1) == 0))
    def _():
        wvt_s[...] = w_ref[:, W_DV * LANES:(W_DV + 4) * LANES].astype(F32).T.astype(BF16)

    x = x_ref[0]
    h = _rms(x, g_ref[...]) * (1.0 + mod_ref[0, 1:2, :]) + mod_ref[0, 0:1, :]
    hb = h.astype(BF16)

    def mm(col, width):
        return jnp.dot(hb, w_ref[:, col * LANES:(col + width) * LANES], preferred_element_type=F32)

    def put(col, val):
        o_ref[0, :, col * LANES:col * LANES + val.shape[1]] = val.astype(BF16)

    for j in range(0, COL_GD - COL_GR, 4):
        put(COL_GR + j, jax.nn.sigmoid(mm(W_GR + j, 4)))
        put(COL_GD + j, jax.nn.sigmoid(mm(W_GD + j, 4)))
    for src, dst, t in ((W_RQ, COL_RQ, 0), (W_RK, COL_RK, 2)):
        cos, sin = rot_ref[t], rot_ref[t + 1]
        r = mm(src, RET_HEADS)
        for hh in range(RET_HEADS):
            put(dst + hh, _rot_pairs(r[:, hh * LANES:(hh + 1) * LANES], cos, sin))
    put(COL_RV, mm(W_RV, 4))
    r = mm(W_RG, 4)
    put(COL_RG, r * jax.nn.sigmoid(r))
    for src1, src2, dst, scale in ((W_DQ1, W_DQ2, COL_DQ, DIFF_DH ** -0.5 * LOG2E), (W_DK1, W_DK2, COL_DK, None)):
        m1, m2 = mm(src1, 2), mm(src2, 2)
        for hh in range(DIFF_HEADS):
            blk = slice((hh // 2) * LANES, (hh // 2 + 1) * LANES)
            val = _pair_maps(m1[:, blk], m2[:, blk], hh % 2 == 1)
            put(dst + hh, val if scale is None else val * scale)
    vt_ref[0] = lax.dot_general(wvt_s[...], hb, (((1,), (1,)), ((), ())),
                                preferred_element_type=F32).astype(BF16)


def _inproj(x, g, mod, w, rot):
    B, S, D = x.shape
    tm = PROJ_TILE
    vw = DIFF_HEADS * DIFF_DV
    return pl.pallas_call(
        _inproj_kernel,
        grid=(B, S // tm),
        in_specs=[pl.BlockSpec((1, tm, D), lambda b, i: (b, i, 0)),
                  _resident((1, D), lambda b, i: (0, 0)),
                  pl.BlockSpec((1, 6, D), lambda b, i: (b, 0, 0)),
                  _resident(w.shape, lambda b, i: (0, 0)),
                  pl.BlockSpec((4, tm, LANES), lambda b, i: (0, i, 0))],
        out_specs=[pl.BlockSpec((1, tm, PROJ_COLS), lambda b, i: (b, i, 0)),
                   pl.BlockSpec((1, vw, tm), lambda b, i: (b, 0, i))],
        out_shape=[jax.ShapeDtypeStruct((B, S, PROJ_COLS), BF16),
                   jax.ShapeDtypeStruct((B, vw, S), BF16)],
        scratch_shapes=[pltpu.VMEM((vw, D), BF16)],
        compiler_params=_cparams(("arbitrary", "arbitrary")),
        name="inproj",
    )(x, g, mod, w, rot)


def _ret_kernel(cd_ref, q_ref, k_ref, v_ref, rg_ref, dec_ref, dm_ref, gn_ref, o_ref, acc_ref):
    hh = pl.program_id(1)
    S = q_ref.shape[1]
    C = RET_CHUNK
    nc = S // C
    dmat = dm_ref[0]
    qdf, kdf, qdb, kdb = dec_ref[0, 0], dec_ref[0, 1], dec_ref[0, 2], dec_ref[0, 3]
    nt = (((1,), (1,)), ((), ()))
    tn = (((0,), (0,)), ((), ()))

    def chunk(n):
        sl = pl.ds(n * C, C)
        return sl, q_ref[0, sl, :], k_ref[0, sl, :], v_ref[0, sl, :]

    def scaled(t, dec):
        return (t.astype(F32) * dec).astype(BF16)

    state = jnp.zeros((RET_DK, RET_DV), F32)
    for n in range(nc):
        sl, q, k, v = chunk(n)
        s = lax.dot_general(q, k, nt, preferred_element_type=F32) * dmat
        inner = jnp.dot(s.astype(BF16), v, preferred_element_type=F32)
        cross = jnp.dot(scaled(q, qdf), state.astype(BF16), preferred_element_type=F32)
        acc_ref[sl, :] = inner + cross
        kv = lax.dot_general(scaled(k, kdf), v, tn, preferred_element_type=F32)
        state = cd_ref[hh, 0] * state + kv
    state = jnp.zeros((RET_DK, RET_DV), F32)
    for n in reversed(range(nc)):
        sl, q, k, v = chunk(n)
        cross = jnp.dot(scaled(q, qdb), state.astype(BF16), preferred_element_type=F32)
        acc_ref[sl, :] = acc_ref[sl, :] + cross
        kv = lax.dot_general(scaled(k, kdb), v, tn, preferred_element_type=F32)
        state = cd_ref[hh, 1] * state + kv

    y = acc_ref[...]
    mu = jnp.mean(y, axis=-1, keepdims=True)
    yc = y - mu
    var = jnp.mean(yc * yc, axis=-1, keepdims=True)
    yn = yc * lax.rsqrt(var + NORM_EPS) * gn_ref[...]
    o_ref[0] = (rg_ref[0].astype(F32) * yn).astype(BF16)


def _ret(proj, cd, dec, dmat, gn):
    B, S, _ = proj.shape
    H = RET_HEADS
    C = RET_CHUNK

    def col(base):
        return pl.BlockSpec((1, S, LANES), lambda b, h: (b, 0, base + h))

    return pl.pallas_call(
        _ret_kernel,
        grid=(B, H),
        in_specs=[pl.BlockSpec(memory_space=pltpu.SMEM),
                  col(COL_RQ), col(COL_RK), col(COL_RV), col(COL_RG),
                  pl.BlockSpec((1, 4, C, LANES), lambda b, h: (h, 0, 0, 0)),
                  pl.BlockSpec((1, C, C), lambda b, h: (h, 0, 0)),
                  pl.BlockSpec((1, LANES), lambda b, h: (0, h))],
        out_specs=pl.BlockSpec((1, S, LANES), lambda b, h: (b, 0, h)),
        out_shape=jax.ShapeDtypeStruct((B, S, H * RET_DV), BF16),
        scratch_shapes=[pltpu.VMEM((S, RET_DV), F32)],
        compiler_params=_cparams(("arbitrary", "arbitrary")),
        name="ret",
    )(cd, proj, proj, proj, proj, dec, dmat, gn)


def _diff_kernel(bmax_ref, lam_ref, q_ref, k_ref, vt_ref, t_ref, g_ref, o_ref, p_ref):
    hh = pl.program_id(0)
    S = k_ref.shape[1]
    TQ, KB = Q_TILE, KEY_CHUNK
    nk = S // KB
    lv = lam_ref[...]
    lam = (jnp.exp(jnp.sum(lv[0:1] * lv[1:2], axis=1, keepdims=True))
           - jnp.exp(jnp.sum(lv[2:3] * lv[3:4], axis=1, keepdims=True)) + LAM_INIT)
    first_half = lax.broadcasted_iota(jnp.int32, (TQ, LANES), 1) < DIFF_DH
    nt = (((1,), (1,)), ((), ()))
    bmax = bmax_ref[hh]
    kf = k_ref[0].astype(F32)
    kmax = jnp.sqrt(jnp.max(jnp.sum(kf * kf, axis=1, keepdims=True), axis=0, keepdims=True))
    sel_r = lax.broadcasted_iota(jnp.int32, (8, LANES), 0)
    sel_c = lax.broadcasted_iota(jnp.int32, (8, LANES), 1)
    sel = jnp.where((sel_c < DIFF_DH) == (sel_r == 0), 1.0, 0.0)
    sel = jnp.where(sel_r < 2, sel, 0.0).astype(BF16)

    def masked_q(qb):
        q = q_ref[0, pl.ds(pl.multiple_of(qb * TQ, TQ), TQ), :]
        zero = jnp.zeros_like(q)
        return q, (jnp.where(first_half, q, zero), jnp.where(first_half, zero, q))

    def finish(ot, qb):
        ms = jnp.mean(ot * ot, axis=0, keepdims=True)
        y = ot * lax.rsqrt(ms + NORM_EPS) * g_ref[...] * (1.0 - LAM_INIT)
        o_ref[0, pl.ds(pl.multiple_of(qb * TQ, TQ), TQ), :] = y.T.astype(BF16)

    def exponentials(qb, slot):
        w0 = pl.multiple_of((S - TQ) - qb * TQ, TQ)
        q, qz = masked_q(qb)
        qsq = (q.astype(F32) * q.astype(F32)).astype(BF16)
        n2 = lax.dot_general(sel, qsq, nt, preferred_element_type=F32) * 1.01
        m = [jnp.sqrt(n2[i:i + 1]) * kmax + bmax for i in range(2)]
        l = [jnp.zeros((8, TQ), F32) for _ in range(2)]
        for c in range(nk):
            rows = slice(c * KB, (c + 1) * KB)
            kb = k_ref[0, rows, :]
            bias = t_ref[0, pl.ds(w0 + c * KB, KB), :]
            for i in range(2):
                s = lax.dot_general(kb, qz[i], nt, preferred_element_type=F32)
                p = jnp.exp2((s - m[i]) + bias)
                l[i] = l[i] + jnp.sum(p.reshape(KB // 8, 8, TQ), axis=0)
                p_ref[slot, i, rows, :] = p.astype(BF16)
        return [jnp.sum(l[i], axis=0, keepdims=True) for i in range(2)]

    def values(qb, slot, lr):
        ratio = jnp.broadcast_to(lam * lr[0] / lr[1], (16, TQ)).astype(BF16)
        ot = jnp.zeros((DIFF_DV, TQ), F32)
        for c in range(nk):
            rows = slice(c * KB, (c + 1) * KB)
            p1 = p_ref[slot, 0, rows, :].reshape(KB // 16, 16, TQ)
            p2 = p_ref[slot, 1, rows, :].reshape(KB // 16, 16, TQ)
            a = (p1 - ratio * p2).reshape(KB, TQ)
            ot = ot + jnp.dot(vt_ref[0, :, rows], a, preferred_element_type=F32)
        finish(ot * (1.0 / lr[0]), qb)

    def fast_pair(j, lmin):
        la = exponentials(2 * j, 0)
        lb = exponentials(2 * j + 1, 1)
        values(2 * j, 0, la)
        values(2 * j + 1, 1, lb)
        return jnp.minimum(lmin, jnp.minimum(jnp.minimum(la[0], la[1]), jnp.minimum(lb[0], lb[1])))

    lmin = lax.fori_loop(0, S // (2 * TQ), fast_pair, jnp.full((1, TQ), jnp.inf, F32))

    @pl.when(jnp.min(lmin) < L_MIN)
    def _():
        def exact_tile(qb, carry):
            w0 = pl.multiple_of((S - TQ) - qb * TQ, TQ)
            _, qz = masked_q(qb)
            bias = t_ref[0, pl.ds(w0, S), :]
            attn = None
            for i, scale in ((0, 1.0), (1, lam)):
                s = lax.dot_general(k_ref[0], qz[i], nt, preferred_element_type=F32) + bias
                p = jnp.exp2(s - jnp.max(s, axis=0, keepdims=True))
                part = p * (scale / jnp.sum(p, axis=0, keepdims=True))
                attn = part if attn is None else attn - part
            finish(jnp.dot(vt_ref[0], attn.astype(BF16), preferred_element_type=F32), qb)
            return carry

        lax.fori_loop(0, S // TQ, exact_tile, 0)


def _diff(proj, vt, lam_vecs, ttab, bmax, g):
    B, S, _ = proj.shape
    H = DIFF_HEADS

    def col(base):
        return pl.BlockSpec((1, S, LANES), lambda h, b: (b, 0, base + h))

    return pl.pallas_call(
        _diff_kernel,
        grid=(H, B),
        in_specs=[pl.BlockSpec(memory_space=pltpu.SMEM),
                  pl.BlockSpec((4, DIFF_DH), lambda h, b: (0, 0)),
                  col(COL_DQ), col(COL_DK),
                  pl.BlockSpec((1, DIFF_DV, S), lambda h, b: (b, h, 0)),
                  pl.BlockSpec((1, 2 * S - Q_TILE, Q_TILE), lambda h, b: (h, 0, 0)),
                  pl.BlockSpec((DIFF_DV, Q_TILE), lambda h, b: (0, 0))],
        out_specs=pl.BlockSpec((1, S, LANES), lambda h, b: (b, 0, h)),
        out_shape=jax.ShapeDtypeStruct((B, S, H * DIFF_DV), BF16),
        scratch_shapes=[pltpu.VMEM((2, 2, S, Q_TILE), BF16)],
        compiler_params=_cparams(("arbitrary", "arbitrary")),
        name="diff",
    )(bmax, lam_vecs, proj, proj, vt, ttab, g)


def _pack_halves(x):
    w = x.shape[1] // 2
    lo = pltpu.bitcast(x[:, :w].astype(BF16).astype(F32), jnp.int32)
    hi = pltpu.bitcast(x[:, w:].astype(BF16).astype(F32), jnp.int32)
    return lax.shift_right_logical(lo, jnp.full_like(lo, 16)) | (hi & jnp.int32(-65536))


def _unpack_halves(words):
    lo = pltpu.bitcast(lax.shift_left(words, jnp.full_like(words, 16)), F32)
    hi = pltpu.bitcast(words & jnp.int32(-65536), F32)
    return jnp.concatenate([lo, hi], axis=1).astype(BF16)


def _merge_kernel(x_ref, yr_ref, yd_ref, gr_ref, gd_ref, mod_ref, wr_ref, wd_ref, wo_ref,
                  g_ref, wrt_ref, x1_ref, h2_ref, lg_ref):
    half = x_ref.shape[1] // 2
    for r in range(2):
        rows = slice(r * half, (r + 1) * half)
        a = jnp.dot(yr_ref[0, rows, :], wr_ref[...], preferred_element_type=F32)
        d = jnp.dot(yd_ref[0, rows, :], wd_ref[...], preferred_element_type=F32)
        merged = gr_ref[0, rows, :].astype(F32) * a + gd_ref[0, rows, :].astype(F32) * d
        o = jnp.dot(merged.astype(BF16), wo_ref[...], preferred_element_type=F32)
        x1 = x_ref[0, rows, :] + mod_ref[0, 2:3, :] * o
        x1_ref[0, rows, :] = x1
        h2 = _rms(x1, g_ref[...]) * (1.0 + mod_ref[0, 4:5, :]) + mod_ref[0, 3:4, :]
        h2_ref[0, rows, :] = _pack_halves(h2)
        lg_ref[0, :, rows] = lax.dot_general(wrt_ref[...], h2.astype(BF16), (((1,), (1,)), ((), ())),
                                             preferred_element_type=F32)


def _merge(x, yr, yd, proj, mod, wr, wd, wo, g, wrt):
    B, S, D = x.shape
    tm = PROJ_TILE
    E = wrt.shape[0]
    gate_w = COL_GD - COL_GR
    return pl.pallas_call(
        _merge_kernel,
        grid=(B, S // tm),
        in_specs=[pl.BlockSpec((1, tm, D), lambda b, i: (b, i, 0)),
                  pl.BlockSpec((1, tm, yr.shape[2]), lambda b, i: (b, i, 0)),
                  pl.BlockSpec((1, tm, yd.shape[2]), lambda b, i: (b, i, 0)),
                  pl.BlockSpec((1, tm, gate_w * LANES), lambda b, i: (b, i, COL_GR // gate_w)),
                  pl.BlockSpec((1, tm, gate_w * LANES), lambda b, i: (b, i, COL_GD // gate_w)),
                  pl.BlockSpec((1, 6, D), lambda b, i: (b, 0, 0)),
                  _resident(wr.shape, lambda b, i: (0, 0)),
                  _resident(wd.shape, lambda b, i: (0, 0)),
                  _resident(wo.shape, lambda b, i: (0, 0)),
                  _resident((1, D), lambda b, i: (0, 0)),
                  _resident(wrt.shape, lambda b, i: (0, 0))],
        out_specs=[pl.BlockSpec((1, tm, D), lambda b, i: (b, i, 0)),
                   pl.BlockSpec((1, tm, D // 2), lambda b, i: (b, i, 0)),
                   pl.BlockSpec((1, E, tm), lambda b, i: (b, 0, i))],
        out_shape=[jax.ShapeDtypeStruct((B, S, D), F32),
                   jax.ShapeDtypeStruct((B, S, D // 2), jnp.int32),
                   jax.ShapeDtypeStruct((B, E, S), F32)],
        compiler_params=_cparams(("arbitrary", "arbitrary")),
        name="merge",
    )(x, yr, yd, proj, proj, mod, wr, wd, wo, g, wrt)


def _lane_prefix(m, tri):
    E, S = m.shape
    off = jnp.zeros((E, 1), F32)
    parts = []
    for j in range(S // LANES):
        blk = m[:, j * LANES:(j + 1) * LANES]
        parts.append(jnp.dot(blk.astype(BF16), tri, preferred_element_type=F32) + off)
        off = off + jnp.sum(blk, axis=1, keepdims=True)
    return jnp.concatenate(parts, axis=1)


def _route_kernel(lg_ref, rank_ref, gate_ref, *, cap):
    lg = lg_ref[0]
    e = jnp.exp(lg - jnp.max(lg, axis=0, keepdims=True))
    aff = e / jnp.sum(e, axis=0, keepdims=True)
    bits = pltpu.bitcast(aff, jnp.int32)
    E = lg.shape[0]

    def count(mask):
        return jnp.sum(jnp.where(mask, 1.0, 0.0), axis=1, keepdims=True)

    def reaches(cand):
        return count(bits >= cand) >= cap

    thr = jnp.zeros((E, 1), jnp.int32)
    for hi in range(30, 0, -2):
        c1, c2 = thr | (1 << hi), thr | (1 << (hi - 1))
        c3 = c1 | (1 << (hi - 1))
        thr = jnp.where(reaches(c3), c3, jnp.where(reaches(c1), c1, jnp.where(reaches(c2), c2, thr)))
    thr = jnp.where(reaches(thr | 1), thr | 1, thr)
    gt = bits > thr
    eq = bits == thr
    need = cap - count(gt)
    r = lax.broadcasted_iota(jnp.int32, (LANES, LANES), 0)
    c = lax.broadcasted_iota(jnp.int32, (LANES, LANES), 1)
    tri = jnp.where(r < c, 1.0, 0.0).astype(BF16)
    eq_before = _lane_prefix(jnp.where(eq, 1.0, 0.0), tri)
    sel = gt | (eq & (eq_before < need))
    slot = _lane_prefix(jnp.where(sel, 1.0, 0.0), tri)
    rank_ref[0] = jnp.where(sel, slot, -1.0).astype(jnp.int32)
    gate_ref[0] = jnp.where(sel, aff, 0.0)


def _route(logits, cap):
    B, E, S = logits.shape
    spec = pl.BlockSpec((1, E, S), lambda b: (b, 0, 0))
    return pl.pallas_call(
        functools.partial(_route_kernel, cap=cap),
        grid=(B,),
        in_specs=[spec],
        out_specs=[spec, spec],
        out_shape=[jax.ShapeDtypeStruct((B, E, S), jnp.int32),
                   jax.ShapeDtypeStruct((B, E, S), F32)],
        compiler_params=_cparams(("arbitrary",)),
        name="route",
    )(logits)


def _gather_rows(table, rank, cap):
    B, E, S = rank.shape
    W = table.shape[1]
    workers = SC_CORES * SC_SUBCORES
    pairs = E * B
    per = pairs // workers
    assert per * workers == pairs and B & (B - 1) == 0 and cap % SC_GATHER_ROWS == 0 and S % SC_LANES == 0
    shift = B.bit_length() - 1
    mesh = plsc.VectorSubcoreMesh(core_axis_name="c", subcore_axis_name="s")

    R = SC_GATHER_ROWS
    n_chunks = cap // R

    def body(table_hbm, rank_hbm, out_hbm, rank_v, idx_v, buf_v, gsem, wsem):
        wid = lax.axis_index("s") * SC_CORES + lax.axis_index("c")

        def fetch(c):
            return pltpu.make_async_copy(table_hbm.at[idx_v.at[pl.ds(c * R, R)]], buf_v.at[c % 2], gsem.at[c % 2])

        def flush(p, c):
            return pltpu.make_async_copy(buf_v.at[c % 2], out_hbm.at[pl.ds(p * cap + c * R, R)], wsem.at[c % 2])

        for j in range(per):
            p = wid * per + j
            e = lax.shift_right_logical(p, shift)
            b = p & (B - 1)
            pltpu.sync_copy(rank_hbm.at[b, e], rank_v)
            base = b * S

            @pl.loop(0, S, step=SC_LANES)
            def _(t0):
                r = rank_v[pl.ds(t0, SC_LANES)]
                tok = lax.iota(jnp.int32, SC_LANES) + (base + t0)
                plsc.store_scatter(idx_v, [r], tok, mask=r >= 0)

            fetch(0).start()
            for c in range(n_chunks):
                fetch(c).wait()
                if c >= 1:
                    flush(p, c - 1).wait()
                if c + 1 < n_chunks:
                    fetch(c + 1).start()
                flush(p, c).start()
            flush(p, n_chunks - 1).wait()

    return pl.kernel(
        body,
        out_type=jax.ShapeDtypeStruct((pairs * cap, W), jnp.int32),
        mesh=mesh,
        scratch_types=[pltpu.VMEM((S,), jnp.int32), pltpu.VMEM((cap,), jnp.int32),
                       pltpu.VMEM((2, R, W), jnp.int32),
                       pltpu.SemaphoreType.DMA((2,)), pltpu.SemaphoreType.DMA((2,))],
        compiler_params=pltpu.CompilerParams(needs_layout_passes=False),
    )(table, rank)


def _moe_kernel(rank_ref, gate_ref, x_ref, wg_ref, wu_ref, wd_ref, o_ref, wg_s, wu_s, wd_s, *, cap):
    r, b = pl.program_id(0), pl.program_id(1)
    n_exp = pl.num_programs(0) - 1

    def load(slot):
        rows_in, rows_ff = wg_ref.shape[1], wd_ref.shape[1]
        wg_s[slot, pl.ds(pl.multiple_of(b * rows_in, rows_in), rows_in), :] = wg_ref[0].astype(BF16)
        wu_s[slot, pl.ds(pl.multiple_of(b * rows_in, rows_in), rows_in), :] = wu_ref[0].astype(BF16)
        wd_s[slot, pl.ds(pl.multiple_of(b * rows_ff, rows_ff), rows_ff), :] = wd_ref[0].astype(BF16)

    def compute(slot):
        e = pl.ds(r - 1, 1)
        rank = rank_ref[0, e, :]
        S = rank.shape[1]
        pick = lax.broadcasted_iota(jnp.int32, (cap, S), 0) == rank
        xin = _unpack_halves(x_ref[0])
        a = jnp.dot(xin, wg_s[slot], preferred_element_type=F32)
        u = jnp.dot(xin, wu_s[slot], preferred_element_type=F32)
        act = (a * jax.nn.sigmoid(a) * u).astype(BF16)
        y = jnp.dot(act, wd_s[slot], preferred_element_type=F32)
        g = jnp.sum(jnp.where(pick, gate_ref[0, e, :], 0.0), axis=1, keepdims=True)
        o_ref[0, 0] = (y * g).astype(BF16)

    @pl.when(r == 0)
    def _():
        load(0)
        o_ref[...] = jnp.zeros_like(o_ref)

    for parity in range(2):
        @pl.when((r > 0) & (r < n_exp) & (r % 2 == parity))
        def _():
            compute(1 - parity)
            load(parity)

    for parity in range(2):
        @pl.when((r == n_exp) & (r % 2 == parity))
        def _():
            compute(1 - parity)


def _moe(rank, gate, xin, wg, wu, wd, cap):
    B, E, S = rank.shape
    D, Fd = wg.shape[1], wg.shape[2]
    assert D % B == 0 and Fd % B == 0

    def cur(r):
        return jnp.maximum(r - 1, 0)

    def nxt(r):
        return jnp.minimum(r, E - 1)

    row = pl.BlockSpec((1, E, S), lambda r, b: (b, 0, 0))
    return pl.pallas_call(
        functools.partial(_moe_kernel, cap=cap),
        grid=(E + 1, B),
        in_specs=[row, row,
                  pl.BlockSpec((1, cap, D // 2), lambda r, b: (cur(r) * B + b, 0, 0)),
                  pl.BlockSpec((1, D // B, Fd), lambda r, b: (nxt(r), b, 0)),
                  pl.BlockSpec((1, D // B, Fd), lambda r, b: (nxt(r), b, 0)),
                  pl.BlockSpec((1, Fd // B, D), lambda r, b: (nxt(r), b, 0))],
        out_specs=pl.BlockSpec((1, 1, cap, D), lambda r, b: (jnp.where(r == 0, E, r - 1), b, 0, 0)),
        out_shape=jax.ShapeDtypeStruct((E + 1, B, cap, D), BF16),
        scratch_shapes=[pltpu.VMEM((2, D, Fd), BF16), pltpu.VMEM((2, D, Fd), BF16),
                        pltpu.VMEM((2, Fd, D), BF16)],
        compiler_params=_cparams(("arbitrary", "arbitrary")),
        name="moe",
    )(rank, gate, xin.reshape(E * B, cap, D // 2), wg, wu, wd)


def _scat_kernel(win_ref, ok_ref, rt_ref, y_ref, x1_ref, mod_ref, g_ref, o_ref, *, cap):
    b, i = pl.program_id(0), pl.program_id(1)
    tm, E = rt_ref.shape[1], rt_ref.shape[2]
    D = y_ref.shape[3]
    ts, group = SCAT_TOKENS, MXU_DEPTH // SCAT_WINDOW
    subs = tm // ts

    for j in range(subs):
        rows = slice(j * ts, (j + 1) * ts)
        sub = i * subs + j
        rt = rt_ref[0, rows, :]

        def finish(moe, rows=rows):
            x2 = x1_ref[0, rows, :] + mod_ref[0, 5:6, :] * moe
            o_ref[0, rows, :] = _rms(x2, g_ref[...])

        @pl.when(ok_ref[b, sub] != 0)
        def _():
            slot = lax.broadcasted_iota(jnp.int32, (ts, SCAT_WINDOW), 1)
            moe = jnp.zeros((ts, D), F32)
            for e in range(0, E, group):
                starts = [pl.multiple_of(win_ref[b, sub, e + k], BF16_SUBLANES) for k in range(group)]
                place = jnp.concatenate(
                    [jnp.where(rt[:, e + k:e + k + 1] - starts[k] == slot, 1.0, 0.0).astype(BF16)
                     for k in range(group)], axis=1)
                y = jnp.concatenate([y_ref[e + k, 0, pl.ds(starts[k], SCAT_WINDOW), :] for k in range(group)],
                                    axis=0)
                moe = moe + jnp.dot(place, y, preferred_element_type=F32)
            finish(moe)

        @pl.when(ok_ref[b, sub] == 0)
        def _():
            slot = lax.broadcasted_iota(jnp.int32, (ts, cap), 1)
            place = jnp.concatenate(
                [jnp.where(rt[:, e:e + 1] == slot, 1.0, 0.0).astype(BF16) for e in range(E)], axis=1)
            y = y_ref[0:E, 0].reshape(E * cap, D)
            finish(jnp.dot(place, y, preferred_element_type=F32))


def _scat_windows(rank, cap, tm):
    B, E, S = rank.shape
    counts = jnp.sum((rank >= 0).reshape(B, E, S // tm, tm), axis=-1, dtype=jnp.int32)
    ends = jnp.cumsum(counts, axis=-1)
    start = ends - counts
    win = jnp.minimum(start // BF16_SUBLANES * BF16_SUBLANES, cap - SCAT_WINDOW)
    ok = jnp.all(ends - win <= SCAT_WINDOW, axis=1)
    return jnp.swapaxes(win, 1, 2), ok.astype(jnp.int32)


def _scat(rank, ybuf, x1, mod, g, cap):
    B, S, D = x1.shape
    E = rank.shape[1]
    tm = TOKEN_TILE
    win, ok = _scat_windows(rank, cap, SCAT_TOKENS)
    smem = pl.BlockSpec(memory_space=pltpu.SMEM)
    return pl.pallas_call(
        functools.partial(_scat_kernel, cap=cap),
        grid=(B, S // tm),
        in_specs=[smem, smem,
                  pl.BlockSpec((1, tm, E), lambda b, i: (b, i, 0)),
                  pl.BlockSpec(ybuf.shape[:1] + (1, cap, D), lambda b, i: (0, b, 0, 0)),
                  pl.BlockSpec((1, tm, D), lambda b, i: (b, i, 0)),
                  pl.BlockSpec((1, 6, D), lambda b, i: (b, 0, 0)),
                  _resident((1, D), lambda b, i: (0, 0))],
        out_specs=pl.BlockSpec((1, tm, D), lambda b, i: (b, i, 0)),
        out_shape=jax.ShapeDtypeStruct((B, S, D), F32),
        compiler_params=_cparams(("arbitrary", "arbitrary")),
        name="scat",
    )(win, ok, jnp.swapaxes(rank, 1, 2), ybuf, x1, mod, g)


def _rot_tables(S):
    half = RET_DK // 2
    inv = 1.0 / (RET_THETA_BASE ** np.linspace(0.0, 1.0, half))
    ang = np.arange(S, dtype=np.float64)[:, None] * inv[None, :]
    cos = np.repeat(np.cos(ang), 2, axis=1)
    sin = np.stack([-np.sin(ang), np.sin(ang)], axis=2).reshape(S, RET_DK)
    sc = RET_DK ** -0.5
    return jnp.asarray(np.stack([cos, sin, cos * sc, sin * sc]), F32)


def _ret_tables():
    C = RET_CHUNK
    heads = np.arange(RET_HEADS, dtype=np.float64)
    lgf = np.log1p(-np.exp2(-RET_FWD_DECAY_OFFSET - heads))[:, None]
    lgb = np.log1p(-np.exp2(-RET_BWD_DECAY_OFFSET - heads))[:, None]
    idx = np.arange(C, dtype=np.float64)
    diff = idx[:, None] - idx[None, :]
    dmat = np.where(diff >= 0,
                    np.exp(np.maximum(diff, 0.0)[None] * lgf[:, :, None]),
                    np.exp(np.maximum(-diff, 0.0)[None] * lgb[:, :, None]))
    dec = np.stack([np.exp((idx + 1)[None, :] * lgf),
                    np.exp((C - 1 - idx)[None, :] * lgf),
                    np.exp((C - idx)[None, :] * lgb),
                    np.exp(idx[None, :] * lgb)], axis=1)
    dec = np.broadcast_to(dec[..., None], dec.shape + (LANES,))
    cd = np.concatenate([np.exp(C * lgf), np.exp(C * lgb)], axis=1)
    return jnp.asarray(cd, F32), jnp.asarray(dec, F32), jnp.asarray(dmat, F32)


def _t5_bucket(rel):
    nb = N_BUCKETS // 2
    max_exact = nb // 2
    ret = (rel > 0).astype(jnp.int32) * nb
    n = jnp.abs(rel)
    large = max_exact + (jnp.log(jnp.maximum(n, 1).astype(F32) / max_exact)
                         / math.log(MAX_DISTANCE / max_exact) * (nb - max_exact)).astype(jnp.int32)
    large = jnp.minimum(large, nb - 1)
    return ret + jnp.where(n < max_exact, n, large)


def _bias_table(rel_bias, S):
    TQ, M = Q_TILE, MAX_DISTANCE
    reach = TQ + M - 1
    rel = jnp.concatenate([jnp.arange(-reach, reach + 1, dtype=jnp.int32),
                           jnp.array([-(S - 1), S - 1], jnp.int32)])
    f = rel_bias[_t5_bucket(rel)].astype(F32).T * LOG2E
    H = f.shape[0]
    lo, hi = f[:, -2], f[:, -1]
    L = 2 * reach + 1
    u = jnp.concatenate([f[:, :L], jnp.zeros((H, 1), F32)], axis=1)
    shifted = jnp.tile(u, (1, TQ))[:, :TQ * L].reshape(H, TQ, L)
    band = jnp.swapaxes(shifted[:, :, TQ - 1:TQ - 1 + TQ + 2 * M], 1, 2)
    side = S - TQ - M
    table = jnp.concatenate([jnp.broadcast_to(lo[:, None, None], (H, side, TQ)), band,
                             jnp.broadcast_to(hi[:, None, None], (H, side, TQ))], axis=1)
    return table, jnp.max(f, axis=1)


def kernel(x, c, w_ada, b_ada, norm_mix_g, w_in, ret_gn_g, diff_subln_g, lambda_q1, lambda_k1, lambda_q2, lambda_k2, w_ret_out, w_diff_out, w_o, rel_bias, norm_ffn_g, w_router, w_exp_gate, w_exp_up, w_exp_down, final_g):
    B, S, D = x.shape
    cap = CAPACITY_FACTOR * S // N_EXPERTS
    l = 0

    mod = _ada(c, w_ada[l], b_ada[l]).reshape(B, 6, D)
    proj, vt = _inproj(x, norm_mix_g[l].reshape(1, D), mod, w_in[l].astype(BF16), _rot_tables(S))

    cd, dec, dmat = _ret_tables()
    y_ret = _ret(proj, cd, dec, dmat, ret_gn_g[l].reshape(1, -1))

    lam_vecs = jnp.stack([lambda_q1[l], lambda_k1[l], lambda_q2[l], lambda_k2[l]]).astype(F32)
    ttab, bmax = _bias_table(rel_bias, S)
    subln = jnp.broadcast_to(diff_subln_g[l].astype(F32)[:, None], (DIFF_DV, Q_TILE))
    y_diff = _diff(proj, vt, lam_vecs, ttab, bmax, subln)

    x1, h2, logits = _merge(x, y_ret, y_diff, proj, mod,
                            w_ret_out[l].astype(BF16), w_diff_out[l].astype(BF16), w_o[l].astype(BF16),
                            norm_ffn_g[l].reshape(1, D), w_router[l].T.astype(BF16))

    rank, gate = _route(logits, cap)
    xin = _gather_rows(h2.reshape(B * S, D // 2), rank, cap)
    ybuf = _moe(rank, gate, xin, w_exp_gate[l], w_exp_up[l], w_exp_down[l], cap)
    return _scat(rank, ybuf, x1, mod, final_g.reshape(1, D), cap)
```

```python
import functools
import math

import numpy as np
import jax
import jax.numpy as jnp
from jax import lax
from jax.experimental import pallas as pl
from jax.experimental.pallas import tpu as pltpu
from jax.experimental.pallas import tpu_sc as plsc

F32 = jnp.float32
BF16 = jnp.bfloat16

RET_HEADS = 4
RET_DK = 128
RET_DV = 128
RET_FWD_DECAY_OFFSET = 5.0
RET_BWD_DECAY_OFFSET = 5.5
RET_THETA_BASE = 10000.0
DIFF_HEADS = 4
DIFF_DH = 64
DIFF_DV = 2 * DIFF_DH
N_BUCKETS = 32
MAX_DISTANCE = 128
N_EXPERTS = 16
CAPACITY_FACTOR = 2
NORM_EPS = 1e-6
LAM_INIT = 0.8 - 0.6 * math.exp(-0.3 * 0)
LOG2E = math.log2(math.e)

LANES = 128
BF16_SUBLANES = 16
VMEM_LIMIT_BYTES = 56 * 2**20

SC_CORES = 2
SC_SUBCORES = 16
SC_LANES = 16
SC_GATHER_ROWS = 64

TOKEN_TILE = 512
PROJ_TILE = 1024
RET_CHUNK = 256
Q_TILE = 256
KEY_CHUNK = 512
L_MIN = 2.0 ** -60
SCAT_WINDOW = 128
MOE_SEQS_PER_STEP = 4

W_RQ, W_RK, W_RV, W_RG = 0, 4, 8, 12
W_DQ1, W_DQ2, W_DK1, W_DK2, W_DV = 16, 18, 20, 22, 24
W_GR, W_GD = 28, 36

COL_GR, COL_GD = 0, 8
COL_RQ, COL_RK, COL_RV, COL_RG = 16, 20, 24, 28
COL_DQ, COL_DK = 32, 36
PROJ_COLS = 40 * LANES


def _cparams(sem):
    return pltpu.CompilerParams(dimension_semantics=sem, vmem_limit_bytes=VMEM_LIMIT_BYTES)


def _resident(shape, index_map):
    return pl.BlockSpec(shape, index_map, pipeline_mode=pl.Buffered(1))


def _ada_kernel(c_ref, w_ref, b_ref, o_ref):
    c = c_ref[...]
    a = c * jax.nn.sigmoid(c)
    o_ref[...] = jnp.dot(a, w_ref[...], preferred_element_type=F32,
                         precision=lax.Precision.HIGHEST) + b_ref[...]


def _ada(c, w, b):
    B, D = c.shape
    n = w.shape[1] // D
    return pl.pallas_call(
        _ada_kernel,
        grid=(n,),
        in_specs=[pl.BlockSpec((B, D), lambda j: (0, 0)),
                  pl.BlockSpec((D, D), lambda j: (0, j)),
                  pl.BlockSpec((1, D), lambda j: (0, j))],
        out_specs=pl.BlockSpec((B, D), lambda j: (0, j)),
        out_shape=jax.ShapeDtypeStruct((B, n * D), F32),
        compiler_params=_cparams(("arbitrary",)),
        name="ada",
    )(c, w, b.reshape(1, -1))


def _rms(x, g):
    ms = jnp.mean(x * x, axis=-1, keepdims=True)
    return x * lax.rsqrt(ms + NORM_EPS) * g


def _rot_pairs(x, cos, sin):
    even = (lax.broadcasted_iota(jnp.int32, x.shape, 1) & 1) == 0
    partner = jnp.where(even, pltpu.roll(x, LANES - 1, axis=1), pltpu.roll(x, 1, axis=1))
    return x * cos + partner * sin


def _pair_maps(a, b, second):
    low = lax.broadcasted_iota(jnp.int32, a.shape, 1) < DIFF_DH
    if second:
        return jnp.where(low, pltpu.roll(a, DIFF_DH, axis=1), b)
    return jnp.where(low, a, pltpu.roll(b, DIFF_DH, axis=1))


def _inproj_kernel(x_ref, g_ref, mod_ref, w_ref, rot_ref, o_ref, vt_ref, wvt_s):
    @pl.when((pl.program_id(0) == 0) & (pl.program_id(1) == 0))
    def _():
        wvt_s[...] = w_ref[:, W_DV * LANES:(W_DV + 4) * LANES].astype(F32).T.astype(BF16)

    x = x_ref[0]
    h = _rms(x, g_ref[...]) * (1.0 + mod_ref[0, 1:2, :]) + mod_ref[0, 0:1, :]
    hb = h.astype(BF16)

    def mm(col, width):
        return jnp.dot(hb, w_ref[:, col * LANES:(col + width) * LANES], preferred_element_type=F32)

    def put(col, val):
        o_ref[0, :, col * LANES:col * LANES + val.shape[1]] = val.astype(BF16)

    for j in range(0, COL_GD - COL_GR, 4):
        put(COL_GR + j, jax.nn.sigmoid(mm(W_GR + j, 4)))
        put(COL_GD + j, jax.nn.sigmoid(mm(W_GD + j, 4)))
    for src, dst, t in ((W_RQ, COL_RQ, 0), (W_RK, COL_RK, 2)):
        cos, sin = rot_ref[t], rot_ref[t + 1]
        r = mm(src, RET_HEADS)
        for hh in range(RET_HEADS):
            put(dst + hh, _rot_pairs(r[:, hh * LANES:(hh + 1) * LANES], cos, sin))
    put(COL_RV, mm(W_RV, 4))
    r = mm(W_RG, 4)
    put(COL_RG, r * jax.nn.sigmoid(r))
    for src1, src2, dst, scale in ((W_DQ1, W_DQ2, COL_DQ, DIFF_DH ** -0.5 * LOG2E), (W_DK1, W_DK2, COL_DK, None)):
        m1, m2 = mm(src1, 2), mm(src2, 2)
        for hh in range(DIFF_HEADS):
            blk = slice((hh // 2) * LANES, (hh // 2 + 1) * LANES)
            val = _pair_maps(m1[:, blk], m2[:, blk], hh % 2 == 1)
            put(dst + hh, val if scale is None else val * scale)
    vt_ref[0] = lax.dot_general(wvt_s[...], hb, (((1,), (1,)), ((), ())),
                                preferred_element_type=F32).astype(BF16)


def _inproj(x, g, mod, w, rot):
    B, S, D = x.shape
    tm = PROJ_TILE
    vw = DIFF_HEADS * DIFF_DV
    return pl.pallas_call(
        _inproj_kernel,
        grid=(B, S // tm),
        in_specs=[pl.BlockSpec((1, tm, D), lambda b, i: (b, i, 0)),
                  _resident((1, D), lambda b, i: (0, 0)),
                  pl.BlockSpec((1, 6, D), lambda b, i: (b, 0, 0)),
                  _resident(w.shape, lambda b, i: (0, 0)),
                  pl.BlockSpec((4, tm, LANES), lambda b, i: (0, i, 0))],
        out_specs=[pl.BlockSpec((1, tm, PROJ_COLS), lambda b, i: (b, i, 0)),
                   pl.BlockSpec((1, vw, tm), lambda b, i: (b, 0, i))],
        out_shape=[jax.ShapeDtypeStruct((B, S, PROJ_COLS), BF16),
                   jax.ShapeDtypeStruct((B, vw, S), BF16)],
        scratch_shapes=[pltpu.VMEM((vw, D), BF16)],
        compiler_params=_cparams(("arbitrary", "arbitrary")),
        name="inproj",
    )(x, g, mod, w, rot)


def _ret_kernel(cd_ref, q_ref, k_ref, v_ref, rg_ref, dec_ref, dm_ref, gn_ref, o_ref, acc_ref):
    hh = pl.program_id(1)
    S = q_ref.shape[1]
    C = RET_CHUNK
    nc = S // C
    dmat = dm_ref[0]
    qdf, kdf, qdb, kdb = dec_ref[0, 0], dec_ref[0, 1], dec_ref[0, 2], dec_ref[0, 3]
    nt = (((1,), (1,)), ((), ()))
    tn = (((0,), (0,)), ((), ()))

    def chunk(n):
        sl = pl.ds(n * C, C)
        return sl, q_ref[0, sl, :], k_ref[0, sl, :], v_ref[0, sl, :]

    def scaled(t, dec):
        return (t.astype(F32) * dec).astype(BF16)

    state = jnp.zeros((RET_DK, RET_DV), F32)
    for n in range(nc):
        sl, q, k, v = chunk(n)
        s = lax.dot_general(q, k, nt, preferred_element_type=F32) * dmat
        inner = jnp.dot(s.astype(BF16), v, preferred_element_type=F32)
        cross = jnp.dot(scaled(q, qdf), state.astype(BF16), preferred_element_type=F32)
        acc_ref[sl, :] = inner + cross
        kv = lax.dot_general(scaled(k, kdf), v, tn, preferred_element_type=F32)
        state = cd_ref[hh, 0] * state + kv
    state = jnp.zeros((RET_DK, RET_DV), F32)
    for n in reversed(range(nc)):
        sl, q, k, v = chunk(n)
        cross = jnp.dot(scaled(q, qdb), state.astype(BF16), preferred_element_type=F32)
        acc_ref[sl, :] = acc_ref[sl, :] + cross
        kv = lax.dot_general(scaled(k, kdb), v, tn, preferred_element_type=F32)
        state = cd_ref[hh, 1] * state + kv

    y = acc_ref[...]
    mu = jnp.mean(y, axis=-1, keepdims=True)
    yc = y - mu
    var = jnp.mean(yc * yc, axis=-1, keepdims=True)
    yn = yc * lax.rsqrt(var + NORM_EPS) * gn_ref[...]
    o_ref[0] = (rg_ref[0].astype(F32) * yn).astype(BF16)


def _ret(proj, cd, dec, dmat, gn):
    B, S, _ = proj.shape
    H = RET_HEADS
    C = RET_CHUNK

    def col(base):
        return pl.BlockSpec((1, S, LANES), lambda b, h: (b, 0, base + h))

    return pl.pallas_call(
        _ret_kernel,
        grid=(B, H),
        in_specs=[pl.BlockSpec(memory_space=pltpu.SMEM),
                  col(COL_RQ), col(COL_RK), col(COL_RV), col(COL_RG),
                  pl.BlockSpec((1, 4, C, LANES), lambda b, h: (h, 0, 0, 0)),
                  pl.BlockSpec((1, C, C), lambda b, h: (h, 0, 0)),
                  pl.BlockSpec((1, LANES), lambda b, h: (0, h))],
        out_specs=pl.BlockSpec((1, S, LANES), lambda b, h: (b, 0, h)),
        out_shape=jax.ShapeDtypeStruct((B, S, H * RET_DV), BF16),
        scratch_shapes=[pltpu.VMEM((S, RET_DV), F32)],
        compiler_params=_cparams(("arbitrary", "arbitrary")),
        name="ret",
    )(cd, proj, proj, proj, proj, dec, dmat, gn)


def _diff_kernel(bmax_ref, lam_ref, q_ref, k_ref, vt_ref, t_ref, g_ref, o_ref, p_ref):
    hh = pl.program_id(0)
    S = k_ref.shape[1]
    TQ, KB = Q_TILE, KEY_CHUNK
    nk = S // KB
    lv = lam_ref[...]
    lam = (jnp.exp(jnp.sum(lv[0:1] * lv[1:2], axis=1, keepdims=True))
           - jnp.exp(jnp.sum(lv[2:3] * lv[3:4], axis=1, keepdims=True)) + LAM_INIT)
    first_half = lax.broadcasted_iota(jnp.int32, (TQ, LANES), 1) < DIFF_DH
    nt = (((1,), (1,)), ((), ()))
    bmax = bmax_ref[hh]
    kf = k_ref[0].astype(F32)
    kmax = jnp.sqrt(jnp.max(jnp.sum(kf * kf, axis=1, keepdims=True), axis=0, keepdims=True))
    sel_r = lax.broadcasted_iota(jnp.int32, (8, LANES), 0)
    sel_c = lax.broadcasted_iota(jnp.int32, (8, LANES), 1)
    sel = jnp.where((sel_c < DIFF_DH) == (sel_r == 0), 1.0, 0.0)
    sel = jnp.where(sel_r < 2, sel, 0.0).astype(BF16)

    def masked_q(qb):
        q = q_ref[0, pl.ds(pl.multiple_of(qb * TQ, TQ), TQ), :]
        zero = jnp.zeros_like(q)
        return q, (jnp.where(first_half, q, zero), jnp.where(first_half, zero, q))

    def finish(ot, qb):
        ms = jnp.mean(ot * ot, axis=0, keepdims=True)
        y = ot * lax.rsqrt(ms + NORM_EPS) * g_ref[...] * (1.0 - LAM_INIT)
        o_ref[0, pl.ds(pl.multiple_of(qb * TQ, TQ), TQ), :] = y.T.astype(BF16)

    def exponentials(qb, slot):
        w0 = pl.multiple_of((S - TQ) - qb * TQ, TQ)
        q, qz = masked_q(qb)
        qsq = (q.astype(F32) * q.astype(F32)).astype(BF16)
        n2 = lax.dot_general(sel, qsq, nt, preferred_element_type=F32) * 1.01
        m = [jnp.sqrt(n2[i:i + 1]) * kmax + bmax for i in range(2)]
        l = [jnp.zeros((8, TQ), F32) for _ in range(2)]
        for c in range(nk):
            rows = slice(c * KB, (c + 1) * KB)
            kb = k_ref[0, rows, :]
            bias = t_ref[0, pl.ds(w0 + c * KB, KB), :]
            for i in range(2):
                s = lax.dot_general(kb, qz[i], nt, preferred_element_type=F32)
                p = jnp.exp2((s - m[i]) + bias)
                l[i] = l[i] + jnp.sum(p.reshape(KB // 8, 8, TQ), axis=0)
                p_ref[slot, i, rows, :] = p.astype(BF16)
        return [jnp.sum(l[i], axis=0, keepdims=True) for i in range(2)]

    def values(qb, slot, lr):
        ratio = jnp.broadcast_to(lam * lr[0] / lr[1], (16, TQ)).astype(BF16)
        ot = jnp.zeros((DIFF_DV, TQ), F32)
        for c in range(nk):
            rows = slice(c * KB, (c + 1) * KB)
            p1 = p_ref[slot, 0, rows, :].reshape(KB // 16, 16, TQ)
            p2 = p_ref[slot, 1, rows, :].reshape(KB // 16, 16, TQ)
            a = (p1 - ratio * p2).reshape(KB, TQ)
            ot = ot + jnp.dot(vt_ref[0, :, rows], a, preferred_element_type=F32)
        finish(ot * (1.0 / lr[0]), qb)

    def fast_pair(j, lmin):
        la = exponentials(2 * j, 0)
        lb = exponentials(2 * j + 1, 1)
        values(2 * j, 0, la)
        values(2 * j + 1, 1, lb)
        return jnp.minimum(lmin, jnp.minimum(jnp.minimum(la[0], la[1]), jnp.minimum(lb[0], lb[1])))

    lmin = lax.fori_loop(0, S // (2 * TQ), fast_pair, jnp.full((1, TQ), jnp.inf, F32))

    @pl.when(jnp.min(lmin) < L_MIN)
    def _():
        def exact_tile(qb, carry):
            w0 = pl.multiple_of((S - TQ) - qb * TQ, TQ)
            _, qz = masked_q(qb)
            bias = t_ref[0, pl.ds(w0, S), :]
            attn = None
            for i, scale in ((0, 1.0), (1, lam)):
                s = lax.dot_general(k_ref[0], qz[i], nt, preferred_element_type=F32) + bias
                p = jnp.exp2(s - jnp.max(s, axis=0, keepdims=True))
                part = p * (scale / jnp.sum(p, axis=0, keepdims=True))
                attn = part if attn is None else attn - part
            finish(jnp.dot(vt_ref[0], attn.astype(BF16), preferred_element_type=F32), qb)
            return carry

        lax.fori_loop(0, S // TQ, exact_tile, 0)


def _diff(proj, vt, lam_vecs, ttab, bmax, g):
    B, S, _ = proj.shape
    H = DIFF_HEADS

    def col(base):
        return pl.BlockSpec((1, S, LANES), lambda h, b: (b, 0, base + h))

    return pl.pallas_call(
        _diff_kernel,
        grid=(H, B),
        in_specs=[pl.BlockSpec(memory_space=pltpu.SMEM),
                  pl.BlockSpec((4, DIFF_DH), lambda h, b: (0, 0)),
                  col(COL_DQ), col(COL_DK),
                  pl.BlockSpec((1, DIFF_DV, S), lambda h, b: (b, h, 0)),
                  pl.BlockSpec((1, 2 * S - Q_TILE, Q_TILE), lambda h, b: (h, 0, 0)),
                  pl.BlockSpec((DIFF_DV, Q_TILE), lambda h, b: (0, 0))],
        out_specs=pl.BlockSpec((1, S, LANES), lambda h, b: (b, 0, h)),
        out_shape=jax.ShapeDtypeStruct((B, S, H * DIFF_DV), BF16),
        scratch_shapes=[pltpu.VMEM((2, 2, S, Q_TILE), BF16)],
        compiler_params=_cparams(("arbitrary", "arbitrary")),
        name="diff",
    )(bmax, lam_vecs, proj, proj, vt, ttab, g)


def _pack_halves(x):
    w = x.shape[1] // 2
    lo = pltpu.bitcast(x[:, :w].astype(BF16).astype(F32), jnp.int32)
    hi = pltpu.bitcast(x[:, w:].astype(BF16).astype(F32), jnp.int32)
    return lax.shift_right_logical(lo, jnp.full_like(lo, 16)) | (hi & jnp.int32(-65536))


def _unpack_halves(words):
    lo = pltpu.bitcast(lax.shift_left(words, jnp.full_like(words, 16)), F32)
    hi = pltpu.bitcast(words & jnp.int32(-65536), F32)
    return jnp.concatenate([lo, hi], axis=1).astype(BF16)


def _merge_kernel(x_ref, yr_ref, yd_ref, gr_ref, gd_ref, mod_ref, wr_ref, wd_ref, wo_ref,
                  g_ref, wrt_ref, x1_ref, h2_ref, lg_ref):
    parts = 4
    step = x_ref.shape[1] // parts

    def project(r):
        rows = slice(r * step, (r + 1) * step)
        a = jnp.dot(yr_ref[0, rows, :], wr_ref[...], preferred_element_type=F32)
        d = jnp.dot(yd_ref[0, rows, :], wd_ref[...], preferred_element_type=F32)
        merged = gr_ref[0, rows, :].astype(F32) * a + gd_ref[0, rows, :].astype(F32) * d
        return jnp.dot(merged.astype(BF16), wo_ref[...], preferred_element_type=F32)

    def epilogue(r, o):
        rows = slice(r * step, (r + 1) * step)
        x1 = x_ref[0, rows, :] + mod_ref[0, 2:3, :] * o
        x1_ref[0, rows, :] = x1
        h2 = _rms(x1, g_ref[...]) * (1.0 + mod_ref[0, 4:5, :]) + mod_ref[0, 3:4, :]
        h2_ref[0, rows, :] = _pack_halves(h2)
        lg_ref[0, :, rows] = lax.dot_general(wrt_ref[...], h2.astype(BF16), (((1,), (1,)), ((), ())),
                                             preferred_element_type=F32)

    o = project(0)
    for r in range(parts):
        nxt = project(r + 1) if r + 1 < parts else None
        epilogue(r, o)
        o = nxt


def _merge(x, yr, yd, proj, mod, wr, wd, wo, g, wrt):
    B, S, D = x.shape
    tm = PROJ_TILE
    E = wrt.shape[0]
    gate_w = COL_GD - COL_GR
    return pl.pallas_call(
        _merge_kernel,
        grid=(B, S // tm),
        in_specs=[pl.BlockSpec((1, tm, D), lambda b, i: (b, i, 0)),
                  pl.BlockSpec((1, tm, yr.shape[2]), lambda b, i: (b, i, 0)),
                  pl.BlockSpec((1, tm, yd.shape[2]), lambda b, i: (b, i, 0)),
                  pl.BlockSpec((1, tm, gate_w * LANES), lambda b, i: (b, i, COL_GR // gate_w)),
                  pl.BlockSpec((1, tm, gate_w * LANES), lambda b, i: (b, i, COL_GD // gate_w)),
                  pl.BlockSpec((1, 6, D), lambda b, i: (b, 0, 0)),
                  _resident(wr.shape, lambda b, i: (0, 0)),
                  _resident(wd.shape, lambda b, i: (0, 0)),
                  _resident(wo.shape, lambda b, i: (0, 0)),
                  _resident((1, D), lambda b, i: (0, 0)),
                  _resident(wrt.shape, lambda b, i: (0, 0))],
        out_specs=[pl.BlockSpec((1, tm, D), lambda b, i: (b, i, 0)),
                   pl.BlockSpec((1, tm, D // 2), lambda b, i: (b, i, 0)),
                   pl.BlockSpec((1, E, tm), lambda b, i: (b, 0, i))],
        out_shape=[jax.ShapeDtypeStruct((B, S, D), F32),
                   jax.ShapeDtypeStruct((B, S, D // 2), jnp.int32),
                   jax.ShapeDtypeStruct((B, E, S), F32)],
        compiler_params=_cparams(("arbitrary", "arbitrary")),
        name="merge",
    )(x, yr, yd, proj, proj, mod, wr, wd, wo, g, wrt)


def _lane_prefix(m, tri):
    E, S = m.shape
    off = jnp.zeros((E, 1), F32)
    parts = []
    for j in range(S // LANES):
        blk = m[:, j * LANES:(j + 1) * LANES]
        parts.append(jnp.dot(blk.astype(BF16), tri, preferred_element_type=F32) + off)
        off = off + jnp.sum(blk, axis=1, keepdims=True)
    return jnp.concatenate(parts, axis=1)


def _route_kernel(lg_ref, rank_ref, gate_ref, *, cap):
    lg = lg_ref[0]
    e = jnp.exp(lg - jnp.max(lg, axis=0, keepdims=True))
    aff = e / jnp.sum(e, axis=0, keepdims=True)
    bits = pltpu.bitcast(aff, jnp.int32)
    E = lg.shape[0]

    def count(mask):
        return jnp.sum(jnp.where(mask, 1.0, 0.0), axis=1, keepdims=True)

    def reaches(cand):
        return count(bits >= cand) >= cap

    thr = jnp.zeros((E, 1), jnp.int32)
    for hi in range(30, 0, -2):
        c1, c2 = thr | (1 << hi), thr | (1 << (hi - 1))
        c3 = c1 | (1 << (hi - 1))
        thr = jnp.where(reaches(c3), c3, jnp.where(reaches(c1), c1, jnp.where(reaches(c2), c2, thr)))
    thr = jnp.where(reaches(thr | 1), thr | 1, thr)
    gt = bits > thr
    eq = bits == thr
    need = cap - count(gt)
    r = lax.broadcasted_iota(jnp.int32, (LANES, LANES), 0)
    c = lax.broadcasted_iota(jnp.int32, (LANES, LANES), 1)
    tri = jnp.where(r < c, 1.0, 0.0).astype(BF16)
    eq_before = _lane_prefix(jnp.where(eq, 1.0, 0.0), tri)
    sel = gt | (eq & (eq_before < need))
    slot = _lane_prefix(jnp.where(sel, 1.0, 0.0), tri)
    rank_ref[0] = jnp.where(sel, slot, -1.0).astype(jnp.int32)
    gate_ref[0] = jnp.where(sel, aff, 0.0)


def _route(logits, cap):
    B, E, S = logits.shape
    spec = pl.BlockSpec((1, E, S), lambda b: (b, 0, 0))
    return pl.pallas_call(
        functools.partial(_route_kernel, cap=cap),
        grid=(B,),
        in_specs=[spec],
        out_specs=[spec, spec],
        out_shape=[jax.ShapeDtypeStruct((B, E, S), jnp.int32),
                   jax.ShapeDtypeStruct((B, E, S), F32)],
        compiler_params=_cparams(("arbitrary",)),
        name="route",
    )(logits)


def _gather_rows(table, rank, cap):
    B, E, S = rank.shape
    W = table.shape[1]
    workers = SC_CORES * SC_SUBCORES
    pairs = E * B
    per = pairs // workers
    assert per * workers == pairs and B & (B - 1) == 0 and cap % SC_GATHER_ROWS == 0 and S % SC_LANES == 0
    shift = B.bit_length() - 1
    mesh = plsc.VectorSubcoreMesh(core_axis_name="c", subcore_axis_name="s")

    R = SC_GATHER_ROWS
    n_chunks = cap // R

    def body(table_hbm, rank_hbm, out_hbm, rank_v, idx_v, buf_v, gsem, wsem):
        wid = lax.axis_index("s") * SC_CORES + lax.axis_index("c")

        def fetch(c):
            return pltpu.make_async_copy(table_hbm.at[idx_v.at[pl.ds(c * R, R)]], buf_v.at[c % 2], gsem.at[c % 2])

        def flush(p, c):
            return pltpu.make_async_copy(buf_v.at[c % 2], out_hbm.at[pl.ds(p * cap + c * R, R)], wsem.at[c % 2])

        for j in range(per):
            p = wid * per + j
            e = lax.shift_right_logical(p, shift)
            b = p & (B - 1)
            pltpu.sync_copy(rank_hbm.at[b, e], rank_v)
            base = b * S

            @pl.loop(0, S, step=SC_LANES)
            def _(t0):
                r = rank_v[pl.ds(t0, SC_LANES)]
                tok = lax.iota(jnp.int32, SC_LANES) + (base + t0)
                plsc.store_scatter(idx_v, [r], tok, mask=r >= 0)

            fetch(0).start()
            for c in range(n_chunks):
                fetch(c).wait()
                if c >= 1:
                    flush(p, c - 1).wait()
                if c + 1 < n_chunks:
                    fetch(c + 1).start()
                flush(p, c).start()
            flush(p, n_chunks - 1).wait()

    return pl.kernel(
        body,
        out_type=jax.ShapeDtypeStruct((pairs * cap, W), jnp.int32),
        mesh=mesh,
        scratch_types=[pltpu.VMEM((S,), jnp.int32), pltpu.VMEM((cap,), jnp.int32),
                       pltpu.VMEM((2, R, W), jnp.int32),
                       pltpu.SemaphoreType.DMA((2,)), pltpu.SemaphoreType.DMA((2,))],
        compiler_params=pltpu.CompilerParams(needs_layout_passes=False),
    )(table, rank)


def _moe_kernel(rank_ref, gate_ref, x_ref, wg_ref, wu_ref, wd_ref, o_ref, wg_s, wu_s, wd_s, *, cap):
    r, b = pl.program_id(0), pl.program_id(1)
    n_exp = pl.num_programs(0) - 1

    @pl.when(r < n_exp)
    def _():
        slot = r % 2
        rows_in, rows_ff = wg_ref.shape[1], wd_ref.shape[1]
        wg_s[slot, pl.ds(pl.multiple_of(b * rows_in, rows_in), rows_in), :] = wg_ref[0].astype(BF16)
        wu_s[slot, pl.ds(pl.multiple_of(b * rows_in, rows_in), rows_in), :] = wu_ref[0].astype(BF16)
        wd_s[slot, pl.ds(pl.multiple_of(b * rows_ff, rows_ff), rows_ff), :] = wd_ref[0].astype(BF16)

    @pl.when(r == 0)
    def _():
        o_ref[...] = jnp.zeros_like(o_ref)

    @pl.when(r > 0)
    def _():
        slot = (r - 1) % 2
        S = rank_ref.shape[3]
        for j in range(x_ref.shape[0]):
            rank = rank_ref[j, 0]
            pick = lax.broadcasted_iota(jnp.int32, (cap, S), 0) == rank
            xin = _unpack_halves(x_ref[j])
            a = jnp.dot(xin, wg_s[slot], preferred_element_type=F32)
            u = jnp.dot(xin, wu_s[slot], preferred_element_type=F32)
            act = (a * jax.nn.sigmoid(a) * u).astype(BF16)
            y = jnp.dot(act, wd_s[slot], preferred_element_type=F32)
            g = jnp.sum(jnp.where(pick, gate_ref[j, 0], 0.0), axis=1, keepdims=True)
            o_ref[0, j] = (y * g).astype(BF16)


def _moe(rank, gate, xin, wg, wu, wd, cap):
    B, E, S = rank.shape
    D, Fd = wg.shape[1], wg.shape[2]
    G = MOE_SEQS_PER_STEP
    steps = B // G
    assert steps * G == B and D % steps == 0 and Fd % steps == 0

    def cur(r):
        return jnp.maximum(r - 1, 0)

    def nxt(r):
        return jnp.minimum(r, E - 1)

    row = pl.BlockSpec((G, 1, 1, S), lambda r, b: (b, cur(r), 0, 0))
    return pl.pallas_call(
        functools.partial(_moe_kernel, cap=cap),
        grid=(E + 1, steps),
        in_specs=[row, row,
                  pl.BlockSpec((G, cap, D // 2), lambda r, b: (cur(r) * steps + b, 0, 0)),
                  pl.BlockSpec((1, D // steps, Fd), lambda r, b: (nxt(r), b, 0)),
                  pl.BlockSpec((1, D // steps, Fd), lambda r, b: (nxt(r), b, 0)),
                  pl.BlockSpec((1, Fd // steps, D), lambda r, b: (nxt(r), b, 0))],
        out_specs=pl.BlockSpec((1, G, cap, D), lambda r, b: (jnp.where(r == 0, E, r - 1), b, 0, 0)),
        out_shape=jax.ShapeDtypeStruct((E + 1, B, cap, D), BF16),
        scratch_shapes=[pltpu.VMEM((2, D, Fd), BF16), pltpu.VMEM((2, D, Fd), BF16),
                        pltpu.VMEM((2, Fd, D), BF16)],
        compiler_params=_cparams(("arbitrary", "arbitrary")),
        name="moe",
    )(rank.reshape(B, E, 1, S), gate.reshape(B, E, 1, S), xin.reshape(E * B, cap, D // 2), wg, wu, wd)


def _scat_kernel(win_ref, ok_ref, rt_ref, y_ref, x1_ref, mod_ref, g_ref, o_ref, *, cap):
    b, i = pl.program_id(0), pl.program_id(1)
    rt = rt_ref[0]
    tm, E = rt.shape
    D = y_ref.shape[3]

    def finish(moe):
        x2 = x1_ref[0] + mod_ref[0, 5:6, :] * moe
        o_ref[0] = _rms(x2, g_ref[...])

    @pl.when(ok_ref[b, i] != 0)
    def _():
        slot = lax.broadcasted_iota(jnp.int32, (tm, SCAT_WINDOW), 1)
        moe = jnp.zeros((tm, D), F32)
        for e in range(0, E, 2):
            starts = [pl.multiple_of(win_ref[b, i, e + k], BF16_SUBLANES) for k in range(2)]
            place = jnp.concatenate(
                [jnp.where(rt[:, e + k:e + k + 1] - starts[k] == slot, 1.0, 0.0).astype(BF16)
                 for k in range(2)], axis=1)
            y = jnp.concatenate([y_ref[e + k, 0, pl.ds(starts[k], SCAT_WINDOW), :] for k in range(2)], axis=0)
            moe = moe + jnp.dot(place, y, preferred_element_type=F32)
        finish(moe)

    @pl.when(ok_ref[b, i] == 0)
    def _():
        slot = lax.broadcasted_iota(jnp.int32, (tm, cap), 1)
        place = jnp.concatenate(
            [jnp.where(rt[:, e:e + 1] == slot, 1.0, 0.0).astype(BF16) for e in range(E)], axis=1)
        y = y_ref[0:E, 0].reshape(E * cap, D)
        finish(jnp.dot(place, y, preferred_element_type=F32))


def _scat_windows(rank, cap, tm):
    B, E, S = rank.shape
    counts = jnp.sum((rank >= 0).reshape(B, E, S // tm, tm), axis=-1, dtype=jnp.int32)
    ends = jnp.cumsum(counts, axis=-1)
    start = ends - counts
    win = jnp.minimum(start // BF16_SUBLANES * BF16_SUBLANES, cap - SCAT_WINDOW)
    ok = jnp.all(ends - win <= SCAT_WINDOW, axis=1)
    return jnp.swapaxes(win, 1, 2), ok.astype(jnp.int32)


def _scat(rank, ybuf, x1, mod, g, cap):
    B, S, D = x1.shape
    E = rank.shape[1]
    tm = TOKEN_TILE
    win, ok = _scat_windows(rank, cap, tm)
    smem = pl.BlockSpec(memory_space=pltpu.SMEM)
    return pl.pallas_call(
        functools.partial(_scat_kernel, cap=cap),
        grid=(B, S // tm),
        in_specs=[smem, smem,
                  pl.BlockSpec((1, tm, E), lambda b, i: (b, i, 0)),
                  pl.BlockSpec(ybuf.shape[:1] + (1, cap, D), lambda b, i: (0, b, 0, 0)),
                  pl.BlockSpec((1, tm, D), lambda b, i: (b, i, 0)),
                  pl.BlockSpec((1, 6, D), lambda b, i: (b, 0, 0)),
                  _resident((1, D), lambda b, i: (0, 0))],
        out_specs=pl.BlockSpec((1, tm, D), lambda b, i: (b, i, 0)),
        out_shape=jax.ShapeDtypeStruct((B, S, D), F32),
        compiler_params=_cparams(("arbitrary", "arbitrary")),
        name="scat",
    )(win, ok, jnp.swapaxes(rank, 1, 2), ybuf, x1, mod, g)


def _rot_tables(S):
    half = RET_DK // 2
    inv = 1.0 / (RET_THETA_BASE ** np.linspace(0.0, 1.0, half))
    ang = np.arange(S, dtype=np.float64)[:, None] * inv[None, :]
    cos = np.repeat(np.cos(ang), 2, axis=1)
    sin = np.stack([-np.sin(ang), np.sin(ang)], axis=2).reshape(S, RET_DK)
    sc = RET_DK ** -0.5
    return jnp.asarray(np.stack([cos, sin, cos * sc, sin * sc]), F32)


def _ret_tables():
    C = RET_CHUNK
    heads = np.arange(RET_HEADS, dtype=np.float64)
    lgf = np.log1p(-np.exp2(-RET_FWD_DECAY_OFFSET - heads))[:, None]
    lgb = np.log1p(-np.exp2(-RET_BWD_DECAY_OFFSET - heads))[:, None]
    idx = np.arange(C, dtype=np.float64)
    diff = idx[:, None] - idx[None, :]
    dmat = np.where(diff >= 0,
                    np.exp(np.maximum(diff, 0.0)[None] * lgf[:, :, None]),
                    np.exp(np.maximum(-diff, 0.0)[None] * lgb[:, :, None]))
    dec = np.stack([np.exp((idx + 1)[None, :] * lgf),
                    np.exp((C - 1 - idx)[None, :] * lgf),
                    np.exp((C - idx)[None, :] * lgb),
                    np.exp(idx[None, :] * lgb)], axis=1)
    dec = np.broadcast_to(dec[..., None], dec.shape + (LANES,))
    cd = np.concatenate([np.exp(C * lgf), np.exp(C * lgb)], axis=1)
    return jnp.asarray(cd, F32), jnp.asarray(dec, F32), jnp.asarray(dmat, F32)


def _t5_bucket(rel):
    nb = N_BUCKETS // 2
    max_exact = nb // 2
    ret = (rel > 0).astype(jnp.int32) * nb
    n = jnp.abs(rel)
    large = max_exact + (jnp.log(jnp.maximum(n, 1).astype(F32) / max_exact)
                         / math.log(MAX_DISTANCE / max_exact) * (nb - max_exact)).astype(jnp.int32)
    large = jnp.minimum(large, nb - 1)
    return ret + jnp.where(n < max_exact, n, large)


def _bias_table(rel_bias, S):
    TQ, M = Q_TILE, MAX_DISTANCE
    reach = TQ + M - 1
    rel = jnp.concatenate([jnp.arange(-reach, reach + 1, dtype=jnp.int32),
                           jnp.array([-(S - 1), S - 1], jnp.int32)])
    f = rel_bias[_t5_bucket(rel)].astype(F32).T * LOG2E
    H = f.shape[0]
    lo, hi = f[:, -2], f[:, -1]
    L = 2 * reach + 1
    u = jnp.concatenate([f[:, :L], jnp.zeros((H, 1), F32)], axis=1)
    shifted = jnp.tile(u, (1, TQ))[:, :TQ * L].reshape(H, TQ, L)
    band = jnp.swapaxes(shifted[:, :, TQ - 1:TQ - 1 + TQ + 2 * M], 1, 2)
    side = S - TQ - M
    table = jnp.concatenate([jnp.broadcast_to(lo[:, None, None], (H, side, TQ)), band,
                             jnp.broadcast_to(hi[:, None, None], (H, side, TQ))], axis=1)
    return table, jnp.max(f, axis=1)


def kernel(x, c, w_ada, b_ada, norm_mix_g, w_in, ret_gn_g, diff_subln_g, lambda_q1, lambda_k1, lambda_q2, lambda_k2, w_ret_out, w_diff_out, w_o, rel_bias, norm_ffn_g, w_router, w_exp_gate, w_exp_up, w_exp_down, final_g):
    B, S, D = x.shape
    cap = CAPACITY_FACTOR * S // N_EXPERTS
    l = 0

    mod = _ada(c, w_ada[l], b_ada[l]).reshape(B, 6, D)
    proj, vt = _inproj(x, norm_mix_g[l].reshape(1, D), mod, w_in[l].astype(BF16), _rot_tables(S))

    cd, dec, dmat = _ret_tables()
    y_ret = _ret(proj, cd, dec, dmat, ret_gn_g[l].reshape(1, -1))

    lam_vecs = jnp.stack([lambda_q1[l], lambda_k1[l], lambda_q2[l], lambda_k2[l]]).astype(F32)
    ttab, bmax = _bias_table(rel_bias, S)
    subln = jnp.broadcast_to(diff_subln_g[l].astype(F32)[:, None], (DIFF_DV, Q_TILE))
    y_diff = _diff(proj, vt, lam_vecs, ttab, bmax, subln)

    x1, h2, logits = _merge(x, y_ret, y_diff, proj, mod,
                            w_ret_out[l].astype(BF16), w_diff_out[l].astype(BF16), w_o[l].astype(BF16),
                            norm_ffn_g[l].reshape(1, D), w_router[l].T.astype(BF16))

    rank, gate = _route(logits, cap)
    xin = _gather_rows(h2.reshape(B * S, D // 2), rank, cap)
    ybuf = _moe(rank, gate, xin, w_exp_gate[l], w_exp_up[l], w_exp_down[l], cap)
    return _scat(rank, ybuf, x1, mod, final_g.reshape(1, D), cap)
```

```python
import functools
import math

import numpy as np
import jax
import jax.numpy as jnp
from jax import lax
from jax.experimental import pallas as pl
from jax.experimental.pallas import tpu as pltpu
from jax.experimental.pallas import tpu_sc as plsc

F32 = jnp.float32
BF16 = jnp.bfloat16

RET_HEADS = 4
RET_DK = 128
RET_DV = 128
RET_FWD_DECAY_OFFSET = 5.0
RET_BWD_DECAY_OFFSET = 5.5
RET_THETA_BASE = 10000.0
DIFF_HEADS = 4
DIFF_DH = 64
DIFF_DV = 2 * DIFF_DH
N_BUCKETS = 32
MAX_DISTANCE = 128
N_EXPERTS = 16
CAPACITY_FACTOR = 2
NORM_EPS = 1e-6
LAM_INIT = 0.8 - 0.6 * math.exp(-0.3 * 0)
LOG2E = math.log2(math.e)

LANES = 128
BF16_SUBLANES = 16
VMEM_LIMIT_BYTES = 56 * 2**20

SC_CORES = 2
SC_SUBCORES = 16
SC_LANES = 16
SC_GATHER_ROWS = 64

TOKEN_TILE = 512
PROJ_TILE = 1024
RET_CHUNK = 256
Q_TILE = 256
KEY_CHUNK = 512
L_MIN = 2.0 ** -60
SCAT_WINDOW = 128
MOE_SEQS_PER_STEP = 4

W_RQ, W_RK, W_RV, W_RG = 0, 4, 8, 12
W_DQ1, W_DQ2, W_DK1, W_DK2, W_DV = 16, 18, 20, 22, 24
W_GR, W_GD = 28, 36

COL_GR, COL_GD = 0, 8
COL_RQ, COL_RK, COL_RV, COL_RG = 16, 20, 24, 28
COL_DQ, COL_DK = 32, 36
PROJ_COLS = 40 * LANES


def _cparams(sem):
    return pltpu.CompilerParams(dimension_semantics=sem, vmem_limit_bytes=VMEM_LIMIT_BYTES)


def _resident(shape, index_map):
    return pl.BlockSpec(shape, index_map, pipeline_mode=pl.Buffered(1))


def _ada_kernel(c_ref, w_ref, b_ref, o_ref):
    c = c_ref[...]
    a = c * jax.nn.sigmoid(c)
    o_ref[...] = jnp.dot(a, w_ref[...], preferred_element_type=F32,
                         precision=lax.Precision.HIGHEST) + b_ref[...]


def _ada(c, w, b):
    B, D = c.shape
    n = w.shape[1] // D
    return pl.pallas_call(
        _ada_kernel,
        grid=(n,),
        in_specs=[pl.BlockSpec((B, D), lambda j: (0, 0)),
                  pl.BlockSpec((D, D), lambda j: (0, j)),
                  pl.BlockSpec((1, D), lambda j: (0, j))],
        out_specs=pl.BlockSpec((B, D), lambda j: (0, j)),
        out_shape=jax.ShapeDtypeStruct((B, n * D), F32),
        compiler_params=_cparams(("arbitrary",)),
        name="ada",
    )(c, w, b.reshape(1, -1))


def _rms(x, g):
    ms = jnp.mean(x * x, axis=-1, keepdims=True)
    return x * lax.rsqrt(ms + NORM_EPS) * g


def _rot_pairs(x, cos, sin):
    even = (lax.broadcasted_iota(jnp.int32, x.shape, 1) & 1) == 0
    partner = jnp.where(even, pltpu.roll(x, LANES - 1, axis=1), pltpu.roll(x, 1, axis=1))
    return x * cos + partner * sin


def _pair_maps(a, b, second):
    low = lax.broadcasted_iota(jnp.int32, a.shape, 1) < DIFF_DH
    if second:
        return jnp.where(low, pltpu.roll(a, DIFF_DH, axis=1), b)
    return jnp.where(low, a, pltpu.roll(b, DIFF_DH, axis=1))


def _inproj_kernel(x_ref, g_ref, mod_ref, w_ref, rot_ref, o_ref, vt_ref, wvt_s):
    @pl.when((pl.program_id(0) == 0) & (pl.program_id(1) == 0))
    def _():
        wvt_s[...] = w_ref[:, W_DV * LANES:(W_DV + 4) * LANES].astype(F32).T.astype(BF16)

    parts = 4
    step = x_ref.shape[1] // parts

    def normed(p):
        x = x_ref[0, p * step:(p + 1) * step, :]
        h = _rms(x, g_ref[...]) * (1.0 + mod_ref[0, 1:2, :]) + mod_ref[0, 0:1, :]
        return h.astype(BF16)

    def project(p, hb):
        rows = slice(p * step, (p + 1) * step)

        def mm(col, width):
            return jnp.dot(hb, w_ref[:, col * LANES:(col + width) * LANES], preferred_element_type=F32)

        def put(col, val):
            o_ref[0, rows, col * LANES:col * LANES + val.shape[1]] = val.astype(BF16)

        for j in range(0, COL_GD - COL_GR, 4):
            put(COL_GR + j, jax.nn.sigmoid(mm(W_GR + j, 4)))
            put(COL_GD + j, jax.nn.sigmoid(mm(W_GD + j, 4)))
        for src, dst, t in ((W_RQ, COL_RQ, 0), (W_RK, COL_RK, 2)):
            cos, sin = rot_ref[t, rows, :], rot_ref[t + 1, rows, :]
            r = mm(src, RET_HEADS)
            for hh in range(RET_HEADS):
                put(dst + hh, _rot_pairs(r[:, hh * LANES:(hh + 1) * LANES], cos, sin))
        put(COL_RV, mm(W_RV, 4))
        r = mm(W_RG, 4)
        put(COL_RG, r * jax.nn.sigmoid(r))
        for src1, src2, dst, scale in ((W_DQ1, W_DQ2, COL_DQ, DIFF_DH ** -0.5 * LOG2E),
                                       (W_DK1, W_DK2, COL_DK, None)):
            m1, m2 = mm(src1, 2), mm(src2, 2)
            for hh in range(DIFF_HEADS):
                blk = slice((hh // 2) * LANES, (hh // 2 + 1) * LANES)
                val = _pair_maps(m1[:, blk], m2[:, blk], hh % 2 == 1)
                put(dst + hh, val if scale is None else val * scale)
        vt_ref[0, :, rows] = lax.dot_general(wvt_s[...], hb, (((1,), (1,)), ((), ())),
                                             preferred_element_type=F32).astype(BF16)

    hb = normed(0)
    for p in range(parts):
        nxt = normed(p + 1) if p + 1 < parts else None
        project(p, hb)
        hb = nxt


def _inproj(x, g, mod, w, rot):
    B, S, D = x.shape
    tm = PROJ_TILE
    vw = DIFF_HEADS * DIFF_DV
    return pl.pallas_call(
        _inproj_kernel,
        grid=(B, S // tm),
        in_specs=[pl.BlockSpec((1, tm, D), lambda b, i: (b, i, 0)),
                  _resident((1, D), lambda b, i: (0, 0)),
                  pl.BlockSpec((1, 6, D), lambda b, i: (b, 0, 0)),
                  _resident(w.shape, lambda b, i: (0, 0)),
                  pl.BlockSpec((4, tm, LANES), lambda b, i: (0, i, 0))],
        out_specs=[pl.BlockSpec((1, tm, PROJ_COLS), lambda b, i: (b, i, 0)),
                   pl.BlockSpec((1, vw, tm), lambda b, i: (b, 0, i))],
        out_shape=[jax.ShapeDtypeStruct((B, S, PROJ_COLS), BF16),
                   jax.ShapeDtypeStruct((B, vw, S), BF16)],
        scratch_shapes=[pltpu.VMEM((vw, D), BF16)],
        compiler_params=_cparams(("arbitrary", "arbitrary")),
        name="inproj",
    )(x, g, mod, w, rot)


def _ret_kernel(cd_ref, q_ref, k_ref, v_ref, rg_ref, dec_ref, dm_ref, gn_ref, o_ref, acc_ref):
    S = q_ref.shape[1]
    H = dm_ref.shape[0]
    C = RET_CHUNK
    nc = S // C
    nt = (((1,), (1,)), ((), ()))
    tn = (((0,), (0,)), ((), ()))

    def chunk(n, h):
        sl, lanes = pl.ds(n * C, C), slice(h * LANES, (h + 1) * LANES)
        return sl, q_ref[0, sl, lanes], k_ref[0, sl, lanes], v_ref[0, sl, lanes]

    def scaled(t, dec):
        return (t.astype(F32) * dec).astype(BF16)

    def forward(heads):
        state = {h: jnp.zeros((RET_DK, RET_DV), F32) for h in heads}
        for n in range(nc):
            for h in heads:
                sl, q, k, v = chunk(n, h)
                s = lax.dot_general(q, k, nt, preferred_element_type=F32) * dm_ref[h]
                inner = jnp.dot(s.astype(BF16), v, preferred_element_type=F32)
                cross = jnp.dot(scaled(q, dec_ref[h, 0]), state[h].astype(BF16), preferred_element_type=F32)
                acc_ref[h, sl, :] = inner + cross
                kv = lax.dot_general(scaled(k, dec_ref[h, 1]), v, tn, preferred_element_type=F32)
                state[h] = cd_ref[h, 0] * state[h] + kv

    def backward(heads):
        state = {h: jnp.zeros((RET_DK, RET_DV), F32) for h in heads}
        for n in reversed(range(nc)):
            for h in heads:
                sl, q, k, v = chunk(n, h)
                cross = jnp.dot(scaled(q, dec_ref[h, 2]), state[h].astype(BF16), preferred_element_type=F32)
                acc_ref[h, sl, :] = acc_ref[h, sl, :] + cross
                kv = lax.dot_general(scaled(k, dec_ref[h, 3]), v, tn, preferred_element_type=F32)
                state[h] = cd_ref[h, 1] * state[h] + kv

    def normalise(heads):
        for h in heads:
            lanes = slice(h * LANES, (h + 1) * LANES)
            y = acc_ref[h]
            mu = jnp.mean(y, axis=-1, keepdims=True)
            yc = y - mu
            var = jnp.mean(yc * yc, axis=-1, keepdims=True)
            yn = yc * lax.rsqrt(var + NORM_EPS) * gn_ref[:, lanes]
            o_ref[0, :, lanes] = (rg_ref[0, :, lanes].astype(F32) * yn).astype(BF16)

    first, second = tuple(range(H // 2)), tuple(range(H // 2, H))
    forward(first)
    backward(first)
    forward(second)
    normalise(first)
    backward(second)
    normalise(second)


def _ret(proj, cd, dec, dmat, gn):
    B, S, _ = proj.shape
    H = RET_HEADS
    C = RET_CHUNK

    def cols(base):
        return pl.BlockSpec((1, S, H * LANES), lambda b: (b, 0, base // H))

    return pl.pallas_call(
        _ret_kernel,
        grid=(B,),
        in_specs=[pl.BlockSpec(memory_space=pltpu.SMEM),
                  cols(COL_RQ), cols(COL_RK), cols(COL_RV), cols(COL_RG),
                  _resident((H, 4, C, LANES), lambda b: (0, 0, 0, 0)),
                  _resident((H, C, C), lambda b: (0, 0, 0)),
                  _resident((1, H * LANES), lambda b: (0, 0))],
        out_specs=pl.BlockSpec((1, S, H * LANES), lambda b: (b, 0, 0)),
        out_shape=jax.ShapeDtypeStruct((B, S, H * RET_DV), BF16),
        scratch_shapes=[pltpu.VMEM((H, S, RET_DV), F32)],
        compiler_params=_cparams(("arbitrary",)),
        name="ret",
    )(cd, proj, proj, proj, proj, dec, dmat, gn)


def _diff_kernel(bmax_ref, lam_ref, q_ref, k_ref, vt_ref, t_ref, g_ref, o_ref, p_ref):
    hh = pl.program_id(0)
    S = k_ref.shape[1]
    TQ, KB = Q_TILE, KEY_CHUNK
    nk = S // KB
    lv = lam_ref[...]
    lam = (jnp.exp(jnp.sum(lv[0:1] * lv[1:2], axis=1, keepdims=True))
           - jnp.exp(jnp.sum(lv[2:3] * lv[3:4], axis=1, keepdims=True)) + LAM_INIT)
    first_half = lax.broadcasted_iota(jnp.int32, (TQ, LANES), 1) < DIFF_DH
    nt = (((1,), (1,)), ((), ()))
    bmax = bmax_ref[hh]
    kf = k_ref[0].astype(F32)
    kmax = jnp.sqrt(jnp.max(jnp.sum(kf * kf, axis=1, keepdims=True), axis=0, keepdims=True))
    sel_r = lax.broadcasted_iota(jnp.int32, (8, LANES), 0)
    sel_c = lax.broadcasted_iota(jnp.int32, (8, LANES), 1)
    sel = jnp.where((sel_c < DIFF_DH) == (sel_r == 0), 1.0, 0.0)
    sel = jnp.where(sel_r < 2, sel, 0.0).astype(BF16)

    def masked_q(qb):
        q = q_ref[0, pl.ds(pl.multiple_of(qb * TQ, TQ), TQ), :]
        zero = jnp.zeros_like(q)
        return q, (jnp.where(first_half, q, zero), jnp.where(first_half, zero, q))

    def finish(ot, qb):
        ms = jnp.mean(ot * ot, axis=0, keepdims=True)
        y = ot * lax.rsqrt(ms + NORM_EPS) * g_ref[...] * (1.0 - LAM_INIT)
        o_ref[0, pl.ds(pl.multiple_of(qb * TQ, TQ), TQ), :] = y.T.astype(BF16)

    def exponentials(qb, slot):
        w0 = pl.multiple_of((S - TQ) - qb * TQ, TQ)
        q, qz = masked_q(qb)
        qsq = (q.astype(F32) * q.astype(F32)).astype(BF16)
        n2 = lax.dot_general(sel, qsq, nt, preferred_element_type=F32) * 1.01
        m = [jnp.sqrt(n2[i:i + 1]) * kmax + bmax for i in range(2)]
        l = [jnp.zeros((8, TQ), F32) for _ in range(2)]
        for c in range(nk):
            rows = slice(c * KB, (c + 1) * KB)
            kb = k_ref[0, rows, :]
            bias = t_ref[0, pl.ds(w0 + c * KB, KB), :]
            for i in range(2):
                s = lax.dot_general(kb, qz[i], nt, preferred_element_type=F32)
                p = jnp.exp2((s - m[i]) + bias)
                l[i] = l[i] + jnp.sum(p.reshape(KB // 8, 8, TQ), axis=0)
                p_ref[slot, i, rows, :] = p.astype(BF16)
        return [jnp.sum(l[i], axis=0, keepdims=True) for i in range(2)]

    def values(qb, slot, lr):
        ratio = jnp.broadcast_to(lam * lr[0] / lr[1], (16, TQ)).astype(BF16)
        ot = jnp.zeros((DIFF_DV, TQ), F32)
        for c in range(nk):
            rows = slice(c * KB, (c + 1) * KB)
            p1 = p_ref[slot, 0, rows, :].reshape(KB // 16, 16, TQ)
            p2 = p_ref[slot, 1, rows, :].reshape(KB // 16, 16, TQ)
            a = (p1 - ratio * p2).reshape(KB, TQ)
            ot = ot + jnp.dot(vt_ref[0, :, rows], a, preferred_element_type=F32)
        finish(ot * (1.0 / lr[0]), qb)

    def fast_pair(j, lmin):
        la = exponentials(2 * j, 0)
        lb = exponentials(2 * j + 1, 1)
        values(2 * j, 0, la)
        values(2 * j + 1, 1, lb)
        return jnp.minimum(lmin, jnp.minimum(jnp.minimum(la[0], la[1]), jnp.minimum(lb[0], lb[1])))

    lmin = lax.fori_loop(0, S // (2 * TQ), fast_pair, jnp.full((1, TQ), jnp.inf, F32))

    @pl.when(jnp.min(lmin) < L_MIN)
    def _():
        def exact_tile(qb, carry):
            w0 = pl.multiple_of((S - TQ) - qb * TQ, TQ)
            _, qz = masked_q(qb)
            bias = t_ref[0, pl.ds(w0, S), :]
            attn = None
            for i, scale in ((0, 1.0), (1, lam)):
                s = lax.dot_general(k_ref[0], qz[i], nt, preferred_element_type=F32) + bias
                p = jnp.exp2(s - jnp.max(s, axis=0, keepdims=True))
                part = p * (scale / jnp.sum(p, axis=0, keepdims=True))
                attn = part if attn is None else attn - part
            finish(jnp.dot(vt_ref[0], attn.astype(BF16), preferred_element_type=F32), qb)
            return carry

        lax.fori_loop(0, S // TQ, exact_tile, 0)


def _diff(proj, vt, lam_vecs, ttab, bmax, g):
    B, S, _ = proj.shape
    H = DIFF_HEADS

    def col(base):
        return pl.BlockSpec((1, S, LANES), lambda h, b: (b, 0, base + h))

    return pl.pallas_call(
        _diff_kernel,
        grid=(H, B),
        in_specs=[pl.BlockSpec(memory_space=pltpu.SMEM),
                  pl.BlockSpec((4, DIFF_DH), lambda h, b: (0, 0)),
                  col(COL_DQ), col(COL_DK),
                  pl.BlockSpec((1, DIFF_DV, S), lambda h, b: (b, h, 0)),
                  pl.BlockSpec((1, 2 * S - Q_TILE, Q_TILE), lambda h, b: (h, 0, 0)),
                  pl.BlockSpec((DIFF_DV, Q_TILE), lambda h, b: (0, 0))],
        out_specs=pl.BlockSpec((1, S, LANES), lambda h, b: (b, 0, h)),
        out_shape=jax.ShapeDtypeStruct((B, S, H * DIFF_DV), BF16),
        scratch_shapes=[pltpu.VMEM((2, 2, S, Q_TILE), BF16)],
        compiler_params=_cparams(("arbitrary", "arbitrary")),
        name="diff",
    )(bmax, lam_vecs, proj, proj, vt, ttab, g)


def _pack_halves(x):
    w = x.shape[1] // 2
    lo = pltpu.bitcast(x[:, :w].astype(BF16).astype(F32), jnp.int32)
    hi = pltpu.bitcast(x[:, w:].astype(BF16).astype(F32), jnp.int32)
    return lax.shift_right_logical(lo, jnp.full_like(lo, 16)) | (hi & jnp.int32(-65536))


def _unpack_halves(words):
    lo = pltpu.bitcast(lax.shift_left(words, jnp.full_like(words, 16)), F32)
    hi = pltpu.bitcast(words & jnp.int32(-65536), F32)
    return jnp.concatenate([lo, hi], axis=1).astype(BF16)


def _merge_kernel(x_ref, yr_ref, yd_ref, gr_ref, gd_ref, mod_ref, wr_ref, wd_ref, wo_ref,
                  g_ref, wrt_ref, x1_ref, h2_ref, lg_ref):
    parts = 4
    step = x_ref.shape[1] // parts

    def project(r):
        rows = slice(r * step, (r + 1) * step)
        a = jnp.dot(yr_ref[0, rows, :], wr_ref[...], preferred_element_type=F32)
        d = jnp.dot(yd_ref[0, rows, :], wd_ref[...], preferred_element_type=F32)
        merged = gr_ref[0, rows, :].astype(F32) * a + gd_ref[0, rows, :].astype(F32) * d
        return jnp.dot(merged.astype(BF16), wo_ref[...], preferred_element_type=F32)

    def epilogue(r, o):
        rows = slice(r * step, (r + 1) * step)
        x1 = x_ref[0, rows, :] + mod_ref[0, 2:3, :] * o
        x1_ref[0, rows, :] = x1
        h2 = _rms(x1, g_ref[...]) * (1.0 + mod_ref[0, 4:5, :]) + mod_ref[0, 3:4, :]
        h2_ref[0, rows, :] = _pack_halves(h2)
        lg_ref[0, :, rows] = lax.dot_general(wrt_ref[...], h2.astype(BF16), (((1,), (1,)), ((), ())),
                                             preferred_element_type=F32)

    o = project(0)
    for r in range(parts):
        nxt = project(r + 1) if r + 1 < parts else None
        epilogue(r, o)
        o = nxt


def _merge(x, yr, yd, proj, mod, wr, wd, wo, g, wrt):
    B, S, D = x.shape
    tm = PROJ_TILE
    E = wrt.shape[0]
    gate_w = COL_GD - COL_GR
    return pl.pallas_call(
        _merge_kernel,
        grid=(B, S // tm),
        in_specs=[pl.BlockSpec((1, tm, D), lambda b, i: (b, i, 0)),
                  pl.BlockSpec((1, tm, yr.shape[2]), lambda b, i: (b, i, 0)),
                  pl.BlockSpec((1, tm, yd.shape[2]), lambda b, i: (b, i, 0)),
                  pl.BlockSpec((1, tm, gate_w * LANES), lambda b, i: (b, i, COL_GR // gate_w)),
                  pl.BlockSpec((1, tm, gate_w * LANES), lambda b, i: (b, i, COL_GD // gate_w)),
                  pl.BlockSpec((1, 6, D), lambda b, i: (b, 0, 0)),
                  _resident(wr.shape, lambda b, i: (0, 0)),
                  _resident(wd.shape, lambda b, i: (0, 0)),
                  _resident(wo.shape, lambda b, i: (0, 0)),
                  _resident((1, D), lambda b, i: (0, 0)),
                  _resident(wrt.shape, lambda b, i: (0, 0))],
        out_specs=[pl.BlockSpec((1, tm, D), lambda b, i: (b, i, 0)),
                   pl.BlockSpec((1, tm, D // 2), lambda b, i: (b, i, 0)),
                   pl.BlockSpec((1, E, tm), lambda b, i: (b, 0, i))],
        out_shape=[jax.ShapeDtypeStruct((B, S, D), F32),
                   jax.ShapeDtypeStruct((B, S, D // 2), jnp.int32),
                   jax.ShapeDtypeStruct((B, E, S), F32)],
        compiler_params=_cparams(("arbitrary", "arbitrary")),
        name="merge",
    )(x, yr, yd, proj, proj, mod, wr, wd, wo, g, wrt)


def _lane_prefix(m, tri):
    E, S = m.shape
    off = jnp.zeros((E, 1), F32)
    parts = []
    for j in range(S // LANES):
        blk = m[:, j * LANES:(j + 1) * LANES]
        parts.append(jnp.dot(blk.astype(BF16), tri, preferred_element_type=F32) + off)
        off = off + jnp.sum(blk, axis=1, keepdims=True)
    return jnp.concatenate(parts, axis=1)


def _route_kernel(lg_ref, rank_ref, gate_ref, *, cap):
    lg = lg_ref[0]
    e = jnp.exp(lg - jnp.max(lg, axis=0, keepdims=True))
    aff = e / jnp.sum(e, axis=0, keepdims=True)
    bits = pltpu.bitcast(aff, jnp.int32)
    E = lg.shape[0]

    def count(mask):
        return jnp.sum(jnp.where(mask, 1.0, 0.0), axis=1, keepdims=True)

    def reaches(cand):
        return count(bits >= cand) >= cap

    thr = jnp.zeros((E, 1), jnp.int32)
    for hi in range(30, 0, -2):
        c1, c2 = thr | (1 << hi), thr | (1 << (hi - 1))
        c3 = c1 | (1 << (hi - 1))
        thr = jnp.where(reaches(c3), c3, jnp.where(reaches(c1), c1, jnp.where(reaches(c2), c2, thr)))
    thr = jnp.where(reaches(thr | 1), thr | 1, thr)
    gt = bits > thr
    eq = bits == thr
    need = cap - count(gt)
    r = lax.broadcasted_iota(jnp.int32, (LANES, LANES), 0)
    c = lax.broadcasted_iota(jnp.int32, (LANES, LANES), 1)
    tri = jnp.where(r < c, 1.0, 0.0).astype(BF16)
    eq_before = _lane_prefix(jnp.where(eq, 1.0, 0.0), tri)
    sel = gt | (eq & (eq_before < need))
    slot = _lane_prefix(jnp.where(sel, 1.0, 0.0), tri)
    rank_ref[0] = jnp.where(sel, slot, -1.0).astype(jnp.int32)
    gate_ref[0] = jnp.where(sel, aff, 0.0)


def _route(logits, cap):
    B, E, S = logits.shape
    spec = pl.BlockSpec((1, E, S), lambda b: (b, 0, 0))
    return pl.pallas_call(
        functools.partial(_route_kernel, cap=cap),
        grid=(B,),
        in_specs=[spec],
        out_specs=[spec, spec],
        out_shape=[jax.ShapeDtypeStruct((B, E, S), jnp.int32),
                   jax.ShapeDtypeStruct((B, E, S), F32)],
        compiler_params=_cparams(("arbitrary",)),
        name="route",
    )(logits)


def _gather_rows(table, rank, cap):
    B, E, S = rank.shape
    W = table.shape[1]
    workers = SC_CORES * SC_SUBCORES
    pairs = E * B
    per = pairs // workers
    assert per * workers == pairs and B & (B - 1) == 0 and cap % SC_GATHER_ROWS == 0 and S % SC_LANES == 0
    shift = B.bit_length() - 1
    mesh = plsc.VectorSubcoreMesh(core_axis_name="c", subcore_axis_name="s")

    R = SC_GATHER_ROWS
    n_chunks = cap // R

    def body(table_hbm, rank_hbm, out_hbm, rank_v, idx_v, buf_v, gsem, wsem):
        wid = lax.axis_index("s") * SC_CORES + lax.axis_index("c")

        def fetch(c):
            return pltpu.make_async_copy(table_hbm.at[idx_v.at[pl.ds(c * R, R)]], buf_v.at[c % 2], gsem.at[c % 2])

        def flush(p, c):
            return pltpu.make_async_copy(buf_v.at[c % 2], out_hbm.at[pl.ds(p * cap + c * R, R)], wsem.at[c % 2])

        for j in range(per):
            p = wid * per + j
            e = lax.shift_right_logical(p, shift)
            b = p & (B - 1)
            pltpu.sync_copy(rank_hbm.at[b, e], rank_v)
            base = b * S

            @pl.loop(0, S, step=SC_LANES)
            def _(t0):
                r = rank_v[pl.ds(t0, SC_LANES)]
                tok = lax.iota(jnp.int32, SC_LANES) + (base + t0)
                plsc.store_scatter(idx_v, [r], tok, mask=r >= 0)

            fetch(0).start()
            for c in range(n_chunks):
                fetch(c).wait()
                if c >= 1:
                    flush(p, c - 1).wait()
                if c + 1 < n_chunks:
                    fetch(c + 1).start()
                flush(p, c).start()
            flush(p, n_chunks - 1).wait()

    return pl.kernel(
        body,
        out_type=jax.ShapeDtypeStruct((pairs * cap, W), jnp.int32),
        mesh=mesh,
        scratch_types=[pltpu.VMEM((S,), jnp.int32), pltpu.VMEM((cap,), jnp.int32),
                       pltpu.VMEM((2, R, W), jnp.int32),
                       pltpu.SemaphoreType.DMA((2,)), pltpu.SemaphoreType.DMA((2,))],
        compiler_params=pltpu.CompilerParams(needs_layout_passes=False),
    )(table, rank)


def _moe_kernel(rank_ref, gate_ref, x_ref, wg_ref, wu_ref, wd_ref, o_ref, wg_s, wu_s, wd_s, *, cap):
    r, b = pl.program_id(0), pl.program_id(1)
    n_exp = pl.num_programs(0) - 1

    @pl.when(r < n_exp)
    def _():
        slot = r % 2
        rows_in, rows_ff = wg_ref.shape[1], wd_ref.shape[1]
        wg_s[slot, pl.ds(pl.multiple_of(b * rows_in, rows_in), rows_in), :] = wg_ref[0].astype(BF16)
        wu_s[slot, pl.ds(pl.multiple_of(b * rows_in, rows_in), rows_in), :] = wu_ref[0].astype(BF16)
        wd_s[slot, pl.ds(pl.multiple_of(b * rows_ff, rows_ff), rows_ff), :] = wd_ref[0].astype(BF16)

    @pl.when(r == 0)
    def _():
        o_ref[...] = jnp.zeros_like(o_ref)

    @pl.when(r > 0)
    def _():
        slot = (r - 1) % 2
        S = rank_ref.shape[3]
        for j in range(x_ref.shape[0]):
            rank = rank_ref[j, 0]
            pick = lax.broadcasted_iota(jnp.int32, (cap, S), 0) == rank
            xin = _unpack_halves(x_ref[j])
            a = jnp.dot(xin, wg_s[slot], preferred_element_type=F32)
            u = jnp.dot(xin, wu_s[slot], preferred_element_type=F32)
            act = (a * jax.nn.sigmoid(a) * u).astype(BF16)
            y = jnp.dot(act, wd_s[slot], preferred_element_type=F32)
            g = jnp.sum(jnp.where(pick, gate_ref[j, 0], 0.0), axis=1, keepdims=True)
            o_ref[0, j] = (y * g).astype(BF16)


def _moe(rank, gate, xin, wg, wu, wd, cap):
    B, E, S = rank.shape
    D, Fd = wg.shape[1], wg.shape[2]
    G = MOE_SEQS_PER_STEP
    steps = B // G
    assert steps * G == B and D % steps == 0 and Fd % steps == 0

    def cur(r):
        return jnp.maximum(r - 1, 0)

    def nxt(r):
        return jnp.minimum(r, E - 1)

    row = pl.BlockSpec((G, 1, 1, S), lambda r, b: (b, cur(r), 0, 0))
    return pl.pallas_call(
        functools.partial(_moe_kernel, cap=cap),
        grid=(E + 1, steps),
        in_specs=[row, row,
                  pl.BlockSpec((G, cap, D // 2), lambda r, b: (cur(r) * steps + b, 0, 0)),
                  pl.BlockSpec((1, D // steps, Fd), lambda r, b: (nxt(r), b, 0)),
                  pl.BlockSpec((1, D // steps, Fd), lambda r, b: (nxt(r), b, 0)),
                  pl.BlockSpec((1, Fd // steps, D), lambda r, b: (nxt(r), b, 0))],
        out_specs=pl.BlockSpec((1, G, cap, D), lambda r, b: (jnp.where(r == 0, E, r - 1), b, 0, 0)),
        out_shape=jax.ShapeDtypeStruct((E + 1, B, cap, D), BF16),
        scratch_shapes=[pltpu.VMEM((2, D, Fd), BF16), pltpu.VMEM((2, D, Fd), BF16),
                        pltpu.VMEM((2, Fd, D), BF16)],
        compiler_params=_cparams(("arbitrary", "arbitrary")),
        name="moe",
    )(rank.reshape(B, E, 1, S), gate.reshape(B, E, 1, S), xin.reshape(E * B, cap, D // 2), wg, wu, wd)


def _scat_kernel(win_ref, ok_ref, rt_ref, y_ref, x1_ref, mod_ref, g_ref, o_ref, *, cap):
    b, i = pl.program_id(0), pl.program_id(1)
    rt = rt_ref[0]
    tm, E = rt.shape
    D = y_ref.shape[3]

    def finish(moe):
        x2 = x1_ref[0] + mod_ref[0, 5:6, :] * moe
        o_ref[0] = _rms(x2, g_ref[...])

    @pl.when(ok_ref[b, i] != 0)
    def _():
        slot = lax.broadcasted_iota(jnp.int32, (tm, SCAT_WINDOW), 1)
        moe = jnp.zeros((tm, D), F32)
        for e in range(0, E, 2):
            starts = [pl.multiple_of(win_ref[b, i, e + k], BF16_SUBLANES) for k in range(2)]
            place = jnp.concatenate(
                [jnp.where(rt[:, e + k:e + k + 1] - starts[k] == slot, 1.0, 0.0).astype(BF16)
                 for k in range(2)], axis=1)
            y = jnp.concatenate([y_ref[e + k, 0, pl.ds(starts[k], SCAT_WINDOW), :] for k in range(2)], axis=0)
            moe = moe + jnp.dot(place, y, preferred_element_type=F32)
        finish(moe)

    @pl.when(ok_ref[b, i] == 0)
    def _():
        slot = lax.broadcasted_iota(jnp.int32, (tm, cap), 1)
        place = jnp.concatenate(
            [jnp.where(rt[:, e:e + 1] == slot, 1.0, 0.0).astype(BF16) for e in range(E)], axis=1)
        y = y_ref[0:E, 0].reshape(E * cap, D)
        finish(jnp.dot(place, y, preferred_element_type=F32))


def _scat_windows(rank, cap, tm):
    B, E, S = rank.shape
    counts = jnp.sum((rank >= 0).reshape(B, E, S // tm, tm), axis=-1, dtype=jnp.int32)
    ends = jnp.cumsum(counts, axis=-1)
    start = ends - counts
    win = jnp.minimum(start // BF16_SUBLANES * BF16_SUBLANES, cap - SCAT_WINDOW)
    ok = jnp.all(ends - win <= SCAT_WINDOW, axis=1)
    return jnp.swapaxes(win, 1, 2), ok.astype(jnp.int32)


def _scat(rank, ybuf, x1, mod, g, cap):
    B, S, D = x1.shape
    E = rank.shape[1]
    tm = TOKEN_TILE
    win, ok = _scat_windows(rank, cap, tm)
    smem = pl.BlockSpec(memory_space=pltpu.SMEM)
    return pl.pallas_call(
        functools.partial(_scat_kernel, cap=cap),
        grid=(B, S // tm),
        in_specs=[smem, smem,
                  pl.BlockSpec((1, tm, E), lambda b, i: (b, i, 0)),
                  pl.BlockSpec(ybuf.shape[:1] + (1, cap, D), lambda b, i: (0, b, 0, 0)),
                  pl.BlockSpec((1, tm, D), lambda b, i: (b, i, 0)),
                  pl.BlockSpec((1, 6, D), lambda b, i: (b, 0, 0)),
                  _resident((1, D), lambda b, i: (0, 0))],
        out_specs=pl.BlockSpec((1, tm, D), lambda b, i: (b, i, 0)),
        out_shape=jax.ShapeDtypeStruct((B, S, D), F32),
        compiler_params=_cparams(("arbitrary", "arbitrary")),
        name="scat",
    )(win, ok, jnp.swapaxes(rank, 1, 2), ybuf, x1, mod, g)


def _rot_tables(S):
    half = RET_DK // 2
    inv = 1.0 / (RET_THETA_BASE ** np.linspace(0.0, 1.0, half))
    ang = np.arange(S, dtype=np.float64)[:, None] * inv[None, :]
    cos = np.repeat(np.cos(ang), 2, axis=1)
    sin = np.stack([-np.sin(ang), np.sin(ang)], axis=2).reshape(S, RET_DK)
    sc = RET_DK ** -0.5
    return jnp.asarray(np.stack([cos, sin, cos * sc, sin * sc]), F32)


def _ret_tables():
    C = RET_CHUNK
    heads = np.arange(RET_HEADS, dtype=np.float64)
    lgf = np.log1p(-np.exp2(-RET_FWD_DECAY_OFFSET - heads))[:, None]
    lgb = np.log1p(-np.exp2(-RET_BWD_DECAY_OFFSET - heads))[:, None]
    idx = np.arange(C, dtype=np.float64)
    diff = idx[:, None] - idx[None, :]
    dmat = np.where(diff >= 0,
                    np.exp(np.maximum(diff, 0.0)[None] * lgf[:, :, None]),
                    np.exp(np.maximum(-diff, 0.0)[None] * lgb[:, :, None]))
    dec = np.stack([np.exp((idx + 1)[None, :] * lgf),
                    np.exp((C - 1 - idx)[None, :] * lgf),
                    np.exp((C - idx)[None, :] * lgb),
                    np.exp(idx[None, :] * lgb)], axis=1)
    dec = np.broadcast_to(dec[..., None], dec.shape + (LANES,))
    cd = np.concatenate([np.exp(C * lgf), np.exp(C * lgb)], axis=1)
    return jnp.asarray(cd, F32), jnp.asarray(dec, F32), jnp.asarray(dmat, F32)


def _t5_bucket(rel):
    nb = N_BUCKETS // 2
    max_exact = nb // 2
    ret = (rel > 0).astype(jnp.int32) * nb
    n = jnp.abs(rel)
    large = max_exact + (jnp.log(jnp.maximum(n, 1).astype(F32) / max_exact)
                         / math.log(MAX_DISTANCE / max_exact) * (nb - max_exact)).astype(jnp.int32)
    large = jnp.minimum(large, nb - 1)
    return ret + jnp.where(n < max_exact, n, large)


def _bias_table(rel_bias, S):
    TQ, M = Q_TILE, MAX_DISTANCE
    reach = TQ + M - 1
    rel = jnp.concatenate([jnp.arange(-reach, reach + 1, dtype=jnp.int32),
                           jnp.array([-(S - 1), S - 1], jnp.int32)])
    f = rel_bias[_t5_bucket(rel)].astype(F32).T * LOG2E
    H = f.shape[0]
    lo, hi = f[:, -2], f[:, -1]
    L = 2 * reach + 1
    u = jnp.concatenate([f[:, :L], jnp.zeros((H, 1), F32)], axis=1)
    shifted = jnp.tile(u, (1, TQ))[:, :TQ * L].reshape(H, TQ, L)
    band = jnp.swapaxes(shifted[:, :, TQ - 1:TQ - 1 + TQ + 2 * M], 1, 2)
    side = S - TQ - M
    table = jnp.concatenate([jnp.broadcast_to(lo[:, None, None], (H, side, TQ)), band,
                             jnp.broadcast_to(hi[:, None, None], (H, side, TQ))], axis=1)
    return table, jnp.max(f, axis=1)


def kernel(x, c, w_ada, b_ada, norm_mix_g, w_in, ret_gn_g, diff_subln_g, lambda_q1, lambda_k1, lambda_q2, lambda_k2, w_ret_out, w_diff_out, w_o, rel_bias, norm_ffn_g, w_router, w_exp_gate, w_exp_up, w_exp_down, final_g):
    B, S, D = x.shape
    cap = CAPACITY_FACTOR * S // N_EXPERTS
    l = 0

    mod = _ada(c, w_ada[l], b_ada[l]).reshape(B, 6, D)
    proj, vt = _inproj(x, norm_mix_g[l].reshape(1, D), mod, w_in[l].astype(BF16), _rot_tables(S))

    cd, dec, dmat = _ret_tables()
    y_ret = _ret(proj, cd, dec, dmat, ret_gn_g[l].reshape(1, -1))

    lam_vecs = jnp.stack([lambda_q1[l], lambda_k1[l], lambda_q2[l], lambda_k2[l]]).astype(F32)
    ttab, bmax = _bias_table(rel_bias, S)
    subln = jnp.broadcast_to(diff_subln_g[l].astype(F32)[:, None], (DIFF_DV, Q_TILE))
    y_diff = _diff(proj, vt, lam_vecs, ttab, bmax, subln)

    x1, h2, logits = _merge(x, y_ret, y_diff, proj, mod,
                            w_ret_out[l].astype(BF16), w_diff_out[l].astype(BF16), w_o[l].astype(BF16),
                            norm_ffn_g[l].reshape(1, D), w_router[l].T.astype(BF16))

    rank, gate = _route(logits, cap)
    xin = _gather_rows(h2.reshape(B * S, D // 2), rank, cap)
    ybuf = _moe(rank, gate, xin, w_exp_gate[l], w_exp_up[l], w_exp_down[l], cap)
    return _scat(rank, ybuf, x1, mod, final_g.reshape(1, D), cap)
```

```python
import functools
import math

import numpy as np
import jax
import jax.numpy as jnp
from jax import lax
from jax.experimental import pallas as pl
from jax.experimental.pallas import tpu as pltpu
from jax.experimental.pallas import tpu_sc as plsc

F32 = jnp.float32
BF16 = jnp.bfloat16

RET_HEADS = 4
RET_DK = 128
RET_DV = 128
RET_FWD_DECAY_OFFSET = 5.0
RET_BWD_DECAY_OFFSET = 5.5
RET_THETA_BASE = 10000.0
DIFF_HEADS = 4
DIFF_DH = 64
DIFF_DV = 2 * DIFF_DH
N_BUCKETS = 32
MAX_DISTANCE = 128
N_EXPERTS = 16
CAPACITY_FACTOR = 2
NORM_EPS = 1e-6
LAM_INIT = 0.8 - 0.6 * math.exp(-0.3 * 0)
LOG2E = math.log2(math.e)

LANES = 128
BF16_SUBLANES = 16
VMEM_LIMIT_BYTES = 56 * 2**20

SC_CORES = 2
SC_SUBCORES = 16
SC_LANES = 16
SC_GATHER_ROWS = 64

TOKEN_TILE = 512
PROJ_TILE = 1024
RET_CHUNK = 256
Q_TILE = 256
KEY_CHUNK = 512
DIFF_TILES_PER_LOOP = 8
L_MIN = 2.0 ** -60
SCAT_WINDOW = 128
MOE_SEQS_PER_STEP = 4

W_RQ, W_RK, W_RV, W_RG = 0, 4, 8, 12
W_DQ1, W_DQ2, W_DK1, W_DK2, W_DV = 16, 18, 20, 22, 24
W_GR, W_GD = 28, 36

COL_GR, COL_GD = 0, 8
COL_RQ, COL_RK, COL_RV, COL_RG = 16, 20, 24, 28
COL_DQ, COL_DK = 32, 36
PROJ_COLS = 40 * LANES


def _cparams(sem):
    return pltpu.CompilerParams(dimension_semantics=sem, vmem_limit_bytes=VMEM_LIMIT_BYTES)


def _resident(shape, index_map):
    return pl.BlockSpec(shape, index_map, pipeline_mode=pl.Buffered(1))


def _ada_kernel(c_ref, w_ref, b_ref, o_ref):
    c = c_ref[...]
    a = c * jax.nn.sigmoid(c)
    o_ref[...] = jnp.dot(a, w_ref[...], preferred_element_type=F32,
                         precision=lax.Precision.HIGHEST) + b_ref[...]


def _ada(c, w, b):
    B, D = c.shape
    n = w.shape[1] // D
    return pl.pallas_call(
        _ada_kernel,
        grid=(n,),
        in_specs=[pl.BlockSpec((B, D), lambda j: (0, 0)),
                  pl.BlockSpec((D, D), lambda j: (0, j)),
                  pl.BlockSpec((1, D), lambda j: (0, j))],
        out_specs=pl.BlockSpec((B, D), lambda j: (0, j)),
        out_shape=jax.ShapeDtypeStruct((B, n * D), F32),
        compiler_params=_cparams(("arbitrary",)),
        name="ada",
    )(c, w, b.reshape(1, -1))


def _rms(x, g):
    ms = jnp.mean(x * x, axis=-1, keepdims=True)
    return x * lax.rsqrt(ms + NORM_EPS) * g


def _rot_pairs(x, cos, sin):
    even = (lax.broadcasted_iota(jnp.int32, x.shape, 1) & 1) == 0
    partner = jnp.where(even, pltpu.roll(x, LANES - 1, axis=1), pltpu.roll(x, 1, axis=1))
    return x * cos + partner * sin


def _pair_maps(a, b, second):
    low = lax.broadcasted_iota(jnp.int32, a.shape, 1) < DIFF_DH
    if second:
        return jnp.where(low, pltpu.roll(a, DIFF_DH, axis=1), b)
    return jnp.where(low, a, pltpu.roll(b, DIFF_DH, axis=1))


def _inproj_kernel(x_ref, g_ref, mod_ref, w_ref, rot_ref, o_ref, vt_ref, wvt_s):
    @pl.when((pl.program_id(0) == 0) & (pl.program_id(1) == 0))
    def _():
        wvt_s[...] = w_ref[:, W_DV * LANES:(W_DV + 4) * LANES].astype(F32).T.astype(BF16)

    parts = 4
    step = x_ref.shape[1] // parts

    def normed(p):
        x = x_ref[0, p * step:(p + 1) * step, :]
        h = _rms(x, g_ref[...]) * (1.0 + mod_ref[0, 1:2, :]) + mod_ref[0, 0:1, :]
        return h.astype(BF16)

    def project(p, hb):
        rows = slice(p * step, (p + 1) * step)

        def mm(col, width):
            return jnp.dot(hb, w_ref[:, col * LANES:(col + width) * LANES], preferred_element_type=F32)

        def put(col, val):
            o_ref[0, rows, col * LANES:col * LANES + val.shape[1]] = val.astype(BF16)

        for j in range(0, COL_GD - COL_GR, 4):
            put(COL_GR + j, jax.nn.sigmoid(mm(W_GR + j, 4)))
            put(COL_GD + j, jax.nn.sigmoid(mm(W_GD + j, 4)))
        for src, dst, t in ((W_RQ, COL_RQ, 0), (W_RK, COL_RK, 2)):
            cos, sin = rot_ref[t, rows, :], rot_ref[t + 1, rows, :]
            r = mm(src, RET_HEADS)
            for hh in range(RET_HEADS):
                put(dst + hh, _rot_pairs(r[:, hh * LANES:(hh + 1) * LANES], cos, sin))
        put(COL_RV, mm(W_RV, 4))
        r = mm(W_RG, 4)
        put(COL_RG, r * jax.nn.sigmoid(r))
        for src1, src2, dst, scale in ((W_DQ1, W_DQ2, COL_DQ, DIFF_DH ** -0.5 * LOG2E),
                                       (W_DK1, W_DK2, COL_DK, None)):
            m1, m2 = mm(src1, 2), mm(src2, 2)
            for hh in range(DIFF_HEADS):
                blk = slice((hh // 2) * LANES, (hh // 2 + 1) * LANES)
                val = _pair_maps(m1[:, blk], m2[:, blk], hh % 2 == 1)
                put(dst + hh, val if scale is None else val * scale)
        vt_ref[0, :, rows] = lax.dot_general(wvt_s[...], hb, (((1,), (1,)), ((), ())),
                                             preferred_element_type=F32).astype(BF16)

    hb = normed(0)
    for p in range(parts):
        nxt = normed(p + 1) if p + 1 < parts else None
        project(p, hb)
        hb = nxt


def _inproj(x, g, mod, w, rot):
    B, S, D = x.shape
    tm = PROJ_TILE
    vw = DIFF_HEADS * DIFF_DV
    return pl.pallas_call(
        _inproj_kernel,
        grid=(B, S // tm),
        in_specs=[pl.BlockSpec((1, tm, D), lambda b, i: (b, i, 0)),
                  _resident((1, D), lambda b, i: (0, 0)),
                  pl.BlockSpec((1, 6, D), lambda b, i: (b, 0, 0)),
                  _resident(w.shape, lambda b, i: (0, 0)),
                  pl.BlockSpec((4, tm, LANES), lambda b, i: (0, i, 0))],
        out_specs=[pl.BlockSpec((1, tm, PROJ_COLS), lambda b, i: (b, i, 0)),
                   pl.BlockSpec((1, vw, tm), lambda b, i: (b, 0, i))],
        out_shape=[jax.ShapeDtypeStruct((B, S, PROJ_COLS), BF16),
                   jax.ShapeDtypeStruct((B, vw, S), BF16)],
        scratch_shapes=[pltpu.VMEM((vw, D), BF16)],
        compiler_params=_cparams(("arbitrary", "arbitrary")),
        name="inproj",
    )(x, g, mod, w, rot)


def _ret_kernel(cd_ref, q_ref, k_ref, v_ref, rg_ref, dec_ref, dm_ref, gn_ref, o_ref, acc_ref):
    S = q_ref.shape[1]
    H = dm_ref.shape[0]
    C = RET_CHUNK
    nc = S // C
    nt = (((1,), (1,)), ((), ()))
    tn = (((0,), (0,)), ((), ()))

    def chunk(n, h):
        sl, lanes = pl.ds(n * C, C), slice(h * LANES, (h + 1) * LANES)
        return sl, q_ref[0, sl, lanes], k_ref[0, sl, lanes], v_ref[0, sl, lanes]

    def scaled(t, dec):
        return (t.astype(F32) * dec).astype(BF16)

    def forward(heads):
        state = {h: jnp.zeros((RET_DK, RET_DV), F32) for h in heads}
        for n in range(nc):
            for h in heads:
                sl, q, k, v = chunk(n, h)
                s = lax.dot_general(q, k, nt, preferred_element_type=F32) * dm_ref[h]
                inner = jnp.dot(s.astype(BF16), v, preferred_element_type=F32)
                cross = jnp.dot(scaled(q, dec_ref[h, 0]), state[h].astype(BF16), preferred_element_type=F32)
                acc_ref[h, sl, :] = inner + cross
                kv = lax.dot_general(scaled(k, dec_ref[h, 1]), v, tn, preferred_element_type=F32)
                state[h] = cd_ref[h, 0] * state[h] + kv

    def backward(heads):
        state = {h: jnp.zeros((RET_DK, RET_DV), F32) for h in heads}
        for n in reversed(range(nc)):
            for h in heads:
                sl, q, k, v = chunk(n, h)
                cross = jnp.dot(scaled(q, dec_ref[h, 2]), state[h].astype(BF16), preferred_element_type=F32)
                acc_ref[h, sl, :] = acc_ref[h, sl, :] + cross
                kv = lax.dot_general(scaled(k, dec_ref[h, 3]), v, tn, preferred_element_type=F32)
                state[h] = cd_ref[h, 1] * state[h] + kv

    def normalise(heads):
        for h in heads:
            lanes = slice(h * LANES, (h + 1) * LANES)
            y = acc_ref[h]
            mu = jnp.mean(y, axis=-1, keepdims=True)
            yc = y - mu
            var = jnp.mean(yc * yc, axis=-1, keepdims=True)
            yn = yc * lax.rsqrt(var + NORM_EPS) * gn_ref[:, lanes]
            o_ref[0, :, lanes] = (rg_ref[0, :, lanes].astype(F32) * yn).astype(BF16)

    first, second = tuple(range(H // 2)), tuple(range(H // 2, H))
    forward(first)
    backward(first)
    forward(second)
    normalise(first)
    backward(second)
    normalise(second)


def _ret(proj, cd, dec, dmat, gn):
    B, S, _ = proj.shape
    H = RET_HEADS
    C = RET_CHUNK

    def cols(base):
        return pl.BlockSpec((1, S, H * LANES), lambda b: (b, 0, base // H))

    return pl.pallas_call(
        _ret_kernel,
        grid=(B,),
        in_specs=[pl.BlockSpec(memory_space=pltpu.SMEM),
                  cols(COL_RQ), cols(COL_RK), cols(COL_RV), cols(COL_RG),
                  _resident((H, 4, C, LANES), lambda b: (0, 0, 0, 0)),
                  _resident((H, C, C), lambda b: (0, 0, 0)),
                  _resident((1, H * LANES), lambda b: (0, 0))],
        out_specs=pl.BlockSpec((1, S, H * LANES), lambda b: (b, 0, 0)),
        out_shape=jax.ShapeDtypeStruct((B, S, H * RET_DV), BF16),
        scratch_shapes=[pltpu.VMEM((H, S, RET_DV), F32)],
        compiler_params=_cparams(("arbitrary",)),
        name="ret",
    )(cd, proj, proj, proj, proj, dec, dmat, gn)


def _diff_kernel(bmax_ref, lam_ref, q_ref, k_ref, vt_ref, t_ref, g_ref, o_ref, p_ref):
    hh = pl.program_id(0)
    S = k_ref.shape[1]
    TQ, KB = Q_TILE, KEY_CHUNK
    nk = S // KB
    lv = lam_ref[...]
    lam = (jnp.exp(jnp.sum(lv[0:1] * lv[1:2], axis=1, keepdims=True))
           - jnp.exp(jnp.sum(lv[2:3] * lv[3:4], axis=1, keepdims=True)) + LAM_INIT)
    first_half = lax.broadcasted_iota(jnp.int32, (TQ, LANES), 1) < DIFF_DH
    nt = (((1,), (1,)), ((), ()))
    bmax = bmax_ref[hh]
    kf = k_ref[0].astype(F32)
    kmax = jnp.sqrt(jnp.max(jnp.sum(kf * kf, axis=1, keepdims=True), axis=0, keepdims=True))
    sel_r = lax.broadcasted_iota(jnp.int32, (8, LANES), 0)
    sel_c = lax.broadcasted_iota(jnp.int32, (8, LANES), 1)
    sel = jnp.where((sel_c < DIFF_DH) == (sel_r == 0), 1.0, 0.0)
    sel = jnp.where(sel_r < 2, sel, 0.0).astype(BF16)

    def masked_q(qb):
        q = q_ref[0, pl.ds(pl.multiple_of(qb * TQ, TQ), TQ), :]
        zero = jnp.zeros_like(q)
        return q, (jnp.where(first_half, q, zero), jnp.where(first_half, zero, q))

    def finish(ot, qb):
        ms = jnp.mean(ot * ot, axis=0, keepdims=True)
        y = ot * lax.rsqrt(ms + NORM_EPS) * g_ref[...] * (1.0 - LAM_INIT)
        o_ref[0, pl.ds(pl.multiple_of(qb * TQ, TQ), TQ), :] = y.T.astype(BF16)

    def exponentials(qb, slot):
        w0 = pl.multiple_of((S - TQ) - qb * TQ, TQ)
        q, qz = masked_q(qb)
        qsq = (q.astype(F32) * q.astype(F32)).astype(BF16)
        n2 = lax.dot_general(sel, qsq, nt, preferred_element_type=F32) * 1.01
        m = [jnp.sqrt(n2[i:i + 1]) * kmax + bmax for i in range(2)]
        l = [jnp.zeros((8, TQ), F32) for _ in range(2)]
        for c in range(nk):
            rows = slice(c * KB, (c + 1) * KB)
            kb = k_ref[0, rows, :]
            bias = t_ref[0, pl.ds(w0 + c * KB, KB), :]
            for i in range(2):
                s = lax.dot_general(kb, qz[i], nt, preferred_element_type=F32)
                p = jnp.exp2((s - m[i]) + bias)
                l[i] = l[i] + jnp.sum(p.reshape(KB // 8, 8, TQ), axis=0)
                p_ref[slot, i, rows, :] = p.astype(BF16)
        return [jnp.sum(l[i], axis=0, keepdims=True) for i in range(2)]

    def values(qb, slot, lr):
        ratio = jnp.broadcast_to(lam * lr[0] / lr[1], (16, TQ)).astype(BF16)
        ot = jnp.zeros((DIFF_DV, TQ), F32)
        for c in range(nk):
            rows = slice(c * KB, (c + 1) * KB)
            p1 = p_ref[slot, 0, rows, :].reshape(KB // 16, 16, TQ)
            p2 = p_ref[slot, 1, rows, :].reshape(KB // 16, 16, TQ)
            a = (p1 - ratio * p2).reshape(KB, TQ)
            ot = ot + jnp.dot(vt_ref[0, :, rows], a, preferred_element_type=F32)
        finish(ot * (1.0 / lr[0]), qb)

    def fast_group(j, lmin):
        base = j * DIFF_TILES_PER_LOOP
        sums = exponentials(base, 0)
        for t in range(DIFF_TILES_PER_LOOP):
            lmin = jnp.minimum(lmin, jnp.minimum(sums[0], sums[1]))
            nxt = exponentials(base + t + 1, (t + 1) % 2) if t + 1 < DIFF_TILES_PER_LOOP else None
            values(base + t, t % 2, sums)
            sums = nxt
        return lmin

    lmin = lax.fori_loop(0, S // (DIFF_TILES_PER_LOOP * TQ), fast_group, jnp.full((1, TQ), jnp.inf, F32))

    @pl.when(jnp.min(lmin) < L_MIN)
    def _():
        def exact_tile(qb, carry):
            w0 = pl.multiple_of((S - TQ) - qb * TQ, TQ)
            _, qz = masked_q(qb)
            bias = t_ref[0, pl.ds(w0, S), :]
            attn = None
            for i, scale in ((0, 1.0), (1, lam)):
                s = lax.dot_general(k_ref[0], qz[i], nt, preferred_element_type=F32) + bias
                p = jnp.exp2(s - jnp.max(s, axis=0, keepdims=True))
                part = p * (scale / jnp.sum(p, axis=0, keepdims=True))
                attn = part if attn is None else attn - part
            finish(jnp.dot(vt_ref[0], attn.astype(BF16), preferred_element_type=F32), qb)
            return carry

        lax.fori_loop(0, S // TQ, exact_tile, 0)


def _diff(proj, vt, lam_vecs, ttab, bmax, g):
    B, S, _ = proj.shape
    H = DIFF_HEADS

    def col(base):
        return pl.BlockSpec((1, S, LANES), lambda h, b: (b, 0, base + h))

    return pl.pallas_call(
        _diff_kernel,
        grid=(H, B),
        in_specs=[pl.BlockSpec(memory_space=pltpu.SMEM),
                  pl.BlockSpec((4, DIFF_DH), lambda h, b: (0, 0)),
                  col(COL_DQ), col(COL_DK),
                  pl.BlockSpec((1, DIFF_DV, S), lambda h, b: (b, h, 0)),
                  pl.BlockSpec((1, 2 * S - Q_TILE, Q_TILE), lambda h, b: (h, 0, 0)),
                  pl.BlockSpec((DIFF_DV, Q_TILE), lambda h, b: (0, 0))],
        out_specs=pl.BlockSpec((1, S, LANES), lambda h, b: (b, 0, h)),
        out_shape=jax.ShapeDtypeStruct((B, S, H * DIFF_DV), BF16),
        scratch_shapes=[pltpu.VMEM((2, 2, S, Q_TILE), BF16)],
        compiler_params=_cparams(("arbitrary", "arbitrary")),
        name="diff",
    )(bmax, lam_vecs, proj, proj, vt, ttab, g)


def _pack_halves(x):
    w = x.shape[1] // 2
    lo = pltpu.bitcast(x[:, :w].astype(BF16).astype(F32), jnp.int32)
    hi = pltpu.bitcast(x[:, w:].astype(BF16).astype(F32), jnp.int32)
    return lax.shift_right_logical(lo, jnp.full_like(lo, 16)) | (hi & jnp.int32(-65536))


def _unpack_halves(words):
    lo = pltpu.bitcast(lax.shift_left(words, jnp.full_like(words, 16)), F32)
    hi = pltpu.bitcast(words & jnp.int32(-65536), F32)
    return jnp.concatenate([lo, hi], axis=1).astype(BF16)


def _merge_kernel(x_ref, yr_ref, yd_ref, gr_ref, gd_ref, mod_ref, wr_ref, wd_ref, wo_ref,
                  g_ref, wrt_ref, x1_ref, h2_ref, lg_ref):
    parts = 4
    step = x_ref.shape[1] // parts

    def project(r):
        rows = slice(r * step, (r + 1) * step)
        a = jnp.dot(yr_ref[0, rows, :], wr_ref[...], preferred_element_type=F32)
        d = jnp.dot(yd_ref[0, rows, :], wd_ref[...], preferred_element_type=F32)
        merged = gr_ref[0, rows, :].astype(F32) * a + gd_ref[0, rows, :].astype(F32) * d
        return jnp.dot(merged.astype(BF16), wo_ref[...], preferred_element_type=F32)

    def epilogue(r, o):
        rows = slice(r * step, (r + 1) * step)
        x1 = x_ref[0, rows, :] + mod_ref[0, 2:3, :] * o
        x1_ref[0, rows, :] = x1
        h2 = _rms(x1, g_ref[...]) * (1.0 + mod_ref[0, 4:5, :]) + mod_ref[0, 3:4, :]
        h2_ref[0, rows, :] = _pack_halves(h2)
        lg_ref[0, :, rows] = lax.dot_general(wrt_ref[...], h2.astype(BF16), (((1,), (1,)), ((), ())),
                                             preferred_element_type=F32)

    o = project(0)
    for r in range(parts):
        nxt = project(r + 1) if r + 1 < parts else None
        epilogue(r, o)
        o = nxt


def _merge(x, yr, yd, proj, mod, wr, wd, wo, g, wrt):
    B, S, D = x.shape
    tm = PROJ_TILE
    E = wrt.shape[0]
    gate_w = COL_GD - COL_GR
    return pl.pallas_call(
        _merge_kernel,
        grid=(B, S // tm),
        in_specs=[pl.BlockSpec((1, tm, D), lambda b, i: (b, i, 0)),
                  pl.BlockSpec((1, tm, yr.shape[2]), lambda b, i: (b, i, 0)),
                  pl.BlockSpec((1, tm, yd.shape[2]), lambda b, i: (b, i, 0)),
                  pl.BlockSpec((1, tm, gate_w * LANES), lambda b, i: (b, i, COL_GR // gate_w)),
                  pl.BlockSpec((1, tm, gate_w * LANES), lambda b, i: (b, i, COL_GD // gate_w)),
                  pl.BlockSpec((1, 6, D), lambda b, i: (b, 0, 0)),
                  _resident(wr.shape, lambda b, i: (0, 0)),
                  _resident(wd.shape, lambda b, i: (0, 0)),
                  _resident(wo.shape, lambda b, i: (0, 0)),
                  _resident((1, D), lambda b, i: (0, 0)),
                  _resident(wrt.shape, lambda b, i: (0, 0))],
        out_specs=[pl.BlockSpec((1, tm, D), lambda b, i: (b, i, 0)),
                   pl.BlockSpec((1, tm, D // 2), lambda b, i: (b, i, 0)),
                   pl.BlockSpec((1, E, tm), lambda b, i: (b, 0, i))],
        out_shape=[jax.ShapeDtypeStruct((B, S, D), F32),
                   jax.ShapeDtypeStruct((B, S, D // 2), jnp.int32),
                   jax.ShapeDtypeStruct((B, E, S), F32)],
        compiler_params=_cparams(("arbitrary", "arbitrary")),
        name="merge",
    )(x, yr, yd, proj, proj, mod, wr, wd, wo, g, wrt)


def _lane_prefix(m, tri):
    E, S = m.shape
    off = jnp.zeros((E, 1), F32)
    parts = []
    for j in range(S // LANES):
        blk = m[:, j * LANES:(j + 1) * LANES]
        parts.append(jnp.dot(blk.astype(BF16), tri, preferred_element_type=F32) + off)
        off = off + jnp.sum(blk, axis=1, keepdims=True)
    return jnp.concatenate(parts, axis=1)


def _route_kernel(lg_ref, rank_ref, gate_ref, *, cap):
    lg = lg_ref[0]
    e = jnp.exp(lg - jnp.max(lg, axis=0, keepdims=True))
    aff = e / jnp.sum(e, axis=0, keepdims=True)
    bits = pltpu.bitcast(aff, jnp.int32)
    E = lg.shape[0]

    def count(mask):
        return jnp.sum(jnp.where(mask, 1.0, 0.0), axis=1, keepdims=True)

    def reaches(cand):
        return count(bits >= cand) >= cap

    thr = jnp.zeros((E, 1), jnp.int32)
    for hi in range(30, 0, -2):
        c1, c2 = thr | (1 << hi), thr | (1 << (hi - 1))
        c3 = c1 | (1 << (hi - 1))
        thr = jnp.where(reaches(c3), c3, jnp.where(reaches(c1), c1, jnp.where(reaches(c2), c2, thr)))
    thr = jnp.where(reaches(thr | 1), thr | 1, thr)
    gt = bits > thr
    eq = bits == thr
    need = cap - count(gt)
    r = lax.broadcasted_iota(jnp.int32, (LANES, LANES), 0)
    c = lax.broadcasted_iota(jnp.int32, (LANES, LANES), 1)
    tri = jnp.where(r < c, 1.0, 0.0).astype(BF16)
    eq_before = _lane_prefix(jnp.where(eq, 1.0, 0.0), tri)
    sel = gt | (eq & (eq_before < need))
    slot = _lane_prefix(jnp.where(sel, 1.0, 0.0), tri)
    rank_ref[0] = jnp.where(sel, slot, -1.0).astype(jnp.int32)
    gate_ref[0] = jnp.where(sel, aff, 0.0)


def _route(logits, cap):
    B, E, S = logits.shape
    spec = pl.BlockSpec((1, E, S), lambda b: (b, 0, 0))
    return pl.pallas_call(
        functools.partial(_route_kernel, cap=cap),
        grid=(B,),
        in_specs=[spec],
        out_specs=[spec, spec],
        out_shape=[jax.ShapeDtypeStruct((B, E, S), jnp.int32),
                   jax.ShapeDtypeStruct((B, E, S), F32)],
        compiler_params=_cparams(("arbitrary",)),
        name="route",
    )(logits)


def _gather_rows(table, rank, cap):
    B, E, S = rank.shape
    W = table.shape[1]
    workers = SC_CORES * SC_SUBCORES
    pairs = E * B
    per = pairs // workers
    assert per * workers == pairs and B & (B - 1) == 0 and cap % SC_GATHER_ROWS == 0 and S % SC_LANES == 0
    shift = B.bit_length() - 1
    mesh = plsc.VectorSubcoreMesh(core_axis_name="c", subcore_axis_name="s")

    R = SC_GATHER_ROWS
    n_chunks = cap // R

    def body(table_hbm, rank_hbm, out_hbm, rank_v, idx_v, buf_v, gsem, wsem):
        wid = lax.axis_index("s") * SC_CORES + lax.axis_index("c")

        def fetch(c):
            return pltpu.make_async_copy(table_hbm.at[idx_v.at[pl.ds(c * R, R)]], buf_v.at[c % 2], gsem.at[c % 2])

        def flush(p, c):
            return pltpu.make_async_copy(buf_v.at[c % 2], out_hbm.at[pl.ds(p * cap + c * R, R)], wsem.at[c % 2])

        for j in range(per):
            p = wid * per + j
            e = lax.shift_right_logical(p, shift)
            b = p & (B - 1)
            pltpu.sync_copy(rank_hbm.at[b, e], rank_v)
            base = b * S

            @pl.loop(0, S, step=SC_LANES)
            def _(t0):
                r = rank_v[pl.ds(t0, SC_LANES)]
                tok = lax.iota(jnp.int32, SC_LANES) + (base + t0)
                plsc.store_scatter(idx_v, [r], tok, mask=r >= 0)

            fetch(0).start()
            for c in range(n_chunks):
                fetch(c).wait()
                if c >= 1:
                    flush(p, c - 1).wait()
                if c + 1 < n_chunks:
                    fetch(c + 1).start()
                flush(p, c).start()
            flush(p, n_chunks - 1).wait()

    return pl.kernel(
        body,
        out_type=jax.ShapeDtypeStruct((pairs * cap, W), jnp.int32),
        mesh=mesh,
        scratch_types=[pltpu.VMEM((S,), jnp.int32), pltpu.VMEM((cap,), jnp.int32),
                       pltpu.VMEM((2, R, W), jnp.int32),
                       pltpu.SemaphoreType.DMA((2,)), pltpu.SemaphoreType.DMA((2,))],
        compiler_params=pltpu.CompilerParams(needs_layout_passes=False),
    )(table, rank)


def _moe_kernel(rank_ref, gate_ref, x_ref, wg_ref, wu_ref, wd_ref, o_ref, wg_s, wu_s, wd_s, *, cap):
    r, b = pl.program_id(0), pl.program_id(1)
    n_exp = pl.num_programs(0) - 1

    @pl.when(r < n_exp)
    def _():
        slot = r % 2
        rows_in, rows_ff = wg_ref.shape[1], wd_ref.shape[1]
        wg_s[slot, pl.ds(pl.multiple_of(b * rows_in, rows_in), rows_in), :] = wg_ref[0].astype(BF16)
        wu_s[slot, pl.ds(pl.multiple_of(b * rows_in, rows_in), rows_in), :] = wu_ref[0].astype(BF16)
        wd_s[slot, pl.ds(pl.multiple_of(b * rows_ff, rows_ff), rows_ff), :] = wd_ref[0].astype(BF16)

    @pl.when(r == 0)
    def _():
        o_ref[...] = jnp.zeros_like(o_ref)

    @pl.when(r > 0)
    def _():
        slot = (r - 1) % 2
        S = rank_ref.shape[3]
        for j in range(x_ref.shape[0]):
            rank = rank_ref[j, 0]
            pick = lax.broadcasted_iota(jnp.int32, (cap, S), 0) == rank
            xin = _unpack_halves(x_ref[j])
            a = jnp.dot(xin, wg_s[slot], preferred_element_type=F32)
            u = jnp.dot(xin, wu_s[slot], preferred_element_type=F32)
            act = (a * jax.nn.sigmoid(a) * u).astype(BF16)
            y = jnp.dot(act, wd_s[slot], preferred_element_type=F32)
            g = jnp.sum(jnp.where(pick, gate_ref[j, 0], 0.0), axis=1, keepdims=True)
            o_ref[0, j] = (y * g).astype(BF16)


def _moe(rank, gate, xin, wg, wu, wd, cap):
    B, E, S = rank.shape
    D, Fd = wg.shape[1], wg.shape[2]
    G = MOE_SEQS_PER_STEP
    steps = B // G
    assert steps * G == B and D % steps == 0 and Fd % steps == 0

    def cur(r):
        return jnp.maximum(r - 1, 0)

    def nxt(r):
        return jnp.minimum(r, E - 1)

    row = pl.BlockSpec((G, 1, 1, S), lambda r, b: (b, cur(r), 0, 0))
    return pl.pallas_call(
        functools.partial(_moe_kernel, cap=cap),
        grid=(E + 1, steps),
        in_specs=[row, row,
                  pl.BlockSpec((G, cap, D // 2), lambda r, b: (cur(r) * steps + b, 0, 0)),
                  pl.BlockSpec((1, D // steps, Fd), lambda r, b: (nxt(r), b, 0)),
                  pl.BlockSpec((1, D // steps, Fd), lambda r, b: (nxt(r), b, 0)),
                  pl.BlockSpec((1, Fd // steps, D), lambda r, b: (nxt(r), b, 0))],
        out_specs=pl.BlockSpec((1, G, cap, D), lambda r, b: (jnp.where(r == 0, E, r - 1), b, 0, 0)),
        out_shape=jax.ShapeDtypeStruct((E + 1, B, cap, D), BF16),
        scratch_shapes=[pltpu.VMEM((2, D, Fd), BF16), pltpu.VMEM((2, D, Fd), BF16),
                        pltpu.VMEM((2, Fd, D), BF16)],
        compiler_params=_cparams(("arbitrary", "arbitrary")),
        name="moe",
    )(rank.reshape(B, E, 1, S), gate.reshape(B, E, 1, S), xin.reshape(E * B, cap, D // 2), wg, wu, wd)


def _scat_kernel(win_ref, ok_ref, rt_ref, y_ref, x1_ref, mod_ref, g_ref, o_ref, *, cap):
    b, i = pl.program_id(0), pl.program_id(1)
    rt = rt_ref[0]
    tm, E = rt.shape
    D = y_ref.shape[3]

    def finish(moe):
        x2 = x1_ref[0] + mod_ref[0, 5:6, :] * moe
        o_ref[0] = _rms(x2, g_ref[...])

    @pl.when(ok_ref[b, i] != 0)
    def _():
        slot = lax.broadcasted_iota(jnp.int32, (tm, SCAT_WINDOW), 1)
        moe = jnp.zeros((tm, D), F32)
        for e in range(0, E, 2):
            starts = [pl.multiple_of(win_ref[b, i, e + k], BF16_SUBLANES) for k in range(2)]
            place = jnp.concatenate(
                [jnp.where(rt[:, e + k:e + k + 1] - starts[k] == slot, 1.0, 0.0).astype(BF16)
                 for k in range(2)], axis=1)
            y = jnp.concatenate([y_ref[e + k, 0, pl.ds(starts[k], SCAT_WINDOW), :] for k in range(2)], axis=0)
            moe = moe + jnp.dot(place, y, preferred_element_type=F32)
        finish(moe)

    @pl.when(ok_ref[b, i] == 0)
    def _():
        slot = lax.broadcasted_iota(jnp.int32, (tm, cap), 1)
        place = jnp.concatenate(
            [jnp.where(rt[:, e:e + 1] == slot, 1.0, 0.0).astype(BF16) for e in range(E)], axis=1)
        y = y_ref[0:E, 0].reshape(E * cap, D)
        finish(jnp.dot(place, y, preferred_element_type=F32))


def _scat_windows(rank, cap, tm):
    B, E, S = rank.shape
    counts = jnp.sum((rank >= 0).reshape(B, E, S // tm, tm), axis=-1, dtype=jnp.int32)
    ends = jnp.cumsum(counts, axis=-1)
    start = ends - counts
    win = jnp.minimum(start // BF16_SUBLANES * BF16_SUBLANES, cap - SCAT_WINDOW)
    ok = jnp.all(ends - win <= SCAT_WINDOW, axis=1)
    return jnp.swapaxes(win, 1, 2), ok.astype(jnp.int32)


def _scat(rank, ybuf, x1, mod, g, cap):
    B, S, D = x1.shape
    E = rank.shape[1]
    tm = TOKEN_TILE
    win, ok = _scat_windows(rank, cap, tm)
    smem = pl.BlockSpec(memory_space=pltpu.SMEM)
    return pl.pallas_call(
        functools.partial(_scat_kernel, cap=cap),
        grid=(B, S // tm),
        in_specs=[smem, smem,
                  pl.BlockSpec((1, tm, E), lambda b, i: (b, i, 0)),
                  pl.BlockSpec(ybuf.shape[:1] + (1, cap, D), lambda b, i: (0, b, 0, 0)),
                  pl.BlockSpec((1, tm, D), lambda b, i: (b, i, 0)),
                  pl.BlockSpec((1, 6, D), lambda b, i: (b, 0, 0)),
                  _resident((1, D), lambda b, i: (0, 0))],
        out_specs=pl.BlockSpec((1, tm, D), lambda b, i: (b, i, 0)),
        out_shape=jax.ShapeDtypeStruct((B, S, D), F32),
        compiler_params=_cparams(("arbitrary", "arbitrary")),
        name="scat",
    )(win, ok, jnp.swapaxes(rank, 1, 2), ybuf, x1, mod, g)


def _rot_tables(S):
    half = RET_DK // 2
    inv = 1.0 / (RET_THETA_BASE ** np.linspace(0.0, 1.0, half))
    ang = np.arange(S, dtype=np.float64)[:, None] * inv[None, :]
    cos = np.repeat(np.cos(ang), 2, axis=1)
    sin = np.stack([-np.sin(ang), np.sin(ang)], axis=2).reshape(S, RET_DK)
    sc = RET_DK ** -0.5
    return jnp.asarray(np.stack([cos, sin, cos * sc, sin * sc]), F32)


def _ret_tables():
    C = RET_CHUNK
    heads = np.arange(RET_HEADS, dtype=np.float64)
    lgf = np.log1p(-np.exp2(-RET_FWD_DECAY_OFFSET - heads))[:, None]
    lgb = np.log1p(-np.exp2(-RET_BWD_DECAY_OFFSET - heads))[:, None]
    idx = np.arange(C, dtype=np.float64)
    diff = idx[:, None] - idx[None, :]
    dmat = np.where(diff >= 0,
                    np.exp(np.maximum(diff, 0.0)[None] * lgf[:, :, None]),
                    np.exp(np.maximum(-diff, 0.0)[None] * lgb[:, :, None]))
    dec = np.stack([np.exp((idx + 1)[None, :] * lgf),
                    np.exp((C - 1 - idx)[None, :] * lgf),
                    np.exp((C - idx)[None, :] * lgb),
                    np.exp(idx[None, :] * lgb)], axis=1)
    dec = np.broadcast_to(dec[..., None], dec.shape + (LANES,))
    cd = np.concatenate([np.exp(C * lgf), np.exp(C * lgb)], axis=1)
    return jnp.asarray(cd, F32), jnp.asarray(dec, F32), jnp.asarray(dmat, F32)


def _t5_bucket(rel):
    nb = N_BUCKETS // 2
    max_exact = nb // 2
    ret = (rel > 0).astype(jnp.int32) * nb
    n = jnp.abs(rel)
    large = max_exact + (jnp.log(jnp.maximum(n, 1).astype(F32) / max_exact)
                         / math.log(MAX_DISTANCE / max_exact) * (nb - max_exact)).astype(jnp.int32)
    large = jnp.minimum(large, nb - 1)
    return ret + jnp.where(n < max_exact, n, large)


def _bias_table(rel_bias, S):
    TQ, M = Q_TILE, MAX_DISTANCE
    reach = TQ + M - 1
    rel = jnp.concatenate([jnp.arange(-reach, reach + 1, dtype=jnp.int32),
                           jnp.array([-(S - 1), S - 1], jnp.int32)])
    f = rel_bias[_t5_bucket(rel)].astype(F32).T * LOG2E
    H = f.shape[0]
    lo, hi = f[:, -2], f[:, -1]
    L = 2 * reach + 1
    u = jnp.concatenate([f[:, :L], jnp.zeros((H, 1), F32)], axis=1)
    shifted = jnp.tile(u, (1, TQ))[:, :TQ * L].reshape(H, TQ, L)
    band = jnp.swapaxes(shifted[:, :, TQ - 1:TQ - 1 + TQ + 2 * M], 1, 2)
    side = S - TQ - M
    table = jnp.concatenate([jnp.broadcast_to(lo[:, None, None], (H, side, TQ)), band,
                             jnp.broadcast_to(hi[:, None, None], (H, side, TQ))], axis=1)
    return table, jnp.max(f, axis=1)


def kernel(x, c, w_ada, b_ada, norm_mix_g, w_in, ret_gn_g, diff_subln_g, lambda_q1, lambda_k1, lambda_q2, lambda_k2, w_ret_out, w_diff_out, w_o, rel_bias, norm_ffn_g, w_router, w_exp_gate, w_exp_up, w_exp_down, final_g):
    B, S, D = x.shape
    cap = CAPACITY_FACTOR * S // N_EXPERTS
    l = 0

    mod = _ada(c, w_ada[l], b_ada[l]).reshape(B, 6, D)
    proj, vt = _inproj(x, norm_mix_g[l].reshape(1, D), mod, w_in[l].astype(BF16), _rot_tables(S))

    cd, dec, dmat = _ret_tables()
    y_ret = _ret(proj, cd, dec, dmat, ret_gn_g[l].reshape(1, -1))

    lam_vecs = jnp.stack([lambda_q1[l], lambda_k1[l], lambda_q2[l], lambda_k2[l]]).astype(F32)
    ttab, bmax = _bias_table(rel_bias, S)
    subln = jnp.broadcast_to(diff_subln_g[l].astype(F32)[:, None], (DIFF_DV, Q_TILE))
    y_diff = _diff(proj, vt, lam_vecs, ttab, bmax, subln)

    x1, h2, logits = _merge(x, y_ret, y_diff, proj, mod,
                            w_ret_out[l].astype(BF16), w_diff_out[l].astype(BF16), w_o[l].astype(BF16),
                            norm_ffn_g[l].reshape(1, D), w_router[l].T.astype(BF16))

    rank, gate = _route(logits, cap)
    xin = _gather_rows(h2.reshape(B * S, D // 2), rank, cap)
    ybuf = _moe(rank, gate, xin, w_exp_gate[l], w_exp_up[l], w_exp_down[l], cap)
    return _scat(rank, ybuf, x1, mod, final_g.reshape(1, D), cap)
```

```python
import functools
import math

import numpy as np
import jax
import jax.numpy as jnp
from jax import lax
from jax.experimental import pallas as pl
from jax.experimental.pallas import tpu as pltpu
from jax.experimental.pallas import tpu_sc as plsc

F32 = jnp.float32
BF16 = jnp.bfloat16

RET_HEADS = 4
RET_DK = 128
RET_DV = 128
RET_FWD_DECAY_OFFSET = 5.0
RET_BWD_DECAY_OFFSET = 5.5
RET_THETA_BASE = 10000.0
DIFF_HEADS = 4
DIFF_DH = 64
DIFF_DV = 2 * DIFF_DH
N_BUCKETS = 32
MAX_DISTANCE = 128
N_EXPERTS = 16
CAPACITY_FACTOR = 2
NORM_EPS = 1e-6
LAM_INIT = 0.8 - 0.6 * math.exp(-0.3 * 0)
LOG2E = math.log2(math.e)

LANES = 128
BF16_SUBLANES = 16
VMEM_LIMIT_BYTES = 56 * 2**20

SC_CORES = 2
SC_SUBCORES = 16
SC_LANES = 16
SC_GATHER_ROWS = 64

TOKEN_TILE = 512
PROJ_TILE = 1024
RET_CHUNK = 256
Q_TILE = 256
KEY_CHUNK = 512
DIFF_TILES_PER_LOOP = 8
L_MIN = 2.0 ** -60
BIAS_BAND = Q_TILE + 2 * MAX_DISTANCE
SCAT_WINDOW = 128
MOE_SEQS_PER_STEP = 4
ROUTE_SEQS_PER_STEP = 4

W_RQ, W_RK, W_RV, W_RG = 0, 4, 8, 12
W_DQ1, W_DQ2, W_DK1, W_DK2, W_DV = 16, 18, 20, 22, 24
W_GR, W_GD = 28, 36

COL_GR, COL_GD = 0, 8
COL_RQ, COL_RK, COL_RV, COL_RG = 16, 20, 24, 28
COL_DQ, COL_DK = 32, 36
PROJ_COLS = 40 * LANES


def _cparams(sem):
    return pltpu.CompilerParams(dimension_semantics=sem, vmem_limit_bytes=VMEM_LIMIT_BYTES)


def _resident(shape, index_map):
    return pl.BlockSpec(shape, index_map, pipeline_mode=pl.Buffered(1))


def _ada_kernel(c_ref, w_ref, b_ref, o_ref):
    c = c_ref[...]
    a = c * jax.nn.sigmoid(c)
    o_ref[...] = jnp.dot(a, w_ref[...], preferred_element_type=F32,
                         precision=lax.Precision.HIGHEST) + b_ref[...]


def _ada(c, w, b):
    B, D = c.shape
    n = w.shape[1] // D
    return pl.pallas_call(
        _ada_kernel,
        grid=(n,),
        in_specs=[pl.BlockSpec((B, D), lambda j: (0, 0)),
                  pl.BlockSpec((D, D), lambda j: (0, j)),
                  pl.BlockSpec((1, D), lambda j: (0, j))],
        out_specs=pl.BlockSpec((B, D), lambda j: (0, j)),
        out_shape=jax.ShapeDtypeStruct((B, n * D), F32),
        compiler_params=_cparams(("arbitrary",)),
        name="ada",
    )(c, w, b.reshape(1, -1))


def _rms(x, g):
    ms = jnp.mean(x * x, axis=-1, keepdims=True)
    return x * lax.rsqrt(ms + NORM_EPS) * g


def _rot_pairs(x, cos, sin):
    even = (lax.broadcasted_iota(jnp.int32, x.shape, 1) & 1) == 0
    partner = jnp.where(even, pltpu.roll(x, LANES - 1, axis=1), pltpu.roll(x, 1, axis=1))
    return x * cos + partner * sin


def _pair_maps(a, b, second):
    low = lax.broadcasted_iota(jnp.int32, a.shape, 1) < DIFF_DH
    if second:
        return jnp.where(low, pltpu.roll(a, DIFF_DH, axis=1), b)
    return jnp.where(low, a, pltpu.roll(b, DIFF_DH, axis=1))


def _inproj_kernel(x_ref, g_ref, mod_ref, w_ref, rot_ref, o_ref, vt_ref, wvt_s):
    @pl.when((pl.program_id(0) == 0) & (pl.program_id(1) == 0))
    def _():
        wvt_s[...] = w_ref[:, W_DV * LANES:(W_DV + 4) * LANES].astype(F32).T.astype(BF16)

    parts = 4
    step = x_ref.shape[1] // parts

    def normed(p):
        x = x_ref[0, p * step:(p + 1) * step, :]
        h = _rms(x, g_ref[...]) * (1.0 + mod_ref[0, 1:2, :]) + mod_ref[0, 0:1, :]
        return h.astype(BF16)

    def project(p, hb):
        rows = slice(p * step, (p + 1) * step)

        def mm(col, width):
            return jnp.dot(hb, w_ref[:, col * LANES:(col + width) * LANES], preferred_element_type=F32)

        def put(col, val):
            o_ref[0, rows, col * LANES:col * LANES + val.shape[1]] = val.astype(BF16)

        for j in range(0, COL_GD - COL_GR, 4):
            put(COL_GR + j, jax.nn.sigmoid(mm(W_GR + j, 4)))
            put(COL_GD + j, jax.nn.sigmoid(mm(W_GD + j, 4)))
        for src, dst, t in ((W_RQ, COL_RQ, 0), (W_RK, COL_RK, 2)):
            cos, sin = rot_ref[t, rows, :], rot_ref[t + 1, rows, :]
            r = mm(src, RET_HEADS)
            for hh in range(RET_HEADS):
                put(dst + hh, _rot_pairs(r[:, hh * LANES:(hh + 1) * LANES], cos, sin))
        put(COL_RV, mm(W_RV, 4))
        r = mm(W_RG, 4)
        put(COL_RG, r * jax.nn.sigmoid(r))
        for src1, src2, dst, scale in ((W_DQ1, W_DQ2, COL_DQ, DIFF_DH ** -0.5 * LOG2E),
                                       (W_DK1, W_DK2, COL_DK, None)):
            m1, m2 = mm(src1, 2), mm(src2, 2)
            for hh in range(DIFF_HEADS):
                blk = slice((hh // 2) * LANES, (hh // 2 + 1) * LANES)
                val = _pair_maps(m1[:, blk], m2[:, blk], hh % 2 == 1)
                put(dst + hh, val if scale is None else val * scale)
        vt_ref[0, :, rows] = lax.dot_general(wvt_s[...], hb, (((1,), (1,)), ((), ())),
                                             preferred_element_type=F32).astype(BF16)

    hb = normed(0)
    for p in range(parts):
        nxt = normed(p + 1) if p + 1 < parts else None
        project(p, hb)
        hb = nxt


def _inproj(x, g, mod, w, rot):
    B, S, D = x.shape
    tm = PROJ_TILE
    vw = DIFF_HEADS * DIFF_DV
    return pl.pallas_call(
        _inproj_kernel,
        grid=(B, S // tm),
        in_specs=[pl.BlockSpec((1, tm, D), lambda b, i: (b, i, 0)),
                  _resident((1, D), lambda b, i: (0, 0)),
                  pl.BlockSpec((1, 6, D), lambda b, i: (b, 0, 0)),
                  _resident(w.shape, lambda b, i: (0, 0)),
                  pl.BlockSpec((4, tm, LANES), lambda b, i: (0, i, 0))],
        out_specs=[pl.BlockSpec((1, tm, PROJ_COLS), lambda b, i: (b, i, 0)),
                   pl.BlockSpec((1, vw, tm), lambda b, i: (b, 0, i))],
        out_shape=[jax.ShapeDtypeStruct((B, S, PROJ_COLS), BF16),
                   jax.ShapeDtypeStruct((B, vw, S), BF16)],
        scratch_shapes=[pltpu.VMEM((vw, D), BF16)],
        compiler_params=_cparams(("arbitrary", "arbitrary")),
        name="inproj",
    )(x, g, mod, w, rot)


def _ret_kernel(cd_ref, q_ref, k_ref, v_ref, rg_ref, dec_ref, dm_ref, gn_ref, o_ref, acc_ref):
    S = q_ref.shape[1]
    H = dm_ref.shape[0]
    C = RET_CHUNK
    nc = S // C
    nt = (((1,), (1,)), ((), ()))
    tn = (((0,), (0,)), ((), ()))

    def chunk(n, h):
        sl, lanes = pl.ds(n * C, C), slice(h * LANES, (h + 1) * LANES)
        return sl, q_ref[0, sl, lanes], k_ref[0, sl, lanes], v_ref[0, sl, lanes]

    def scaled(t, dec):
        return (t.astype(F32) * dec).astype(BF16)

    def forward(heads):
        state = {h: jnp.zeros((RET_DK, RET_DV), F32) for h in heads}
        for n in range(nc):
            for h in heads:
                sl, q, k, v = chunk(n, h)
                s = lax.dot_general(q, k, nt, preferred_element_type=F32) * dm_ref[h]
                inner = jnp.dot(s.astype(BF16), v, preferred_element_type=F32)
                cross = jnp.dot(scaled(q, dec_ref[h, 0]), state[h].astype(BF16), preferred_element_type=F32)
                acc_ref[h, sl, :] = inner + cross
                kv = lax.dot_general(scaled(k, dec_ref[h, 1]), v, tn, preferred_element_type=F32)
                state[h] = cd_ref[h, 0] * state[h] + kv

    def backward(heads):
        state = {h: jnp.zeros((RET_DK, RET_DV), F32) for h in heads}
        for n in reversed(range(nc)):
            for h in heads:
                sl, q, k, v = chunk(n, h)
                cross = jnp.dot(scaled(q, dec_ref[h, 2]), state[h].astype(BF16), preferred_element_type=F32)
                acc_ref[h, sl, :] = acc_ref[h, sl, :] + cross
                kv = lax.dot_general(scaled(k, dec_ref[h, 3]), v, tn, preferred_element_type=F32)
                state[h] = cd_ref[h, 1] * state[h] + kv

    def normalise(heads):
        for h in heads:
            lanes = slice(h * LANES, (h + 1) * LANES)
            y = acc_ref[h]
            mu = jnp.mean(y, axis=-1, keepdims=True)
            yc = y - mu
            var = jnp.mean(yc * yc, axis=-1, keepdims=True)
            yn = yc * lax.rsqrt(var + NORM_EPS) * gn_ref[:, lanes]
            o_ref[0, :, lanes] = (rg_ref[0, :, lanes].astype(F32) * yn).astype(BF16)

    first, second = tuple(range(H // 2)), tuple(range(H // 2, H))
    forward(first)
    backward(first)
    forward(second)
    normalise(first)
    backward(second)
    normalise(second)


def _ret(proj, cd, dec, dmat, gn):
    B, S, _ = proj.shape
    H = RET_HEADS
    C = RET_CHUNK

    def cols(base):
        return pl.BlockSpec((1, S, H * LANES), lambda b: (b, 0, base // H))

    return pl.pallas_call(
        _ret_kernel,
        grid=(B,),
        in_specs=[pl.BlockSpec(memory_space=pltpu.SMEM),
                  cols(COL_RQ), cols(COL_RK), cols(COL_RV), cols(COL_RG),
                  _resident((H, 4, C, LANES), lambda b: (0, 0, 0, 0)),
                  _resident((H, C, C), lambda b: (0, 0, 0)),
                  _resident((1, H * LANES), lambda b: (0, 0))],
        out_specs=pl.BlockSpec((1, S, H * LANES), lambda b: (b, 0, 0)),
        out_shape=jax.ShapeDtypeStruct((B, S, H * RET_DV), BF16),
        scratch_shapes=[pltpu.VMEM((H, S, RET_DV), F32)],
        compiler_params=_cparams(("arbitrary",)),
        name="ret",
    )(cd, proj, proj, proj, proj, dec, dmat, gn)


def _diff_kernel(bstat_ref, lam_ref, q_ref, k_ref, vt_ref, brow_ref, g_ref, o_ref, p_ref, t_ref):
    hh = pl.program_id(0)
    S = k_ref.shape[1]
    TQ, KB = Q_TILE, KEY_CHUNK
    nk = S // KB
    lv = lam_ref[...]
    lam = (jnp.exp(jnp.sum(lv[0:1] * lv[1:2], axis=1, keepdims=True))
           - jnp.exp(jnp.sum(lv[2:3] * lv[3:4], axis=1, keepdims=True)) + LAM_INIT)
    first_half = lax.broadcasted_iota(jnp.int32, (TQ, LANES), 1) < DIFF_DH
    nt = (((1,), (1,)), ((), ()))
    bmax = bstat_ref[hh, 0]

    @pl.when(pl.program_id(1) == 0)
    def _():
        side = S - TQ - MAX_DISTANCE
        t_ref[0:side, :] = jnp.full((side, TQ), bstat_ref[hh, 1], F32)
        rows = jnp.broadcast_to(brow_ref[0], (BIAS_BAND, brow_ref.shape[2]))
        shift = brow_ref.shape[2] - (BIAS_BAND - 1 + MAX_DISTANCE)
        t_ref[side:side + BIAS_BAND, :] = pltpu.roll(rows, shift, 1, stride=1, stride_axis=0)[:, :TQ]
        t_ref[side + BIAS_BAND:, :] = jnp.full((side, TQ), bstat_ref[hh, 2], F32)
    kf = k_ref[0].astype(F32)
    kmax = jnp.sqrt(jnp.max(jnp.sum(kf * kf, axis=1, keepdims=True), axis=0, keepdims=True))
    sel_r = lax.broadcasted_iota(jnp.int32, (8, LANES), 0)
    sel_c = lax.broadcasted_iota(jnp.int32, (8, LANES), 1)
    sel = jnp.where((sel_c < DIFF_DH) == (sel_r == 0), 1.0, 0.0)
    sel = jnp.where(sel_r < 2, sel, 0.0).astype(BF16)

    def masked_q(qb):
        q = q_ref[0, pl.ds(pl.multiple_of(qb * TQ, TQ), TQ), :]
        zero = jnp.zeros_like(q)
        return q, (jnp.where(first_half, q, zero), jnp.where(first_half, zero, q))

    def finish(ot, qb):
        ms = jnp.mean(ot * ot, axis=0, keepdims=True)
        y = ot * lax.rsqrt(ms + NORM_EPS) * g_ref[...] * (1.0 - LAM_INIT)
        o_ref[0, pl.ds(pl.multiple_of(qb * TQ, TQ), TQ), :] = y.T.astype(BF16)

    def exponentials(qb, slot):
        w0 = pl.multiple_of((S - TQ) - qb * TQ, TQ)
        q, qz = masked_q(qb)
        qsq = (q.astype(F32) * q.astype(F32)).astype(BF16)
        n2 = lax.dot_general(sel, qsq, nt, preferred_element_type=F32) * 1.01
        m = [jnp.sqrt(n2[i:i + 1]) * kmax + bmax for i in range(2)]
        l = [jnp.zeros((8, TQ), F32) for _ in range(2)]
        for c in range(nk):
            rows = slice(c * KB, (c + 1) * KB)
            kb = k_ref[0, rows, :]
            bias = t_ref[pl.ds(w0 + c * KB, KB), :]
            for i in range(2):
                s = lax.dot_general(kb, qz[i], nt, preferred_element_type=F32)
                p = jnp.exp2((s - m[i]) + bias)
                l[i] = l[i] + jnp.sum(p.reshape(KB // 8, 8, TQ), axis=0)
                p_ref[slot, i, rows, :] = p.astype(BF16)
        return [jnp.sum(l[i], axis=0, keepdims=True) for i in range(2)]

    def values(qb, slot, lr):
        ratio = jnp.broadcast_to(lam * lr[0] / lr[1], (16, TQ)).astype(BF16)
        ot = jnp.zeros((DIFF_DV, TQ), F32)
        for c in range(nk):
            rows = slice(c * KB, (c + 1) * KB)
            p1 = p_ref[slot, 0, rows, :].reshape(KB // 16, 16, TQ)
            p2 = p_ref[slot, 1, rows, :].reshape(KB // 16, 16, TQ)
            a = (p1 - ratio * p2).reshape(KB, TQ)
            ot = ot + jnp.dot(vt_ref[0, :, rows], a, preferred_element_type=F32)
        finish(ot * (1.0 / lr[0]), qb)

    def fast_group(j, lmin):
        base = j * DIFF_TILES_PER_LOOP
        sums = exponentials(base, 0)
        for t in range(DIFF_TILES_PER_LOOP):
            lmin = jnp.minimum(lmin, jnp.minimum(sums[0], sums[1]))
            nxt = exponentials(base + t + 1, (t + 1) % 2) if t + 1 < DIFF_TILES_PER_LOOP else None
            values(base + t, t % 2, sums)
            sums = nxt
        return lmin

    lmin = lax.fori_loop(0, S // (DIFF_TILES_PER_LOOP * TQ), fast_group, jnp.full((1, TQ), jnp.inf, F32))

    @pl.when(jnp.min(lmin) < L_MIN)
    def _():
        def exact_tile(qb, carry):
            w0 = pl.multiple_of((S - TQ) - qb * TQ, TQ)
            _, qz = masked_q(qb)
            bias = t_ref[pl.ds(w0, S), :]
            attn = None
            for i, scale in ((0, 1.0), (1, lam)):
                s = lax.dot_general(k_ref[0], qz[i], nt, preferred_element_type=F32) + bias
                p = jnp.exp2(s - jnp.max(s, axis=0, keepdims=True))
                part = p * (scale / jnp.sum(p, axis=0, keepdims=True))
                attn = part if attn is None else attn - part
            finish(jnp.dot(vt_ref[0], attn.astype(BF16), preferred_element_type=F32), qb)
            return carry

        lax.fori_loop(0, S // TQ, exact_tile, 0)


def _diff(proj, vt, lam_vecs, brow, bstat, g):
    B, S, _ = proj.shape
    H = DIFF_HEADS

    def col(base):
        return pl.BlockSpec((1, S, LANES), lambda h, b: (b, 0, base + h))

    return pl.pallas_call(
        _diff_kernel,
        grid=(H, B),
        in_specs=[pl.BlockSpec(memory_space=pltpu.SMEM),
                  pl.BlockSpec((4, DIFF_DH), lambda h, b: (0, 0)),
                  col(COL_DQ), col(COL_DK),
                  pl.BlockSpec((1, DIFF_DV, S), lambda h, b: (b, h, 0)),
                  pl.BlockSpec((1, 1, brow.shape[2]), lambda h, b: (h, 0, 0)),
                  pl.BlockSpec((DIFF_DV, Q_TILE), lambda h, b: (0, 0))],
        out_specs=pl.BlockSpec((1, S, LANES), lambda h, b: (b, 0, h)),
        out_shape=jax.ShapeDtypeStruct((B, S, H * DIFF_DV), BF16),
        scratch_shapes=[pltpu.VMEM((2, 2, S, Q_TILE), BF16), pltpu.VMEM((2 * S - Q_TILE, Q_TILE), F32)],
        compiler_params=_cparams(("arbitrary", "arbitrary")),
        name="diff",
    )(bstat, lam_vecs, proj, proj, vt, brow, g)


def _pack_halves(x):
    w = x.shape[1] // 2
    lo = pltpu.bitcast(x[:, :w].astype(BF16).astype(F32), jnp.int32)
    hi = pltpu.bitcast(x[:, w:].astype(BF16).astype(F32), jnp.int32)
    return lax.shift_right_logical(lo, jnp.full_like(lo, 16)) | (hi & jnp.int32(-65536))


def _unpack_halves(words):
    lo = pltpu.bitcast(lax.shift_left(words, jnp.full_like(words, 16)), F32)
    hi = pltpu.bitcast(words & jnp.int32(-65536), F32)
    return jnp.concatenate([lo, hi], axis=1).astype(BF16)


def _merge_kernel(x_ref, yr_ref, yd_ref, gr_ref, gd_ref, mod_ref, wr_ref, wd_ref, wo_ref,
                  g_ref, wrt_ref, x1_ref, h2_ref, lg_ref):
    parts = 4
    step = x_ref.shape[1] // parts

    def project(r):
        rows = slice(r * step, (r + 1) * step)
        a = jnp.dot(yr_ref[0, rows, :], wr_ref[...], preferred_element_type=F32)
        d = jnp.dot(yd_ref[0, rows, :], wd_ref[...], preferred_element_type=F32)
        merged = gr_ref[0, rows, :].astype(F32) * a + gd_ref[0, rows, :].astype(F32) * d
        return jnp.dot(merged.astype(BF16), wo_ref[...], preferred_element_type=F32)

    def epilogue(r, o):
        rows = slice(r * step, (r + 1) * step)
        x1 = x_ref[0, rows, :] + mod_ref[0, 2:3, :] * o
        x1_ref[0, rows, :] = x1
        h2 = _rms(x1, g_ref[...]) * (1.0 + mod_ref[0, 4:5, :]) + mod_ref[0, 3:4, :]
        h2_ref[0, rows, :] = _pack_halves(h2)
        lg_ref[0, :, rows] = lax.dot_general(wrt_ref[...], h2.astype(BF16), (((1,), (1,)), ((), ())),
                                             preferred_element_type=F32)

    o = project(0)
    for r in range(parts):
        nxt = project(r + 1) if r + 1 < parts else None
        epilogue(r, o)
        o = nxt


def _merge(x, yr, yd, proj, mod, wr, wd, wo, g, wrt):
    B, S, D = x.shape
    tm = PROJ_TILE
    E = wrt.shape[0]
    gate_w = COL_GD - COL_GR
    return pl.pallas_call(
        _merge_kernel,
        grid=(B, S // tm),
        in_specs=[pl.BlockSpec((1, tm, D), lambda b, i: (b, i, 0)),
                  pl.BlockSpec((1, tm, yr.shape[2]), lambda b, i: (b, i, 0)),
                  pl.BlockSpec((1, tm, yd.shape[2]), lambda b, i: (b, i, 0)),
                  pl.BlockSpec((1, tm, gate_w * LANES), lambda b, i: (b, i, COL_GR // gate_w)),
                  pl.BlockSpec((1, tm, gate_w * LANES), lambda b, i: (b, i, COL_GD // gate_w)),
                  pl.BlockSpec((1, 6, D), lambda b, i: (b, 0, 0)),
                  _resident(wr.shape, lambda b, i: (0, 0)),
                  _resident(wd.shape, lambda b, i: (0, 0)),
                  _resident(wo.shape, lambda b, i: (0, 0)),
                  _resident((1, D), lambda b, i: (0, 0)),
                  _resident(wrt.shape, lambda b, i: (0, 0))],
        out_specs=[pl.BlockSpec((1, tm, D), lambda b, i: (b, i, 0)),
                   pl.BlockSpec((1, tm, D // 2), lambda b, i: (b, i, 0)),
                   pl.BlockSpec((1, E, tm), lambda b, i: (b, 0, i))],
        out_shape=[jax.ShapeDtypeStruct((B, S, D), F32),
                   jax.ShapeDtypeStruct((B, S, D // 2), jnp.int32),
                   jax.ShapeDtypeStruct((B, E, S), F32)],
        compiler_params=_cparams(("arbitrary", "arbitrary")),
        name="merge",
    )(x, yr, yd, proj, proj, mod, wr, wd, wo, g, wrt)


def _lane_prefix(m, tri):
    E, S = m.shape
    off = jnp.zeros((E, 1), F32)
    parts = []
    for j in range(S // LANES):
        blk = m[:, j * LANES:(j + 1) * LANES]
        parts.append(jnp.dot(blk.astype(BF16), tri, preferred_element_type=F32) + off)
        off = off + jnp.sum(blk, axis=1, keepdims=True)
    return jnp.concatenate(parts, axis=1)


def _route_kernel(lg_ref, rank_ref, gate_ref, *, cap):
    for j in range(lg_ref.shape[0]):
        rank_ref[j], gate_ref[j] = _route_one(lg_ref[j], cap)


def _route_one(lg, cap):
    e = jnp.exp(lg - jnp.max(lg, axis=0, keepdims=True))
    aff = e / jnp.sum(e, axis=0, keepdims=True)
    bits = pltpu.bitcast(aff, jnp.int32)
    E = lg.shape[0]

    def count(mask):
        return jnp.sum(jnp.where(mask, 1.0, 0.0), axis=1, keepdims=True)

    def reaches(cand):
        return count(bits >= cand) >= cap

    thr = jnp.zeros((E, 1), jnp.int32)
    for hi in range(30, 0, -2):
        c1, c2 = thr | (1 << hi), thr | (1 << (hi - 1))
        c3 = c1 | (1 << (hi - 1))
        thr = jnp.where(reaches(c3), c3, jnp.where(reaches(c1), c1, jnp.where(reaches(c2), c2, thr)))
    thr = jnp.where(reaches(thr | 1), thr | 1, thr)
    gt = bits > thr
    eq = bits == thr
    need = cap - count(gt)
    r = lax.broadcasted_iota(jnp.int32, (LANES, LANES), 0)
    c = lax.broadcasted_iota(jnp.int32, (LANES, LANES), 1)
    tri = jnp.where(r < c, 1.0, 0.0).astype(BF16)
    eq_before = _lane_prefix(jnp.where(eq, 1.0, 0.0), tri)
    sel = gt | (eq & (eq_before < need))
    slot = _lane_prefix(jnp.where(sel, 1.0, 0.0), tri)
    return jnp.where(sel, slot, -1.0).astype(jnp.int32), jnp.where(sel, aff, 0.0)


def _route(logits, cap):
    B, E, S = logits.shape
    G = ROUTE_SEQS_PER_STEP
    assert B % G == 0
    spec = pl.BlockSpec((G, E, S), lambda b: (b, 0, 0))
    return pl.pallas_call(
        functools.partial(_route_kernel, cap=cap),
        grid=(B // G,),
        in_specs=[spec],
        out_specs=[spec, spec],
        out_shape=[jax.ShapeDtypeStruct((B, E, S), jnp.int32),
                   jax.ShapeDtypeStruct((B, E, S), F32)],
        compiler_params=_cparams(("arbitrary",)),
        name="route",
    )(logits)


def _gather_rows(table, rank, cap):
    B, E, S = rank.shape
    W = table.shape[1]
    workers = SC_CORES * SC_SUBCORES
    pairs = E * B
    per = pairs // workers
    assert per * workers == pairs and B & (B - 1) == 0 and cap % SC_GATHER_ROWS == 0 and S % SC_LANES == 0
    shift = B.bit_length() - 1
    mesh = plsc.VectorSubcoreMesh(core_axis_name="c", subcore_axis_name="s")

    R = SC_GATHER_ROWS
    n_chunks = cap // R

    def body(table_hbm, rank_hbm, out_hbm, rank_v, idx_v, buf_v, gsem, wsem):
        wid = lax.axis_index("s") * SC_CORES + lax.axis_index("c")

        def fetch(c):
            return pltpu.make_async_copy(table_hbm.at[idx_v.at[pl.ds(c * R, R)]], buf_v.at[c % 2], gsem.at[c % 2])

        def flush(p, c):
            return pltpu.make_async_copy(buf_v.at[c % 2], out_hbm.at[pl.ds(p * cap + c * R, R)], wsem.at[c % 2])

        for j in range(per):
            p = wid * per + j
            e = lax.shift_right_logical(p, shift)
            b = p & (B - 1)
            pltpu.sync_copy(rank_hbm.at[b, e], rank_v)
            base = b * S

            @pl.loop(0, S, step=SC_LANES)
            def _(t0):
                r = rank_v[pl.ds(t0, SC_LANES)]
                tok = lax.iota(jnp.int32, SC_LANES) + (base + t0)
                plsc.store_scatter(idx_v, [r], tok, mask=r >= 0)

            fetch(0).start()
            for c in range(n_chunks):
                fetch(c).wait()
                if c >= 1:
                    flush(p, c - 1).wait()
                if c + 1 < n_chunks:
                    fetch(c + 1).start()
                flush(p, c).start()
            flush(p, n_chunks - 1).wait()

    return pl.kernel(
        body,
        out_type=jax.ShapeDtypeStruct((pairs * cap, W), jnp.int32),
        mesh=mesh,
        scratch_types=[pltpu.VMEM((S,), jnp.int32), pltpu.VMEM((cap,), jnp.int32),
                       pltpu.VMEM((2, R, W), jnp.int32),
                       pltpu.SemaphoreType.DMA((2,)), pltpu.SemaphoreType.DMA((2,))],
        compiler_params=pltpu.CompilerParams(needs_layout_passes=False),
    )(table, rank)


def _moe_kernel(rank_ref, gate_ref, x_ref, wg_ref, wu_ref, wd_ref, o_ref, wg_s, wu_s, wd_s, *, cap):
    r, b = pl.program_id(0), pl.program_id(1)
    n_exp = pl.num_programs(0) - 1

    @pl.when(r < n_exp)
    def _():
        slot = r % 2
        rows_in, rows_ff = wg_ref.shape[1], wd_ref.shape[1]
        wg_s[slot, pl.ds(pl.multiple_of(b * rows_in, rows_in), rows_in), :] = wg_ref[0].astype(BF16)
        wu_s[slot, pl.ds(pl.multiple_of(b * rows_in, rows_in), rows_in), :] = wu_ref[0].astype(BF16)
        wd_s[slot, pl.ds(pl.multiple_of(b * rows_ff, rows_ff), rows_ff), :] = wd_ref[0].astype(BF16)

    @pl.when(r == 0)
    def _():
        o_ref[...] = jnp.zeros_like(o_ref)

    @pl.when(r > 0)
    def _():
        slot = (r - 1) % 2
        S = rank_ref.shape[3]
        for j in range(x_ref.shape[0]):
            rank = rank_ref[j, 0]
            pick = lax.broadcasted_iota(jnp.int32, (cap, S), 0) == rank
            xin = _unpack_halves(x_ref[j])
            a = jnp.dot(xin, wg_s[slot], preferred_element_type=F32)
            u = jnp.dot(xin, wu_s[slot], preferred_element_type=F32)
            act = (a * jax.nn.sigmoid(a) * u).astype(BF16)
            y = jnp.dot(act, wd_s[slot], preferred_element_type=F32)
            g = jnp.sum(jnp.where(pick, gate_ref[j, 0], 0.0), axis=1, keepdims=True)
            o_ref[0, j] = (y * g).astype(BF16)


def _moe(rank, gate, xin, wg, wu, wd, cap):
    B, E, S = rank.shape
    D, Fd = wg.shape[1], wg.shape[2]
    G = MOE_SEQS_PER_STEP
    steps = B // G
    assert steps * G == B and D % steps == 0 and Fd % steps == 0

    def cur(r):
        return jnp.maximum(r - 1, 0)

    def nxt(r):
        return jnp.minimum(r, E - 1)

    row = pl.BlockSpec((G, 1, 1, S), lambda r, b: (b, cur(r), 0, 0))
    return pl.pallas_call(
        functools.partial(_moe_kernel, cap=cap),
        grid=(E + 1, steps),
        in_specs=[row, row,
                  pl.BlockSpec((G, cap, D // 2), lambda r, b: (cur(r) * steps + b, 0, 0)),
                  pl.BlockSpec((1, D // steps, Fd), lambda r, b: (nxt(r), b, 0)),
                  pl.BlockSpec((1, D // steps, Fd), lambda r, b: (nxt(r), b, 0)),
                  pl.BlockSpec((1, Fd // steps, D), lambda r, b: (nxt(r), b, 0))],
        out_specs=pl.BlockSpec((1, G, cap, D), lambda r, b: (jnp.where(r == 0, E, r - 1), b, 0, 0)),
        out_shape=jax.ShapeDtypeStruct((E + 1, B, cap, D), BF16),
        scratch_shapes=[pltpu.VMEM((2, D, Fd), BF16), pltpu.VMEM((2, D, Fd), BF16),
                        pltpu.VMEM((2, Fd, D), BF16)],
        compiler_params=_cparams(("arbitrary", "arbitrary")),
        name="moe",
    )(rank.reshape(B, E, 1, S), gate.reshape(B, E, 1, S), xin.reshape(E * B, cap, D // 2), wg, wu, wd)


def _scat_kernel(win_ref, ok_ref, rt_ref, y_ref, x1_ref, mod_ref, g_ref, o_ref, *, cap):
    b, i = pl.program_id(0), pl.program_id(1)
    rt = rt_ref[0]
    tm, E = rt.shape
    D = y_ref.shape[3]

    def finish(moe):
        x2 = x1_ref[0] + mod_ref[0, 5:6, :] * moe
        o_ref[0] = _rms(x2, g_ref[...])

    @pl.when(ok_ref[b, i] != 0)
    def _():
        slot = lax.broadcasted_iota(jnp.int32, (tm, SCAT_WINDOW), 1)
        moe = jnp.zeros((tm, D), F32)
        for e in range(0, E, 2):
            starts = [pl.multiple_of(win_ref[b, i, e + k], BF16_SUBLANES) for k in range(2)]
            place = jnp.concatenate(
                [jnp.where(rt[:, e + k:e + k + 1] - starts[k] == slot, 1.0, 0.0).astype(BF16)
                 for k in range(2)], axis=1)
            y = jnp.concatenate([y_ref[e + k, 0, pl.ds(starts[k], SCAT_WINDOW), :] for k in range(2)], axis=0)
            moe = moe + jnp.dot(place, y, preferred_element_type=F32)
        finish(moe)

    @pl.when(ok_ref[b, i] == 0)
    def _():
        slot = lax.broadcasted_iota(jnp.int32, (tm, cap), 1)
        place = jnp.concatenate(
            [jnp.where(rt[:, e:e + 1] == slot, 1.0, 0.0).astype(BF16) for e in range(E)], axis=1)
        y = y_ref[0:E, 0].reshape(E * cap, D)
        finish(jnp.dot(place, y, preferred_element_type=F32))


def _scat_windows(rank, cap, tm):
    B, E, S = rank.shape
    counts = jnp.sum((rank >= 0).reshape(B, E, S // tm, tm), axis=-1, dtype=jnp.int32)
    ends = jnp.cumsum(counts, axis=-1)
    start = ends - counts
    win = jnp.minimum(start // BF16_SUBLANES * BF16_SUBLANES, cap - SCAT_WINDOW)
    ok = jnp.all(ends - win <= SCAT_WINDOW, axis=1)
    return jnp.swapaxes(win, 1, 2), ok.astype(jnp.int32)


def _scat(rank, ybuf, x1, mod, g, cap):
    B, S, D = x1.shape
    E = rank.shape[1]
    tm = TOKEN_TILE
    win, ok = _scat_windows(rank, cap, tm)
    smem = pl.BlockSpec(memory_space=pltpu.SMEM)
    return pl.pallas_call(
        functools.partial(_scat_kernel, cap=cap),
        grid=(B, S // tm),
        in_specs=[smem, smem,
                  pl.BlockSpec((1, tm, E), lambda b, i: (b, i, 0)),
                  pl.BlockSpec(ybuf.shape[:1] + (1, cap, D), lambda b, i: (0, b, 0, 0)),
                  pl.BlockSpec((1, tm, D), lambda b, i: (b, i, 0)),
                  pl.BlockSpec((1, 6, D), lambda b, i: (b, 0, 0)),
                  _resident((1, D), lambda b, i: (0, 0))],
        out_specs=pl.BlockSpec((1, tm, D), lambda b, i: (b, i, 0)),
        out_shape=jax.ShapeDtypeStruct((B, S, D), F32),
        compiler_params=_cparams(("arbitrary", "arbitrary")),
        name="scat",
    )(win, ok, jnp.swapaxes(rank, 1, 2), ybuf, x1, mod, g)


def _rot_tables(S):
    half = RET_DK // 2
    inv = 1.0 / (RET_THETA_BASE ** np.linspace(0.0, 1.0, half))
    ang = np.arange(S, dtype=np.float64)[:, None] * inv[None, :]
    cos = np.repeat(np.cos(ang), 2, axis=1)
    sin = np.stack([-np.sin(ang), np.sin(ang)], axis=2).reshape(S, RET_DK)
    sc = RET_DK ** -0.5
    return jnp.asarray(np.stack([cos, sin, cos * sc, sin * sc]), F32)


def _ret_tables():
    C = RET_CHUNK
    heads = np.arange(RET_HEADS, dtype=np.float64)
    lgf = np.log1p(-np.exp2(-RET_FWD_DECAY_OFFSET - heads))[:, None]
    lgb = np.log1p(-np.exp2(-RET_BWD_DECAY_OFFSET - heads))[:, None]
    idx = np.arange(C, dtype=np.float64)
    diff = idx[:, None] - idx[None, :]
    dmat = np.where(diff >= 0,
                    np.exp(np.maximum(diff, 0.0)[None] * lgf[:, :, None]),
                    np.exp(np.maximum(-diff, 0.0)[None] * lgb[:, :, None]))
    dec = np.stack([np.exp((idx + 1)[None, :] * lgf),
                    np.exp((C - 1 - idx)[None, :] * lgf),
                    np.exp((C - idx)[None, :] * lgb),
                    np.exp(idx[None, :] * lgb)], axis=1)
    dec = np.broadcast_to(dec[..., None], dec.shape + (LANES,))
    cd = np.concatenate([np.exp(C * lgf), np.exp(C * lgb)], axis=1)
    return jnp.asarray(cd, F32), jnp.asarray(dec, F32), jnp.asarray(dmat, F32)


def _t5_bucket(rel):
    nb = N_BUCKETS // 2
    max_exact = nb // 2
    ret = (rel > 0).astype(jnp.int32) * nb
    n = jnp.abs(rel)
    large = max_exact + (jnp.log(jnp.maximum(n, 1).astype(F32) / max_exact)
                         / math.log(MAX_DISTANCE / max_exact) * (nb - max_exact)).astype(jnp.int32)
    large = jnp.minimum(large, nb - 1)
    return ret + jnp.where(n < max_exact, n, large)


def _bias_rows(rel_bias):
    reach = BIAS_BAND - 1
    rel = jnp.concatenate([jnp.arange(reach, -reach - 1, -1, dtype=jnp.int32),
                           jnp.array([-(2 ** 20), 2 ** 20], jnp.int32)])
    f = rel_bias[_t5_bucket(rel)].astype(F32).T * LOG2E
    rows = jnp.pad(f[:, :-2], ((0, 0), (0, 2 * BIAS_BAND - (2 * reach + 1))))
    stats = jnp.stack([jnp.max(f, axis=1), f[:, -2], f[:, -1]], axis=1)
    return rows[:, None, :], stats


def kernel(x, c, w_ada, b_ada, norm_mix_g, w_in, ret_gn_g, diff_subln_g, lambda_q1, lambda_k1, lambda_q2, lambda_k2, w_ret_out, w_diff_out, w_o, rel_bias, norm_ffn_g, w_router, w_exp_gate, w_exp_up, w_exp_down, final_g):
    B, S, D = x.shape
    cap = CAPACITY_FACTOR * S // N_EXPERTS
    l = 0

    mod = _ada(c, w_ada[l], b_ada[l]).reshape(B, 6, D)
    proj, vt = _inproj(x, norm_mix_g[l].reshape(1, D), mod, w_in[l].astype(BF16), _rot_tables(S))

    cd, dec, dmat = _ret_tables()
    y_ret = _ret(proj, cd, dec, dmat, ret_gn_g[l].reshape(1, -1))

    lam_vecs = jnp.stack([lambda_q1[l], lambda_k1[l], lambda_q2[l], lambda_k2[l]]).astype(F32)
    brow, bstat = _bias_rows(rel_bias)
    subln = jnp.broadcast_to(diff_subln_g[l].astype(F32)[:, None], (DIFF_DV, Q_TILE))
    y_diff = _diff(proj, vt, lam_vecs, brow, bstat, subln)

    x1, h2, logits = _merge(x, y_ret, y_diff, proj, mod,
                            w_ret_out[l].astype(BF16), w_diff_out[l].astype(BF16), w_o[l].astype(BF16),
                            norm_ffn_g[l].reshape(1, D), w_router[l].T.astype(BF16))

    rank, gate = _route(logits, cap)
    xin = _gather_rows(h2.reshape(B * S, D // 2), rank, cap)
    ybuf = _moe(rank, gate, xin, w_exp_gate[l], w_exp_up[l], w_exp_down[l], cap)
    return _scat(rank, ybuf, x1, mod, final_g.reshape(1, D), cap)
```

```python
import functools
import math

import numpy as np
import jax
import jax.numpy as jnp
from jax import lax
from jax.experimental import pallas as pl
from jax.experimental.pallas import tpu as pltpu
from jax.experimental.pallas import tpu_sc as plsc

F32 = jnp.float32
BF16 = jnp.bfloat16

RET_HEADS = 4
RET_DK = 128
RET_DV = 128
RET_FWD_DECAY_OFFSET = 5.0
RET_BWD_DECAY_OFFSET = 5.5
RET_THETA_BASE = 10000.0
DIFF_HEADS = 4
DIFF_DH = 64
DIFF_DV = 2 * DIFF_DH
N_BUCKETS = 32
MAX_DISTANCE = 128
N_EXPERTS = 16
CAPACITY_FACTOR = 2
NORM_EPS = 1e-6
LAM_INIT = 0.8 - 0.6 * math.exp(-0.3 * 0)
LOG2E = math.log2(math.e)

LANES = 128
BF16_SUBLANES = 16
VMEM_LIMIT_BYTES = 56 * 2**20

SC_CORES = 2
SC_SUBCORES = 16
SC_LANES = 16
SC_GATHER_ROWS = 64

TOKEN_TILE = 512
PROJ_TILE = 1024
RET_CHUNK = 256
Q_TILE = 256
KEY_CHUNK = 512
DIFF_TILES_PER_LOOP = 8
L_MIN = 2.0 ** -60
BIAS_BAND = Q_TILE + 2 * MAX_DISTANCE
SCAT_WINDOW = 128
MOE_SEQS_PER_STEP = 4
ROUTE_SEQS_PER_STEP = 4
MOE_GROUPS = 2

W_RQ, W_RK, W_RV, W_RG = 0, 4, 8, 12
W_DQ1, W_DQ2, W_DK1, W_DK2, W_DV = 16, 18, 20, 22, 24
W_GR, W_GD = 28, 36

COL_GR, COL_GD = 0, 8
COL_RQ, COL_RK, COL_RV, COL_RG = 16, 20, 24, 28
COL_DQ, COL_DK = 32, 36
PROJ_COLS = 40 * LANES


def _cparams(sem):
    return pltpu.CompilerParams(dimension_semantics=sem, vmem_limit_bytes=VMEM_LIMIT_BYTES)


def _resident(shape, index_map):
    return pl.BlockSpec(shape, index_map, pipeline_mode=pl.Buffered(1))


def _ada_kernel(c_ref, w_ref, b_ref, o_ref):
    c = c_ref[...]
    a = c * jax.nn.sigmoid(c)
    o_ref[...] = jnp.dot(a, w_ref[...], preferred_element_type=F32,
                         precision=lax.Precision.HIGHEST) + b_ref[...]


def _ada(c, w, b):
    B, D = c.shape
    n = w.shape[1] // D
    return pl.pallas_call(
        _ada_kernel,
        grid=(n,),
        in_specs=[pl.BlockSpec((B, D), lambda j: (0, 0)),
                  pl.BlockSpec((D, D), lambda j: (0, j)),
                  pl.BlockSpec((1, D), lambda j: (0, j))],
        out_specs=pl.BlockSpec((B, D), lambda j: (0, j)),
        out_shape=jax.ShapeDtypeStruct((B, n * D), F32),
        compiler_params=_cparams(("arbitrary",)),
        name="ada",
    )(c, w, b.reshape(1, -1))


def _rms(x, g):
    ms = jnp.mean(x * x, axis=-1, keepdims=True)
    return x * lax.rsqrt(ms + NORM_EPS) * g


def _rot_pairs(x, cos, sin):
    even = (lax.broadcasted_iota(jnp.int32, x.shape, 1) & 1) == 0
    partner = jnp.where(even, pltpu.roll(x, LANES - 1, axis=1), pltpu.roll(x, 1, axis=1))
    return x * cos + partner * sin


def _pair_maps(a, b, second):
    low = lax.broadcasted_iota(jnp.int32, a.shape, 1) < DIFF_DH
    if second:
        return jnp.where(low, pltpu.roll(a, DIFF_DH, axis=1), b)
    return jnp.where(low, a, pltpu.roll(b, DIFF_DH, axis=1))


def _inproj_kernel(x_ref, g_ref, mod_ref, w_ref, rot_ref, o_ref, vt_ref, wvt_s):
    @pl.when((pl.program_id(0) == 0) & (pl.program_id(1) == 0))
    def _():
        wvt_s[...] = w_ref[:, W_DV * LANES:(W_DV + 4) * LANES].astype(F32).T.astype(BF16)

    parts = 4
    step = x_ref.shape[1] // parts

    def normed(p):
        x = x_ref[0, p * step:(p + 1) * step, :]
        h = _rms(x, g_ref[...]) * (1.0 + mod_ref[0, 1:2, :]) + mod_ref[0, 0:1, :]
        return h.astype(BF16)

    def project(p, hb):
        rows = slice(p * step, (p + 1) * step)

        def mm(col, width):
            return jnp.dot(hb, w_ref[:, col * LANES:(col + width) * LANES], preferred_element_type=F32)

        def put(col, val):
            o_ref[0, rows, col * LANES:col * LANES + val.shape[1]] = val.astype(BF16)

        for j in range(0, COL_GD - COL_GR, 4):
            put(COL_GR + j, jax.nn.sigmoid(mm(W_GR + j, 4)))
            put(COL_GD + j, jax.nn.sigmoid(mm(W_GD + j, 4)))
        for src, dst, t in ((W_RQ, COL_RQ, 0), (W_RK, COL_RK, 2)):
            cos, sin = rot_ref[t, rows, :], rot_ref[t + 1, rows, :]
            r = mm(src, RET_HEADS)
            for hh in range(RET_HEADS):
                put(dst + hh, _rot_pairs(r[:, hh * LANES:(hh + 1) * LANES], cos, sin))
        put(COL_RV, mm(W_RV, 4))
        r = mm(W_RG, 4)
        put(COL_RG, r * jax.nn.sigmoid(r))
        for src1, src2, dst, scale in ((W_DQ1, W_DQ2, COL_DQ, DIFF_DH ** -0.5 * LOG2E),
                                       (W_DK1, W_DK2, COL_DK, None)):
            m1, m2 = mm(src1, 2), mm(src2, 2)
            for hh in range(DIFF_HEADS):
                blk = slice((hh // 2) * LANES, (hh // 2 + 1) * LANES)
                val = _pair_maps(m1[:, blk], m2[:, blk], hh % 2 == 1)
                put(dst + hh, val if scale is None else val * scale)
        vt_ref[0, :, rows] = lax.dot_general(wvt_s[...], hb, (((1,), (1,)), ((), ())),
                                             preferred_element_type=F32).astype(BF16)

    hb = normed(0)
    for p in range(parts):
        nxt = normed(p + 1) if p + 1 < parts else None
        project(p, hb)
        hb = nxt


def _inproj(x, g, mod, w, rot):
    B, S, D = x.shape
    tm = PROJ_TILE
    vw = DIFF_HEADS * DIFF_DV
    return pl.pallas_call(
        _inproj_kernel,
        grid=(B, S // tm),
        in_specs=[pl.BlockSpec((1, tm, D), lambda b, i: (b, i, 0)),
                  _resident((1, D), lambda b, i: (0, 0)),
                  pl.BlockSpec((1, 6, D), lambda b, i: (b, 0, 0)),
                  _resident(w.shape, lambda b, i: (0, 0)),
                  pl.BlockSpec((4, tm, LANES), lambda b, i: (0, i, 0))],
        out_specs=[pl.BlockSpec((1, tm, PROJ_COLS), lambda b, i: (b, i, 0)),
                   pl.BlockSpec((1, vw, tm), lambda b, i: (b, 0, i))],
        out_shape=[jax.ShapeDtypeStruct((B, S, PROJ_COLS), BF16),
                   jax.ShapeDtypeStruct((B, vw, S), BF16)],
        scratch_shapes=[pltpu.VMEM((vw, D), BF16)],
        compiler_params=_cparams(("arbitrary", "arbitrary")),
        name="inproj",
    )(x, g, mod, w, rot)


def _ret_kernel(cd_ref, q_ref, k_ref, v_ref, rg_ref, dec_ref, dm_ref, gn_ref, o_ref, acc_ref):
    S = q_ref.shape[1]
    H = dm_ref.shape[0]
    C = RET_CHUNK
    nc = S // C
    nt = (((1,), (1,)), ((), ()))
    tn = (((0,), (0,)), ((), ()))

    def chunk(n, h):
        sl, lanes = pl.ds(n * C, C), slice(h * LANES, (h + 1) * LANES)
        return sl, q_ref[0, sl, lanes], k_ref[0, sl, lanes], v_ref[0, sl, lanes]

    def scaled(t, dec):
        return (t.astype(F32) * dec).astype(BF16)

    def forward(heads):
        state = {h: jnp.zeros((RET_DK, RET_DV), F32) for h in heads}
        for n in range(nc):
            for h in heads:
                sl, q, k, v = chunk(n, h)
                s = lax.dot_general(q, k, nt, preferred_element_type=F32) * dm_ref[h]
                inner = jnp.dot(s.astype(BF16), v, preferred_element_type=F32)
                cross = jnp.dot(scaled(q, dec_ref[h, 0]), state[h].astype(BF16), preferred_element_type=F32)
                acc_ref[h, sl, :] = inner + cross
                kv = lax.dot_general(scaled(k, dec_ref[h, 1]), v, tn, preferred_element_type=F32)
                state[h] = cd_ref[h, 0] * state[h] + kv

    def backward(heads):
        state = {h: jnp.zeros((RET_DK, RET_DV), F32) for h in heads}
        for n in reversed(range(nc)):
            for h in heads:
                sl, q, k, v = chunk(n, h)
                cross = jnp.dot(scaled(q, dec_ref[h, 2]), state[h].astype(BF16), preferred_element_type=F32)
                acc_ref[h, sl, :] = acc_ref[h, sl, :] + cross
                kv = lax.dot_general(scaled(k, dec_ref[h, 3]), v, tn, preferred_element_type=F32)
                state[h] = cd_ref[h, 1] * state[h] + kv

    def normalise(heads):
        for h in heads:
            lanes = slice(h * LANES, (h + 1) * LANES)
            y = acc_ref[h]
            mu = jnp.mean(y, axis=-1, keepdims=True)
            yc = y - mu
            var = jnp.mean(yc * yc, axis=-1, keepdims=True)
            yn = yc * lax.rsqrt(var + NORM_EPS) * gn_ref[:, lanes]
            o_ref[0, :, lanes] = (rg_ref[0, :, lanes].astype(F32) * yn).astype(BF16)

    first, second = tuple(range(H // 2)), tuple(range(H // 2, H))
    forward(first)
    backward(first)
    forward(second)
    normalise(first)
    backward(second)
    normalise(second)


def _ret(proj, cd, dec, dmat, gn):
    B, S, _ = proj.shape
    H = RET_HEADS
    C = RET_CHUNK

    def cols(base):
        return pl.BlockSpec((1, S, H * LANES), lambda b: (b, 0, base // H))

    return pl.pallas_call(
        _ret_kernel,
        grid=(B,),
        in_specs=[pl.BlockSpec(memory_space=pltpu.SMEM),
                  cols(COL_RQ), cols(COL_RK), cols(COL_RV), cols(COL_RG),
                  _resident((H, 4, C, LANES), lambda b: (0, 0, 0, 0)),
                  _resident((H, C, C), lambda b: (0, 0, 0)),
                  _resident((1, H * LANES), lambda b: (0, 0))],
        out_specs=pl.BlockSpec((1, S, H * LANES), lambda b: (b, 0, 0)),
        out_shape=jax.ShapeDtypeStruct((B, S, H * RET_DV), BF16),
        scratch_shapes=[pltpu.VMEM((H, S, RET_DV), F32)],
        compiler_params=_cparams(("arbitrary",)),
        name="ret",
    )(cd, proj, proj, proj, proj, dec, dmat, gn)


def _diff_kernel(bstat_ref, lam_ref, q_ref, k_ref, vt_ref, brow_ref, g_ref, o_ref, p_ref, t_ref):
    hh = pl.program_id(0)
    S = k_ref.shape[1]
    TQ, KB = Q_TILE, KEY_CHUNK
    nk = S // KB
    lv = lam_ref[...]
    lam = (jnp.exp(jnp.sum(lv[0:1] * lv[1:2], axis=1, keepdims=True))
           - jnp.exp(jnp.sum(lv[2:3] * lv[3:4], axis=1, keepdims=True)) + LAM_INIT)
    first_half = lax.broadcasted_iota(jnp.int32, (TQ, LANES), 1) < DIFF_DH
    nt = (((1,), (1,)), ((), ()))
    bmax = bstat_ref[hh, 0]

    @pl.when(pl.program_id(1) == 0)
    def _():
        side = S - TQ - MAX_DISTANCE
        t_ref[0:side, :] = jnp.full((side, TQ), bstat_ref[hh, 1], F32)
        rows = jnp.broadcast_to(brow_ref[0], (BIAS_BAND, brow_ref.shape[2]))
        shift = brow_ref.shape[2] - (BIAS_BAND - 1 + MAX_DISTANCE)
        t_ref[side:side + BIAS_BAND, :] = pltpu.roll(rows, shift, 1, stride=1, stride_axis=0)[:, :TQ]
        t_ref[side + BIAS_BAND:, :] = jnp.full((side, TQ), bstat_ref[hh, 2], F32)
    kf = k_ref[0].astype(F32)
    kmax = jnp.sqrt(jnp.max(jnp.sum(kf * kf, axis=1, keepdims=True), axis=0, keepdims=True))
    sel_r = lax.broadcasted_iota(jnp.int32, (8, LANES), 0)
    sel_c = lax.broadcasted_iota(jnp.int32, (8, LANES), 1)
    sel = jnp.where((sel_c < DIFF_DH) == (sel_r == 0), 1.0, 0.0)
    sel = jnp.where(sel_r < 2, sel, 0.0).astype(BF16)

    def masked_q(qb):
        q = q_ref[0, pl.ds(pl.multiple_of(qb * TQ, TQ), TQ), :]
        zero = jnp.zeros_like(q)
        return q, (jnp.where(first_half, q, zero), jnp.where(first_half, zero, q))

    def finish(ot, qb):
        ms = jnp.mean(ot * ot, axis=0, keepdims=True)
        y = ot * lax.rsqrt(ms + NORM_EPS) * g_ref[...] * (1.0 - LAM_INIT)
        o_ref[0, pl.ds(pl.multiple_of(qb * TQ, TQ), TQ), :] = y.T.astype(BF16)

    def exponentials(qb, slot):
        w0 = pl.multiple_of((S - TQ) - qb * TQ, TQ)
        q, qz = masked_q(qb)
        qsq = (q.astype(F32) * q.astype(F32)).astype(BF16)
        n2 = lax.dot_general(sel, qsq, nt, preferred_element_type=F32) * 1.01
        m = [jnp.sqrt(n2[i:i + 1]) * kmax + bmax for i in range(2)]
        l = [jnp.zeros((8, TQ), F32) for _ in range(2)]
        for c in range(nk):
            rows = slice(c * KB, (c + 1) * KB)
            kb = k_ref[0, rows, :]
            bias = t_ref[pl.ds(w0 + c * KB, KB), :]
            for i in range(2):
                s = lax.dot_general(kb, qz[i], nt, preferred_element_type=F32)
                p = jnp.exp2((s - m[i]) + bias)
                l[i] = l[i] + jnp.sum(p.reshape(KB // 8, 8, TQ), axis=0)
                p_ref[slot, i, rows, :] = p.astype(BF16)
        return [jnp.sum(l[i], axis=0, keepdims=True) for i in range(2)]

    def values(qb, slot, lr):
        ratio = jnp.broadcast_to(lam * lr[0] / lr[1], (16, TQ)).astype(BF16)
        ot = jnp.zeros((DIFF_DV, TQ), F32)
        for c in range(nk):
            rows = slice(c * KB, (c + 1) * KB)
            p1 = p_ref[slot, 0, rows, :].reshape(KB // 16, 16, TQ)
            p2 = p_ref[slot, 1, rows, :].reshape(KB // 16, 16, TQ)
            a = (p1 - ratio * p2).reshape(KB, TQ)
            ot = ot + jnp.dot(vt_ref[0, :, rows], a, preferred_element_type=F32)
        finish(ot * (1.0 / lr[0]), qb)

    def fast_group(j, lmin):
        base = j * DIFF_TILES_PER_LOOP
        sums = exponentials(base, 0)
        for t in range(DIFF_TILES_PER_LOOP):
            lmin = jnp.minimum(lmin, jnp.minimum(sums[0], sums[1]))
            nxt = exponentials(base + t + 1, (t + 1) % 2) if t + 1 < DIFF_TILES_PER_LOOP else None
            values(base + t, t % 2, sums)
            sums = nxt
        return lmin

    lmin = lax.fori_loop(0, S // (DIFF_TILES_PER_LOOP * TQ), fast_group, jnp.full((1, TQ), jnp.inf, F32))

    @pl.when(jnp.min(lmin) < L_MIN)
    def _():
        def exact_tile(qb, carry):
            w0 = pl.multiple_of((S - TQ) - qb * TQ, TQ)
            _, qz = masked_q(qb)
            bias = t_ref[pl.ds(w0, S), :]
            attn = None
            for i, scale in ((0, 1.0), (1, lam)):
                s = lax.dot_general(k_ref[0], qz[i], nt, preferred_element_type=F32) + bias
                p = jnp.exp2(s - jnp.max(s, axis=0, keepdims=True))
                part = p * (scale / jnp.sum(p, axis=0, keepdims=True))
                attn = part if attn is None else attn - part
            finish(jnp.dot(vt_ref[0], attn.astype(BF16), preferred_element_type=F32), qb)
            return carry

        lax.fori_loop(0, S // TQ, exact_tile, 0)


def _diff(proj, vt, lam_vecs, brow, bstat, g):
    B, S, _ = proj.shape
    H = DIFF_HEADS

    def col(base):
        return pl.BlockSpec((1, S, LANES), lambda h, b: (b, 0, base + h))

    return pl.pallas_call(
        _diff_kernel,
        grid=(H, B),
        in_specs=[pl.BlockSpec(memory_space=pltpu.SMEM),
                  pl.BlockSpec((4, DIFF_DH), lambda h, b: (0, 0)),
                  col(COL_DQ), col(COL_DK),
                  pl.BlockSpec((1, DIFF_DV, S), lambda h, b: (b, h, 0)),
                  pl.BlockSpec((1, 1, brow.shape[2]), lambda h, b: (h, 0, 0)),
                  pl.BlockSpec((DIFF_DV, Q_TILE), lambda h, b: (0, 0))],
        out_specs=pl.BlockSpec((1, S, LANES), lambda h, b: (b, 0, h)),
        out_shape=jax.ShapeDtypeStruct((B, S, H * DIFF_DV), BF16),
        scratch_shapes=[pltpu.VMEM((2, 2, S, Q_TILE), BF16), pltpu.VMEM((2 * S - Q_TILE, Q_TILE), F32)],
        compiler_params=_cparams(("arbitrary", "arbitrary")),
        name="diff",
    )(bstat, lam_vecs, proj, proj, vt, brow, g)


def _pack_halves(x):
    w = x.shape[1] // 2
    lo = pltpu.bitcast(x[:, :w].astype(BF16).astype(F32), jnp.int32)
    hi = pltpu.bitcast(x[:, w:].astype(BF16).astype(F32), jnp.int32)
    return lax.shift_right_logical(lo, jnp.full_like(lo, 16)) | (hi & jnp.int32(-65536))


def _unpack_halves(words):
    lo = pltpu.bitcast(lax.shift_left(words, jnp.full_like(words, 16)), F32)
    hi = pltpu.bitcast(words & jnp.int32(-65536), F32)
    return jnp.concatenate([lo, hi], axis=1).astype(BF16)


def _merge_kernel(x_ref, yr_ref, yd_ref, gr_ref, gd_ref, mod_ref, wr_ref, wd_ref, wo_ref,
                  g_ref, wrt_ref, x1_ref, h2_ref, lg_ref):
    parts = 4
    step = x_ref.shape[1] // parts

    def project(r):
        rows = slice(r * step, (r + 1) * step)
        a = jnp.dot(yr_ref[0, rows, :], wr_ref[...], preferred_element_type=F32)
        d = jnp.dot(yd_ref[0, rows, :], wd_ref[...], preferred_element_type=F32)
        merged = gr_ref[0, rows, :].astype(F32) * a + gd_ref[0, rows, :].astype(F32) * d
        return jnp.dot(merged.astype(BF16), wo_ref[...], preferred_element_type=F32)

    def epilogue(r, o):
        rows = slice(r * step, (r + 1) * step)
        x1 = x_ref[0, rows, :] + mod_ref[0, 2:3, :] * o
        x1_ref[0, rows, :] = x1
        h2 = _rms(x1, g_ref[...]) * (1.0 + mod_ref[0, 4:5, :]) + mod_ref[0, 3:4, :]
        h2_ref[0, rows, :] = _pack_halves(h2)
        lg_ref[0, :, rows] = lax.dot_general(wrt_ref[...], h2.astype(BF16), (((1,), (1,)), ((), ())),
                                             preferred_element_type=F32)

    o = project(0)
    for r in range(parts):
        nxt = project(r + 1) if r + 1 < parts else None
        epilogue(r, o)
        o = nxt


def _merge(x, yr, yd, proj, mod, wr, wd, wo, g, wrt):
    B, S, D = x.shape
    tm = PROJ_TILE
    E = wrt.shape[0]
    gate_w = COL_GD - COL_GR
    return pl.pallas_call(
        _merge_kernel,
        grid=(B, S // tm),
        in_specs=[pl.BlockSpec((1, tm, D), lambda b, i: (b, i, 0)),
                  pl.BlockSpec((1, tm, yr.shape[2]), lambda b, i: (b, i, 0)),
                  pl.BlockSpec((1, tm, yd.shape[2]), lambda b, i: (b, i, 0)),
                  pl.BlockSpec((1, tm, gate_w * LANES), lambda b, i: (b, i, COL_GR // gate_w)),
                  pl.BlockSpec((1, tm, gate_w * LANES), lambda b, i: (b, i, COL_GD // gate_w)),
                  pl.BlockSpec((1, 6, D), lambda b, i: (b, 0, 0)),
                  _resident(wr.shape, lambda b, i: (0, 0)),
                  _resident(wd.shape, lambda b, i: (0, 0)),
                  _resident(wo.shape, lambda b, i: (0, 0)),
                  _resident((1, D), lambda b, i: (0, 0)),
                  _resident(wrt.shape, lambda b, i: (0, 0))],
        out_specs=[pl.BlockSpec((1, tm, D), lambda b, i: (b, i, 0)),
                   pl.BlockSpec((1, tm, D // 2), lambda b, i: (b, i, 0)),
                   pl.BlockSpec((1, E, tm), lambda b, i: (b, 0, i))],
        out_shape=[jax.ShapeDtypeStruct((B, S, D), F32),
                   jax.ShapeDtypeStruct((B, S, D // 2), jnp.int32),
                   jax.ShapeDtypeStruct((B, E, S), F32)],
        compiler_params=_cparams(("arbitrary", "arbitrary")),
        name="merge",
    )(x, yr, yd, proj, proj, mod, wr, wd, wo, g, wrt)


def _lane_prefix(m, tri):
    E, S = m.shape
    off = jnp.zeros((E, 1), F32)
    parts = []
    for j in range(S // LANES):
        blk = m[:, j * LANES:(j + 1) * LANES]
        parts.append(jnp.dot(blk.astype(BF16), tri, preferred_element_type=F32) + off)
        off = off + jnp.sum(blk, axis=1, keepdims=True)
    return jnp.concatenate(parts, axis=1)


def _route_kernel(lg_ref, rank_ref, gate_ref, *, cap):
    for j in range(lg_ref.shape[0]):
        rank_ref[j], gate_ref[j] = _route_one(lg_ref[j], cap)


def _route_one(lg, cap):
    e = jnp.exp(lg - jnp.max(lg, axis=0, keepdims=True))
    aff = e / jnp.sum(e, axis=0, keepdims=True)
    bits = pltpu.bitcast(aff, jnp.int32)
    E = lg.shape[0]

    def count(mask):
        return jnp.sum(jnp.where(mask, 1.0, 0.0), axis=1, keepdims=True)

    def reaches(cand):
        return count(bits >= cand) >= cap

    thr = jnp.zeros((E, 1), jnp.int32)
    for hi in range(30, 0, -2):
        c1, c2 = thr | (1 << hi), thr | (1 << (hi - 1))
        c3 = c1 | (1 << (hi - 1))
        thr = jnp.where(reaches(c3), c3, jnp.where(reaches(c1), c1, jnp.where(reaches(c2), c2, thr)))
    thr = jnp.where(reaches(thr | 1), thr | 1, thr)
    gt = bits > thr
    eq = bits == thr
    need = cap - count(gt)
    r = lax.broadcasted_iota(jnp.int32, (LANES, LANES), 0)
    c = lax.broadcasted_iota(jnp.int32, (LANES, LANES), 1)
    tri = jnp.where(r < c, 1.0, 0.0).astype(BF16)
    eq_before = _lane_prefix(jnp.where(eq, 1.0, 0.0), tri)
    sel = gt | (eq & (eq_before < need))
    slot = _lane_prefix(jnp.where(sel, 1.0, 0.0), tri)
    return jnp.where(sel, slot, -1.0).astype(jnp.int32), jnp.where(sel, aff, 0.0)


def _route(logits, cap):
    B, E, S = logits.shape
    G = ROUTE_SEQS_PER_STEP
    assert B % G == 0
    spec = pl.BlockSpec((G, E, S), lambda b: (b, 0, 0))
    return pl.pallas_call(
        functools.partial(_route_kernel, cap=cap),
        grid=(B // G,),
        in_specs=[spec],
        out_specs=[spec, spec],
        out_shape=[jax.ShapeDtypeStruct((B, E, S), jnp.int32),
                   jax.ShapeDtypeStruct((B, E, S), F32)],
        compiler_params=_cparams(("arbitrary",)),
        name="route",
    )(logits)


def _gather_rows(table, rank, cap):
    B, E, S = rank.shape
    W = table.shape[1]
    workers = SC_CORES * SC_SUBCORES
    pairs = E * B
    per = pairs // workers
    assert per * workers == pairs and B & (B - 1) == 0 and cap % SC_GATHER_ROWS == 0 and S % SC_LANES == 0
    shift = B.bit_length() - 1
    mesh = plsc.VectorSubcoreMesh(core_axis_name="c", subcore_axis_name="s")

    R = SC_GATHER_ROWS
    n_chunks = cap // R

    def body(table_hbm, rank_hbm, out_hbm, rank_v, idx_v, buf_v, gsem, wsem):
        wid = lax.axis_index("s") * SC_CORES + lax.axis_index("c")

        def fetch(c):
            return pltpu.make_async_copy(table_hbm.at[idx_v.at[pl.ds(c * R, R)]], buf_v.at[c % 2], gsem.at[c % 2])

        def flush(p, c):
            return pltpu.make_async_copy(buf_v.at[c % 2], out_hbm.at[pl.ds(p * cap + c * R, R)], wsem.at[c % 2])

        for j in range(per):
            p = wid * per + j
            e = lax.shift_right_logical(p, shift)
            b = p & (B - 1)
            pltpu.sync_copy(rank_hbm.at[b, e], rank_v)
            base = b * S

            @pl.loop(0, S, step=SC_LANES)
            def _(t0):
                r = rank_v[pl.ds(t0, SC_LANES)]
                tok = lax.iota(jnp.int32, SC_LANES) + (base + t0)
                plsc.store_scatter(idx_v, [r], tok, mask=r >= 0)

            fetch(0).start()
            for c in range(n_chunks):
                fetch(c).wait()
                if c >= 1:
                    flush(p, c - 1).wait()
                if c + 1 < n_chunks:
                    fetch(c + 1).start()
                flush(p, c).start()
            flush(p, n_chunks - 1).wait()

    return pl.kernel(
        body,
        out_type=jax.ShapeDtypeStruct((pairs * cap, W), jnp.int32),
        mesh=mesh,
        scratch_types=[pltpu.VMEM((S,), jnp.int32), pltpu.VMEM((cap,), jnp.int32),
                       pltpu.VMEM((2, R, W), jnp.int32),
                       pltpu.SemaphoreType.DMA((2,)), pltpu.SemaphoreType.DMA((2,))],
        compiler_params=pltpu.CompilerParams(needs_layout_passes=False),
    )(table, rank)


def _moe_kernel(rank_ref, gate_ref, x_ref, wg_ref, wu_ref, wd_ref, o_ref, wg_s, wu_s, wd_s, *, cap):
    r, b = pl.program_id(0), pl.program_id(1)
    n_exp = pl.num_programs(0) - 1

    @pl.when(r < n_exp)
    def _():
        slot = r % 2
        rows_in, rows_ff = wg_ref.shape[1], wd_ref.shape[1]
        wg_s[slot, pl.ds(pl.multiple_of(b * rows_in, rows_in), rows_in), :] = wg_ref[0].astype(BF16)
        wu_s[slot, pl.ds(pl.multiple_of(b * rows_in, rows_in), rows_in), :] = wu_ref[0].astype(BF16)
        wd_s[slot, pl.ds(pl.multiple_of(b * rows_ff, rows_ff), rows_ff), :] = wd_ref[0].astype(BF16)

    @pl.when(r == 0)
    def _():
        o_ref[...] = jnp.zeros_like(o_ref)

    @pl.when(r > 0)
    def _():
        slot = (r - 1) % 2
        S = rank_ref.shape[3]
        for j in range(x_ref.shape[0]):
            rank = rank_ref[j, 0]
            pick = lax.broadcasted_iota(jnp.int32, (cap, S), 0) == rank
            xin = _unpack_halves(x_ref[j])
            a = jnp.dot(xin, wg_s[slot], preferred_element_type=F32)
            u = jnp.dot(xin, wu_s[slot], preferred_element_type=F32)
            act = (a * jax.nn.sigmoid(a) * u).astype(BF16)
            y = jnp.dot(act, wd_s[slot], preferred_element_type=F32)
            g = jnp.sum(jnp.where(pick, gate_ref[j, 0], 0.0), axis=1, keepdims=True)
            o_ref[0, j] = (y * g).astype(BF16)


def _moe(rank, gate, xin, wg, wu, wd, cap, e0, n_e):
    B, E, S = rank.shape
    D, Fd = wg.shape[1], wg.shape[2]
    G = MOE_SEQS_PER_STEP
    steps = B // G
    assert steps * G == B and D % steps == 0 and Fd % steps == 0

    def cur(r):
        return e0 + jnp.maximum(r - 1, 0)

    def nxt(r):
        return e0 + jnp.minimum(r, n_e - 1)

    row = pl.BlockSpec((G, 1, 1, S), lambda r, b: (b, cur(r), 0, 0))
    return pl.pallas_call(
        functools.partial(_moe_kernel, cap=cap),
        grid=(n_e + 1, steps),
        in_specs=[row, row,
                  pl.BlockSpec((G, cap, D // 2), lambda r, b: (jnp.maximum(r - 1, 0) * steps + b, 0, 0)),
                  pl.BlockSpec((1, D // steps, Fd), lambda r, b: (nxt(r), b, 0)),
                  pl.BlockSpec((1, D // steps, Fd), lambda r, b: (nxt(r), b, 0)),
                  pl.BlockSpec((1, Fd // steps, D), lambda r, b: (nxt(r), b, 0))],
        out_specs=pl.BlockSpec((1, G, cap, D), lambda r, b: (jnp.where(r == 0, n_e, r - 1), b, 0, 0)),
        out_shape=jax.ShapeDtypeStruct((n_e + 1, B, cap, D), BF16),
        scratch_shapes=[pltpu.VMEM((2, D, Fd), BF16), pltpu.VMEM((2, D, Fd), BF16),
                        pltpu.VMEM((2, Fd, D), BF16)],
        compiler_params=_cparams(("arbitrary", "arbitrary")),
        name="moe",
    )(rank.reshape(B, E, 1, S), gate.reshape(B, E, 1, S), xin.reshape(n_e * B, cap, D // 2), wg, wu, wd)


def _scat_kernel(win_ref, ok_ref, rt_ref, *refs, cap):
    *y_refs, x1_ref, mod_ref, g_ref, o_ref = refs
    b, i = pl.program_id(0), pl.program_id(1)
    rt = rt_ref[0]
    tm, E = rt.shape
    D = x1_ref.shape[2]
    per = E // len(y_refs)

    def finish(moe):
        x2 = x1_ref[0] + mod_ref[0, 5:6, :] * moe
        o_ref[0] = _rms(x2, g_ref[...])

    @pl.when(ok_ref[b, i] != 0)
    def _():
        slot = lax.broadcasted_iota(jnp.int32, (tm, SCAT_WINDOW), 1)
        moe = jnp.zeros((tm, D), F32)
        for e in range(0, E, 2):
            starts = [pl.multiple_of(win_ref[b, i, e + k], BF16_SUBLANES) for k in range(2)]
            place = jnp.concatenate(
                [jnp.where(rt[:, e + k:e + k + 1] - starts[k] == slot, 1.0, 0.0).astype(BF16)
                 for k in range(2)], axis=1)
            y = jnp.concatenate([y_refs[(e + k) // per][(e + k) % per, 0, pl.ds(starts[k], SCAT_WINDOW), :]
                                 for k in range(2)], axis=0)
            moe = moe + jnp.dot(place, y, preferred_element_type=F32)
        finish(moe)

    @pl.when(ok_ref[b, i] == 0)
    def _():
        slot = lax.broadcasted_iota(jnp.int32, (tm, cap), 1)
        place = jnp.concatenate(
            [jnp.where(rt[:, e:e + 1] == slot, 1.0, 0.0).astype(BF16) for e in range(E)], axis=1)
        y = jnp.concatenate([y_ref[0:per, 0].reshape(per * cap, D) for y_ref in y_refs], axis=0)
        finish(jnp.dot(place, y, preferred_element_type=F32))


def _scat_windows(rank, cap, tm):
    B, E, S = rank.shape
    counts = jnp.sum((rank >= 0).reshape(B, E, S // tm, tm), axis=-1, dtype=jnp.int32)
    ends = jnp.cumsum(counts, axis=-1)
    start = ends - counts
    win = jnp.minimum(start // BF16_SUBLANES * BF16_SUBLANES, cap - SCAT_WINDOW)
    ok = jnp.all(ends - win <= SCAT_WINDOW, axis=1)
    return jnp.swapaxes(win, 1, 2), ok.astype(jnp.int32)


def _scat(rank, ybufs, x1, mod, g, cap):
    B, S, D = x1.shape
    E = rank.shape[1]
    tm = TOKEN_TILE
    win, ok = _scat_windows(rank, cap, tm)
    smem = pl.BlockSpec(memory_space=pltpu.SMEM)
    return pl.pallas_call(
        functools.partial(_scat_kernel, cap=cap),
        grid=(B, S // tm),
        in_specs=[smem, smem,
                  pl.BlockSpec((1, tm, E), lambda b, i: (b, i, 0)),
                  *[pl.BlockSpec(y.shape[:1] + (1, cap, D), lambda b, i: (0, b, 0, 0)) for y in ybufs],
                  pl.BlockSpec((1, tm, D), lambda b, i: (b, i, 0)),
                  pl.BlockSpec((1, 6, D), lambda b, i: (b, 0, 0)),
                  _resident((1, D), lambda b, i: (0, 0))],
        out_specs=pl.BlockSpec((1, tm, D), lambda b, i: (b, i, 0)),
        out_shape=jax.ShapeDtypeStruct((B, S, D), F32),
        compiler_params=_cparams(("arbitrary", "arbitrary")),
        name="scat",
    )(win, ok, jnp.swapaxes(rank, 1, 2), *ybufs, x1, mod, g)


def _rot_tables(S):
    half = RET_DK // 2
    inv = 1.0 / (RET_THETA_BASE ** np.linspace(0.0, 1.0, half))
    ang = np.arange(S, dtype=np.float64)[:, None] * inv[None, :]
    cos = np.repeat(np.cos(ang), 2, axis=1)
    sin = np.stack([-np.sin(ang), np.sin(ang)], axis=2).reshape(S, RET_DK)
    sc = RET_DK ** -0.5
    return jnp.asarray(np.stack([cos, sin, cos * sc, sin * sc]), F32)


def _ret_tables():
    C = RET_CHUNK
    heads = np.arange(RET_HEADS, dtype=np.float64)
    lgf = np.log1p(-np.exp2(-RET_FWD_DECAY_OFFSET - heads))[:, None]
    lgb = np.log1p(-np.exp2(-RET_BWD_DECAY_OFFSET - heads))[:, None]
    idx = np.arange(C, dtype=np.float64)
    diff = idx[:, None] - idx[None, :]
    dmat = np.where(diff >= 0,
                    np.exp(np.maximum(diff, 0.0)[None] * lgf[:, :, None]),
                    np.exp(np.maximum(-diff, 0.0)[None] * lgb[:, :, None]))
    dec = np.stack([np.exp((idx + 1)[None, :] * lgf),
                    np.exp((C - 1 - idx)[None, :] * lgf),
                    np.exp((C - idx)[None, :] * lgb),
                    np.exp(idx[None, :] * lgb)], axis=1)
    dec = np.broadcast_to(dec[..., None], dec.shape + (LANES,))
    cd = np.concatenate([np.exp(C * lgf), np.exp(C * lgb)], axis=1)
    return jnp.asarray(cd, F32), jnp.asarray(dec, F32), jnp.asarray(dmat, F32)


def _t5_bucket(rel):
    nb = N_BUCKETS // 2
    max_exact = nb // 2
    ret = (rel > 0).astype(jnp.int32) * nb
    n = jnp.abs(rel)
    large = max_exact + (jnp.log(jnp.maximum(n, 1).astype(F32) / max_exact)
                         / math.log(MAX_DISTANCE / max_exact) * (nb - max_exact)).astype(jnp.int32)
    large = jnp.minimum(large, nb - 1)
    return ret + jnp.where(n < max_exact, n, large)


def _bias_rows(rel_bias):
    reach = BIAS_BAND - 1
    rel = jnp.concatenate([jnp.arange(reach, -reach - 1, -1, dtype=jnp.int32),
                           jnp.array([-(2 ** 20), 2 ** 20], jnp.int32)])
    f = rel_bias[_t5_bucket(rel)].astype(F32).T * LOG2E
    rows = jnp.pad(f[:, :-2], ((0, 0), (0, 2 * BIAS_BAND - (2 * reach + 1))))
    stats = jnp.stack([jnp.max(f, axis=1), f[:, -2], f[:, -1]], axis=1)
    return rows[:, None, :], stats


def kernel(x, c, w_ada, b_ada, norm_mix_g, w_in, ret_gn_g, diff_subln_g, lambda_q1, lambda_k1, lambda_q2, lambda_k2, w_ret_out, w_diff_out, w_o, rel_bias, norm_ffn_g, w_router, w_exp_gate, w_exp_up, w_exp_down, final_g):
    B, S, D = x.shape
    cap = CAPACITY_FACTOR * S // N_EXPERTS
    l = 0

    mod = _ada(c, w_ada[l], b_ada[l]).reshape(B, 6, D)
    proj, vt = _inproj(x, norm_mix_g[l].reshape(1, D), mod, w_in[l].astype(BF16), _rot_tables(S))

    cd, dec, dmat = _ret_tables()
    y_ret = _ret(proj, cd, dec, dmat, ret_gn_g[l].reshape(1, -1))

    lam_vecs = jnp.stack([lambda_q1[l], lambda_k1[l], lambda_q2[l], lambda_k2[l]]).astype(F32)
    brow, bstat = _bias_rows(rel_bias)
    subln = jnp.broadcast_to(diff_subln_g[l].astype(F32)[:, None], (DIFF_DV, Q_TILE))
    y_diff = _diff(proj, vt, lam_vecs, brow, bstat, subln)

    x1, h2, logits = _merge(x, y_ret, y_diff, proj, mod,
                            w_ret_out[l].astype(BF16), w_diff_out[l].astype(BF16), w_o[l].astype(BF16),
                            norm_ffn_g[l].reshape(1, D), w_router[l].T.astype(BF16))

    rank, gate = _route(logits, cap)
    table, ybufs = h2.reshape(B * S, D // 2), []
    n_e = N_EXPERTS // MOE_GROUPS
    for k in range(MOE_GROUPS):
        xin = _gather_rows(table, rank[:, k * n_e:(k + 1) * n_e], cap)
        ybufs.append(_moe(rank, gate, xin, w_exp_gate[l], w_exp_up[l], w_exp_down[l], cap, k * n_e, n_e))
    return _scat(rank, ybufs, x1, mod, final_g.reshape(1, D), cap)
```

```python
import functools
import math

import numpy as np
import jax
import jax.numpy as jnp
from jax import lax
from jax.experimental import pallas as pl
from jax.experimental.pallas import tpu as pltpu
from jax.experimental.pallas import tpu_sc as plsc

F32 = jnp.float32
BF16 = jnp.bfloat16

RET_HEADS = 4
RET_DK = 128
RET_DV = 128
RET_FWD_DECAY_OFFSET = 5.0
RET_BWD_DECAY_OFFSET = 5.5
RET_THETA_BASE = 10000.0
DIFF_HEADS = 4
DIFF_DH = 64
DIFF_DV = 2 * DIFF_DH
N_BUCKETS = 32
MAX_DISTANCE = 128
N_EXPERTS = 16
CAPACITY_FACTOR = 2
NORM_EPS = 1e-6
LAM_INIT = 0.8 - 0.6 * math.exp(-0.3 * 0)
LOG2E = math.log2(math.e)

LANES = 128
BF16_SUBLANES = 16
VMEM_LIMIT_BYTES = 56 * 2**20

SC_CORES = 2
SC_SUBCORES = 16
SC_LANES = 16
SC_GATHER_ROWS = 64

TOKEN_TILE = 512
PROJ_TILE = 1024
RET_CHUNK = 256
Q_TILE = 256
KEY_CHUNK = 512
DIFF_TILES_PER_LOOP = 8
L_MIN = 2.0 ** -60
BIAS_BAND = Q_TILE + 2 * MAX_DISTANCE
SCAT_WINDOW = 128
MOE_SEQS_PER_STEP = 4
ROUTE_SEQS_PER_STEP = 4
MOE_GROUPS = 2

W_RQ, W_RK, W_RV, W_RG = 0, 4, 8, 12
W_DQ1, W_DQ2, W_DK1, W_DK2, W_DV = 16, 18, 20, 22, 24
W_GR, W_GD = 28, 36

COL_RQ, COL_RK, COL_RV, COL_RG = 0, 4, 8, 12
COL_DQ, COL_DK = 16, 20
PROJ_COLS = 24 * LANES


def _cparams(sem):
    return pltpu.CompilerParams(dimension_semantics=sem, vmem_limit_bytes=VMEM_LIMIT_BYTES)


def _resident(shape, index_map):
    return pl.BlockSpec(shape, index_map, pipeline_mode=pl.Buffered(1))


def _ada_kernel(c_ref, w_ref, b_ref, o_ref):
    c = c_ref[...]
    a = c * jax.nn.sigmoid(c)
    o_ref[...] = jnp.dot(a, w_ref[...], preferred_element_type=F32,
                         precision=lax.Precision.HIGHEST) + b_ref[...]


def _ada(c, w, b):
    B, D = c.shape
    n = w.shape[1] // D
    return pl.pallas_call(
        _ada_kernel,
        grid=(n,),
        in_specs=[pl.BlockSpec((B, D), lambda j: (0, 0)),
                  pl.BlockSpec((D, D), lambda j: (0, j)),
                  pl.BlockSpec((1, D), lambda j: (0, j))],
        out_specs=pl.BlockSpec((B, D), lambda j: (0, j)),
        out_shape=jax.ShapeDtypeStruct((B, n * D), F32),
        compiler_params=_cparams(("arbitrary",)),
        name="ada",
    )(c, w, b.reshape(1, -1))


def _rms(x, g):
    ms = jnp.mean(x * x, axis=-1, keepdims=True)
    return x * lax.rsqrt(ms + NORM_EPS) * g


def _rot_pairs(x, cos, sin):
    even = (lax.broadcasted_iota(jnp.int32, x.shape, 1) & 1) == 0
    partner = jnp.where(even, pltpu.roll(x, LANES - 1, axis=1), pltpu.roll(x, 1, axis=1))
    return x * cos + partner * sin


def _pair_maps(a, b, second):
    low = lax.broadcasted_iota(jnp.int32, a.shape, 1) < DIFF_DH
    if second:
        return jnp.where(low, pltpu.roll(a, DIFF_DH, axis=1), b)
    return jnp.where(low, a, pltpu.roll(b, DIFF_DH, axis=1))


def _inproj_kernel(x_ref, g_ref, mod_ref, w_ref, rot_ref, o_ref, vt_ref, wvt_s):
    @pl.when((pl.program_id(0) == 0) & (pl.program_id(1) == 0))
    def _():
        wvt_s[...] = w_ref[:, W_DV * LANES:(W_DV + 4) * LANES].astype(F32).T.astype(BF16)

    parts = 4
    step = x_ref.shape[1] // parts

    def normed(p):
        x = x_ref[0, p * step:(p + 1) * step, :]
        h = _rms(x, g_ref[...]) * (1.0 + mod_ref[0, 1:2, :]) + mod_ref[0, 0:1, :]
        return h.astype(BF16)

    def project(p, hb):
        rows = slice(p * step, (p + 1) * step)

        def mm(col, width):
            return jnp.dot(hb, w_ref[:, col * LANES:(col + width) * LANES], preferred_element_type=F32)

        def put(col, val):
            o_ref[0, rows, col * LANES:col * LANES + val.shape[1]] = val.astype(BF16)

        for src, dst, t in ((W_RQ, COL_RQ, 0), (W_RK, COL_RK, 2)):
            cos, sin = rot_ref[t, rows, :], rot_ref[t + 1, rows, :]
            r = mm(src, RET_HEADS)
            for hh in range(RET_HEADS):
                put(dst + hh, _rot_pairs(r[:, hh * LANES:(hh + 1) * LANES], cos, sin))
        put(COL_RV, mm(W_RV, 4))
        r = mm(W_RG, 4)
        put(COL_RG, r * jax.nn.sigmoid(r))
        for src1, src2, dst, scale in ((W_DQ1, W_DQ2, COL_DQ, DIFF_DH ** -0.5 * LOG2E),
                                       (W_DK1, W_DK2, COL_DK, None)):
            m1, m2 = mm(src1, 2), mm(src2, 2)
            for hh in range(DIFF_HEADS):
                blk = slice((hh // 2) * LANES, (hh // 2 + 1) * LANES)
                val = _pair_maps(m1[:, blk], m2[:, blk], hh % 2 == 1)
                put(dst + hh, val if scale is None else val * scale)
        vt_ref[0, :, rows] = lax.dot_general(wvt_s[...], hb, (((1,), (1,)), ((), ())),
                                             preferred_element_type=F32).astype(BF16)

    hb = normed(0)
    for p in range(parts):
        nxt = normed(p + 1) if p + 1 < parts else None
        project(p, hb)
        hb = nxt


def _inproj(x, g, mod, w, rot):
    B, S, D = x.shape
    tm = PROJ_TILE
    vw = DIFF_HEADS * DIFF_DV
    return pl.pallas_call(
        _inproj_kernel,
        grid=(B, S // tm),
        in_specs=[pl.BlockSpec((1, tm, D), lambda b, i: (b, i, 0)),
                  _resident((1, D), lambda b, i: (0, 0)),
                  pl.BlockSpec((1, 6, D), lambda b, i: (b, 0, 0)),
                  _resident(w.shape, lambda b, i: (0, 0)),
                  pl.BlockSpec((4, tm, LANES), lambda b, i: (0, i, 0))],
        out_specs=[pl.BlockSpec((1, tm, PROJ_COLS), lambda b, i: (b, i, 0)),
                   pl.BlockSpec((1, vw, tm), lambda b, i: (b, 0, i))],
        out_shape=[jax.ShapeDtypeStruct((B, S, PROJ_COLS), BF16),
                   jax.ShapeDtypeStruct((B, vw, S), BF16)],
        scratch_shapes=[pltpu.VMEM((vw, D), BF16)],
        compiler_params=_cparams(("arbitrary", "arbitrary")),
        name="inproj",
    )(x, g, mod, w, rot)


def _ret_kernel(cd_ref, q_ref, k_ref, v_ref, rg_ref, dec_ref, dm_ref, gn_ref, o_ref, acc_ref):
    S = q_ref.shape[1]
    H = dm_ref.shape[0]
    C = RET_CHUNK
    nc = S // C
    nt = (((1,), (1,)), ((), ()))
    tn = (((0,), (0,)), ((), ()))

    def chunk(n, h):
        sl, lanes = pl.ds(n * C, C), slice(h * LANES, (h + 1) * LANES)
        return sl, q_ref[0, sl, lanes], k_ref[0, sl, lanes], v_ref[0, sl, lanes]

    def scaled(t, dec):
        return (t.astype(F32) * dec).astype(BF16)

    def forward(heads):
        state = {h: jnp.zeros((RET_DK, RET_DV), F32) for h in heads}
        for n in range(nc):
            for h in heads:
                sl, q, k, v = chunk(n, h)
                s = lax.dot_general(q, k, nt, preferred_element_type=F32) * dm_ref[h]
                inner = jnp.dot(s.astype(BF16), v, preferred_element_type=F32)
                cross = jnp.dot(scaled(q, dec_ref[h, 0]), state[h].astype(BF16), preferred_element_type=F32)
                acc_ref[h, sl, :] = inner + cross
                kv = lax.dot_general(scaled(k, dec_ref[h, 1]), v, tn, preferred_element_type=F32)
                state[h] = cd_ref[h, 0] * state[h] + kv

    def backward(heads):
        state = {h: jnp.zeros((RET_DK, RET_DV), F32) for h in heads}
        for n in reversed(range(nc)):
            for h in heads:
                sl, q, k, v = chunk(n, h)
                cross = jnp.dot(scaled(q, dec_ref[h, 2]), state[h].astype(BF16), preferred_element_type=F32)
                acc_ref[h, sl, :] = acc_ref[h, sl, :] + cross
                kv = lax.dot_general(scaled(k, dec_ref[h, 3]), v, tn, preferred_element_type=F32)
                state[h] = cd_ref[h, 1] * state[h] + kv

    def normalise(heads):
        for h in heads:
            lanes = slice(h * LANES, (h + 1) * LANES)
            y = acc_ref[h]
            mu = jnp.mean(y, axis=-1, keepdims=True)
            yc = y - mu
            var = jnp.mean(yc * yc, axis=-1, keepdims=True)
            yn = yc * lax.rsqrt(var + NORM_EPS) * gn_ref[:, lanes]
            o_ref[0, :, lanes] = (rg_ref[0, :, lanes].astype(F32) * yn).astype(BF16)

    first, second = tuple(range(H // 2)), tuple(range(H // 2, H))
    forward(first)
    backward(first)
    forward(second)
    normalise(first)
    backward(second)
    normalise(second)


def _ret(proj, cd, dec, dmat, gn):
    B, S, _ = proj.shape
    H = RET_HEADS
    C = RET_CHUNK

    def cols(base):
        return pl.BlockSpec((1, S, H * LANES), lambda b: (b, 0, base // H))

    return pl.pallas_call(
        _ret_kernel,
        grid=(B,),
        in_specs=[pl.BlockSpec(memory_space=pltpu.SMEM),
                  cols(COL_RQ), cols(COL_RK), cols(COL_RV), cols(COL_RG),
                  _resident((H, 4, C, LANES), lambda b: (0, 0, 0, 0)),
                  _resident((H, C, C), lambda b: (0, 0, 0)),
                  _resident((1, H * LANES), lambda b: (0, 0))],
        out_specs=pl.BlockSpec((1, S, H * LANES), lambda b: (b, 0, 0)),
        out_shape=jax.ShapeDtypeStruct((B, S, H * RET_DV), BF16),
        scratch_shapes=[pltpu.VMEM((H, S, RET_DV), F32)],
        compiler_params=_cparams(("arbitrary",)),
        name="ret",
    )(cd, proj, proj, proj, proj, dec, dmat, gn)


def _diff_kernel(bstat_ref, lam_ref, q_ref, k_ref, vt_ref, brow_ref, g_ref, o_ref, p_ref, t_ref):
    hh = pl.program_id(0)
    S = k_ref.shape[1]
    TQ, KB = Q_TILE, KEY_CHUNK
    nk = S // KB
    lv = lam_ref[...]
    lam = (jnp.exp(jnp.sum(lv[0:1] * lv[1:2], axis=1, keepdims=True))
           - jnp.exp(jnp.sum(lv[2:3] * lv[3:4], axis=1, keepdims=True)) + LAM_INIT)
    first_half = lax.broadcasted_iota(jnp.int32, (TQ, LANES), 1) < DIFF_DH
    nt = (((1,), (1,)), ((), ()))
    bmax = bstat_ref[hh, 0]

    @pl.when(pl.program_id(1) == 0)
    def _():
        side = S - TQ - MAX_DISTANCE
        t_ref[0:side, :] = jnp.full((side, TQ), bstat_ref[hh, 1], F32)
        rows = jnp.broadcast_to(brow_ref[0], (BIAS_BAND, brow_ref.shape[2]))
        shift = brow_ref.shape[2] - (BIAS_BAND - 1 + MAX_DISTANCE)
        t_ref[side:side + BIAS_BAND, :] = pltpu.roll(rows, shift, 1, stride=1, stride_axis=0)[:, :TQ]
        t_ref[side + BIAS_BAND:, :] = jnp.full((side, TQ), bstat_ref[hh, 2], F32)
    kf = k_ref[0].astype(F32)
    kmax = jnp.sqrt(jnp.max(jnp.sum(kf * kf, axis=1, keepdims=True), axis=0, keepdims=True))
    sel_r = lax.broadcasted_iota(jnp.int32, (8, LANES), 0)
    sel_c = lax.broadcasted_iota(jnp.int32, (8, LANES), 1)
    sel = jnp.where((sel_c < DIFF_DH) == (sel_r == 0), 1.0, 0.0)
    sel = jnp.where(sel_r < 2, sel, 0.0).astype(BF16)

    def masked_q(qb):
        q = q_ref[0, pl.ds(pl.multiple_of(qb * TQ, TQ), TQ), :]
        zero = jnp.zeros_like(q)
        return q, (jnp.where(first_half, q, zero), jnp.where(first_half, zero, q))

    def finish(ot, qb):
        ms = jnp.mean(ot * ot, axis=0, keepdims=True)
        y = ot * lax.rsqrt(ms + NORM_EPS) * g_ref[...] * (1.0 - LAM_INIT)
        o_ref[0, pl.ds(pl.multiple_of(qb * TQ, TQ), TQ), :] = y.T.astype(BF16)

    def exponentials(qb, slot):
        w0 = pl.multiple_of((S - TQ) - qb * TQ, TQ)
        q, qz = masked_q(qb)
        qsq = (q.astype(F32) * q.astype(F32)).astype(BF16)
        n2 = lax.dot_general(sel, qsq, nt, preferred_element_type=F32) * 1.01
        m = [jnp.sqrt(n2[i:i + 1]) * kmax + bmax for i in range(2)]
        l = [jnp.zeros((8, TQ), F32) for _ in range(2)]
        for c in range(nk):
            rows = slice(c * KB, (c + 1) * KB)
            kb = k_ref[0, rows, :]
            bias = t_ref[pl.ds(w0 + c * KB, KB), :]
            for i in range(2):
                s = lax.dot_general(kb, qz[i], nt, preferred_element_type=F32)
                p = jnp.exp2((s - m[i]) + bias)
                l[i] = l[i] + jnp.sum(p.reshape(KB // 8, 8, TQ), axis=0)
                p_ref[slot, i, rows, :] = p.astype(BF16)
        return [jnp.sum(l[i], axis=0, keepdims=True) for i in range(2)]

    def values(qb, slot, lr):
        ratio = jnp.broadcast_to(lam * lr[0] / lr[1], (16, TQ)).astype(BF16)
        ot = jnp.zeros((DIFF_DV, TQ), F32)
        for c in range(nk):
            rows = slice(c * KB, (c + 1) * KB)
            p1 = p_ref[slot, 0, rows, :].reshape(KB // 16, 16, TQ)
            p2 = p_ref[slot, 1, rows, :].reshape(KB // 16, 16, TQ)
            a = (p1 - ratio * p2).reshape(KB, TQ)
            ot = ot + jnp.dot(vt_ref[0, :, rows], a, preferred_element_type=F32)
        finish(ot * (1.0 / lr[0]), qb)

    def fast_group(j, lmin):
        base = j * DIFF_TILES_PER_LOOP
        sums = exponentials(base, 0)
        for t in range(DIFF_TILES_PER_LOOP):
            lmin = jnp.minimum(lmin, jnp.minimum(sums[0], sums[1]))
            nxt = exponentials(base + t + 1, (t + 1) % 2) if t + 1 < DIFF_TILES_PER_LOOP else None
            values(base + t, t % 2, sums)
            sums = nxt
        return lmin

    lmin = lax.fori_loop(0, S // (DIFF_TILES_PER_LOOP * TQ), fast_group, jnp.full((1, TQ), jnp.inf, F32))

    @pl.when(jnp.min(lmin) < L_MIN)
    def _():
        def exact_tile(qb, carry):
            w0 = pl.multiple_of((S - TQ) - qb * TQ, TQ)
            _, qz = masked_q(qb)
            bias = t_ref[pl.ds(w0, S), :]
            attn = None
            for i, scale in ((0, 1.0), (1, lam)):
                s = lax.dot_general(k_ref[0], qz[i], nt, preferred_element_type=F32) + bias
                p = jnp.exp2(s - jnp.max(s, axis=0, keepdims=True))
                part = p * (scale / jnp.sum(p, axis=0, keepdims=True))
                attn = part if attn is None else attn - part
            finish(jnp.dot(vt_ref[0], attn.astype(BF16), preferred_element_type=F32), qb)
            return carry

        lax.fori_loop(0, S // TQ, exact_tile, 0)


def _diff(proj, vt, lam_vecs, brow, bstat, g):
    B, S, _ = proj.shape
    H = DIFF_HEADS

    def col(base):
        return pl.BlockSpec((1, S, LANES), lambda h, b: (b, 0, base + h))

    return pl.pallas_call(
        _diff_kernel,
        grid=(H, B),
        in_specs=[pl.BlockSpec(memory_space=pltpu.SMEM),
                  pl.BlockSpec((4, DIFF_DH), lambda h, b: (0, 0)),
                  col(COL_DQ), col(COL_DK),
                  pl.BlockSpec((1, DIFF_DV, S), lambda h, b: (b, h, 0)),
                  pl.BlockSpec((1, 1, brow.shape[2]), lambda h, b: (h, 0, 0)),
                  pl.BlockSpec((DIFF_DV, Q_TILE), lambda h, b: (0, 0))],
        out_specs=pl.BlockSpec((1, S, LANES), lambda h, b: (b, 0, h)),
        out_shape=jax.ShapeDtypeStruct((B, S, H * DIFF_DV), BF16),
        scratch_shapes=[pltpu.VMEM((2, 2, S, Q_TILE), BF16), pltpu.VMEM((2 * S - Q_TILE, Q_TILE), F32)],
        compiler_params=_cparams(("arbitrary", "arbitrary")),
        name="diff",
    )(bstat, lam_vecs, proj, proj, vt, brow, g)


def _pack_halves(x):
    w = x.shape[1] // 2
    lo = pltpu.bitcast(x[:, :w].astype(BF16).astype(F32), jnp.int32)
    hi = pltpu.bitcast(x[:, w:].astype(BF16).astype(F32), jnp.int32)
    return lax.shift_right_logical(lo, jnp.full_like(lo, 16)) | (hi & jnp.int32(-65536))


def _unpack_halves(words):
    lo = pltpu.bitcast(lax.shift_left(words, jnp.full_like(words, 16)), F32)
    hi = pltpu.bitcast(words & jnp.int32(-65536), F32)
    return jnp.concatenate([lo, hi], axis=1).astype(BF16)


def _merge_kernel(x_ref, yr_ref, yd_ref, mod_ref, gmix_ref, wg_ref, wr_ref, wd_ref, wo_ref,
                  g_ref, wrt_ref, x1_ref, h2_ref, lg_ref):
    parts = 4
    gate_w = wg_ref.shape[1] // 2
    step = x_ref.shape[1] // parts

    def project(r):
        rows = slice(r * step, (r + 1) * step)
        h = _rms(x_ref[0, rows, :], gmix_ref[...]) * (1.0 + mod_ref[0, 1:2, :]) + mod_ref[0, 0:1, :]
        gates = jax.nn.sigmoid(jnp.dot(h.astype(BF16), wg_ref[...], preferred_element_type=F32))
        a = jnp.dot(yr_ref[0, rows, :], wr_ref[...], preferred_element_type=F32)
        d = jnp.dot(yd_ref[0, rows, :], wd_ref[...], preferred_element_type=F32)
        merged = gates[:, :gate_w] * a + gates[:, gate_w:] * d
        return jnp.dot(merged.astype(BF16), wo_ref[...], preferred_element_type=F32)

    def epilogue(r, o):
        rows = slice(r * step, (r + 1) * step)
        x1 = x_ref[0, rows, :] + mod_ref[0, 2:3, :] * o
        x1_ref[0, rows, :] = x1
        h2 = _rms(x1, g_ref[...]) * (1.0 + mod_ref[0, 4:5, :]) + mod_ref[0, 3:4, :]
        h2_ref[0, rows, :] = _pack_halves(h2)
        lg_ref[0, :, rows] = lax.dot_general(wrt_ref[...], h2.astype(BF16), (((1,), (1,)), ((), ())),
                                             preferred_element_type=F32)

    o = project(0)
    for r in range(parts):
        nxt = project(r + 1) if r + 1 < parts else None
        epilogue(r, o)
        o = nxt


def _merge(x, yr, yd, mod, gmix, wg, wr, wd, wo, g, wrt):
    B, S, D = x.shape
    tm = PROJ_TILE
    E = wrt.shape[0]
    return pl.pallas_call(
        _merge_kernel,
        grid=(B, S // tm),
        in_specs=[pl.BlockSpec((1, tm, D), lambda b, i: (b, i, 0)),
                  pl.BlockSpec((1, tm, yr.shape[2]), lambda b, i: (b, i, 0)),
                  pl.BlockSpec((1, tm, yd.shape[2]), lambda b, i: (b, i, 0)),
                  pl.BlockSpec((1, 6, D), lambda b, i: (b, 0, 0)),
                  _resident((1, D), lambda b, i: (0, 0)),
                  _resident(wg.shape, lambda b, i: (0, 0)),
                  _resident(wr.shape, lambda b, i: (0, 0)),
                  _resident(wd.shape, lambda b, i: (0, 0)),
                  _resident(wo.shape, lambda b, i: (0, 0)),
                  _resident((1, D), lambda b, i: (0, 0)),
                  _resident(wrt.shape, lambda b, i: (0, 0))],
        out_specs=[pl.BlockSpec((1, tm, D), lambda b, i: (b, i, 0)),
                   pl.BlockSpec((1, tm, D // 2), lambda b, i: (b, i, 0)),
                   pl.BlockSpec((1, E, tm), lambda b, i: (b, 0, i))],
        out_shape=[jax.ShapeDtypeStruct((B, S, D), F32),
                   jax.ShapeDtypeStruct((B, S, D // 2), jnp.int32),
                   jax.ShapeDtypeStruct((B, E, S), F32)],
        compiler_params=_cparams(("arbitrary", "arbitrary")),
        name="merge",
    )(x, yr, yd, mod, gmix, wg, wr, wd, wo, g, wrt)


def _lane_prefix(m, tri):
    E, S = m.shape
    off = jnp.zeros((E, 1), F32)
    parts = []
    for j in range(S // LANES):
        blk = m[:, j * LANES:(j + 1) * LANES]
        parts.append(jnp.dot(blk.astype(BF16), tri, preferred_element_type=F32) + off)
        off = off + jnp.sum(blk, axis=1, keepdims=True)
    return jnp.concatenate(parts, axis=1)


def _route_kernel(lg_ref, rank_ref, gate_ref, *, cap):
    for j in range(lg_ref.shape[0]):
        rank_ref[j], gate_ref[j] = _route_one(lg_ref[j], cap)


def _route_one(lg, cap):
    e = jnp.exp(lg - jnp.max(lg, axis=0, keepdims=True))
    aff = e / jnp.sum(e, axis=0, keepdims=True)
    bits = pltpu.bitcast(aff, jnp.int32)
    E = lg.shape[0]

    def count(mask):
        return jnp.sum(jnp.where(mask, 1.0, 0.0), axis=1, keepdims=True)

    def reaches(cand):
        return count(bits >= cand) >= cap

    thr = jnp.zeros((E, 1), jnp.int32)
    for hi in range(30, 0, -2):
        c1, c2 = thr | (1 << hi), thr | (1 << (hi - 1))
        c3 = c1 | (1 << (hi - 1))
        thr = jnp.where(reaches(c3), c3, jnp.where(reaches(c1), c1, jnp.where(reaches(c2), c2, thr)))
    thr = jnp.where(reaches(thr | 1), thr | 1, thr)
    gt = bits > thr
    eq = bits == thr
    need = cap - count(gt)
    r = lax.broadcasted_iota(jnp.int32, (LANES, LANES), 0)
    c = lax.broadcasted_iota(jnp.int32, (LANES, LANES), 1)
    tri = jnp.where(r < c, 1.0, 0.0).astype(BF16)
    eq_before = _lane_prefix(jnp.where(eq, 1.0, 0.0), tri)
    sel = gt | (eq & (eq_before < need))
    slot = _lane_prefix(jnp.where(sel, 1.0, 0.0), tri)
    return jnp.where(sel, slot, -1.0).astype(jnp.int32), jnp.where(sel, aff, 0.0)


def _route(logits, cap):
    B, E, S = logits.shape
    G = ROUTE_SEQS_PER_STEP
    assert B % G == 0
    spec = pl.BlockSpec((G, E, S), lambda b: (b, 0, 0))
    return pl.pallas_call(
        functools.partial(_route_kernel, cap=cap),
        grid=(B // G,),
        in_specs=[spec],
        out_specs=[spec, spec],
        out_shape=[jax.ShapeDtypeStruct((B, E, S), jnp.int32),
                   jax.ShapeDtypeStruct((B, E, S), F32)],
        compiler_params=_cparams(("arbitrary",)),
        name="route",
    )(logits)


def _gather_rows(table, rank, cap):
    B, E, S = rank.shape
    W = table.shape[1]
    workers = SC_CORES * SC_SUBCORES
    pairs = E * B
    per = pairs // workers
    assert per * workers == pairs and B & (B - 1) == 0 and cap % SC_GATHER_ROWS == 0 and S % SC_LANES == 0
    shift = B.bit_length() - 1
    mesh = plsc.VectorSubcoreMesh(core_axis_name="c", subcore_axis_name="s")

    R = SC_GATHER_ROWS
    n_chunks = cap // R

    def body(table_hbm, rank_hbm, out_hbm, rank_v, idx_v, buf_v, gsem, wsem):
        wid = lax.axis_index("s") * SC_CORES + lax.axis_index("c")

        def fetch(c):
            return pltpu.make_async_copy(table_hbm.at[idx_v.at[pl.ds(c * R, R)]], buf_v.at[c % 2], gsem.at[c % 2])

        def flush(p, c):
            return pltpu.make_async_copy(buf_v.at[c % 2], out_hbm.at[pl.ds(p * cap + c * R, R)], wsem.at[c % 2])

        for j in range(per):
            p = wid * per + j
            e = lax.shift_right_logical(p, shift)
            b = p & (B - 1)
            pltpu.sync_copy(rank_hbm.at[b, e], rank_v)
            base = b * S

            @pl.loop(0, S, step=SC_LANES)
            def _(t0):
                r = rank_v[pl.ds(t0, SC_LANES)]
                tok = lax.iota(jnp.int32, SC_LANES) + (base + t0)
                plsc.store_scatter(idx_v, [r], tok, mask=r >= 0)

            fetch(0).start()
            for c in range(n_chunks):
                fetch(c).wait()
                if c >= 1:
                    flush(p, c - 1).wait()
                if c + 1 < n_chunks:
                    fetch(c + 1).start()
                flush(p, c).start()
            flush(p, n_chunks - 1).wait()

    return pl.kernel(
        body,
        out_type=jax.ShapeDtypeStruct((pairs * cap, W), jnp.int32),
        mesh=mesh,
        scratch_types=[pltpu.VMEM((S,), jnp.int32), pltpu.VMEM((cap,), jnp.int32),
                       pltpu.VMEM((2, R, W), jnp.int32),
                       pltpu.SemaphoreType.DMA((2,)), pltpu.SemaphoreType.DMA((2,))],
        compiler_params=pltpu.CompilerParams(needs_layout_passes=False),
    )(table, rank)


def _moe_kernel(rank_ref, gate_ref, x_ref, wg_ref, wu_ref, wd_ref, o_ref, wg_s, wu_s, wd_s, *, cap):
    r, b = pl.program_id(0), pl.program_id(1)
    n_exp = pl.num_programs(0) - 1

    @pl.when(r < n_exp)
    def _():
        slot = r % 2
        rows_in, rows_ff = wg_ref.shape[1], wd_ref.shape[1]
        wg_s[slot, pl.ds(pl.multiple_of(b * rows_in, rows_in), rows_in), :] = wg_ref[0].astype(BF16)
        wu_s[slot, pl.ds(pl.multiple_of(b * rows_in, rows_in), rows_in), :] = wu_ref[0].astype(BF16)
        wd_s[slot, pl.ds(pl.multiple_of(b * rows_ff, rows_ff), rows_ff), :] = wd_ref[0].astype(BF16)

    @pl.when(r == 0)
    def _():
        o_ref[...] = jnp.zeros_like(o_ref)

    @pl.when(r > 0)
    def _():
        slot = (r - 1) % 2
        S = rank_ref.shape[3]
        for j in range(x_ref.shape[0]):
            rank = rank_ref[j, 0]
            pick = lax.broadcasted_iota(jnp.int32, (cap, S), 0) == rank
            xin = _unpack_halves(x_ref[j])
            a = jnp.dot(xin, wg_s[slot], preferred_element_type=F32)
            u = jnp.dot(xin, wu_s[slot], preferred_element_type=F32)
            act = (a * jax.nn.sigmoid(a) * u).astype(BF16)
            y = jnp.dot(act, wd_s[slot], preferred_element_type=F32)
            g = jnp.sum(jnp.where(pick, gate_ref[j, 0], 0.0), axis=1, keepdims=True)
            o_ref[0, j] = (y * g).astype(BF16)


def _moe(rank, gate, xin, wg, wu, wd, cap, e0, n_e):
    B, E, S = rank.shape
    D, Fd = wg.shape[1], wg.shape[2]
    G = MOE_SEQS_PER_STEP
    steps = B // G
    assert steps * G == B and D % steps == 0 and Fd % steps == 0

    def cur(r):
        return e0 + jnp.maximum(r - 1, 0)

    def nxt(r):
        return e0 + jnp.minimum(r, n_e - 1)

    row = pl.BlockSpec((G, 1, 1, S), lambda r, b: (b, cur(r), 0, 0))
    return pl.pallas_call(
        functools.partial(_moe_kernel, cap=cap),
        grid=(n_e + 1, steps),
        in_specs=[row, row,
                  pl.BlockSpec((G, cap, D // 2), lambda r, b: (jnp.maximum(r - 1, 0) * steps + b, 0, 0)),
                  pl.BlockSpec((1, D // steps, Fd), lambda r, b: (nxt(r), b, 0)),
                  pl.BlockSpec((1, D // steps, Fd), lambda r, b: (nxt(r), b, 0)),
                  pl.BlockSpec((1, Fd // steps, D), lambda r, b: (nxt(r), b, 0))],
        out_specs=pl.BlockSpec((1, G, cap, D), lambda r, b: (jnp.where(r == 0, n_e, r - 1), b, 0, 0)),
        out_shape=jax.ShapeDtypeStruct((n_e + 1, B, cap, D), BF16),
        scratch_shapes=[pltpu.VMEM((2, D, Fd), BF16), pltpu.VMEM((2, D, Fd), BF16),
                        pltpu.VMEM((2, Fd, D), BF16)],
        compiler_params=_cparams(("arbitrary", "arbitrary")),
        name="moe",
    )(rank.reshape(B, E, 1, S), gate.reshape(B, E, 1, S), xin.reshape(n_e * B, cap, D // 2), wg, wu, wd)


def _scat_kernel(win_ref, ok_ref, rt_ref, *refs, cap):
    *y_refs, x1_ref, mod_ref, g_ref, o_ref = refs
    b, i = pl.program_id(0), pl.program_id(1)
    rt = rt_ref[0]
    tm, E = rt.shape
    D = x1_ref.shape[2]
    per = E // len(y_refs)

    def finish(moe):
        x2 = x1_ref[0] + mod_ref[0, 5:6, :] * moe
        o_ref[0] = _rms(x2, g_ref[...])

    @pl.when(ok_ref[b, i] != 0)
    def _():
        slot = lax.broadcasted_iota(jnp.int32, (tm, SCAT_WINDOW), 1)
        moe = jnp.zeros((tm, D), F32)
        for e in range(0, E, 2):
            starts = [pl.multiple_of(win_ref[b, i, e + k], BF16_SUBLANES) for k in range(2)]
            place = jnp.concatenate(
                [jnp.where(rt[:, e + k:e + k + 1] - starts[k] == slot, 1.0, 0.0).astype(BF16)
                 for k in range(2)], axis=1)
            y = jnp.concatenate([y_refs[(e + k) // per][(e + k) % per, 0, pl.ds(starts[k], SCAT_WINDOW), :]
                                 for k in range(2)], axis=0)
            moe = moe + jnp.dot(place, y, preferred_element_type=F32)
        finish(moe)

    @pl.when(ok_ref[b, i] == 0)
    def _():
        slot = lax.broadcasted_iota(jnp.int32, (tm, cap), 1)
        place = jnp.concatenate(
            [jnp.where(rt[:, e:e + 1] == slot, 1.0, 0.0).astype(BF16) for e in range(E)], axis=1)
        y = jnp.concatenate([y_ref[0:per, 0].reshape(per * cap, D) for y_ref in y_refs], axis=0)
        finish(jnp.dot(place, y, preferred_element_type=F32))


def _scat_windows(rank, cap, tm):
    B, E, S = rank.shape
    counts = jnp.sum((rank >= 0).reshape(B, E, S // tm, tm), axis=-1, dtype=jnp.int32)
    ends = jnp.cumsum(counts, axis=-1)
    start = ends - counts
    win = jnp.minimum(start // BF16_SUBLANES * BF16_SUBLANES, cap - SCAT_WINDOW)
    ok = jnp.all(ends - win <= SCAT_WINDOW, axis=1)
    return jnp.swapaxes(win, 1, 2), ok.astype(jnp.int32)


def _scat(rank, ybufs, x1, mod, g, cap):
    B, S, D = x1.shape
    E = rank.shape[1]
    tm = TOKEN_TILE
    win, ok = _scat_windows(rank, cap, tm)
    smem = pl.BlockSpec(memory_space=pltpu.SMEM)
    return pl.pallas_call(
        functools.partial(_scat_kernel, cap=cap),
        grid=(B, S // tm),
        in_specs=[smem, smem,
                  pl.BlockSpec((1, tm, E), lambda b, i: (b, i, 0)),
                  *[pl.BlockSpec(y.shape[:1] + (1, cap, D), lambda b, i: (0, b, 0, 0)) for y in ybufs],
                  pl.BlockSpec((1, tm, D), lambda b, i: (b, i, 0)),
                  pl.BlockSpec((1, 6, D), lambda b, i: (b, 0, 0)),
                  _resident((1, D), lambda b, i: (0, 0))],
        out_specs=pl.BlockSpec((1, tm, D), lambda b, i: (b, i, 0)),
        out_shape=jax.ShapeDtypeStruct((B, S, D), F32),
        compiler_params=_cparams(("arbitrary", "arbitrary")),
        name="scat",
    )(win, ok, jnp.swapaxes(rank, 1, 2), *ybufs, x1, mod, g)


def _rot_tables(S):
    half = RET_DK // 2
    inv = 1.0 / (RET_THETA_BASE ** np.linspace(0.0, 1.0, half))
    ang = np.arange(S, dtype=np.float64)[:, None] * inv[None, :]
    cos = np.repeat(np.cos(ang), 2, axis=1)
    sin = np.stack([-np.sin(ang), np.sin(ang)], axis=2).reshape(S, RET_DK)
    sc = RET_DK ** -0.5
    return jnp.asarray(np.stack([cos, sin, cos * sc, sin * sc]), F32)


def _ret_tables():
    C = RET_CHUNK
    heads = np.arange(RET_HEADS, dtype=np.float64)
    lgf = np.log1p(-np.exp2(-RET_FWD_DECAY_OFFSET - heads))[:, None]
    lgb = np.log1p(-np.exp2(-RET_BWD_DECAY_OFFSET - heads))[:, None]
    idx = np.arange(C, dtype=np.float64)
    diff = idx[:, None] - idx[None, :]
    dmat = np.where(diff >= 0,
                    np.exp(np.maximum(diff, 0.0)[None] * lgf[:, :, None]),
                    np.exp(np.maximum(-diff, 0.0)[None] * lgb[:, :, None]))
    dec = np.stack([np.exp((idx + 1)[None, :] * lgf),
                    np.exp((C - 1 - idx)[None, :] * lgf),
                    np.exp((C - idx)[None, :] * lgb),
                    np.exp(idx[None, :] * lgb)], axis=1)
    dec = np.broadcast_to(dec[..., None], dec.shape + (LANES,))
    cd = np.concatenate([np.exp(C * lgf), np.exp(C * lgb)], axis=1)
    return jnp.asarray(cd, F32), jnp.asarray(dec, F32), jnp.asarray(dmat, F32)


def _t5_bucket(rel):
    nb = N_BUCKETS // 2
    max_exact = nb // 2
    ret = (rel > 0).astype(jnp.int32) * nb
    n = jnp.abs(rel)
    large = max_exact + (jnp.log(jnp.maximum(n, 1).astype(F32) / max_exact)
                         / math.log(MAX_DISTANCE / max_exact) * (nb - max_exact)).astype(jnp.int32)
    large = jnp.minimum(large, nb - 1)
    return ret + jnp.where(n < max_exact, n, large)


def _bias_rows(rel_bias):
    reach = BIAS_BAND - 1
    rel = jnp.concatenate([jnp.arange(reach, -reach - 1, -1, dtype=jnp.int32),
                           jnp.array([-(2 ** 20), 2 ** 20], jnp.int32)])
    f = rel_bias[_t5_bucket(rel)].astype(F32).T * LOG2E
    rows = jnp.pad(f[:, :-2], ((0, 0), (0, 2 * BIAS_BAND - (2 * reach + 1))))
    stats = jnp.stack([jnp.max(f, axis=1), f[:, -2], f[:, -1]], axis=1)
    return rows[:, None, :], stats


def kernel(x, c, w_ada, b_ada, norm_mix_g, w_in, ret_gn_g, diff_subln_g, lambda_q1, lambda_k1, lambda_q2, lambda_k2, w_ret_out, w_diff_out, w_o, rel_bias, norm_ffn_g, w_router, w_exp_gate, w_exp_up, w_exp_down, final_g):
    B, S, D = x.shape
    cap = CAPACITY_FACTOR * S // N_EXPERTS
    l = 0

    mod = _ada(c, w_ada[l], b_ada[l]).reshape(B, 6, D)
    w_in_b = w_in[l].astype(BF16)
    proj, vt = _inproj(x, norm_mix_g[l].reshape(1, D), mod, w_in_b, _rot_tables(S))

    cd, dec, dmat = _ret_tables()
    y_ret = _ret(proj, cd, dec, dmat, ret_gn_g[l].reshape(1, -1))

    lam_vecs = jnp.stack([lambda_q1[l], lambda_k1[l], lambda_q2[l], lambda_k2[l]]).astype(F32)
    brow, bstat = _bias_rows(rel_bias)
    subln = jnp.broadcast_to(diff_subln_g[l].astype(F32)[:, None], (DIFF_DV, Q_TILE))
    y_diff = _diff(proj, vt, lam_vecs, brow, bstat, subln)

    x1, h2, logits = _merge(x, y_ret, y_diff, mod, norm_mix_g[l].reshape(1, D), w_in_b[:, W_GR * LANES:],
                            w_ret_out[l].astype(BF16), w_diff_out[l].astype(BF16), w_o[l].astype(BF16),
                            norm_ffn_g[l].reshape(1, D), w_router[l].T.astype(BF16))

    rank, gate = _route(logits, cap)
    table, ybufs = h2.reshape(B * S, D // 2), []
    n_e = N_EXPERTS // MOE_GROUPS
    for k in range(MOE_GROUPS):
        xin = _gather_rows(table, rank[:, k * n_e:(k + 1) * n_e], cap)
        ybufs.append(_moe(rank, gate, xin, w_exp_gate[l], w_exp_up[l], w_exp_down[l], cap, k * n_e, n_e))
    return _scat(rank, ybufs, x1, mod, final_g.reshape(1, D), cap)
```

```python
import functools
import math

import numpy as np
import jax
import jax.numpy as jnp
from jax import lax
from jax.experimental import pallas as pl
from jax.experimental.pallas import tpu as pltpu
from jax.experimental.pallas import tpu_sc as plsc

F32 = jnp.float32
BF16 = jnp.bfloat16

RET_HEADS = 4
RET_DK = 128
RET_DV = 128
RET_FWD_DECAY_OFFSET = 5.0
RET_BWD_DECAY_OFFSET = 5.5
RET_THETA_BASE = 10000.0
DIFF_HEADS = 4
DIFF_DH = 64
DIFF_DV = 2 * DIFF_DH
N_BUCKETS = 32
MAX_DISTANCE = 128
N_EXPERTS = 16
CAPACITY_FACTOR = 2
NORM_EPS = 1e-6
LAM_INIT = 0.8 - 0.6 * math.exp(-0.3 * 0)
LOG2E = math.log2(math.e)

LANES = 128
BF16_SUBLANES = 16
VMEM_LIMIT_BYTES = 56 * 2**20

SC_CORES = 2
SC_SUBCORES = 16
SC_LANES = 16
SC_GATHER_ROWS = 64

TOKEN_TILE = 512
PROJ_TILE = 1024
RET_CHUNK = 256
Q_TILE = 256
KEY_CHUNK = 512
DIFF_TILES_PER_LOOP = 8
L_MIN = 2.0 ** -60
BIAS_BAND = Q_TILE + 2 * MAX_DISTANCE
SCAT_WINDOW = 128
MOE_SEQS_PER_STEP = 4
ROUTE_SEQS_PER_STEP = 4
MOE_GROUPS = 2

W_RQ, W_RK, W_RV, W_RG = 0, 4, 8, 12
W_DQ1, W_DQ2, W_DK1, W_DK2, W_DV = 16, 18, 20, 22, 24
W_GR, W_GD = 28, 36

COL_RQ, COL_RK, COL_RV, COL_RG = 0, 4, 8, 12
COL_DQ, COL_DK = 16, 20
PROJ_COLS = 24 * LANES


def _cparams(sem):
    return pltpu.CompilerParams(dimension_semantics=sem, vmem_limit_bytes=VMEM_LIMIT_BYTES)


def _resident(shape, index_map):
    return pl.BlockSpec(shape, index_map, pipeline_mode=pl.Buffered(1))


def _ada_kernel(c_ref, w_ref, b_ref, o_ref):
    c = c_ref[...]
    a = c * jax.nn.sigmoid(c)
    o_ref[...] = jnp.dot(a, w_ref[...], preferred_element_type=F32,
                         precision=lax.Precision.HIGHEST) + b_ref[...]


def _ada(c, w, b):
    B, D = c.shape
    n = w.shape[1] // D
    return pl.pallas_call(
        _ada_kernel,
        grid=(n,),
        in_specs=[pl.BlockSpec((B, D), lambda j: (0, 0)),
                  pl.BlockSpec((D, D), lambda j: (0, j)),
                  pl.BlockSpec((1, D), lambda j: (0, j))],
        out_specs=pl.BlockSpec((B, D), lambda j: (0, j)),
        out_shape=jax.ShapeDtypeStruct((B, n * D), F32),
        compiler_params=_cparams(("arbitrary",)),
        name="ada",
    )(c, w, b.reshape(1, -1))


def _rms(x, g):
    ms = jnp.mean(x * x, axis=-1, keepdims=True)
    return x * lax.rsqrt(ms + NORM_EPS) * g


def _rot_pairs(x, cos, sin):
    even = (lax.broadcasted_iota(jnp.int32, x.shape, 1) & 1) == 0
    partner = jnp.where(even, pltpu.roll(x, LANES - 1, axis=1), pltpu.roll(x, 1, axis=1))
    return x * cos + partner * sin


def _pair_maps(a, b, second):
    low = lax.broadcasted_iota(jnp.int32, a.shape, 1) < DIFF_DH
    if second:
        return jnp.where(low, pltpu.roll(a, DIFF_DH, axis=1), b)
    return jnp.where(low, a, pltpu.roll(b, DIFF_DH, axis=1))


def _inproj_kernel(x_ref, g_ref, mod_ref, w_ref, rot_ref, o_ref, vt_ref, wvt_s):
    @pl.when((pl.program_id(0) == 0) & (pl.program_id(1) == 0))
    def _():
        wvt_s[...] = w_ref[:, W_DV * LANES:(W_DV + 4) * LANES].astype(F32).T.astype(BF16)

    parts = 4
    step = x_ref.shape[1] // parts

    def normed(p):
        x = x_ref[0, p * step:(p + 1) * step, :]
        h = _rms(x, g_ref[...]) * (1.0 + mod_ref[0, 1:2, :]) + mod_ref[0, 0:1, :]
        return h.astype(BF16)

    def project(p, hb):
        rows = slice(p * step, (p + 1) * step)

        def mm(col, width):
            return jnp.dot(hb, w_ref[:, col * LANES:(col + width) * LANES], preferred_element_type=F32)

        def put(col, val):
            o_ref[0, rows, col * LANES:col * LANES + val.shape[1]] = val.astype(BF16)

        for src, dst, t in ((W_RQ, COL_RQ, 0), (W_RK, COL_RK, 2)):
            cos, sin = rot_ref[t, rows, :], rot_ref[t + 1, rows, :]
            r = mm(src, RET_HEADS)
            for hh in range(RET_HEADS):
                put(dst + hh, _rot_pairs(r[:, hh * LANES:(hh + 1) * LANES], cos, sin))
        put(COL_RV, mm(W_RV, 4))
        r = mm(W_RG, 4)
        put(COL_RG, r * jax.nn.sigmoid(r))
        for src1, src2, dst, scale in ((W_DQ1, W_DQ2, COL_DQ, DIFF_DH ** -0.5 * LOG2E),
                                       (W_DK1, W_DK2, COL_DK, None)):
            m1, m2 = mm(src1, 2), mm(src2, 2)
            for hh in range(DIFF_HEADS):
                blk = slice((hh // 2) * LANES, (hh // 2 + 1) * LANES)
                val = _pair_maps(m1[:, blk], m2[:, blk], hh % 2 == 1)
                put(dst + hh, val if scale is None else val * scale)
        vt_ref[0, :, rows] = lax.dot_general(wvt_s[...], hb, (((1,), (1,)), ((), ())),
                                             preferred_element_type=F32).astype(BF16)

    hb = normed(0)
    for p in range(parts):
        nxt = normed(p + 1) if p + 1 < parts else None
        project(p, hb)
        hb = nxt


def _inproj(x, g, mod, w, rot):
    B, S, D = x.shape
    tm = PROJ_TILE
    vw = DIFF_HEADS * DIFF_DV
    return pl.pallas_call(
        _inproj_kernel,
        grid=(B, S // tm),
        in_specs=[pl.BlockSpec((1, tm, D), lambda b, i: (b, i, 0)),
                  _resident((1, D), lambda b, i: (0, 0)),
                  pl.BlockSpec((1, 6, D), lambda b, i: (b, 0, 0)),
                  _resident(w.shape, lambda b, i: (0, 0)),
                  pl.BlockSpec((4, tm, LANES), lambda b, i: (0, i, 0))],
        out_specs=[pl.BlockSpec((1, tm, PROJ_COLS), lambda b, i: (b, i, 0)),
                   pl.BlockSpec((1, vw, tm), lambda b, i: (b, 0, i))],
        out_shape=[jax.ShapeDtypeStruct((B, S, PROJ_COLS), BF16),
                   jax.ShapeDtypeStruct((B, vw, S), BF16)],
        scratch_shapes=[pltpu.VMEM((vw, D), BF16)],
        compiler_params=_cparams(("arbitrary", "arbitrary")),
        name="inproj",
    )(x, g, mod, w, rot)


def _ret_kernel(cd_ref, q_ref, k_ref, v_ref, rg_ref, dec_ref, dm_ref, gn_ref, o_ref, acc_ref):
    S = q_ref.shape[1]
    H = dm_ref.shape[0]
    C = RET_CHUNK
    nc = S // C
    nt = (((1,), (1,)), ((), ()))
    tn = (((0,), (0,)), ((), ()))

    def chunk(n, h):
        sl, lanes = pl.ds(n * C, C), slice(h * LANES, (h + 1) * LANES)
        return sl, q_ref[0, sl, lanes], k_ref[0, sl, lanes], v_ref[0, sl, lanes]

    def scaled(t, dec):
        return (t.astype(F32) * dec).astype(BF16)

    def forward(heads):
        state = {h: jnp.zeros((RET_DK, RET_DV), F32) for h in heads}
        for n in range(nc):
            for h in heads:
                sl, q, k, v = chunk(n, h)
                s = lax.dot_general(q, k, nt, preferred_element_type=F32) * dm_ref[h]
                inner = jnp.dot(s.astype(BF16), v, preferred_element_type=F32)
                cross = jnp.dot(scaled(q, dec_ref[h, 0]), state[h].astype(BF16), preferred_element_type=F32)
                acc_ref[h, sl, :] = inner + cross
                kv = lax.dot_general(scaled(k, dec_ref[h, 1]), v, tn, preferred_element_type=F32)
                state[h] = cd_ref[h, 0] * state[h] + kv

    def backward(heads):
        state = {h: jnp.zeros((RET_DK, RET_DV), F32) for h in heads}
        for n in reversed(range(nc)):
            for h in heads:
                sl, q, k, v = chunk(n, h)
                cross = jnp.dot(scaled(q, dec_ref[h, 2]), state[h].astype(BF16), preferred_element_type=F32)
                acc_ref[h, sl, :] = acc_ref[h, sl, :] + cross
                kv = lax.dot_general(scaled(k, dec_ref[h, 3]), v, tn, preferred_element_type=F32)
                state[h] = cd_ref[h, 1] * state[h] + kv

    def normalise(heads):
        for h in heads:
            lanes = slice(h * LANES, (h + 1) * LANES)
            y = acc_ref[h]
            mu = jnp.mean(y, axis=-1, keepdims=True)
            yc = y - mu
            var = jnp.mean(yc * yc, axis=-1, keepdims=True)
            yn = yc * lax.rsqrt(var + NORM_EPS) * gn_ref[:, lanes]
            o_ref[0, :, lanes] = (rg_ref[0, :, lanes].astype(F32) * yn).astype(BF16)

    first, second = tuple(range(H // 2)), tuple(range(H // 2, H))
    forward(first)
    backward(first)
    forward(second)
    normalise(first)
    backward(second)
    normalise(second)


def _ret(proj, cd, dec, dmat, gn):
    B, S, _ = proj.shape
    H = RET_HEADS
    C = RET_CHUNK

    def cols(base):
        return pl.BlockSpec((1, S, H * LANES), lambda b: (b, 0, base // H))

    return pl.pallas_call(
        _ret_kernel,
        grid=(B,),
        in_specs=[pl.BlockSpec(memory_space=pltpu.SMEM),
                  cols(COL_RQ), cols(COL_RK), cols(COL_RV), cols(COL_RG),
                  _resident((H, 4, C, LANES), lambda b: (0, 0, 0, 0)),
                  _resident((H, C, C), lambda b: (0, 0, 0)),
                  _resident((1, H * LANES), lambda b: (0, 0))],
        out_specs=pl.BlockSpec((1, S, H * LANES), lambda b: (b, 0, 0)),
        out_shape=jax.ShapeDtypeStruct((B, S, H * RET_DV), BF16),
        scratch_shapes=[pltpu.VMEM((H, S, RET_DV), F32)],
        compiler_params=_cparams(("arbitrary",)),
        name="ret",
    )(cd, proj, proj, proj, proj, dec, dmat, gn)


def _diff_kernel(bstat_ref, lam_ref, q_ref, k_ref, vt_ref, brow_ref, g_ref, o_ref, p_ref, t_ref):
    hh = pl.program_id(0)
    S = k_ref.shape[1]
    TQ, KB = Q_TILE, KEY_CHUNK
    nk = S // KB
    lv = lam_ref[...]
    lam = (jnp.exp(jnp.sum(lv[0:1] * lv[1:2], axis=1, keepdims=True))
           - jnp.exp(jnp.sum(lv[2:3] * lv[3:4], axis=1, keepdims=True)) + LAM_INIT)
    first_half = lax.broadcasted_iota(jnp.int32, (TQ, LANES), 1) < DIFF_DH
    nt = (((1,), (1,)), ((), ()))
    bmax = bstat_ref[hh, 0]

    @pl.when(pl.program_id(1) == 0)
    def _():
        side = S - TQ - MAX_DISTANCE
        t_ref[0:side, :] = jnp.full((side, TQ), bstat_ref[hh, 1], F32)
        rows = jnp.broadcast_to(brow_ref[0], (BIAS_BAND, brow_ref.shape[2]))
        shift = brow_ref.shape[2] - (BIAS_BAND - 1 + MAX_DISTANCE)
        t_ref[side:side + BIAS_BAND, :] = pltpu.roll(rows, shift, 1, stride=1, stride_axis=0)[:, :TQ]
        t_ref[side + BIAS_BAND:, :] = jnp.full((side, TQ), bstat_ref[hh, 2], F32)
    kf = k_ref[0].astype(F32)
    kmax = jnp.sqrt(jnp.max(jnp.sum(kf * kf, axis=1, keepdims=True), axis=0, keepdims=True))
    sel_r = lax.broadcasted_iota(jnp.int32, (8, LANES), 0)
    sel_c = lax.broadcasted_iota(jnp.int32, (8, LANES), 1)
    sel = jnp.where((sel_c < DIFF_DH) == (sel_r == 0), 1.0, 0.0)
    sel = jnp.where(sel_r < 2, sel, 0.0).astype(BF16)

    def masked_q(qb):
        q = q_ref[0, pl.ds(pl.multiple_of(qb * TQ, TQ), TQ), :]
        zero = jnp.zeros_like(q)
        return q, (jnp.where(first_half, q, zero), jnp.where(first_half, zero, q))

    def finish(ot, qb):
        ms = jnp.mean(ot * ot, axis=0, keepdims=True)
        y = ot * lax.rsqrt(ms + NORM_EPS) * g_ref[...] * (1.0 - LAM_INIT)
        o_ref[0, pl.ds(pl.multiple_of(qb * TQ, TQ), TQ), :] = y.T.astype(BF16)

    def exponentials(qb, slot):
        w0 = pl.multiple_of((S - TQ) - qb * TQ, TQ)
        q, qz = masked_q(qb)
        qsq = (q.astype(F32) * q.astype(F32)).astype(BF16)
        n2 = lax.dot_general(sel, qsq, nt, preferred_element_type=F32) * 1.01
        m = [jnp.sqrt(n2[i:i + 1]) * kmax + bmax for i in range(2)]
        l = [jnp.zeros((8, TQ), F32) for _ in range(2)]
        for c in range(nk):
            rows = slice(c * KB, (c + 1) * KB)
            kb = k_ref[0, rows, :]
            bias = t_ref[pl.ds(w0 + c * KB, KB), :]
            for i in range(2):
                s = lax.dot_general(kb, qz[i], nt, preferred_element_type=F32)
                p = jnp.exp2((s - m[i]) + bias)
                l[i] = l[i] + jnp.sum(p.reshape(KB // 8, 8, TQ), axis=0)
                p_ref[slot, i, rows, :] = p.astype(BF16)
        return [jnp.sum(l[i], axis=0, keepdims=True) for i in range(2)]

    def values(qb, slot, lr):
        ratio = jnp.broadcast_to(lam * lr[0] / lr[1], (16, TQ)).astype(BF16)
        ot = jnp.zeros((DIFF_DV, TQ), F32)
        for c in range(nk):
            rows = slice(c * KB, (c + 1) * KB)
            p1 = p_ref[slot, 0, rows, :].reshape(KB // 16, 16, TQ)
            p2 = p_ref[slot, 1, rows, :].reshape(KB // 16, 16, TQ)
            a = (p1 - ratio * p2).reshape(KB, TQ)
            ot = ot + jnp.dot(vt_ref[0, :, rows], a, preferred_element_type=F32)
        finish(ot * (1.0 / lr[0]), qb)

    def fast_group(j, lmin):
        base = j * DIFF_TILES_PER_LOOP
        sums = exponentials(base, 0)
        for t in range(DIFF_TILES_PER_LOOP):
            lmin = jnp.minimum(lmin, jnp.minimum(sums[0], sums[1]))
            nxt = exponentials(base + t + 1, (t + 1) % 2) if t + 1 < DIFF_TILES_PER_LOOP else None
            values(base + t, t % 2, sums)
            sums = nxt
        return lmin

    lmin = lax.fori_loop(0, S // (DIFF_TILES_PER_LOOP * TQ), fast_group, jnp.full((1, TQ), jnp.inf, F32))

    @pl.when(jnp.min(lmin) < L_MIN)
    def _():
        def exact_tile(qb, carry):
            w0 = pl.multiple_of((S - TQ) - qb * TQ, TQ)
            _, qz = masked_q(qb)
            bias = t_ref[pl.ds(w0, S), :]
            attn = None
            for i, scale in ((0, 1.0), (1, lam)):
                s = lax.dot_general(k_ref[0], qz[i], nt, preferred_element_type=F32) + bias
                p = jnp.exp2(s - jnp.max(s, axis=0, keepdims=True))
                part = p * (scale / jnp.sum(p, axis=0, keepdims=True))
                attn = part if attn is None else attn - part
            finish(jnp.dot(vt_ref[0], attn.astype(BF16), preferred_element_type=F32), qb)
            return carry

        lax.fori_loop(0, S // TQ, exact_tile, 0)


def _diff(proj, vt, lam_vecs, brow, bstat, g):
    B, S, _ = proj.shape
    H = DIFF_HEADS

    def col(base):
        return pl.BlockSpec((1, S, LANES), lambda h, b: (b, 0, base + h))

    return pl.pallas_call(
        _diff_kernel,
        grid=(H, B),
        in_specs=[pl.BlockSpec(memory_space=pltpu.SMEM),
                  pl.BlockSpec((4, DIFF_DH), lambda h, b: (0, 0)),
                  col(COL_DQ), col(COL_DK),
                  pl.BlockSpec((1, DIFF_DV, S), lambda h, b: (b, h, 0)),
                  pl.BlockSpec((1, 1, brow.shape[2]), lambda h, b: (h, 0, 0)),
                  pl.BlockSpec((DIFF_DV, Q_TILE), lambda h, b: (0, 0))],
        out_specs=pl.BlockSpec((1, S, LANES), lambda h, b: (b, 0, h)),
        out_shape=jax.ShapeDtypeStruct((B, S, H * DIFF_DV), BF16),
        scratch_shapes=[pltpu.VMEM((2, 2, S, Q_TILE), BF16), pltpu.VMEM((2 * S - Q_TILE, Q_TILE), F32)],
        compiler_params=_cparams(("arbitrary", "arbitrary")),
        name="diff",
    )(bstat, lam_vecs, proj, proj, vt, brow, g)


def _pack_halves(x):
    w = x.shape[1] // 2
    lo = pltpu.bitcast(x[:, :w].astype(BF16).astype(F32), jnp.int32)
    hi = pltpu.bitcast(x[:, w:].astype(BF16).astype(F32), jnp.int32)
    return lax.shift_right_logical(lo, jnp.full_like(lo, 16)) | (hi & jnp.int32(-65536))


def _unpack_halves(words):
    lo = pltpu.bitcast(lax.shift_left(words, jnp.full_like(words, 16)), F32)
    hi = pltpu.bitcast(words & jnp.int32(-65536), F32)
    return jnp.concatenate([lo, hi], axis=1).astype(BF16)


def _merge_kernel(x_ref, yr_ref, yd_ref, mod_ref, gmix_ref, wg_ref, wr_ref, wd_ref, wo_ref,
                  g_ref, wrt_ref, x1_ref, h2_ref, lg_ref):
    parts = 4
    gate_w = wg_ref.shape[1] // 2
    step = x_ref.shape[1] // parts

    def project(r):
        rows = slice(r * step, (r + 1) * step)
        h = _rms(x_ref[0, rows, :], gmix_ref[...]) * (1.0 + mod_ref[0, 1:2, :]) + mod_ref[0, 0:1, :]
        gates = jax.nn.sigmoid(jnp.dot(h.astype(BF16), wg_ref[...], preferred_element_type=F32))
        a = jnp.dot(yr_ref[0, rows, :], wr_ref[...], preferred_element_type=F32)
        d = jnp.dot(yd_ref[0, rows, :], wd_ref[...], preferred_element_type=F32)
        merged = gates[:, :gate_w] * a + gates[:, gate_w:] * d
        return jnp.dot(merged.astype(BF16), wo_ref[...], preferred_element_type=F32)

    def epilogue(r, o):
        rows = slice(r * step, (r + 1) * step)
        x1 = x_ref[0, rows, :] + mod_ref[0, 2:3, :] * o
        x1_ref[0, rows, :] = x1
        h2 = _rms(x1, g_ref[...]) * (1.0 + mod_ref[0, 4:5, :]) + mod_ref[0, 3:4, :]
        h2_ref[0, rows, :] = _pack_halves(h2)
        lg_ref[0, :, rows] = lax.dot_general(wrt_ref[...], h2.astype(BF16), (((1,), (1,)), ((), ())),
                                             preferred_element_type=F32)

    o = project(0)
    for r in range(parts):
        nxt = project(r + 1) if r + 1 < parts else None
        epilogue(r, o)
        o = nxt


def _merge(x, yr, yd, mod, gmix, wg, wr, wd, wo, g, wrt):
    B, S, D = x.shape
    tm = PROJ_TILE
    E = wrt.shape[0]
    return pl.pallas_call(
        _merge_kernel,
        grid=(B, S // tm),
        in_specs=[pl.BlockSpec((1, tm, D), lambda b, i: (b, i, 0)),
                  pl.BlockSpec((1, tm, yr.shape[2]), lambda b, i: (b, i, 0)),
                  pl.BlockSpec((1, tm, yd.shape[2]), lambda b, i: (b, i, 0)),
                  pl.BlockSpec((1, 6, D), lambda b, i: (b, 0, 0)),
                  _resident((1, D), lambda b, i: (0, 0)),
                  _resident(wg.shape, lambda b, i: (0, 0)),
                  _resident(wr.shape, lambda b, i: (0, 0)),
                  _resident(wd.shape, lambda b, i: (0, 0)),
                  _resident(wo.shape, lambda b, i: (0, 0)),
                  _resident((1, D), lambda b, i: (0, 0)),
                  _resident(wrt.shape, lambda b, i: (0, 0))],
        out_specs=[pl.BlockSpec((1, tm, D), lambda b, i: (b, i, 0)),
                   pl.BlockSpec((1, tm, D // 2), lambda b, i: (b, i, 0)),
                   pl.BlockSpec((1, E, tm), lambda b, i: (b, 0, i))],
        out_shape=[jax.ShapeDtypeStruct((B, S, D), F32),
                   jax.ShapeDtypeStruct((B, S, D // 2), jnp.int32),
                   jax.ShapeDtypeStruct((B, E, S), F32)],
        compiler_params=_cparams(("arbitrary", "arbitrary")),
        name="merge",
    )(x, yr, yd, mod, gmix, wg, wr, wd, wo, g, wrt)


def _lane_prefix(m, tri):
    E, S = m.shape
    off = jnp.zeros((E, 1), F32)
    parts = []
    for j in range(S // LANES):
        blk = m[:, j * LANES:(j + 1) * LANES]
        parts.append(jnp.dot(blk.astype(BF16), tri, preferred_element_type=F32) + off)
        off = off + jnp.sum(blk, axis=1, keepdims=True)
    return jnp.concatenate(parts, axis=1)


def _route_kernel(lg_ref, rank_ref, gate_ref, rank_t_ref, *, cap):
    E, S = lg_ref.shape[1], lg_ref.shape[2]
    for j in range(lg_ref.shape[0]):
        rank, gate_ref[j] = _route_one(lg_ref[j], cap)
        rank_ref[j] = rank.astype(jnp.int32)
        padded = jnp.concatenate([rank, jnp.full((LANES - E, S), -1.0, F32)], axis=0)
        rank_t_ref[j] = padded.T.astype(jnp.int32)


def _route_one(lg, cap):
    e = jnp.exp(lg - jnp.max(lg, axis=0, keepdims=True))
    aff = e / jnp.sum(e, axis=0, keepdims=True)
    bits = pltpu.bitcast(aff, jnp.int32)
    E = lg.shape[0]

    def count(mask):
        return jnp.sum(jnp.where(mask, 1.0, 0.0), axis=1, keepdims=True)

    def reaches(cand):
        return count(bits >= cand) >= cap

    thr = jnp.zeros((E, 1), jnp.int32)
    for hi in range(30, 0, -2):
        c1, c2 = thr | (1 << hi), thr | (1 << (hi - 1))
        c3 = c1 | (1 << (hi - 1))
        thr = jnp.where(reaches(c3), c3, jnp.where(reaches(c1), c1, jnp.where(reaches(c2), c2, thr)))
    thr = jnp.where(reaches(thr | 1), thr | 1, thr)
    gt = bits > thr
    eq = bits == thr
    need = cap - count(gt)
    r = lax.broadcasted_iota(jnp.int32, (LANES, LANES), 0)
    c = lax.broadcasted_iota(jnp.int32, (LANES, LANES), 1)
    tri = jnp.where(r < c, 1.0, 0.0).astype(BF16)
    eq_before = _lane_prefix(jnp.where(eq, 1.0, 0.0), tri)
    sel = gt | (eq & (eq_before < need))
    slot = _lane_prefix(jnp.where(sel, 1.0, 0.0), tri)
    return jnp.where(sel, slot, -1.0), jnp.where(sel, aff, 0.0)


def _route(logits, cap):
    B, E, S = logits.shape
    G = ROUTE_SEQS_PER_STEP
    assert B % G == 0
    spec = pl.BlockSpec((G, E, S), lambda b: (b, 0, 0))
    return pl.pallas_call(
        functools.partial(_route_kernel, cap=cap),
        grid=(B // G,),
        in_specs=[spec],
        out_specs=[spec, spec, pl.BlockSpec((G, S, LANES), lambda b: (b, 0, 0))],
        out_shape=[jax.ShapeDtypeStruct((B, E, S), jnp.int32),
                   jax.ShapeDtypeStruct((B, E, S), F32),
                   jax.ShapeDtypeStruct((B, S, LANES), jnp.int32)],
        compiler_params=_cparams(("arbitrary",)),
        name="route",
    )(logits)


def _gather_rows(table, rank, cap):
    B, E, S = rank.shape
    W = table.shape[1]
    workers = SC_CORES * SC_SUBCORES
    pairs = E * B
    per = pairs // workers
    assert per * workers == pairs and B & (B - 1) == 0 and cap % SC_GATHER_ROWS == 0 and S % SC_LANES == 0
    shift = B.bit_length() - 1
    mesh = plsc.VectorSubcoreMesh(core_axis_name="c", subcore_axis_name="s")

    R = SC_GATHER_ROWS
    n_chunks = cap // R

    def body(table_hbm, rank_hbm, out_hbm, rank_v, idx_v, buf_v, gsem, wsem):
        wid = lax.axis_index("s") * SC_CORES + lax.axis_index("c")

        def fetch(c):
            return pltpu.make_async_copy(table_hbm.at[idx_v.at[pl.ds(c * R, R)]], buf_v.at[c % 2], gsem.at[c % 2])

        def flush(p, c):
            return pltpu.make_async_copy(buf_v.at[c % 2], out_hbm.at[pl.ds(p * cap + c * R, R)], wsem.at[c % 2])

        for j in range(per):
            p = wid * per + j
            e = lax.shift_right_logical(p, shift)
            b = p & (B - 1)
            pltpu.sync_copy(rank_hbm.at[b, e], rank_v)
            base = b * S

            @pl.loop(0, S, step=SC_LANES)
            def _(t0):
                r = rank_v[pl.ds(t0, SC_LANES)]
                tok = lax.iota(jnp.int32, SC_LANES) + (base + t0)
                plsc.store_scatter(idx_v, [r], tok, mask=r >= 0)

            fetch(0).start()
            for c in range(n_chunks):
                fetch(c).wait()
                if c >= 1:
                    flush(p, c - 1).wait()
                if c + 1 < n_chunks:
                    fetch(c + 1).start()
                flush(p, c).start()
            flush(p, n_chunks - 1).wait()

    return pl.kernel(
        body,
        out_type=jax.ShapeDtypeStruct((pairs * cap, W), jnp.int32),
        mesh=mesh,
        scratch_types=[pltpu.VMEM((S,), jnp.int32), pltpu.VMEM((cap,), jnp.int32),
                       pltpu.VMEM((2, R, W), jnp.int32),
                       pltpu.SemaphoreType.DMA((2,)), pltpu.SemaphoreType.DMA((2,))],
        compiler_params=pltpu.CompilerParams(needs_layout_passes=False),
    )(table, rank)


def _moe_kernel(rank_ref, gate_ref, x_ref, wg_ref, wu_ref, wd_ref, o_ref, wg_s, wu_s, wd_s, *, cap):
    r, b = pl.program_id(0), pl.program_id(1)
    n_exp = pl.num_programs(0) - 1

    @pl.when(r < n_exp)
    def _():
        slot = r % 2
        rows_in, rows_ff = wg_ref.shape[1], wd_ref.shape[1]
        wg_s[slot, pl.ds(pl.multiple_of(b * rows_in, rows_in), rows_in), :] = wg_ref[0].astype(BF16)
        wu_s[slot, pl.ds(pl.multiple_of(b * rows_in, rows_in), rows_in), :] = wu_ref[0].astype(BF16)
        wd_s[slot, pl.ds(pl.multiple_of(b * rows_ff, rows_ff), rows_ff), :] = wd_ref[0].astype(BF16)

    @pl.when(r == 0)
    def _():
        o_ref[...] = jnp.zeros_like(o_ref)

    @pl.when(r > 0)
    def _():
        slot = (r - 1) % 2
        S = rank_ref.shape[3]
        for j in range(x_ref.shape[0]):
            rank = rank_ref[j, 0]
            pick = lax.broadcasted_iota(jnp.int32, (cap, S), 0) == rank
            xin = _unpack_halves(x_ref[j])
            a = jnp.dot(xin, wg_s[slot], preferred_element_type=F32)
            u = jnp.dot(xin, wu_s[slot], preferred_element_type=F32)
            act = (a * jax.nn.sigmoid(a) * u).astype(BF16)
            y = jnp.dot(act, wd_s[slot], preferred_element_type=F32)
            g = jnp.sum(jnp.where(pick, gate_ref[j, 0], 0.0), axis=1, keepdims=True)
            o_ref[0, j] = (y * g).astype(BF16)


def _moe(rank, gate, xin, wg, wu, wd, cap, e0, n_e):
    B, E, S = rank.shape
    D, Fd = wg.shape[1], wg.shape[2]
    G = MOE_SEQS_PER_STEP
    steps = B // G
    assert steps * G == B and D % steps == 0 and Fd % steps == 0

    def cur(r):
        return e0 + jnp.maximum(r - 1, 0)

    def nxt(r):
        return e0 + jnp.minimum(r, n_e - 1)

    row = pl.BlockSpec((G, 1, 1, S), lambda r, b: (b, cur(r), 0, 0))
    return pl.pallas_call(
        functools.partial(_moe_kernel, cap=cap),
        grid=(n_e + 1, steps),
        in_specs=[row, row,
                  pl.BlockSpec((G, cap, D // 2), lambda r, b: (jnp.maximum(r - 1, 0) * steps + b, 0, 0)),
                  pl.BlockSpec((1, D // steps, Fd), lambda r, b: (nxt(r), b, 0)),
                  pl.BlockSpec((1, D // steps, Fd), lambda r, b: (nxt(r), b, 0)),
                  pl.BlockSpec((1, Fd // steps, D), lambda r, b: (nxt(r), b, 0))],
        out_specs=pl.BlockSpec((1, G, cap, D), lambda r, b: (jnp.where(r == 0, n_e, r - 1), b, 0, 0)),
        out_shape=jax.ShapeDtypeStruct((n_e + 1, B, cap, D), BF16),
        scratch_shapes=[pltpu.VMEM((2, D, Fd), BF16), pltpu.VMEM((2, D, Fd), BF16),
                        pltpu.VMEM((2, Fd, D), BF16)],
        compiler_params=_cparams(("arbitrary", "arbitrary")),
        name="moe",
    )(rank.reshape(B, E, 1, S), gate.reshape(B, E, 1, S), xin.reshape(n_e * B, cap, D // 2), wg, wu, wd)


def _scat_kernel(win_ref, ok_ref, rt_ref, *refs, cap):
    *y_refs, x1_ref, mod_ref, g_ref, o_ref = refs
    b, i = pl.program_id(0), pl.program_id(1)
    rt = rt_ref[0]
    tm, E = rt.shape[0], win_ref.shape[2]
    D = x1_ref.shape[2]
    per = E // len(y_refs)

    def finish(moe):
        x2 = x1_ref[0] + mod_ref[0, 5:6, :] * moe
        o_ref[0] = _rms(x2, g_ref[...])

    @pl.when(ok_ref[b, i] != 0)
    def _():
        slot = lax.broadcasted_iota(jnp.int32, (tm, SCAT_WINDOW), 1)
        moe = jnp.zeros((tm, D), F32)
        for e in range(0, E, 2):
            starts = [pl.multiple_of(win_ref[b, i, e + k], BF16_SUBLANES) for k in range(2)]
            place = jnp.concatenate(
                [jnp.where(rt[:, e + k:e + k + 1] - starts[k] == slot, 1.0, 0.0).astype(BF16)
                 for k in range(2)], axis=1)
            y = jnp.concatenate([y_refs[(e + k) // per][(e + k) % per, 0, pl.ds(starts[k], SCAT_WINDOW), :]
                                 for k in range(2)], axis=0)
            moe = moe + jnp.dot(place, y, preferred_element_type=F32)
        finish(moe)

    @pl.when(ok_ref[b, i] == 0)
    def _():
        slot = lax.broadcasted_iota(jnp.int32, (tm, cap), 1)
        place = jnp.concatenate(
            [jnp.where(rt[:, e:e + 1] == slot, 1.0, 0.0).astype(BF16) for e in range(E)], axis=1)
        y = jnp.concatenate([y_ref[0:per, 0].reshape(per * cap, D) for y_ref in y_refs], axis=0)
        finish(jnp.dot(place, y, preferred_element_type=F32))


def _scat_windows(rank, cap, tm):
    B, E, S = rank.shape
    counts = jnp.sum((rank >= 0).reshape(B, E, S // tm, tm), axis=-1, dtype=jnp.int32)
    ends = jnp.cumsum(counts, axis=-1)
    start = ends - counts
    win = jnp.minimum(start // BF16_SUBLANES * BF16_SUBLANES, cap - SCAT_WINDOW)
    ok = jnp.all(ends - win <= SCAT_WINDOW, axis=1)
    return jnp.swapaxes(win, 1, 2), ok.astype(jnp.int32)


def _scat(rank, rank_t, ybufs, x1, mod, g, cap):
    B, S, D = x1.shape
    E = rank.shape[1]
    tm = TOKEN_TILE
    win, ok = _scat_windows(rank, cap, tm)
    smem = pl.BlockSpec(memory_space=pltpu.SMEM)
    return pl.pallas_call(
        functools.partial(_scat_kernel, cap=cap),
        grid=(B, S // tm),
        in_specs=[smem, smem,
                  pl.BlockSpec((1, tm, LANES), lambda b, i: (b, i, 0)),
                  *[pl.BlockSpec(y.shape[:1] + (1, cap, D), lambda b, i: (0, b, 0, 0)) for y in ybufs],
                  pl.BlockSpec((1, tm, D), lambda b, i: (b, i, 0)),
                  pl.BlockSpec((1, 6, D), lambda b, i: (b, 0, 0)),
                  _resident((1, D), lambda b, i: (0, 0))],
        out_specs=pl.BlockSpec((1, tm, D), lambda b, i: (b, i, 0)),
        out_shape=jax.ShapeDtypeStruct((B, S, D), F32),
        compiler_params=_cparams(("arbitrary", "arbitrary")),
        name="scat",
    )(win, ok, rank_t, *ybufs, x1, mod, g)


def _rot_tables(S):
    half = RET_DK // 2
    inv = 1.0 / (RET_THETA_BASE ** np.linspace(0.0, 1.0, half))
    ang = np.arange(S, dtype=np.float64)[:, None] * inv[None, :]
    cos = np.repeat(np.cos(ang), 2, axis=1)
    sin = np.stack([-np.sin(ang), np.sin(ang)], axis=2).reshape(S, RET_DK)
    sc = RET_DK ** -0.5
    return jnp.asarray(np.stack([cos, sin, cos * sc, sin * sc]), F32)


def _ret_tables():
    C = RET_CHUNK
    heads = np.arange(RET_HEADS, dtype=np.float64)
    lgf = np.log1p(-np.exp2(-RET_FWD_DECAY_OFFSET - heads))[:, None]
    lgb = np.log1p(-np.exp2(-RET_BWD_DECAY_OFFSET - heads))[:, None]
    idx = np.arange(C, dtype=np.float64)
    diff = idx[:, None] - idx[None, :]
    dmat = np.where(diff >= 0,
                    np.exp(np.maximum(diff, 0.0)[None] * lgf[:, :, None]),
                    np.exp(np.maximum(-diff, 0.0)[None] * lgb[:, :, None]))
    dec = np.stack([np.exp((idx + 1)[None, :] * lgf),
                    np.exp((C - 1 - idx)[None, :] * lgf),
                    np.exp((C - idx)[None, :] * lgb),
                    np.exp(idx[None, :] * lgb)], axis=1)
    dec = np.broadcast_to(dec[..., None], dec.shape + (LANES,))
    cd = np.concatenate([np.exp(C * lgf), np.exp(C * lgb)], axis=1)
    return jnp.asarray(cd, F32), jnp.asarray(dec, F32), jnp.asarray(dmat, F32)


def _t5_bucket(rel):
    nb = N_BUCKETS // 2
    max_exact = nb // 2
    ret = (rel > 0).astype(jnp.int32) * nb
    n = jnp.abs(rel)
    large = max_exact + (jnp.log(jnp.maximum(n, 1).astype(F32) / max_exact)
                         / math.log(MAX_DISTANCE / max_exact) * (nb - max_exact)).astype(jnp.int32)
    large = jnp.minimum(large, nb - 1)
    return ret + jnp.where(n < max_exact, n, large)


def _bias_rows(rel_bias):
    reach = BIAS_BAND - 1
    rel = jnp.concatenate([jnp.arange(reach, -reach - 1, -1, dtype=jnp.int32),
                           jnp.array([-(2 ** 20), 2 ** 20], jnp.int32)])
    f = rel_bias[_t5_bucket(rel)].astype(F32).T * LOG2E
    rows = jnp.pad(f[:, :-2], ((0, 0), (0, 2 * BIAS_BAND - (2 * reach + 1))))
    stats = jnp.stack([jnp.max(f, axis=1), f[:, -2], f[:, -1]], axis=1)
    return rows[:, None, :], stats


def kernel(x, c, w_ada, b_ada, norm_mix_g, w_in, ret_gn_g, diff_subln_g, lambda_q1, lambda_k1, lambda_q2, lambda_k2, w_ret_out, w_diff_out, w_o, rel_bias, norm_ffn_g, w_router, w_exp_gate, w_exp_up, w_exp_down, final_g):
    B, S, D = x.shape
    cap = CAPACITY_FACTOR * S // N_EXPERTS
    l = 0

    mod = _ada(c, w_ada[l], b_ada[l]).reshape(B, 6, D)
    w_in_b = w_in[l].astype(BF16)
    proj, vt = _inproj(x, norm_mix_g[l].reshape(1, D), mod, w_in_b, _rot_tables(S))

    cd, dec, dmat = _ret_tables()
    y_ret = _ret(proj, cd, dec, dmat, ret_gn_g[l].reshape(1, -1))

    lam_vecs = jnp.stack([lambda_q1[l], lambda_k1[l], lambda_q2[l], lambda_k2[l]]).astype(F32)
    brow, bstat = _bias_rows(rel_bias)
    subln = jnp.broadcast_to(diff_subln_g[l].astype(F32)[:, None], (DIFF_DV, Q_TILE))
    y_diff = _diff(proj, vt, lam_vecs, brow, bstat, subln)

    x1, h2, logits = _merge(x, y_ret, y_diff, mod, norm_mix_g[l].reshape(1, D), w_in_b[:, W_GR * LANES:],
                            w_ret_out[l].astype(BF16), w_diff_out[l].astype(BF16), w_o[l].astype(BF16),
                            norm_ffn_g[l].reshape(1, D), w_router[l].T.astype(BF16))

    rank, gate, rank_t = _route(logits, cap)
    table, ybufs = h2.reshape(B * S, D // 2), []
    n_e = N_EXPERTS // MOE_GROUPS
    for k in range(MOE_GROUPS):
        xin = _gather_rows(table, rank[:, k * n_e:(k + 1) * n_e], cap)
        ybufs.append(_moe(rank, gate, xin, w_exp_gate[l], w_exp_up[l], w_exp_down[l], cap, k * n_e, n_e))
    return _scat(rank, rank_t, ybufs, x1, mod, final_g.reshape(1, D), cap)
```

```python
import functools
import math

import numpy as np
import jax
import jax.numpy as jnp
from jax import lax
from jax.experimental import pallas as pl
from jax.experimental.pallas import tpu as pltpu
from jax.experimental.pallas import tpu_sc as plsc

F32 = jnp.float32
BF16 = jnp.bfloat16

RET_HEADS = 4
RET_DK = 128
RET_DV = 128
RET_FWD_DECAY_OFFSET = 5.0
RET_BWD_DECAY_OFFSET = 5.5
RET_THETA_BASE = 10000.0
DIFF_HEADS = 4
DIFF_DH = 64
DIFF_DV = 2 * DIFF_DH
N_BUCKETS = 32
MAX_DISTANCE = 128
N_EXPERTS = 16
CAPACITY_FACTOR = 2
NORM_EPS = 1e-6
LAM_INIT = 0.8 - 0.6 * math.exp(-0.3 * 0)
LOG2E = math.log2(math.e)

LANES = 128
BF16_SUBLANES = 16
VMEM_LIMIT_BYTES = 56 * 2**20

SC_CORES = 2
SC_SUBCORES = 16
SC_LANES = 16
SC_GATHER_ROWS = 64

TOKEN_TILE = 512
PROJ_TILE = 1024
RET_CHUNK = 256
Q_TILE = 256
KEY_CHUNK = 512
L_MIN = 2.0 ** -60
BIAS_BAND = Q_TILE + 2 * MAX_DISTANCE
SCAT_WINDOW = 128
MOE_SEQS_PER_STEP = 4
ROUTE_SEQS_PER_STEP = 4
MOE_GROUPS = 2

W_RQ, W_RK, W_RV, W_RG = 0, 4, 8, 12
W_DQ1, W_DQ2, W_DK1, W_DK2, W_DV = 16, 18, 20, 22, 24
W_GR, W_GD = 28, 36

COL_RQ, COL_RK, COL_RV, COL_RG = 0, 4, 8, 12
COL_DQ, COL_DK = 16, 20
PROJ_COLS = 24 * LANES


def _cparams(sem):
    return pltpu.CompilerParams(dimension_semantics=sem, vmem_limit_bytes=VMEM_LIMIT_BYTES)


def _resident(shape, index_map):
    return pl.BlockSpec(shape, index_map, pipeline_mode=pl.Buffered(1))


def _ada_kernel(c_ref, w_ref, b_ref, o_ref):
    c = c_ref[...]
    a = c * jax.nn.sigmoid(c)
    o_ref[...] = jnp.dot(a, w_ref[...], preferred_element_type=F32,
                         precision=lax.Precision.HIGHEST) + b_ref[...]


def _ada(c, w, b):
    B, D = c.shape
    n = w.shape[1] // D
    return pl.pallas_call(
        _ada_kernel,
        grid=(n,),
        in_specs=[pl.BlockSpec((B, D), lambda j: (0, 0)),
                  pl.BlockSpec((D, D), lambda j: (0, j)),
                  pl.BlockSpec((1, D), lambda j: (0, j))],
        out_specs=pl.BlockSpec((B, D), lambda j: (0, j)),
        out_shape=jax.ShapeDtypeStruct((B, n * D), F32),
        compiler_params=_cparams(("arbitrary",)),
        name="ada",
    )(c, w, b.reshape(1, -1))


def _rms(x, g):
    ms = jnp.mean(x * x, axis=-1, keepdims=True)
    return x * lax.rsqrt(ms + NORM_EPS) * g


def _rot_pairs(x, cos, sin):
    even = (lax.broadcasted_iota(jnp.int32, x.shape, 1) & 1) == 0
    partner = jnp.where(even, pltpu.roll(x, LANES - 1, axis=1), pltpu.roll(x, 1, axis=1))
    return x * cos + partner * sin


def _pair_maps(a, b, second):
    low = lax.broadcasted_iota(jnp.int32, a.shape, 1) < DIFF_DH
    if second:
        return jnp.where(low, pltpu.roll(a, DIFF_DH, axis=1), b)
    return jnp.where(low, a, pltpu.roll(b, DIFF_DH, axis=1))


def _inproj_kernel(x_ref, g_ref, mod_ref, w_ref, rot_ref, o_ref, vt_ref, wvt_s):
    @pl.when((pl.program_id(0) == 0) & (pl.program_id(1) == 0))
    def _():
        wvt_s[...] = w_ref[:, W_DV * LANES:(W_DV + 4) * LANES].astype(F32).T.astype(BF16)

    parts = 4
    step = x_ref.shape[1] // parts

    def normed(p):
        x = x_ref[0, p * step:(p + 1) * step, :]
        h = _rms(x, g_ref[...]) * (1.0 + mod_ref[0, 1:2, :]) + mod_ref[0, 0:1, :]
        return h.astype(BF16)

    def project(p, hb):
        rows = slice(p * step, (p + 1) * step)

        def mm(col, width):
            return jnp.dot(hb, w_ref[:, col * LANES:(col + width) * LANES], preferred_element_type=F32)

        def put(col, val):
            o_ref[0, rows, col * LANES:col * LANES + val.shape[1]] = val.astype(BF16)

        for src, dst, t in ((W_RQ, COL_RQ, 0), (W_RK, COL_RK, 2)):
            cos, sin = rot_ref[t, rows, :], rot_ref[t + 1, rows, :]
            r = mm(src, RET_HEADS)
            for hh in range(RET_HEADS):
                put(dst + hh, _rot_pairs(r[:, hh * LANES:(hh + 1) * LANES], cos, sin))
        put(COL_RV, mm(W_RV, 4))
        r = mm(W_RG, 4)
        put(COL_RG, r * jax.nn.sigmoid(r))
        for src1, src2, dst, scale in ((W_DQ1, W_DQ2, COL_DQ, DIFF_DH ** -0.5 * LOG2E),
                                       (W_DK1, W_DK2, COL_DK, None)):
            m1, m2 = mm(src1, 2), mm(src2, 2)
            for hh in range(DIFF_HEADS):
                blk = slice((hh // 2) * LANES, (hh // 2 + 1) * LANES)
                val = _pair_maps(m1[:, blk], m2[:, blk], hh % 2 == 1)
                put(dst + hh, val if scale is None else val * scale)
        vt_ref[0, :, rows] = lax.dot_general(wvt_s[...], hb, (((1,), (1,)), ((), ())),
                                             preferred_element_type=F32).astype(BF16)

    hb = normed(0)
    for p in range(parts):
        nxt = normed(p + 1) if p + 1 < parts else None
        project(p, hb)
        hb = nxt


def _inproj(x, g, mod, w, rot):
    B, S, D = x.shape
    tm = PROJ_TILE
    vw = DIFF_HEADS * DIFF_DV
    return pl.pallas_call(
        _inproj_kernel,
        grid=(B, S // tm),
        in_specs=[pl.BlockSpec((1, tm, D), lambda b, i: (b, i, 0)),
                  _resident((1, D), lambda b, i: (0, 0)),
                  pl.BlockSpec((1, 6, D), lambda b, i: (b, 0, 0)),
                  _resident(w.shape, lambda b, i: (0, 0)),
                  pl.BlockSpec((4, tm, LANES), lambda b, i: (0, i, 0))],
        out_specs=[pl.BlockSpec((1, tm, PROJ_COLS), lambda b, i: (b, i, 0)),
                   pl.BlockSpec((1, vw, tm), lambda b, i: (b, 0, i))],
        out_shape=[jax.ShapeDtypeStruct((B, S, PROJ_COLS), BF16),
                   jax.ShapeDtypeStruct((B, vw, S), BF16)],
        scratch_shapes=[pltpu.VMEM((vw, D), BF16)],
        compiler_params=_cparams(("arbitrary", "arbitrary")),
        name="inproj",
    )(x, g, mod, w, rot)


def _ret_kernel(cd_ref, q_ref, k_ref, v_ref, rg_ref, dec_ref, dm_ref, gn_ref, o_ref, acc_ref):
    S = q_ref.shape[1]
    H = dm_ref.shape[0]
    C = RET_CHUNK
    nc = S // C
    nt = (((1,), (1,)), ((), ()))
    tn = (((0,), (0,)), ((), ()))

    def chunk(n, h):
        sl, lanes = pl.ds(n * C, C), slice(h * LANES, (h + 1) * LANES)
        return sl, q_ref[0, sl, lanes], k_ref[0, sl, lanes], v_ref[0, sl, lanes]

    def scaled(t, dec):
        return (t.astype(F32) * dec).astype(BF16)

    def forward(heads):
        state = {h: jnp.zeros((RET_DK, RET_DV), F32) for h in heads}
        for n in range(nc):
            for h in heads:
                sl, q, k, v = chunk(n, h)
                s = lax.dot_general(q, k, nt, preferred_element_type=F32) * dm_ref[h]
                inner = jnp.dot(s.astype(BF16), v, preferred_element_type=F32)
                cross = jnp.dot(scaled(q, dec_ref[h, 0]), state[h].astype(BF16), preferred_element_type=F32)
                acc_ref[h, sl, :] = inner + cross
                kv = lax.dot_general(scaled(k, dec_ref[h, 1]), v, tn, preferred_element_type=F32)
                state[h] = cd_ref[h, 0] * state[h] + kv

    def backward(heads):
        state = {h: jnp.zeros((RET_DK, RET_DV), F32) for h in heads}
        for n in reversed(range(nc)):
            for h in heads:
                sl, q, k, v = chunk(n, h)
                cross = jnp.dot(scaled(q, dec_ref[h, 2]), state[h].astype(BF16), preferred_element_type=F32)
                acc_ref[h, sl, :] = acc_ref[h, sl, :] + cross
                kv = lax.dot_general(scaled(k, dec_ref[h, 3]), v, tn, preferred_element_type=F32)
                state[h] = cd_ref[h, 1] * state[h] + kv

    def normalise(heads):
        for h in heads:
            lanes = slice(h * LANES, (h + 1) * LANES)
            y = acc_ref[h]
            mu = jnp.mean(y, axis=-1, keepdims=True)
            yc = y - mu
            var = jnp.mean(yc * yc, axis=-1, keepdims=True)
            yn = yc * lax.rsqrt(var + NORM_EPS) * gn_ref[:, lanes]
            o_ref[0, :, lanes] = (rg_ref[0, :, lanes].astype(F32) * yn).astype(BF16)

    first, second = tuple(range(H // 2)), tuple(range(H // 2, H))
    forward(first)
    backward(first)
    forward(second)
    normalise(first)
    backward(second)
    normalise(second)


def _ret(proj, cd, dec, dmat, gn):
    B, S, _ = proj.shape
    H = RET_HEADS
    C = RET_CHUNK

    def cols(base):
        return pl.BlockSpec((1, S, H * LANES), lambda b: (b, 0, base // H))

    return pl.pallas_call(
        _ret_kernel,
        grid=(B,),
        in_specs=[pl.BlockSpec(memory_space=pltpu.SMEM),
                  cols(COL_RQ), cols(COL_RK), cols(COL_RV), cols(COL_RG),
                  _resident((H, 4, C, LANES), lambda b: (0, 0, 0, 0)),
                  _resident((H, C, C), lambda b: (0, 0, 0)),
                  _resident((1, H * LANES), lambda b: (0, 0))],
        out_specs=pl.BlockSpec((1, S, H * LANES), lambda b: (b, 0, 0)),
        out_shape=jax.ShapeDtypeStruct((B, S, H * RET_DV), BF16),
        scratch_shapes=[pltpu.VMEM((H, S, RET_DV), F32)],
        compiler_params=_cparams(("arbitrary",)),
        name="ret",
    )(cd, proj, proj, proj, proj, dec, dmat, gn)


def _diff_kernel(bstat_ref, lam_ref, q_ref, k_ref, vt_ref, brow_ref, g_ref, o_ref, p_ref, t_ref):
    hh = pl.program_id(0)
    S = k_ref.shape[1]
    TQ, KB = Q_TILE, KEY_CHUNK
    nk = S // KB
    lv = lam_ref[...]
    lam = (jnp.exp(jnp.sum(lv[0:1] * lv[1:2], axis=1, keepdims=True))
           - jnp.exp(jnp.sum(lv[2:3] * lv[3:4], axis=1, keepdims=True)) + LAM_INIT)
    first_half = lax.broadcasted_iota(jnp.int32, (TQ, LANES), 1) < DIFF_DH
    nt = (((1,), (1,)), ((), ()))
    bmax = bstat_ref[hh, 0]

    band_lo = S - TQ - MAX_DISTANCE

    @pl.when(pl.program_id(1) == 0)
    def _():
        side = band_lo
        t_ref[0:side, :] = jnp.full((side, TQ), bstat_ref[hh, 1], F32)
        rows = jnp.broadcast_to(brow_ref[0], (BIAS_BAND, brow_ref.shape[2]))
        shift = brow_ref.shape[2] - (BIAS_BAND - 1 + MAX_DISTANCE)
        t_ref[side:side + BIAS_BAND, :] = pltpu.roll(rows, shift, 1, stride=1, stride_axis=0)[:, :TQ]
        t_ref[side + BIAS_BAND:, :] = jnp.full((side, TQ), bstat_ref[hh, 2], F32)
    kf = k_ref[0].astype(F32)
    kmax = jnp.sqrt(jnp.max(jnp.sum(kf * kf, axis=1, keepdims=True), axis=0, keepdims=True))
    sel_r = lax.broadcasted_iota(jnp.int32, (8, LANES), 0)
    sel_c = lax.broadcasted_iota(jnp.int32, (8, LANES), 1)
    sel = jnp.where((sel_c < DIFF_DH) == (sel_r == 0), 1.0, 0.0)
    sel = jnp.where(sel_r < 2, sel, 0.0).astype(BF16)

    def tile_rows(qb):
        return pl.ds(qb * TQ if isinstance(qb, int) else pl.multiple_of(qb * TQ, TQ), TQ)

    def masked_q(qb):
        q = q_ref[0, tile_rows(qb), :]
        zero = jnp.zeros_like(q)
        return q, (jnp.where(first_half, q, zero), jnp.where(first_half, zero, q))

    def finish(ot, qb):
        ms = jnp.mean(ot * ot, axis=0, keepdims=True)
        y = ot * lax.rsqrt(ms + NORM_EPS) * g_ref[...] * (1.0 - LAM_INIT)
        o_ref[0, tile_rows(qb), :] = y.T.astype(BF16)

    def exponentials(qb, slot):
        w0 = (S - TQ) - qb * TQ
        q, qz = masked_q(qb)
        qsq = (q.astype(F32) * q.astype(F32)).astype(BF16)
        n2 = lax.dot_general(sel, qsq, nt, preferred_element_type=F32) * 1.01
        m = [jnp.sqrt(n2[i:i + 1]) * kmax + bmax for i in range(2)]
        m_lo = [mi - bstat_ref[hh, 1] for mi in m]
        m_hi = [mi - bstat_ref[hh, 2] for mi in m]
        l = [jnp.zeros((8, TQ), F32) for _ in range(2)]
        for c in range(nk):
            rows = slice(c * KB, (c + 1) * KB)
            kb = k_ref[0, rows, :]
            t0 = w0 + c * KB
            below, above = t0 + KB <= band_lo, t0 >= band_lo + BIAS_BAND
            bias = None if below or above else t_ref[t0:t0 + KB, :]
            for i in range(2):
                s = lax.dot_general(kb, qz[i], nt, preferred_element_type=F32)
                if below:
                    p = jnp.exp2(s - m_lo[i])
                elif above:
                    p = jnp.exp2(s - m_hi[i])
                else:
                    p = jnp.exp2((s - m[i]) + bias)
                l[i] = l[i] + jnp.sum(p.reshape(KB // 8, 8, TQ), axis=0)
                p_ref[slot, i, rows, :] = p.astype(BF16)
        return [jnp.sum(l[i], axis=0, keepdims=True) for i in range(2)]

    def values(qb, slot, lr):
        ratio = jnp.broadcast_to(lam * lr[0] / lr[1], (16, TQ)).astype(BF16)
        ot = jnp.zeros((DIFF_DV, TQ), F32)
        for c in range(nk):
            rows = slice(c * KB, (c + 1) * KB)
            p1 = p_ref[slot, 0, rows, :].reshape(KB // 16, 16, TQ)
            p2 = p_ref[slot, 1, rows, :].reshape(KB // 16, 16, TQ)
            a = (p1 - ratio * p2).reshape(KB, TQ)
            ot = ot + jnp.dot(vt_ref[0, :, rows], a, preferred_element_type=F32)
        finish(ot * (1.0 / lr[0]), qb)

    n_tiles = S // TQ
    lmin = jnp.full((1, TQ), jnp.inf, F32)
    sums = exponentials(0, 0)
    for t in range(n_tiles):
        lmin = jnp.minimum(lmin, jnp.minimum(sums[0], sums[1]))
        nxt = exponentials(t + 1, (t + 1) % 2) if t + 1 < n_tiles else None
        values(t, t % 2, sums)
        sums = nxt

    @pl.when(jnp.min(lmin) < L_MIN)
    def _():
        def exact_tile(qb, carry):
            w0 = pl.multiple_of((S - TQ) - qb * TQ, TQ)
            _, qz = masked_q(qb)
            bias = t_ref[pl.ds(w0, S), :]
            attn = None
            for i, scale in ((0, 1.0), (1, lam)):
                s = lax.dot_general(k_ref[0], qz[i], nt, preferred_element_type=F32) + bias
                p = jnp.exp2(s - jnp.max(s, axis=0, keepdims=True))
                part = p * (scale / jnp.sum(p, axis=0, keepdims=True))
                attn = part if attn is None else attn - part
            finish(jnp.dot(vt_ref[0], attn.astype(BF16), preferred_element_type=F32), qb)
            return carry

        lax.fori_loop(0, S // TQ, exact_tile, 0)


def _diff(proj, vt, lam_vecs, brow, bstat, g):
    B, S, _ = proj.shape
    H = DIFF_HEADS

    def col(base):
        return pl.BlockSpec((1, S, LANES), lambda h, b: (b, 0, base + h))

    return pl.pallas_call(
        _diff_kernel,
        grid=(H, B),
        in_specs=[pl.BlockSpec(memory_space=pltpu.SMEM),
                  pl.BlockSpec((4, DIFF_DH), lambda h, b: (0, 0)),
                  col(COL_DQ), col(COL_DK),
                  pl.BlockSpec((1, DIFF_DV, S), lambda h, b: (b, h, 0)),
                  pl.BlockSpec((1, 1, brow.shape[2]), lambda h, b: (h, 0, 0)),
                  pl.BlockSpec((DIFF_DV, Q_TILE), lambda h, b: (0, 0))],
        out_specs=pl.BlockSpec((1, S, LANES), lambda h, b: (b, 0, h)),
        out_shape=jax.ShapeDtypeStruct((B, S, H * DIFF_DV), BF16),
        scratch_shapes=[pltpu.VMEM((2, 2, S, Q_TILE), BF16), pltpu.VMEM((2 * S - Q_TILE, Q_TILE), F32)],
        compiler_params=_cparams(("arbitrary", "arbitrary")),
        name="diff",
    )(bstat, lam_vecs, proj, proj, vt, brow, g)


def _pack_halves(x):
    w = x.shape[1] // 2
    lo = pltpu.bitcast(x[:, :w].astype(BF16).astype(F32), jnp.int32)
    hi = pltpu.bitcast(x[:, w:].astype(BF16).astype(F32), jnp.int32)
    return lax.shift_right_logical(lo, jnp.full_like(lo, 16)) | (hi & jnp.int32(-65536))


def _unpack_halves(words):
    lo = pltpu.bitcast(lax.shift_left(words, jnp.full_like(words, 16)), F32)
    hi = pltpu.bitcast(words & jnp.int32(-65536), F32)
    return jnp.concatenate([lo, hi], axis=1).astype(BF16)


def _merge_kernel(x_ref, yr_ref, yd_ref, mod_ref, gmix_ref, wg_ref, wr_ref, wd_ref, wo_ref,
                  g_ref, wrt_ref, x1_ref, h2_ref, lg_ref):
    parts = 4
    gate_w = wg_ref.shape[1] // 2
    step = x_ref.shape[1] // parts

    def project(r):
        rows = slice(r * step, (r + 1) * step)
        h = _rms(x_ref[0, rows, :], gmix_ref[...]) * (1.0 + mod_ref[0, 1:2, :]) + mod_ref[0, 0:1, :]
        gates = jax.nn.sigmoid(jnp.dot(h.astype(BF16), wg_ref[...], preferred_element_type=F32))
        a = jnp.dot(yr_ref[0, rows, :], wr_ref[...], preferred_element_type=F32)
        d = jnp.dot(yd_ref[0, rows, :], wd_ref[...], preferred_element_type=F32)
        merged = gates[:, :gate_w] * a + gates[:, gate_w:] * d
        return jnp.dot(merged.astype(BF16), wo_ref[...], preferred_element_type=F32)

    def epilogue(r, o):
        rows = slice(r * step, (r + 1) * step)
        x1 = x_ref[0, rows, :] + mod_ref[0, 2:3, :] * o
        x1_ref[0, rows, :] = x1
        h2 = _rms(x1, g_ref[...]) * (1.0 + mod_ref[0, 4:5, :]) + mod_ref[0, 3:4, :]
        h2_ref[0, rows, :] = _pack_halves(h2)
        lg_ref[0, :, rows] = lax.dot_general(wrt_ref[...], h2.astype(BF16), (((1,), (1,)), ((), ())),
                                             preferred_element_type=F32)

    o = project(0)
    for r in range(parts):
        nxt = project(r + 1) if r + 1 < parts else None
        epilogue(r, o)
        o = nxt


def _merge(x, yr, yd, mod, gmix, wg, wr, wd, wo, g, wrt):
    B, S, D = x.shape
    tm = PROJ_TILE
    E = wrt.shape[0]
    return pl.pallas_call(
        _merge_kernel,
        grid=(B, S // tm),
        in_specs=[pl.BlockSpec((1, tm, D), lambda b, i: (b, i, 0)),
                  pl.BlockSpec((1, tm, yr.shape[2]), lambda b, i: (b, i, 0)),
                  pl.BlockSpec((1, tm, yd.shape[2]), lambda b, i: (b, i, 0)),
                  pl.BlockSpec((1, 6, D), lambda b, i: (b, 0, 0)),
                  _resident((1, D), lambda b, i: (0, 0)),
                  _resident(wg.shape, lambda b, i: (0, 0)),
                  _resident(wr.shape, lambda b, i: (0, 0)),
                  _resident(wd.shape, lambda b, i: (0, 0)),
                  _resident(wo.shape, lambda b, i: (0, 0)),
                  _resident((1, D), lambda b, i: (0, 0)),
                  _resident(wrt.shape, lambda b, i: (0, 0))],
        out_specs=[pl.BlockSpec((1, tm, D), lambda b, i: (b, i, 0)),
                   pl.BlockSpec((1, tm, D // 2), lambda b, i: (b, i, 0)),
                   pl.BlockSpec((1, E, tm), lambda b, i: (b, 0, i))],
        out_shape=[jax.ShapeDtypeStruct((B, S, D), F32),
                   jax.ShapeDtypeStruct((B, S, D // 2), jnp.int32),
                   jax.ShapeDtypeStruct((B, E, S), F32)],
        compiler_params=_cparams(("arbitrary", "arbitrary")),
        name="merge",
    )(x, yr, yd, mod, gmix, wg, wr, wd, wo, g, wrt)


def _lane_prefix(m, tri):
    E, S = m.shape
    off = jnp.zeros((E, 1), F32)
    parts = []
    for j in range(S // LANES):
        blk = m[:, j * LANES:(j + 1) * LANES]
        parts.append(jnp.dot(blk.astype(BF16), tri, preferred_element_type=F32) + off)
        off = off + jnp.sum(blk, axis=1, keepdims=True)
    return jnp.concatenate(parts, axis=1)


def _route_kernel(lg_ref, rank_ref, gate_ref, rank_t_ref, *, cap):
    E, S = lg_ref.shape[1], lg_ref.shape[2]
    for j in range(lg_ref.shape[0]):
        rank, gate_ref[j] = _route_one(lg_ref[j], cap)
        rank_ref[j] = rank.astype(jnp.int32)
        padded = jnp.concatenate([rank, jnp.full((LANES - E, S), -1.0, F32)], axis=0)
        rank_t_ref[j] = padded.T.astype(jnp.int32)


def _route_one(lg, cap):
    e = jnp.exp(lg - jnp.max(lg, axis=0, keepdims=True))
    aff = e / jnp.sum(e, axis=0, keepdims=True)
    bits = pltpu.bitcast(aff, jnp.int32)
    E = lg.shape[0]

    def count(mask):
        return jnp.sum(jnp.where(mask, 1.0, 0.0), axis=1, keepdims=True)

    def reaches(cand):
        return count(bits >= cand) >= cap

    thr = jnp.zeros((E, 1), jnp.int32)
    for hi in range(30, 0, -2):
        c1, c2 = thr | (1 << hi), thr | (1 << (hi - 1))
        c3 = c1 | (1 << (hi - 1))
        thr = jnp.where(reaches(c3), c3, jnp.where(reaches(c1), c1, jnp.where(reaches(c2), c2, thr)))
    thr = jnp.where(reaches(thr | 1), thr | 1, thr)
    gt = bits > thr
    eq = bits == thr
    need = cap - count(gt)
    r = lax.broadcasted_iota(jnp.int32, (LANES, LANES), 0)
    c = lax.broadcasted_iota(jnp.int32, (LANES, LANES), 1)
    tri = jnp.where(r < c, 1.0, 0.0).astype(BF16)
    eq_before = _lane_prefix(jnp.where(eq, 1.0, 0.0), tri)
    sel = gt | (eq & (eq_before < need))
    slot = _lane_prefix(jnp.where(sel, 1.0, 0.0), tri)
    return jnp.where(sel, slot, -1.0), jnp.where(sel, aff, 0.0)


def _route(logits, cap):
    B, E, S = logits.shape
    G = ROUTE_SEQS_PER_STEP
    assert B % G == 0
    spec = pl.BlockSpec((G, E, S), lambda b: (b, 0, 0))
    return pl.pallas_call(
        functools.partial(_route_kernel, cap=cap),
        grid=(B // G,),
        in_specs=[spec],
        out_specs=[spec, spec, pl.BlockSpec((G, S, LANES), lambda b: (b, 0, 0))],
        out_shape=[jax.ShapeDtypeStruct((B, E, S), jnp.int32),
                   jax.ShapeDtypeStruct((B, E, S), F32),
                   jax.ShapeDtypeStruct((B, S, LANES), jnp.int32)],
        compiler_params=_cparams(("arbitrary",)),
        name="route",
    )(logits)


def _gather_rows(table, rank, cap):
    B, E, S = rank.shape
    W = table.shape[1]
    workers = SC_CORES * SC_SUBCORES
    pairs = E * B
    per = pairs // workers
    assert per * workers == pairs and B & (B - 1) == 0 and cap % SC_GATHER_ROWS == 0 and S % SC_LANES == 0
    shift = B.bit_length() - 1
    mesh = plsc.VectorSubcoreMesh(core_axis_name="c", subcore_axis_name="s")

    R = SC_GATHER_ROWS
    n_chunks = cap // R

    def body(table_hbm, rank_hbm, out_hbm, rank_v, idx_v, buf_v, gsem, wsem):
        wid = lax.axis_index("s") * SC_CORES + lax.axis_index("c")

        def fetch(c):
            return pltpu.make_async_copy(table_hbm.at[idx_v.at[pl.ds(c * R, R)]], buf_v.at[c % 2], gsem.at[c % 2])

        def flush(p, c):
            return pltpu.make_async_copy(buf_v.at[c % 2], out_hbm.at[pl.ds(p * cap + c * R, R)], wsem.at[c % 2])

        for j in range(per):
            p = wid * per + j
            e = lax.shift_right_logical(p, shift)
            b = p & (B - 1)
            pltpu.sync_copy(rank_hbm.at[b, e], rank_v)
            base = b * S

            @pl.loop(0, S, step=SC_LANES)
            def _(t0):
                r = rank_v[pl.ds(t0, SC_LANES)]
                tok = lax.iota(jnp.int32, SC_LANES) + (base + t0)
                plsc.store_scatter(idx_v, [r], tok, mask=r >= 0)

            fetch(0).start()
            for c in range(n_chunks):
                fetch(c).wait()
                if c >= 1:
                    flush(p, c - 1).wait()
                if c + 1 < n_chunks:
                    fetch(c + 1).start()
                flush(p, c).start()
            flush(p, n_chunks - 1).wait()

    return pl.kernel(
        body,
        out_type=jax.ShapeDtypeStruct((pairs * cap, W), jnp.int32),
        mesh=mesh,
        scratch_types=[pltpu.VMEM((S,), jnp.int32), pltpu.VMEM((cap,), jnp.int32),
                       pltpu.VMEM((2, R, W), jnp.int32),
                       pltpu.SemaphoreType.DMA((2,)), pltpu.SemaphoreType.DMA((2,))],
        compiler_params=pltpu.CompilerParams(needs_layout_passes=False),
    )(table, rank)


def _moe_kernel(rank_ref, gate_ref, x_ref, wg_ref, wu_ref, wd_ref, o_ref, wg_s, wu_s, wd_s, *, cap):
    r, b = pl.program_id(0), pl.program_id(1)
    n_exp = pl.num_programs(0) - 1

    @pl.when(r < n_exp)
    def _():
        slot = r % 2
        rows_in, rows_ff = wg_ref.shape[1], wd_ref.shape[1]
        wg_s[slot, pl.ds(pl.multiple_of(b * rows_in, rows_in), rows_in), :] = wg_ref[0].astype(BF16)
        wu_s[slot, pl.ds(pl.multiple_of(b * rows_in, rows_in), rows_in), :] = wu_ref[0].astype(BF16)
        wd_s[slot, pl.ds(pl.multiple_of(b * rows_ff, rows_ff), rows_ff), :] = wd_ref[0].astype(BF16)

    @pl.when(r == 0)
    def _():
        o_ref[...] = jnp.zeros_like(o_ref)

    @pl.when(r > 0)
    def _():
        slot = (r - 1) % 2
        S = rank_ref.shape[3]
        for j in range(x_ref.shape[0]):
            rank = rank_ref[j, 0]
            pick = lax.broadcasted_iota(jnp.int32, (cap, S), 0) == rank
            xin = _unpack_halves(x_ref[j])
            a = jnp.dot(xin, wg_s[slot], preferred_element_type=F32)
            u = jnp.dot(xin, wu_s[slot], preferred_element_type=F32)
            act = (a * jax.nn.sigmoid(a) * u).astype(BF16)
            y = jnp.dot(act, wd_s[slot], preferred_element_type=F32)
            g = jnp.sum(jnp.where(pick, gate_ref[j, 0], 0.0), axis=1, keepdims=True)
            o_ref[0, j] = (y * g).astype(BF16)


def _moe(rank, gate, xin, wg, wu, wd, cap, e0, n_e):
    B, E, S = rank.shape
    D, Fd = wg.shape[1], wg.shape[2]
    G = MOE_SEQS_PER_STEP
    steps = B // G
    assert steps * G == B and D % steps == 0 and Fd % steps == 0

    def cur(r):
        return e0 + jnp.maximum(r - 1, 0)

    def nxt(r):
        return e0 + jnp.minimum(r, n_e - 1)

    row = pl.BlockSpec((G, 1, 1, S), lambda r, b: (b, cur(r), 0, 0))
    return pl.pallas_call(
        functools.partial(_moe_kernel, cap=cap),
        grid=(n_e + 1, steps),
        in_specs=[row, row,
                  pl.BlockSpec((G, cap, D // 2), lambda r, b: (jnp.maximum(r - 1, 0) * steps + b, 0, 0)),
                  pl.BlockSpec((1, D // steps, Fd), lambda r, b: (nxt(r), b, 0)),
                  pl.BlockSpec((1, D // steps, Fd), lambda r, b: (nxt(r), b, 0)),
                  pl.BlockSpec((1, Fd // steps, D), lambda r, b: (nxt(r), b, 0))],
        out_specs=pl.BlockSpec((1, G, cap, D), lambda r, b: (jnp.where(r == 0, n_e, r - 1), b, 0, 0)),
        out_shape=jax.ShapeDtypeStruct((n_e + 1, B, cap, D), BF16),
        scratch_shapes=[pltpu.VMEM((2, D, Fd), BF16), pltpu.VMEM((2, D, Fd), BF16),
                        pltpu.VMEM((2, Fd, D), BF16)],
        compiler_params=_cparams(("arbitrary", "arbitrary")),
        name="moe",
    )(rank.reshape(B, E, 1, S), gate.reshape(B, E, 1, S), xin.reshape(n_e * B, cap, D // 2), wg, wu, wd)


def _scat_kernel(win_ref, ok_ref, rt_ref, *refs, cap):
    *y_refs, x1_ref, mod_ref, g_ref, o_ref = refs
    b, i = pl.program_id(0), pl.program_id(1)
    rt = rt_ref[0]
    tm, E = rt.shape[0], win_ref.shape[2]
    D = x1_ref.shape[2]
    per = E // len(y_refs)

    def finish(moe):
        x2 = x1_ref[0] + mod_ref[0, 5:6, :] * moe
        o_ref[0] = _rms(x2, g_ref[...])

    @pl.when(ok_ref[b, i] != 0)
    def _():
        slot = lax.broadcasted_iota(jnp.int32, (tm, SCAT_WINDOW), 1)
        moe = jnp.zeros((tm, D), F32)
        for e in range(0, E, 2):
            starts = [pl.multiple_of(win_ref[b, i, e + k], BF16_SUBLANES) for k in range(2)]
            place = jnp.concatenate(
                [jnp.where(rt[:, e + k:e + k + 1] - starts[k] == slot, 1.0, 0.0).astype(BF16)
                 for k in range(2)], axis=1)
            y = jnp.concatenate([y_refs[(e + k) // per][(e + k) % per, 0, pl.ds(starts[k], SCAT_WINDOW), :]
                                 for k in range(2)], axis=0)
            moe = moe + jnp.dot(place, y, preferred_element_type=F32)
        finish(moe)

    @pl.when(ok_ref[b, i] == 0)
    def _():
        slot = lax.broadcasted_iota(jnp.int32, (tm, cap), 1)
        place = jnp.concatenate(
            [jnp.where(rt[:, e:e + 1] == slot, 1.0, 0.0).astype(BF16) for e in range(E)], axis=1)
        y = jnp.concatenate([y_ref[0:per, 0].reshape(per * cap, D) for y_ref in y_refs], axis=0)
        finish(jnp.dot(place, y, preferred_element_type=F32))


def _scat_windows(rank, cap, tm):
    B, E, S = rank.shape
    counts = jnp.sum((rank >= 0).reshape(B, E, S // tm, tm), axis=-1, dtype=jnp.int32)
    ends = jnp.cumsum(counts, axis=-1)
    start = ends - counts
    win = jnp.minimum(start // BF16_SUBLANES * BF16_SUBLANES, cap - SCAT_WINDOW)
    ok = jnp.all(ends - win <= SCAT_WINDOW, axis=1)
    return jnp.swapaxes(win, 1, 2), ok.astype(jnp.int32)


def _scat(rank, rank_t, ybufs, x1, mod, g, cap):
    B, S, D = x1.shape
    E = rank.shape[1]
    tm = TOKEN_TILE
    win, ok = _scat_windows(rank, cap, tm)
    smem = pl.BlockSpec(memory_space=pltpu.SMEM)
    return pl.pallas_call(
        functools.partial(_scat_kernel, cap=cap),
        grid=(B, S // tm),
        in_specs=[smem, smem,
                  pl.BlockSpec((1, tm, LANES), lambda b, i: (b, i, 0)),
                  *[pl.BlockSpec(y.shape[:1] + (1, cap, D), lambda b, i: (0, b, 0, 0)) for y in ybufs],
                  pl.BlockSpec((1, tm, D), lambda b, i: (b, i, 0)),
                  pl.BlockSpec((1, 6, D), lambda b, i: (b, 0, 0)),
                  _resident((1, D), lambda b, i: (0, 0))],
        out_specs=pl.BlockSpec((1, tm, D), lambda b, i: (b, i, 0)),
        out_shape=jax.ShapeDtypeStruct((B, S, D), F32),
        compiler_params=_cparams(("arbitrary", "arbitrary")),
        name="scat",
    )(win, ok, rank_t, *ybufs, x1, mod, g)


def _rot_tables(S):
    half = RET_DK // 2
    inv = 1.0 / (RET_THETA_BASE ** np.linspace(0.0, 1.0, half))
    ang = np.arange(S, dtype=np.float64)[:, None] * inv[None, :]
    cos = np.repeat(np.cos(ang), 2, axis=1)
    sin = np.stack([-np.sin(ang), np.sin(ang)], axis=2).reshape(S, RET_DK)
    sc = RET_DK ** -0.5
    return jnp.asarray(np.stack([cos, sin, cos * sc, sin * sc]), F32)


def _ret_tables():
    C = RET_CHUNK
    heads = np.arange(RET_HEADS, dtype=np.float64)
    lgf = np.log1p(-np.exp2(-RET_FWD_DECAY_OFFSET - heads))[:, None]
    lgb = np.log1p(-np.exp2(-RET_BWD_DECAY_OFFSET - heads))[:, None]
    idx = np.arange(C, dtype=np.float64)
    diff = idx[:, None] - idx[None, :]
    dmat = np.where(diff >= 0,
                    np.exp(np.maximum(diff, 0.0)[None] * lgf[:, :, None]),
                    np.exp(np.maximum(-diff, 0.0)[None] * lgb[:, :, None]))
    dec = np.stack([np.exp((idx + 1)[None, :] * lgf),
                    np.exp((C - 1 - idx)[None, :] * lgf),
                    np.exp((C - idx)[None, :] * lgb),
                    np.exp(idx[None, :] * lgb)], axis=1)
    dec = np.broadcast_to(dec[..., None], dec.shape + (LANES,))
    cd = np.concatenate([np.exp(C * lgf), np.exp(C * lgb)], axis=1)
    return jnp.asarray(cd, F32), jnp.asarray(dec, F32), jnp.asarray(dmat, F32)


def _t5_bucket(rel):
    nb = N_BUCKETS // 2
    max_exact = nb // 2
    ret = (rel > 0).astype(jnp.int32) * nb
    n = jnp.abs(rel)
    large = max_exact + (jnp.log(jnp.maximum(n, 1).astype(F32) / max_exact)
                         / math.log(MAX_DISTANCE / max_exact) * (nb - max_exact)).astype(jnp.int32)
    large = jnp.minimum(large, nb - 1)
    return ret + jnp.where(n < max_exact, n, large)


def _bias_rows(rel_bias):
    reach = BIAS_BAND - 1
    rel = jnp.concatenate([jnp.arange(reach, -reach - 1, -1, dtype=jnp.int32),
                           jnp.array([-(2 ** 20), 2 ** 20], jnp.int32)])
    f = rel_bias[_t5_bucket(rel)].astype(F32).T * LOG2E
    rows = jnp.pad(f[:, :-2], ((0, 0), (0, 2 * BIAS_BAND - (2 * reach + 1))))
    stats = jnp.stack([jnp.max(f, axis=1), f[:, -2], f[:, -1]], axis=1)
    return rows[:, None, :], stats


def kernel(x, c, w_ada, b_ada, norm_mix_g, w_in, ret_gn_g, diff_subln_g, lambda_q1, lambda_k1, lambda_q2, lambda_k2, w_ret_out, w_diff_out, w_o, rel_bias, norm_ffn_g, w_router, w_exp_gate, w_exp_up, w_exp_down, final_g):
    B, S, D = x.shape
    cap = CAPACITY_FACTOR * S // N_EXPERTS
    l = 0

    mod = _ada(c, w_ada[l], b_ada[l]).reshape(B, 6, D)
    w_in_b = w_in[l].astype(BF16)
    proj, vt = _inproj(x, norm_mix_g[l].reshape(1, D), mod, w_in_b, _rot_tables(S))

    cd, dec, dmat = _ret_tables()
    y_ret = _ret(proj, cd, dec, dmat, ret_gn_g[l].reshape(1, -1))

    lam_vecs = jnp.stack([lambda_q1[l], lambda_k1[l], lambda_q2[l], lambda_k2[l]]).astype(F32)
    brow, bstat = _bias_rows(rel_bias)
    subln = jnp.broadcast_to(diff_subln_g[l].astype(F32)[:, None], (DIFF_DV, Q_TILE))
    y_diff = _diff(proj, vt, lam_vecs, brow, bstat, subln)

    x1, h2, logits = _merge(x, y_ret, y_diff, mod, norm_mix_g[l].reshape(1, D), w_in_b[:, W_GR * LANES:],
                            w_ret_out[l].astype(BF16), w_diff_out[l].astype(BF16), w_o[l].astype(BF16),
                            norm_ffn_g[l].reshape(1, D), w_router[l].T.astype(BF16))

    rank, gate, rank_t = _route(logits, cap)
    table, ybufs = h2.reshape(B * S, D // 2), []
    n_e = N_EXPERTS // MOE_GROUPS
    for k in range(MOE_GROUPS):
        xin = _gather_rows(table, rank[:, k * n_e:(k + 1) * n_e], cap)
        ybufs.append(_moe(rank, gate, xin, w_exp_gate[l], w_exp_up[l], w_exp_down[l], cap, k * n_e, n_e))
    return _scat(rank, rank_t, ybufs, x1, mod, final_g.reshape(1, D), cap)
```

```python
import functools
import math

import numpy as np
import jax
import jax.numpy as jnp
from jax import lax
from jax.experimental import pallas as pl
from jax.experimental.pallas import tpu as pltpu
from jax.experimental.pallas import tpu_sc as plsc

F32 = jnp.float32
BF16 = jnp.bfloat16

RET_HEADS = 4
RET_DK = 128
RET_DV = 128
RET_FWD_DECAY_OFFSET = 5.0
RET_BWD_DECAY_OFFSET = 5.5
RET_THETA_BASE = 10000.0
DIFF_HEADS = 4
DIFF_DH = 64
DIFF_DV = 2 * DIFF_DH
N_BUCKETS = 32
MAX_DISTANCE = 128
N_EXPERTS = 16
CAPACITY_FACTOR = 2
NORM_EPS = 1e-6
LAM_INIT = 0.8 - 0.6 * math.exp(-0.3 * 0)
LOG2E = math.log2(math.e)

LANES = 128
BF16_SUBLANES = 16
VMEM_LIMIT_BYTES = 56 * 2**20

SC_CORES = 2
SC_SUBCORES = 16
SC_LANES = 16
SC_GATHER_ROWS = 64

TOKEN_TILE = 512
PROJ_TILE = 1024
RET_CHUNK = 256
Q_TILE = 256
KEY_CHUNK = 512
L_MIN = 2.0 ** -60
BIAS_BAND = Q_TILE + 2 * MAX_DISTANCE
SCAT_WINDOW = 128
MOE_SEQS_PER_STEP = 4
ROUTE_SEQS_PER_STEP = 4
MOE_GROUPS = 2

W_RQ, W_RK, W_RV, W_RG = 0, 4, 8, 12
W_DQ1, W_DQ2, W_DK1, W_DK2, W_DV = 16, 18, 20, 22, 24
W_GR, W_GD = 28, 36

COL_RQ, COL_RK, COL_RV, COL_RG = 0, 4, 8, 12
COL_DQ, COL_DK = 16, 20
PROJ_COLS = 24 * LANES


def _cparams(sem):
    return pltpu.CompilerParams(dimension_semantics=sem, vmem_limit_bytes=VMEM_LIMIT_BYTES)


def _resident(shape, index_map):
    return pl.BlockSpec(shape, index_map, pipeline_mode=pl.Buffered(1))


def _ada_kernel(c_ref, w_ref, b_ref, o_ref):
    c = c_ref[...]
    a = c * jax.nn.sigmoid(c)
    o_ref[...] = jnp.dot(a, w_ref[...], preferred_element_type=F32,
                         precision=lax.Precision.HIGHEST) + b_ref[...]


def _ada(c, w, b):
    B, D = c.shape
    n = w.shape[1] // D
    return pl.pallas_call(
        _ada_kernel,
        grid=(n,),
        in_specs=[pl.BlockSpec((B, D), lambda j: (0, 0)),
                  pl.BlockSpec((D, D), lambda j: (0, j)),
                  pl.BlockSpec((1, D), lambda j: (0, j))],
        out_specs=pl.BlockSpec((B, D), lambda j: (0, j)),
        out_shape=jax.ShapeDtypeStruct((B, n * D), F32),
        compiler_params=_cparams(("arbitrary",)),
        name="ada",
    )(c, w, b.reshape(1, -1))


def _rms(x, g):
    ms = jnp.mean(x * x, axis=-1, keepdims=True)
    return x * lax.rsqrt(ms + NORM_EPS) * g


def _rot_pairs(x, cos, sin):
    even = (lax.broadcasted_iota(jnp.int32, x.shape, 1) & 1) == 0
    partner = jnp.where(even, pltpu.roll(x, LANES - 1, axis=1), pltpu.roll(x, 1, axis=1))
    return x * cos + partner * sin


def _pair_maps(a, b, second):
    low = lax.broadcasted_iota(jnp.int32, a.shape, 1) < DIFF_DH
    if second:
        return jnp.where(low, pltpu.roll(a, DIFF_DH, axis=1), b)
    return jnp.where(low, a, pltpu.roll(b, DIFF_DH, axis=1))


def _inproj_kernel(x_ref, g_ref, mod_ref, w_ref, rot_ref, o_ref, vt_ref, wvt_s):
    @pl.when((pl.program_id(0) == 0) & (pl.program_id(1) == 0))
    def _():
        wvt_s[...] = w_ref[:, W_DV * LANES:(W_DV + 4) * LANES].astype(F32).T.astype(BF16)

    parts = 4
    step = x_ref.shape[1] // parts

    def normed(p):
        x = x_ref[0, p * step:(p + 1) * step, :]
        h = _rms(x, g_ref[...]) * (1.0 + mod_ref[0, 1:2, :]) + mod_ref[0, 0:1, :]
        return h.astype(BF16)

    def project(p, hb):
        rows = slice(p * step, (p + 1) * step)

        def mm(col, width):
            return jnp.dot(hb, w_ref[:, col * LANES:(col + width) * LANES], preferred_element_type=F32)

        def put(col, val):
            o_ref[0, rows, col * LANES:col * LANES + val.shape[1]] = val.astype(BF16)

        for src, dst, t in ((W_RQ, COL_RQ, 0), (W_RK, COL_RK, 2)):
            cos, sin = rot_ref[t, rows, :], rot_ref[t + 1, rows, :]
            r = mm(src, RET_HEADS)
            for hh in range(RET_HEADS):
                put(dst + hh, _rot_pairs(r[:, hh * LANES:(hh + 1) * LANES], cos, sin))
        put(COL_RV, mm(W_RV, 4))
        r = mm(W_RG, 4)
        put(COL_RG, r * jax.nn.sigmoid(r))
        for src1, src2, dst, scale in ((W_DQ1, W_DQ2, COL_DQ, DIFF_DH ** -0.5 * LOG2E),
                                       (W_DK1, W_DK2, COL_DK, None)):
            m1, m2 = mm(src1, 2), mm(src2, 2)
            for hh in range(DIFF_HEADS):
                blk = slice((hh // 2) * LANES, (hh // 2 + 1) * LANES)
                val = _pair_maps(m1[:, blk], m2[:, blk], hh % 2 == 1)
                put(dst + hh, val if scale is None else val * scale)
        vt_ref[0, :, rows] = lax.dot_general(wvt_s[...], hb, (((1,), (1,)), ((), ())),
                                             preferred_element_type=F32).astype(BF16)

    hb = normed(0)
    for p in range(parts):
        nxt = normed(p + 1) if p + 1 < parts else None
        project(p, hb)
        hb = nxt


def _inproj(x, g, mod, w, rot):
    B, S, D = x.shape
    tm = PROJ_TILE
    vw = DIFF_HEADS * DIFF_DV
    return pl.pallas_call(
        _inproj_kernel,
        grid=(B, S // tm),
        in_specs=[pl.BlockSpec((1, tm, D), lambda b, i: (b, i, 0)),
                  _resident((1, D), lambda b, i: (0, 0)),
                  pl.BlockSpec((1, 6, D), lambda b, i: (b, 0, 0)),
                  _resident(w.shape, lambda b, i: (0, 0)),
                  pl.BlockSpec((4, tm, LANES), lambda b, i: (0, i, 0))],
        out_specs=[pl.BlockSpec((1, tm, PROJ_COLS), lambda b, i: (b, i, 0)),
                   pl.BlockSpec((1, vw, tm), lambda b, i: (b, 0, i))],
        out_shape=[jax.ShapeDtypeStruct((B, S, PROJ_COLS), BF16),
                   jax.ShapeDtypeStruct((B, vw, S), BF16)],
        scratch_shapes=[pltpu.VMEM((vw, D), BF16)],
        compiler_params=_cparams(("arbitrary", "arbitrary")),
        name="inproj",
    )(x, g, mod, w, rot)


def _ret_kernel(cd_ref, q_ref, k_ref, v_ref, rg_ref, dec_ref, dm_ref, gn_ref, o_ref, acc_ref):
    S = q_ref.shape[1]
    H = dm_ref.shape[0]
    C = RET_CHUNK
    nc = S // C
    nt = (((1,), (1,)), ((), ()))
    tn = (((0,), (0,)), ((), ()))

    def chunk(n, h):
        sl, lanes = pl.ds(n * C, C), slice(h * LANES, (h + 1) * LANES)
        return sl, q_ref[0, sl, lanes], k_ref[0, sl, lanes], v_ref[0, sl, lanes]

    def scaled(t, dec):
        return (t.astype(F32) * dec).astype(BF16)

    def forward(heads):
        state = {h: jnp.zeros((RET_DK, RET_DV), F32) for h in heads}
        for n in range(nc):
            for h in heads:
                sl, q, k, v = chunk(n, h)
                s = lax.dot_general(q, k, nt, preferred_element_type=F32) * dm_ref[h]
                inner = jnp.dot(s.astype(BF16), v, preferred_element_type=F32)
                cross = jnp.dot(scaled(q, dec_ref[h, 0]), state[h].astype(BF16), preferred_element_type=F32)
                acc_ref[h, sl, :] = inner + cross
                kv = lax.dot_general(scaled(k, dec_ref[h, 1]), v, tn, preferred_element_type=F32)
                state[h] = cd_ref[h, 0] * state[h] + kv

    def backward(heads):
        state = {h: jnp.zeros((RET_DK, RET_DV), F32) for h in heads}
        for n in reversed(range(nc)):
            for h in heads:
                sl, q, k, v = chunk(n, h)
                cross = jnp.dot(scaled(q, dec_ref[h, 2]), state[h].astype(BF16), preferred_element_type=F32)
                acc_ref[h, sl, :] = acc_ref[h, sl, :] + cross
                kv = lax.dot_general(scaled(k, dec_ref[h, 3]), v, tn, preferred_element_type=F32)
                state[h] = cd_ref[h, 1] * state[h] + kv

    def normalise(heads):
        for h in heads:
            lanes = slice(h * LANES, (h + 1) * LANES)
            y = acc_ref[h]
            mu = jnp.mean(y, axis=-1, keepdims=True)
            yc = y - mu
            var = jnp.mean(yc * yc, axis=-1, keepdims=True)
            yn = yc * lax.rsqrt(var + NORM_EPS) * gn_ref[:, lanes]
            o_ref[0, :, lanes] = (rg_ref[0, :, lanes].astype(F32) * yn).astype(BF16)

    first, second = tuple(range(H // 2)), tuple(range(H // 2, H))
    forward(first)
    backward(first)
    forward(second)
    normalise(first)
    backward(second)
    normalise(second)


def _ret(proj, cd, dec, dmat, gn):
    B, S, _ = proj.shape
    H = RET_HEADS
    C = RET_CHUNK

    def cols(base):
        return pl.BlockSpec((1, S, H * LANES), lambda b: (b, 0, base // H))

    return pl.pallas_call(
        _ret_kernel,
        grid=(B,),
        in_specs=[pl.BlockSpec(memory_space=pltpu.SMEM),
                  cols(COL_RQ), cols(COL_RK), cols(COL_RV), cols(COL_RG),
                  _resident((H, 4, C, LANES), lambda b: (0, 0, 0, 0)),
                  _resident((H, C, C), lambda b: (0, 0, 0)),
                  _resident((1, H * LANES), lambda b: (0, 0))],
        out_specs=pl.BlockSpec((1, S, H * LANES), lambda b: (b, 0, 0)),
        out_shape=jax.ShapeDtypeStruct((B, S, H * RET_DV), BF16),
        scratch_shapes=[pltpu.VMEM((H, S, RET_DV), F32)],
        compiler_params=_cparams(("arbitrary",)),
        name="ret",
    )(cd, proj, proj, proj, proj, dec, dmat, gn)


def _diff_kernel(bstat_ref, lam_ref, q_ref, k_ref, vt_ref, brow_ref, g_ref, o_ref, p_ref, t_ref):
    hh = pl.program_id(0)
    S = k_ref.shape[1]
    TQ, KB = Q_TILE, KEY_CHUNK
    nk = S // KB
    lv = lam_ref[...]
    lam = (jnp.exp(jnp.sum(lv[0:1] * lv[1:2], axis=1, keepdims=True))
           - jnp.exp(jnp.sum(lv[2:3] * lv[3:4], axis=1, keepdims=True)) + LAM_INIT)
    first_half = lax.broadcasted_iota(jnp.int32, (TQ, LANES), 1) < DIFF_DH
    nt = (((1,), (1,)), ((), ()))
    bmax = bstat_ref[hh, 0]

    band_lo = S - TQ - MAX_DISTANCE

    @pl.when(pl.program_id(1) == 0)
    def _():
        side = band_lo
        t_ref[0:side, :] = jnp.full((side, TQ), bstat_ref[hh, 1], F32)
        rows = jnp.broadcast_to(brow_ref[0], (BIAS_BAND, brow_ref.shape[2]))
        shift = brow_ref.shape[2] - (BIAS_BAND - 1 + MAX_DISTANCE)
        t_ref[side:side + BIAS_BAND, :] = pltpu.roll(rows, shift, 1, stride=1, stride_axis=0)[:, :TQ]
        t_ref[side + BIAS_BAND:, :] = jnp.full((side, TQ), bstat_ref[hh, 2], F32)
    kf = k_ref[0].astype(F32)
    kmax = jnp.sqrt(jnp.max(jnp.sum(kf * kf, axis=1, keepdims=True), axis=0, keepdims=True))
    sel_r = lax.broadcasted_iota(jnp.int32, (8, LANES), 0)
    sel_c = lax.broadcasted_iota(jnp.int32, (8, LANES), 1)
    sel = jnp.where((sel_c < DIFF_DH) == (sel_r == 0), 1.0, 0.0)
    sel = jnp.where(sel_r < 2, sel, 0.0).astype(BF16)

    def tile_rows(qb):
        return pl.ds(qb * TQ if isinstance(qb, int) else pl.multiple_of(qb * TQ, TQ), TQ)

    def masked_q(qb):
        q = q_ref[0, tile_rows(qb), :]
        zero = jnp.zeros_like(q)
        return q, (jnp.where(first_half, q, zero), jnp.where(first_half, zero, q))

    def finish(ot, qb):
        ms = jnp.mean(ot * ot, axis=0, keepdims=True)
        y = ot * lax.rsqrt(ms + NORM_EPS) * g_ref[...] * (1.0 - LAM_INIT)
        o_ref[0, tile_rows(qb), :] = y.T.astype(BF16)

    def exponentials(qb, slot):
        w0 = (S - TQ) - qb * TQ
        q, qz = masked_q(qb)
        qsq = (q.astype(F32) * q.astype(F32)).astype(BF16)
        n2 = lax.dot_general(sel, qsq, nt, preferred_element_type=F32) * 1.01
        m = [jnp.sqrt(n2[i:i + 1]) * kmax + bmax for i in range(2)]
        m_lo = [mi - bstat_ref[hh, 1] for mi in m]
        m_hi = [mi - bstat_ref[hh, 2] for mi in m]
        l = [jnp.zeros((8, TQ), F32) for _ in range(2)]
        for c in range(nk):
            rows = slice(c * KB, (c + 1) * KB)
            kb = k_ref[0, rows, :]
            t0 = w0 + c * KB
            below, above = t0 + KB <= band_lo, t0 >= band_lo + BIAS_BAND
            bias = None if below or above else t_ref[t0:t0 + KB, :]
            for i in range(2):
                s = lax.dot_general(kb, qz[i], nt, preferred_element_type=F32)
                if below:
                    p = jnp.exp2(s - m_lo[i])
                elif above:
                    p = jnp.exp2(s - m_hi[i])
                else:
                    p = jnp.exp2((s - m[i]) + bias)
                l[i] = l[i] + jnp.sum(p.reshape(KB // 8, 8, TQ), axis=0)
                p_ref[slot, i, rows, :] = p.astype(BF16)
        return [jnp.sum(l[i], axis=0, keepdims=True) for i in range(2)]

    def values(qb, slot, lr):
        ratio = jnp.broadcast_to(lam * lr[0] / lr[1], (16, TQ)).astype(BF16)
        ot = jnp.zeros((DIFF_DV, TQ), F32)
        for c in range(nk):
            rows = slice(c * KB, (c + 1) * KB)
            p1 = p_ref[slot, 0, rows, :].reshape(KB // 16, 16, TQ)
            p2 = p_ref[slot, 1, rows, :].reshape(KB // 16, 16, TQ)
            a = (p1 - ratio * p2).reshape(KB, TQ)
            ot = ot + jnp.dot(vt_ref[0, :, rows], a, preferred_element_type=F32)
        finish(ot * (1.0 / lr[0]), qb)

    n_tiles = S // TQ
    lmin = jnp.full((1, TQ), jnp.inf, F32)
    sums = exponentials(0, 0)
    for t in range(n_tiles):
        lmin = jnp.minimum(lmin, jnp.minimum(sums[0], sums[1]))
        nxt = exponentials(t + 1, (t + 1) % 2) if t + 1 < n_tiles else None
        values(t, t % 2, sums)
        sums = nxt

    @pl.when(jnp.min(lmin) < L_MIN)
    def _():
        def exact_tile(qb, carry):
            w0 = pl.multiple_of((S - TQ) - qb * TQ, TQ)
            _, qz = masked_q(qb)
            bias = t_ref[pl.ds(w0, S), :]
            attn = None
            for i, scale in ((0, 1.0), (1, lam)):
                s = lax.dot_general(k_ref[0], qz[i], nt, preferred_element_type=F32) + bias
                p = jnp.exp2(s - jnp.max(s, axis=0, keepdims=True))
                part = p * (scale / jnp.sum(p, axis=0, keepdims=True))
                attn = part if attn is None else attn - part
            finish(jnp.dot(vt_ref[0], attn.astype(BF16), preferred_element_type=F32), qb)
            return carry

        lax.fori_loop(0, S // TQ, exact_tile, 0)


def _diff(proj, vt, lam_vecs, brow, bstat, g):
    B, S, _ = proj.shape
    H = DIFF_HEADS

    def col(base):
        return pl.BlockSpec((1, S, LANES), lambda h, b: (b, 0, base + h))

    return pl.pallas_call(
        _diff_kernel,
        grid=(H, B),
        in_specs=[pl.BlockSpec(memory_space=pltpu.SMEM),
                  pl.BlockSpec((4, DIFF_DH), lambda h, b: (0, 0)),
                  col(COL_DQ), col(COL_DK),
                  pl.BlockSpec((1, DIFF_DV, S), lambda h, b: (b, h, 0)),
                  pl.BlockSpec((1, 1, brow.shape[2]), lambda h, b: (h, 0, 0)),
                  pl.BlockSpec((DIFF_DV, Q_TILE), lambda h, b: (0, 0))],
        out_specs=pl.BlockSpec((1, S, LANES), lambda h, b: (b, 0, h)),
        out_shape=jax.ShapeDtypeStruct((B, S, H * DIFF_DV), BF16),
        scratch_shapes=[pltpu.VMEM((2, 2, S, Q_TILE), BF16), pltpu.VMEM((2 * S - Q_TILE, Q_TILE), F32)],
        compiler_params=_cparams(("arbitrary", "arbitrary")),
        name="diff",
    )(bstat, lam_vecs, proj, proj, vt, brow, g)


def _pack_halves(x):
    w = x.shape[1] // 2
    lo = pltpu.bitcast(x[:, :w].astype(BF16).astype(F32), jnp.int32)
    hi = pltpu.bitcast(x[:, w:].astype(BF16).astype(F32), jnp.int32)
    return lax.shift_right_logical(lo, jnp.full_like(lo, 16)) | (hi & jnp.int32(-65536))


def _unpack_halves(words):
    lo = pltpu.bitcast(lax.shift_left(words, jnp.full_like(words, 16)), F32)
    hi = pltpu.bitcast(words & jnp.int32(-65536), F32)
    return jnp.concatenate([lo, hi], axis=1).astype(BF16)


def _merge_kernel(x_ref, yr_ref, yd_ref, mod_ref, gmix_ref, wg_ref, wr_ref, wd_ref, wo_ref,
                  g_ref, wrt_ref, x1_ref, h2_ref, lg_ref):
    parts = 4
    gate_w = wg_ref.shape[1] // 2
    step = x_ref.shape[1] // parts

    def project(r):
        rows = slice(r * step, (r + 1) * step)
        h = _rms(x_ref[0, rows, :], gmix_ref[...]) * (1.0 + mod_ref[0, 1:2, :]) + mod_ref[0, 0:1, :]
        gates = jax.nn.sigmoid(jnp.dot(h.astype(BF16), wg_ref[...], preferred_element_type=F32))
        a = jnp.dot(yr_ref[0, rows, :], wr_ref[...], preferred_element_type=F32)
        d = jnp.dot(yd_ref[0, rows, :], wd_ref[...], preferred_element_type=F32)
        merged = gates[:, :gate_w] * a + gates[:, gate_w:] * d
        return jnp.dot(merged.astype(BF16), wo_ref[...], preferred_element_type=F32)

    def epilogue(r, o):
        rows = slice(r * step, (r + 1) * step)
        x1 = x_ref[0, rows, :] + mod_ref[0, 2:3, :] * o
        x1_ref[0, rows, :] = x1
        h2 = _rms(x1, g_ref[...]) * (1.0 + mod_ref[0, 4:5, :]) + mod_ref[0, 3:4, :]
        h2_ref[0, rows, :] = _pack_halves(h2)
        lg_ref[0, :, rows] = lax.dot_general(wrt_ref[...], h2.astype(BF16), (((1,), (1,)), ((), ())),
                                             preferred_element_type=F32)

    o = project(0)
    for r in range(parts):
        nxt = project(r + 1) if r + 1 < parts else None
        epilogue(r, o)
        o = nxt


def _merge(x, yr, yd, mod, gmix, wg, wr, wd, wo, g, wrt):
    B, S, D = x.shape
    tm = PROJ_TILE
    E = wrt.shape[0]
    return pl.pallas_call(
        _merge_kernel,
        grid=(B, S // tm),
        in_specs=[pl.BlockSpec((1, tm, D), lambda b, i: (b, i, 0)),
                  pl.BlockSpec((1, tm, yr.shape[2]), lambda b, i: (b, i, 0)),
                  pl.BlockSpec((1, tm, yd.shape[2]), lambda b, i: (b, i, 0)),
                  pl.BlockSpec((1, 6, D), lambda b, i: (b, 0, 0)),
                  _resident((1, D), lambda b, i: (0, 0)),
                  _resident(wg.shape, lambda b, i: (0, 0)),
                  _resident(wr.shape, lambda b, i: (0, 0)),
                  _resident(wd.shape, lambda b, i: (0, 0)),
                  _resident(wo.shape, lambda b, i: (0, 0)),
                  _resident((1, D), lambda b, i: (0, 0)),
                  _resident(wrt.shape, lambda b, i: (0, 0))],
        out_specs=[pl.BlockSpec((1, tm, D), lambda b, i: (b, i, 0)),
                   pl.BlockSpec((1, tm, D // 2), lambda b, i: (b, i, 0)),
                   pl.BlockSpec((1, E, tm), lambda b, i: (b, 0, i))],
        out_shape=[jax.ShapeDtypeStruct((B, S, D), F32),
                   jax.ShapeDtypeStruct((B, S, D // 2), jnp.int32),
                   jax.ShapeDtypeStruct((B, E, S), F32)],
        compiler_params=_cparams(("arbitrary", "arbitrary")),
        name="merge",
    )(x, yr, yd, mod, gmix, wg, wr, wd, wo, g, wrt)


def _lane_prefix(m, tri):
    E, S = m.shape
    off = jnp.zeros((E, 1), F32)
    parts = []
    for j in range(S // LANES):
        blk = m[:, j * LANES:(j + 1) * LANES]
        parts.append(jnp.dot(blk.astype(BF16), tri, preferred_element_type=F32) + off)
        off = off + jnp.sum(blk, axis=1, keepdims=True)
    return jnp.concatenate(parts, axis=1)


def _route_kernel(lg_ref, rank_ref, gate_ref, rank_t_ref, win_ref, ok_ref, *, cap):
    E, S = lg_ref.shape[1], lg_ref.shape[2]
    tiles = win_ref.shape[2]
    tm = S // tiles
    lane = lax.broadcasted_iota(jnp.int32, (E, LANES), 1)
    for j in range(lg_ref.shape[0]):
        rank, gate_ref[j], before = _route_one(lg_ref[j], cap)
        start = jnp.zeros((E, LANES), F32)
        end = jnp.full((E, LANES), float(cap), F32)
        for i in range(tiles):
            first = before[:, i * tm:i * tm + 1]
            start = jnp.where(lane == i, first, start)
            if i > 0:
                end = jnp.where(lane == i - 1, first, end)
        win = jnp.minimum(jnp.floor(start * (1.0 / BF16_SUBLANES)) * BF16_SUBLANES, float(cap - SCAT_WINDOW))
        fits = jnp.min(jnp.where(end - win <= SCAT_WINDOW, 1.0, 0.0), axis=0, keepdims=True)
        win_ref[j] = win[:, :tiles].astype(jnp.int32)
        ok_ref[j] = fits[:, :tiles].astype(jnp.int32)
        rank_ref[j] = rank.astype(jnp.int32)
        padded = jnp.concatenate([rank, jnp.full((LANES - E, S), -1.0, F32)], axis=0)
        rank_t_ref[j] = padded.T.astype(jnp.int32)


def _route_one(lg, cap):
    e = jnp.exp(lg - jnp.max(lg, axis=0, keepdims=True))
    aff = e / jnp.sum(e, axis=0, keepdims=True)
    bits = pltpu.bitcast(aff, jnp.int32)
    E = lg.shape[0]

    def count(mask):
        return jnp.sum(jnp.where(mask, 1.0, 0.0), axis=1, keepdims=True)

    def reaches(cand):
        return count(bits >= cand) >= cap

    thr = jnp.zeros((E, 1), jnp.int32)
    for hi in range(30, 0, -2):
        c1, c2 = thr | (1 << hi), thr | (1 << (hi - 1))
        c3 = c1 | (1 << (hi - 1))
        thr = jnp.where(reaches(c3), c3, jnp.where(reaches(c1), c1, jnp.where(reaches(c2), c2, thr)))
    thr = jnp.where(reaches(thr | 1), thr | 1, thr)
    gt = bits > thr
    eq = bits == thr
    need = cap - count(gt)
    r = lax.broadcasted_iota(jnp.int32, (LANES, LANES), 0)
    c = lax.broadcasted_iota(jnp.int32, (LANES, LANES), 1)
    tri = jnp.where(r < c, 1.0, 0.0).astype(BF16)
    eq_before = _lane_prefix(jnp.where(eq, 1.0, 0.0), tri)
    sel = gt | (eq & (eq_before < need))
    slot = _lane_prefix(jnp.where(sel, 1.0, 0.0), tri)
    return jnp.where(sel, slot, -1.0), jnp.where(sel, aff, 0.0), slot


def _route(logits, cap):
    B, E, S = logits.shape
    G = ROUTE_SEQS_PER_STEP
    tiles = S // TOKEN_TILE
    assert B % G == 0
    spec = pl.BlockSpec((G, E, S), lambda b: (b, 0, 0))
    return pl.pallas_call(
        functools.partial(_route_kernel, cap=cap),
        grid=(B // G,),
        in_specs=[spec],
        out_specs=[spec, spec, pl.BlockSpec((G, S, LANES), lambda b: (b, 0, 0)),
                   pl.BlockSpec((G, E, tiles), lambda b: (b, 0, 0)),
                   pl.BlockSpec((G, 1, tiles), lambda b: (b, 0, 0))],
        out_shape=[jax.ShapeDtypeStruct((B, E, S), jnp.int32),
                   jax.ShapeDtypeStruct((B, E, S), F32),
                   jax.ShapeDtypeStruct((B, S, LANES), jnp.int32),
                   jax.ShapeDtypeStruct((B, E, tiles), jnp.int32),
                   jax.ShapeDtypeStruct((B, 1, tiles), jnp.int32)],
        compiler_params=_cparams(("arbitrary",)),
        name="route",
    )(logits)


def _gather_rows(table, rank, cap):
    B, E, S = rank.shape
    W = table.shape[1]
    workers = SC_CORES * SC_SUBCORES
    pairs = E * B
    per = pairs // workers
    assert per * workers == pairs and B & (B - 1) == 0 and cap % SC_GATHER_ROWS == 0 and S % SC_LANES == 0
    shift = B.bit_length() - 1
    mesh = plsc.VectorSubcoreMesh(core_axis_name="c", subcore_axis_name="s")

    R = SC_GATHER_ROWS
    n_chunks = cap // R

    def body(table_hbm, rank_hbm, out_hbm, rank_v, idx_v, buf_v, gsem, wsem):
        wid = lax.axis_index("s") * SC_CORES + lax.axis_index("c")

        def fetch(c):
            return pltpu.make_async_copy(table_hbm.at[idx_v.at[pl.ds(c * R, R)]], buf_v.at[c % 2], gsem.at[c % 2])

        def flush(p, c):
            return pltpu.make_async_copy(buf_v.at[c % 2], out_hbm.at[pl.ds(p * cap + c * R, R)], wsem.at[c % 2])

        for j in range(per):
            p = wid * per + j
            e = lax.shift_right_logical(p, shift)
            b = p & (B - 1)
            pltpu.sync_copy(rank_hbm.at[b, e], rank_v)
            base = b * S

            @pl.loop(0, S, step=SC_LANES)
            def _(t0):
                r = rank_v[pl.ds(t0, SC_LANES)]
                tok = lax.iota(jnp.int32, SC_LANES) + (base + t0)
                plsc.store_scatter(idx_v, [r], tok, mask=r >= 0)

            fetch(0).start()
            for c in range(n_chunks):
                fetch(c).wait()
                if c >= 1:
                    flush(p, c - 1).wait()
                if c + 1 < n_chunks:
                    fetch(c + 1).start()
                flush(p, c).start()
            flush(p, n_chunks - 1).wait()

    return pl.kernel(
        body,
        out_type=jax.ShapeDtypeStruct((pairs * cap, W), jnp.int32),
        mesh=mesh,
        scratch_types=[pltpu.VMEM((S,), jnp.int32), pltpu.VMEM((cap,), jnp.int32),
                       pltpu.VMEM((2, R, W), jnp.int32),
                       pltpu.SemaphoreType.DMA((2,)), pltpu.SemaphoreType.DMA((2,))],
        compiler_params=pltpu.CompilerParams(needs_layout_passes=False),
    )(table, rank)


def _moe_kernel(rank_ref, gate_ref, x_ref, wg_ref, wu_ref, wd_ref, o_ref, wg_s, wu_s, wd_s, *, cap):
    r, b = pl.program_id(0), pl.program_id(1)
    n_exp = pl.num_programs(0) - 1

    @pl.when(r < n_exp)
    def _():
        slot = r % 2
        rows_in, rows_ff = wg_ref.shape[1], wd_ref.shape[1]
        wg_s[slot, pl.ds(pl.multiple_of(b * rows_in, rows_in), rows_in), :] = wg_ref[0].astype(BF16)
        wu_s[slot, pl.ds(pl.multiple_of(b * rows_in, rows_in), rows_in), :] = wu_ref[0].astype(BF16)
        wd_s[slot, pl.ds(pl.multiple_of(b * rows_ff, rows_ff), rows_ff), :] = wd_ref[0].astype(BF16)

    @pl.when(r == 0)
    def _():
        o_ref[...] = jnp.zeros_like(o_ref)

    @pl.when(r > 0)
    def _():
        slot = (r - 1) % 2
        S = rank_ref.shape[3]
        for j in range(x_ref.shape[0]):
            rank = rank_ref[j, 0]
            pick = lax.broadcasted_iota(jnp.int32, (cap, S), 0) == rank
            xin = _unpack_halves(x_ref[j])
            a = jnp.dot(xin, wg_s[slot], preferred_element_type=F32)
            u = jnp.dot(xin, wu_s[slot], preferred_element_type=F32)
            act = (a * jax.nn.sigmoid(a) * u).astype(BF16)
            y = jnp.dot(act, wd_s[slot], preferred_element_type=F32)
            g = jnp.sum(jnp.where(pick, gate_ref[j, 0], 0.0), axis=1, keepdims=True)
            o_ref[0, j] = (y * g).astype(BF16)


def _moe(rank, gate, xin, wg, wu, wd, cap, e0, n_e):
    B, E, S = rank.shape
    D, Fd = wg.shape[1], wg.shape[2]
    G = MOE_SEQS_PER_STEP
    steps = B // G
    assert steps * G == B and D % steps == 0 and Fd % steps == 0

    def cur(r):
        return e0 + jnp.maximum(r - 1, 0)

    def nxt(r):
        return e0 + jnp.minimum(r, n_e - 1)

    row = pl.BlockSpec((G, 1, 1, S), lambda r, b: (b, cur(r), 0, 0))
    return pl.pallas_call(
        functools.partial(_moe_kernel, cap=cap),
        grid=(n_e + 1, steps),
        in_specs=[row, row,
                  pl.BlockSpec((G, cap, D // 2), lambda r, b: (jnp.maximum(r - 1, 0) * steps + b, 0, 0)),
                  pl.BlockSpec((1, D // steps, Fd), lambda r, b: (nxt(r), b, 0)),
                  pl.BlockSpec((1, D // steps, Fd), lambda r, b: (nxt(r), b, 0)),
                  pl.BlockSpec((1, Fd // steps, D), lambda r, b: (nxt(r), b, 0))],
        out_specs=pl.BlockSpec((1, G, cap, D), lambda r, b: (jnp.where(r == 0, n_e, r - 1), b, 0, 0)),
        out_shape=jax.ShapeDtypeStruct((n_e + 1, B, cap, D), BF16),
        scratch_shapes=[pltpu.VMEM((2, D, Fd), BF16), pltpu.VMEM((2, D, Fd), BF16),
                        pltpu.VMEM((2, Fd, D), BF16)],
        compiler_params=_cparams(("arbitrary", "arbitrary")),
        name="moe",
    )(rank.reshape(B, E, 1, S), gate.reshape(B, E, 1, S), xin.reshape(n_e * B, cap, D // 2), wg, wu, wd)


def _scat_kernel(win_ref, ok_ref, rt_ref, *refs, cap):
    *y_refs, x1_ref, mod_ref, g_ref, o_ref = refs
    b, i = pl.program_id(0), pl.program_id(1)
    rt = rt_ref[0]
    tm, E = rt.shape[0], win_ref.shape[1]
    D = x1_ref.shape[2]
    per = E // len(y_refs)

    def finish(moe):
        x2 = x1_ref[0] + mod_ref[0, 5:6, :] * moe
        o_ref[0] = _rms(x2, g_ref[...])

    @pl.when(ok_ref[b, 0, i] != 0)
    def _():
        slot = lax.broadcasted_iota(jnp.int32, (tm, SCAT_WINDOW), 1)
        moe = jnp.zeros((tm, D), F32)
        for e in range(0, E, 2):
            starts = [pl.multiple_of(win_ref[b, e + k, i], BF16_SUBLANES) for k in range(2)]
            place = jnp.concatenate(
                [jnp.where(rt[:, e + k:e + k + 1] - starts[k] == slot, 1.0, 0.0).astype(BF16)
                 for k in range(2)], axis=1)
            y = jnp.concatenate([y_refs[(e + k) // per][(e + k) % per, 0, pl.ds(starts[k], SCAT_WINDOW), :]
                                 for k in range(2)], axis=0)
            moe = moe + jnp.dot(place, y, preferred_element_type=F32)
        finish(moe)

    @pl.when(ok_ref[b, 0, i] == 0)
    def _():
        slot = lax.broadcasted_iota(jnp.int32, (tm, cap), 1)
        place = jnp.concatenate(
            [jnp.where(rt[:, e:e + 1] == slot, 1.0, 0.0).astype(BF16) for e in range(E)], axis=1)
        y = jnp.concatenate([y_ref[0:per, 0].reshape(per * cap, D) for y_ref in y_refs], axis=0)
        finish(jnp.dot(place, y, preferred_element_type=F32))


def _scat(rank_t, win, ok, ybufs, x1, mod, g, cap):
    B, S, D = x1.shape
    tm = TOKEN_TILE
    smem = pl.BlockSpec(memory_space=pltpu.SMEM)
    return pl.pallas_call(
        functools.partial(_scat_kernel, cap=cap),
        grid=(B, S // tm),
        in_specs=[smem, smem,
                  pl.BlockSpec((1, tm, LANES), lambda b, i: (b, i, 0)),
                  *[pl.BlockSpec(y.shape[:1] + (1, cap, D), lambda b, i: (0, b, 0, 0)) for y in ybufs],
                  pl.BlockSpec((1, tm, D), lambda b, i: (b, i, 0)),
                  pl.BlockSpec((1, 6, D), lambda b, i: (b, 0, 0)),
                  _resident((1, D), lambda b, i: (0, 0))],
        out_specs=pl.BlockSpec((1, tm, D), lambda b, i: (b, i, 0)),
        out_shape=jax.ShapeDtypeStruct((B, S, D), F32),
        compiler_params=_cparams(("arbitrary", "arbitrary")),
        name="scat",
    )(win, ok, rank_t, *ybufs, x1, mod, g)


def _rot_tables(S):
    half = RET_DK // 2
    inv = 1.0 / (RET_THETA_BASE ** np.linspace(0.0, 1.0, half))
    ang = np.arange(S, dtype=np.float64)[:, None] * inv[None, :]
    cos = np.repeat(np.cos(ang), 2, axis=1)
    sin = np.stack([-np.sin(ang), np.sin(ang)], axis=2).reshape(S, RET_DK)
    sc = RET_DK ** -0.5
    return jnp.asarray(np.stack([cos, sin, cos * sc, sin * sc]), F32)


def _ret_tables():
    C = RET_CHUNK
    heads = np.arange(RET_HEADS, dtype=np.float64)
    lgf = np.log1p(-np.exp2(-RET_FWD_DECAY_OFFSET - heads))[:, None]
    lgb = np.log1p(-np.exp2(-RET_BWD_DECAY_OFFSET - heads))[:, None]
    idx = np.arange(C, dtype=np.float64)
    diff = idx[:, None] - idx[None, :]
    dmat = np.where(diff >= 0,
                    np.exp(np.maximum(diff, 0.0)[None] * lgf[:, :, None]),
                    np.exp(np.maximum(-diff, 0.0)[None] * lgb[:, :, None]))
    dec = np.stack([np.exp((idx + 1)[None, :] * lgf),
                    np.exp((C - 1 - idx)[None, :] * lgf),
                    np.exp((C - idx)[None, :] * lgb),
                    np.exp(idx[None, :] * lgb)], axis=1)
    dec = np.broadcast_to(dec[..., None], dec.shape + (LANES,))
    cd = np.concatenate([np.exp(C * lgf), np.exp(C * lgb)], axis=1)
    return jnp.asarray(cd, F32), jnp.asarray(dec, F32), jnp.asarray(dmat, F32)


def _t5_bucket(rel):
    nb = N_BUCKETS // 2
    max_exact = nb // 2
    ret = (rel > 0).astype(jnp.int32) * nb
    n = jnp.abs(rel)
    large = max_exact + (jnp.log(jnp.maximum(n, 1).astype(F32) / max_exact)
                         / math.log(MAX_DISTANCE / max_exact) * (nb - max_exact)).astype(jnp.int32)
    large = jnp.minimum(large, nb - 1)
    return ret + jnp.where(n < max_exact, n, large)


def _bias_rows(rel_bias):
    reach = BIAS_BAND - 1
    rel = jnp.concatenate([jnp.arange(reach, -reach - 1, -1, dtype=jnp.int32),
                           jnp.array([-(2 ** 20), 2 ** 20], jnp.int32)])
    f = rel_bias[_t5_bucket(rel)].astype(F32).T * LOG2E
    rows = jnp.pad(f[:, :-2], ((0, 0), (0, 2 * BIAS_BAND - (2 * reach + 1))))
    stats = jnp.stack([jnp.max(f, axis=1), f[:, -2], f[:, -1]], axis=1)
    return rows[:, None, :], stats


def kernel(x, c, w_ada, b_ada, norm_mix_g, w_in, ret_gn_g, diff_subln_g, lambda_q1, lambda_k1, lambda_q2, lambda_k2, w_ret_out, w_diff_out, w_o, rel_bias, norm_ffn_g, w_router, w_exp_gate, w_exp_up, w_exp_down, final_g):
    B, S, D = x.shape
    cap = CAPACITY_FACTOR * S // N_EXPERTS
    l = 0

    mod = _ada(c, w_ada[l], b_ada[l]).reshape(B, 6, D)
    w_in_b = w_in[l].astype(BF16)
    proj, vt = _inproj(x, norm_mix_g[l].reshape(1, D), mod, w_in_b, _rot_tables(S))

    cd, dec, dmat = _ret_tables()
    y_ret = _ret(proj, cd, dec, dmat, ret_gn_g[l].reshape(1, -1))

    lam_vecs = jnp.stack([lambda_q1[l], lambda_k1[l], lambda_q2[l], lambda_k2[l]]).astype(F32)
    brow, bstat = _bias_rows(rel_bias)
    subln = jnp.broadcast_to(diff_subln_g[l].astype(F32)[:, None], (DIFF_DV, Q_TILE))
    y_diff = _diff(proj, vt, lam_vecs, brow, bstat, subln)

    x1, h2, logits = _merge(x, y_ret, y_diff, mod, norm_mix_g[l].reshape(1, D), w_in_b[:, W_GR * LANES:],
                            w_ret_out[l].astype(BF16), w_diff_out[l].astype(BF16), w_o[l].astype(BF16),
                            norm_ffn_g[l].reshape(1, D), w_router[l].T.astype(BF16))

    rank, gate, rank_t, win, ok = _route(logits, cap)
    table, ybufs = h2.reshape(B * S, D // 2), []
    n_e = N_EXPERTS // MOE_GROUPS
    for k in range(MOE_GROUPS):
        xin = _gather_rows(table, rank[:, k * n_e:(k + 1) * n_e], cap)
        ybufs.append(_moe(rank, gate, xin, w_exp_gate[l], w_exp_up[l], w_exp_down[l], cap, k * n_e, n_e))
    return _scat(rank_t, win, ok, ybufs, x1, mod, final_g.reshape(1, D), cap)
```

```python
import functools
import math

import numpy as np
import jax
import jax.numpy as jnp
from jax import lax
from jax.experimental import pallas as pl
from jax.experimental.pallas import tpu as pltpu
from jax.experimental.pallas import tpu_sc as plsc

F32 = jnp.float32
BF16 = jnp.bfloat16

RET_HEADS = 4
RET_DK = 128
RET_DV = 128
RET_FWD_DECAY_OFFSET = 5.0
RET_BWD_DECAY_OFFSET = 5.5
RET_THETA_BASE = 10000.0
DIFF_HEADS = 4
DIFF_DH = 64
DIFF_DV = 2 * DIFF_DH
N_BUCKETS = 32
MAX_DISTANCE = 128
N_EXPERTS = 16
CAPACITY_FACTOR = 2
NORM_EPS = 1e-6
LAM_INIT = 0.8 - 0.6 * math.exp(-0.3 * 0)
LOG2E = math.log2(math.e)

LANES = 128
BF16_SUBLANES = 16
VMEM_LIMIT_BYTES = 56 * 2**20

SC_CORES = 2
SC_SUBCORES = 16
SC_LANES = 16
SC_GATHER_ROWS = 64

TOKEN_TILE = 512
PROJ_TILE = 1024
RET_CHUNK = 256
Q_TILE = 256
KEY_CHUNK = 512
L_MIN = 2.0 ** -60
BIAS_BAND = Q_TILE + 2 * MAX_DISTANCE
SCAT_WINDOW = 128
MOE_SEQS_PER_STEP = 4
ROUTE_SEQS_PER_STEP = 4
MOE_GROUP_SIZES = (2, 6, 8)

W_RQ, W_RK, W_RV, W_RG = 0, 4, 8, 12
W_DQ1, W_DQ2, W_DK1, W_DK2, W_DV = 16, 18, 20, 22, 24
W_GR, W_GD = 28, 36

COL_RQ, COL_RK, COL_RV, COL_RG = 0, 4, 8, 12
COL_DQ, COL_DK = 16, 20
PROJ_COLS = 24 * LANES


def _cparams(sem):
    return pltpu.CompilerParams(dimension_semantics=sem, vmem_limit_bytes=VMEM_LIMIT_BYTES)


def _resident(shape, index_map):
    return pl.BlockSpec(shape, index_map, pipeline_mode=pl.Buffered(1))


def _ada_kernel(c_ref, w_ref, b_ref, o_ref):
    c = c_ref[...]
    a = c * jax.nn.sigmoid(c)
    o_ref[...] = jnp.dot(a, w_ref[...], preferred_element_type=F32,
                         precision=lax.Precision.HIGHEST) + b_ref[...]


def _ada(c, w, b):
    B, D = c.shape
    n = w.shape[1] // D
    return pl.pallas_call(
        _ada_kernel,
        grid=(n,),
        in_specs=[pl.BlockSpec((B, D), lambda j: (0, 0)),
                  pl.BlockSpec((D, D), lambda j: (0, j)),
                  pl.BlockSpec((1, D), lambda j: (0, j))],
        out_specs=pl.BlockSpec((B, D), lambda j: (0, j)),
        out_shape=jax.ShapeDtypeStruct((B, n * D), F32),
        compiler_params=_cparams(("arbitrary",)),
        name="ada",
    )(c, w, b.reshape(1, -1))


def _rms(x, g):
    ms = jnp.mean(x * x, axis=-1, keepdims=True)
    return x * lax.rsqrt(ms + NORM_EPS) * g


def _rot_pairs(x, cos, sin):
    even = (lax.broadcasted_iota(jnp.int32, x.shape, 1) & 1) == 0
    partner = jnp.where(even, pltpu.roll(x, LANES - 1, axis=1), pltpu.roll(x, 1, axis=1))
    return x * cos + partner * sin


def _pair_maps(a, b, second):
    low = lax.broadcasted_iota(jnp.int32, a.shape, 1) < DIFF_DH
    if second:
        return jnp.where(low, pltpu.roll(a, DIFF_DH, axis=1), b)
    return jnp.where(low, a, pltpu.roll(b, DIFF_DH, axis=1))


def _inproj_kernel(x_ref, g_ref, mod_ref, w_ref, rot_ref, o_ref, vt_ref, wvt_s):
    @pl.when((pl.program_id(0) == 0) & (pl.program_id(1) == 0))
    def _():
        wvt_s[...] = w_ref[:, W_DV * LANES:(W_DV + 4) * LANES].astype(F32).T.astype(BF16)

    parts = 4
    step = x_ref.shape[1] // parts

    def normed(p):
        x = x_ref[0, p * step:(p + 1) * step, :]
        h = _rms(x, g_ref[...]) * (1.0 + mod_ref[0, 1:2, :]) + mod_ref[0, 0:1, :]
        return h.astype(BF16)

    def project(p, hb):
        rows = slice(p * step, (p + 1) * step)

        def mm(col, width):
            return jnp.dot(hb, w_ref[:, col * LANES:(col + width) * LANES], preferred_element_type=F32)

        def put(col, val):
            o_ref[0, rows, col * LANES:col * LANES + val.shape[1]] = val.astype(BF16)

        for src, dst, t in ((W_RQ, COL_RQ, 0), (W_RK, COL_RK, 2)):
            cos, sin = rot_ref[t, rows, :], rot_ref[t + 1, rows, :]
            r = mm(src, RET_HEADS)
            for hh in range(RET_HEADS):
                put(dst + hh, _rot_pairs(r[:, hh * LANES:(hh + 1) * LANES], cos, sin))
        put(COL_RV, mm(W_RV, 4))
        r = mm(W_RG, 4)
        put(COL_RG, r * jax.nn.sigmoid(r))
        for src1, src2, dst, scale in ((W_DQ1, W_DQ2, COL_DQ, DIFF_DH ** -0.5 * LOG2E),
                                       (W_DK1, W_DK2, COL_DK, None)):
            m1, m2 = mm(src1, 2), mm(src2, 2)
            for hh in range(DIFF_HEADS):
                blk = slice((hh // 2) * LANES, (hh // 2 + 1) * LANES)
                val = _pair_maps(m1[:, blk], m2[:, blk], hh % 2 == 1)
                put(dst + hh, val if scale is None else val * scale)
        vt_ref[0, :, rows] = lax.dot_general(wvt_s[...], hb, (((1,), (1,)), ((), ())),
                                             preferred_element_type=F32).astype(BF16)

    hb = normed(0)
    for p in range(parts):
        nxt = normed(p + 1) if p + 1 < parts else None
        project(p, hb)
        hb = nxt


def _inproj(x, g, mod, w, rot):
    B, S, D = x.shape
    tm = PROJ_TILE
    vw = DIFF_HEADS * DIFF_DV
    return pl.pallas_call(
        _inproj_kernel,
        grid=(B, S // tm),
        in_specs=[pl.BlockSpec((1, tm, D), lambda b, i: (b, i, 0)),
                  _resident((1, D), lambda b, i: (0, 0)),
                  pl.BlockSpec((1, 6, D), lambda b, i: (b, 0, 0)),
                  _resident(w.shape, lambda b, i: (0, 0)),
                  pl.BlockSpec((4, tm, LANES), lambda b, i: (0, i, 0))],
        out_specs=[pl.BlockSpec((1, tm, PROJ_COLS), lambda b, i: (b, i, 0)),
                   pl.BlockSpec((1, vw, tm), lambda b, i: (b, 0, i))],
        out_shape=[jax.ShapeDtypeStruct((B, S, PROJ_COLS), BF16),
                   jax.ShapeDtypeStruct((B, vw, S), BF16)],
        scratch_shapes=[pltpu.VMEM((vw, D), BF16)],
        compiler_params=_cparams(("arbitrary", "arbitrary")),
        name="inproj",
    )(x, g, mod, w, rot)


def _ret_kernel(cd_ref, q_ref, k_ref, v_ref, rg_ref, dec_ref, dm_ref, gn_ref, o_ref, acc_ref):
    S = q_ref.shape[1]
    H = dm_ref.shape[0]
    C = RET_CHUNK
    nc = S // C
    nt = (((1,), (1,)), ((), ()))
    tn = (((0,), (0,)), ((), ()))

    def chunk(n, h):
        sl, lanes = pl.ds(n * C, C), slice(h * LANES, (h + 1) * LANES)
        return sl, q_ref[0, sl, lanes], k_ref[0, sl, lanes], v_ref[0, sl, lanes]

    def scaled(t, dec):
        return (t.astype(F32) * dec).astype(BF16)

    def forward(heads):
        state = {h: jnp.zeros((RET_DK, RET_DV), F32) for h in heads}
        for n in range(nc):
            for h in heads:
                sl, q, k, v = chunk(n, h)
                s = lax.dot_general(q, k, nt, preferred_element_type=F32) * dm_ref[h]
                inner = jnp.dot(s.astype(BF16), v, preferred_element_type=F32)
                cross = jnp.dot(scaled(q, dec_ref[h, 0]), state[h].astype(BF16), preferred_element_type=F32)
                acc_ref[h, sl, :] = inner + cross
                kv = lax.dot_general(scaled(k, dec_ref[h, 1]), v, tn, preferred_element_type=F32)
                state[h] = cd_ref[h, 0] * state[h] + kv

    def backward(heads):
        state = {h: jnp.zeros((RET_DK, RET_DV), F32) for h in heads}
        for n in reversed(range(nc)):
            for h in heads:
                sl, q, k, v = chunk(n, h)
                cross = jnp.dot(scaled(q, dec_ref[h, 2]), state[h].astype(BF16), preferred_element_type=F32)
                acc_ref[h, sl, :] = acc_ref[h, sl, :] + cross
                kv = lax.dot_general(scaled(k, dec_ref[h, 3]), v, tn, preferred_element_type=F32)
                state[h] = cd_ref[h, 1] * state[h] + kv

    def normalise(heads):
        for h in heads:
            lanes = slice(h * LANES, (h + 1) * LANES)
            y = acc_ref[h]
            mu = jnp.mean(y, axis=-1, keepdims=True)
            yc = y - mu
            var = jnp.mean(yc * yc, axis=-1, keepdims=True)
            yn = yc * lax.rsqrt(var + NORM_EPS) * gn_ref[:, lanes]
            o_ref[0, :, lanes] = (rg_ref[0, :, lanes].astype(F32) * yn).astype(BF16)

    first, second = tuple(range(H // 2)), tuple(range(H // 2, H))
    forward(first)
    backward(first)
    forward(second)
    normalise(first)
    backward(second)
    normalise(second)


def _ret(proj, cd, dec, dmat, gn):
    B, S, _ = proj.shape
    H = RET_HEADS
    C = RET_CHUNK

    def cols(base):
        return pl.BlockSpec((1, S, H * LANES), lambda b: (b, 0, base // H))

    return pl.pallas_call(
        _ret_kernel,
        grid=(B,),
        in_specs=[pl.BlockSpec(memory_space=pltpu.SMEM),
                  cols(COL_RQ), cols(COL_RK), cols(COL_RV), cols(COL_RG),
                  _resident((H, 4, C, LANES), lambda b: (0, 0, 0, 0)),
                  _resident((H, C, C), lambda b: (0, 0, 0)),
                  _resident((1, H * LANES), lambda b: (0, 0))],
        out_specs=pl.BlockSpec((1, S, H * LANES), lambda b: (b, 0, 0)),
        out_shape=jax.ShapeDtypeStruct((B, S, H * RET_DV), BF16),
        scratch_shapes=[pltpu.VMEM((H, S, RET_DV), F32)],
        compiler_params=_cparams(("arbitrary",)),
        name="ret",
    )(cd, proj, proj, proj, proj, dec, dmat, gn)


def _diff_kernel(bstat_ref, lam_ref, q_ref, k_ref, vt_ref, brow_ref, g_ref, o_ref, p_ref, t_ref):
    hh = pl.program_id(0)
    S = k_ref.shape[1]
    TQ, KB = Q_TILE, KEY_CHUNK
    nk = S // KB
    lv = lam_ref[...]
    lam = (jnp.exp(jnp.sum(lv[0:1] * lv[1:2], axis=1, keepdims=True))
           - jnp.exp(jnp.sum(lv[2:3] * lv[3:4], axis=1, keepdims=True)) + LAM_INIT)
    first_half = lax.broadcasted_iota(jnp.int32, (TQ, LANES), 1) < DIFF_DH
    nt = (((1,), (1,)), ((), ()))
    bmax = bstat_ref[hh, 0]

    band_lo = S - TQ - MAX_DISTANCE

    @pl.when(pl.program_id(1) == 0)
    def _():
        side = band_lo
        t_ref[0:side, :] = jnp.full((side, TQ), bstat_ref[hh, 1], F32)
        rows = jnp.broadcast_to(brow_ref[0], (BIAS_BAND, brow_ref.shape[2]))
        shift = brow_ref.shape[2] - (BIAS_BAND - 1 + MAX_DISTANCE)
        t_ref[side:side + BIAS_BAND, :] = pltpu.roll(rows, shift, 1, stride=1, stride_axis=0)[:, :TQ]
        t_ref[side + BIAS_BAND:, :] = jnp.full((side, TQ), bstat_ref[hh, 2], F32)
    kf = k_ref[0].astype(F32)
    kmax = jnp.sqrt(jnp.max(jnp.sum(kf * kf, axis=1, keepdims=True), axis=0, keepdims=True))
    sel_r = lax.broadcasted_iota(jnp.int32, (8, LANES), 0)
    sel_c = lax.broadcasted_iota(jnp.int32, (8, LANES), 1)
    sel = jnp.where((sel_c < DIFF_DH) == (sel_r == 0), 1.0, 0.0)
    sel = jnp.where(sel_r < 2, sel, 0.0).astype(BF16)

    def tile_rows(qb):
        return pl.ds(qb * TQ if isinstance(qb, int) else pl.multiple_of(qb * TQ, TQ), TQ)

    def masked_q(qb):
        q = q_ref[0, tile_rows(qb), :]
        zero = jnp.zeros_like(q)
        return q, (jnp.where(first_half, q, zero), jnp.where(first_half, zero, q))

    def finish(ot, qb):
        ms = jnp.mean(ot * ot, axis=0, keepdims=True)
        y = ot * lax.rsqrt(ms + NORM_EPS) * g_ref[...] * (1.0 - LAM_INIT)
        o_ref[0, tile_rows(qb), :] = y.T.astype(BF16)

    def exponentials(qb, slot):
        w0 = (S - TQ) - qb * TQ
        q, qz = masked_q(qb)
        qsq = (q.astype(F32) * q.astype(F32)).astype(BF16)
        n2 = lax.dot_general(sel, qsq, nt, preferred_element_type=F32) * 1.01
        m = [jnp.sqrt(n2[i:i + 1]) * kmax + bmax for i in range(2)]
        m_lo = [mi - bstat_ref[hh, 1] for mi in m]
        m_hi = [mi - bstat_ref[hh, 2] for mi in m]
        l = [jnp.zeros((8, TQ), F32) for _ in range(2)]
        for c in range(nk):
            rows = slice(c * KB, (c + 1) * KB)
            kb = k_ref[0, rows, :]
            t0 = w0 + c * KB
            below, above = t0 + KB <= band_lo, t0 >= band_lo + BIAS_BAND
            bias = None if below or above else t_ref[t0:t0 + KB, :]
            for i in range(2):
                s = lax.dot_general(kb, qz[i], nt, preferred_element_type=F32)
                if below:
                    p = jnp.exp2(s - m_lo[i])
                elif above:
                    p = jnp.exp2(s - m_hi[i])
                else:
                    p = jnp.exp2((s - m[i]) + bias)
                l[i] = l[i] + jnp.sum(p.reshape(KB // 8, 8, TQ), axis=0)
                p_ref[slot, i, rows, :] = p.astype(BF16)
        return [jnp.sum(l[i], axis=0, keepdims=True) for i in range(2)]

    def values(qb, slot, lr):
        ratio = jnp.broadcast_to(lam * lr[0] / lr[1], (16, TQ)).astype(BF16)
        ot = jnp.zeros((DIFF_DV, TQ), F32)
        for c in range(nk):
            rows = slice(c * KB, (c + 1) * KB)
            p1 = p_ref[slot, 0, rows, :].reshape(KB // 16, 16, TQ)
            p2 = p_ref[slot, 1, rows, :].reshape(KB // 16, 16, TQ)
            a = (p1 - ratio * p2).reshape(KB, TQ)
            ot = ot + jnp.dot(vt_ref[0, :, rows], a, preferred_element_type=F32)
        finish(ot * (1.0 / lr[0]), qb)

    n_tiles = S // TQ
    lmin = jnp.full((1, TQ), jnp.inf, F32)
    sums = exponentials(0, 0)
    for t in range(n_tiles):
        lmin = jnp.minimum(lmin, jnp.minimum(sums[0], sums[1]))
        nxt = exponentials(t + 1, (t + 1) % 2) if t + 1 < n_tiles else None
        values(t, t % 2, sums)
        sums = nxt

    @pl.when(jnp.min(lmin) < L_MIN)
    def _():
        def exact_tile(qb, carry):
            w0 = pl.multiple_of((S - TQ) - qb * TQ, TQ)
            _, qz = masked_q(qb)
            bias = t_ref[pl.ds(w0, S), :]
            attn = None
            for i, scale in ((0, 1.0), (1, lam)):
                s = lax.dot_general(k_ref[0], qz[i], nt, preferred_element_type=F32) + bias
                p = jnp.exp2(s - jnp.max(s, axis=0, keepdims=True))
                part = p * (scale / jnp.sum(p, axis=0, keepdims=True))
                attn = part if attn is None else attn - part
            finish(jnp.dot(vt_ref[0], attn.astype(BF16), preferred_element_type=F32), qb)
            return carry

        lax.fori_loop(0, S // TQ, exact_tile, 0)


def _diff(proj, vt, lam_vecs, brow, bstat, g):
    B, S, _ = proj.shape
    H = DIFF_HEADS

    def col(base):
        return pl.BlockSpec((1, S, LANES), lambda h, b: (b, 0, base + h))

    return pl.pallas_call(
        _diff_kernel,
        grid=(H, B),
        in_specs=[pl.BlockSpec(memory_space=pltpu.SMEM),
                  pl.BlockSpec((4, DIFF_DH), lambda h, b: (0, 0)),
                  col(COL_DQ), col(COL_DK),
                  pl.BlockSpec((1, DIFF_DV, S), lambda h, b: (b, h, 0)),
                  pl.BlockSpec((1, 1, brow.shape[2]), lambda h, b: (h, 0, 0)),
                  pl.BlockSpec((DIFF_DV, Q_TILE), lambda h, b: (0, 0))],
        out_specs=pl.BlockSpec((1, S, LANES), lambda h, b: (b, 0, h)),
        out_shape=jax.ShapeDtypeStruct((B, S, H * DIFF_DV), BF16),
        scratch_shapes=[pltpu.VMEM((2, 2, S, Q_TILE), BF16), pltpu.VMEM((2 * S - Q_TILE, Q_TILE), F32)],
        compiler_params=_cparams(("arbitrary", "arbitrary")),
        name="diff",
    )(bstat, lam_vecs, proj, proj, vt, brow, g)


def _pack_halves(x):
    w = x.shape[1] // 2
    lo = pltpu.bitcast(x[:, :w].astype(BF16).astype(F32), jnp.int32)
    hi = pltpu.bitcast(x[:, w:].astype(BF16).astype(F32), jnp.int32)
    return lax.shift_right_logical(lo, jnp.full_like(lo, 16)) | (hi & jnp.int32(-65536))


def _unpack_halves(words):
    lo = pltpu.bitcast(lax.shift_left(words, jnp.full_like(words, 16)), F32)
    hi = pltpu.bitcast(words & jnp.int32(-65536), F32)
    return jnp.concatenate([lo, hi], axis=1).astype(BF16)


def _merge_kernel(x_ref, yr_ref, yd_ref, mod_ref, gmix_ref, wg_ref, wr_ref, wd_ref, wo_ref,
                  g_ref, wrt_ref, x1_ref, h2_ref, lg_ref):
    parts = 4
    gate_w = wg_ref.shape[1] // 2
    step = x_ref.shape[1] // parts

    def project(r):
        rows = slice(r * step, (r + 1) * step)
        h = _rms(x_ref[0, rows, :], gmix_ref[...]) * (1.0 + mod_ref[0, 1:2, :]) + mod_ref[0, 0:1, :]
        gates = jax.nn.sigmoid(jnp.dot(h.astype(BF16), wg_ref[...], preferred_element_type=F32))
        a = jnp.dot(yr_ref[0, rows, :], wr_ref[...], preferred_element_type=F32)
        d = jnp.dot(yd_ref[0, rows, :], wd_ref[...], preferred_element_type=F32)
        merged = gates[:, :gate_w] * a + gates[:, gate_w:] * d
        return jnp.dot(merged.astype(BF16), wo_ref[...], preferred_element_type=F32)

    def epilogue(r, o):
        rows = slice(r * step, (r + 1) * step)
        x1 = x_ref[0, rows, :] + mod_ref[0, 2:3, :] * o
        x1_ref[0, rows, :] = x1
        h2 = _rms(x1, g_ref[...]) * (1.0 + mod_ref[0, 4:5, :]) + mod_ref[0, 3:4, :]
        h2_ref[0, rows, :] = _pack_halves(h2)
        lg_ref[0, :, rows] = lax.dot_general(wrt_ref[...], h2.astype(BF16), (((1,), (1,)), ((), ())),
                                             preferred_element_type=F32)

    o = project(0)
    for r in range(parts):
        nxt = project(r + 1) if r + 1 < parts else None
        epilogue(r, o)
        o = nxt


def _merge(x, yr, yd, mod, gmix, wg, wr, wd, wo, g, wrt):
    B, S, D = x.shape
    tm = PROJ_TILE
    E = wrt.shape[0]
    return pl.pallas_call(
        _merge_kernel,
        grid=(B, S // tm),
        in_specs=[pl.BlockSpec((1, tm, D), lambda b, i: (b, i, 0)),
                  pl.BlockSpec((1, tm, yr.shape[2]), lambda b, i: (b, i, 0)),
                  pl.BlockSpec((1, tm, yd.shape[2]), lambda b, i: (b, i, 0)),
                  pl.BlockSpec((1, 6, D), lambda b, i: (b, 0, 0)),
                  _resident((1, D), lambda b, i: (0, 0)),
                  _resident(wg.shape, lambda b, i: (0, 0)),
                  _resident(wr.shape, lambda b, i: (0, 0)),
                  _resident(wd.shape, lambda b, i: (0, 0)),
                  _resident(wo.shape, lambda b, i: (0, 0)),
                  _resident((1, D), lambda b, i: (0, 0)),
                  _resident(wrt.shape, lambda b, i: (0, 0))],
        out_specs=[pl.BlockSpec((1, tm, D), lambda b, i: (b, i, 0)),
                   pl.BlockSpec((1, tm, D // 2), lambda b, i: (b, i, 0)),
                   pl.BlockSpec((1, E, tm), lambda b, i: (b, 0, i))],
        out_shape=[jax.ShapeDtypeStruct((B, S, D), F32),
                   jax.ShapeDtypeStruct((B, S, D // 2), jnp.int32),
                   jax.ShapeDtypeStruct((B, E, S), F32)],
        compiler_params=_cparams(("arbitrary", "arbitrary")),
        name="merge",
    )(x, yr, yd, mod, gmix, wg, wr, wd, wo, g, wrt)


def _lane_prefix(m, tri):
    E, S = m.shape
    off = jnp.zeros((E, 1), F32)
    parts = []
    for j in range(S // LANES):
        blk = m[:, j * LANES:(j + 1) * LANES]
        parts.append(jnp.dot(blk.astype(BF16), tri, preferred_element_type=F32) + off)
        off = off + jnp.sum(blk, axis=1, keepdims=True)
    return jnp.concatenate(parts, axis=1)


def _route_kernel(lg_ref, rank_ref, gate_ref, rank_t_ref, win_ref, ok_ref, *, cap):
    E, S = lg_ref.shape[1], lg_ref.shape[2]
    tiles = win_ref.shape[2]
    tm = S // tiles
    lane = lax.broadcasted_iota(jnp.int32, (E, LANES), 1)
    for j in range(lg_ref.shape[0]):
        rank, gate_ref[j], before = _route_one(lg_ref[j], cap)
        start = jnp.zeros((E, LANES), F32)
        end = jnp.full((E, LANES), float(cap), F32)
        for i in range(tiles):
            first = before[:, i * tm:i * tm + 1]
            start = jnp.where(lane == i, first, start)
            if i > 0:
                end = jnp.where(lane == i - 1, first, end)
        win = jnp.minimum(jnp.floor(start * (1.0 / BF16_SUBLANES)) * BF16_SUBLANES, float(cap - SCAT_WINDOW))
        fits = jnp.min(jnp.where(end - win <= SCAT_WINDOW, 1.0, 0.0), axis=0, keepdims=True)
        win_ref[j] = win[:, :tiles].astype(jnp.int32)
        ok_ref[j] = fits[:, :tiles].astype(jnp.int32)
        rank_ref[j] = rank.astype(jnp.int32)
        padded = jnp.concatenate([rank, jnp.full((LANES - E, S), -1.0, F32)], axis=0)
        rank_t_ref[j] = padded.T.astype(jnp.int32)


def _route_one(lg, cap):
    e = jnp.exp(lg - jnp.max(lg, axis=0, keepdims=True))
    aff = e / jnp.sum(e, axis=0, keepdims=True)
    bits = pltpu.bitcast(aff, jnp.int32)
    E = lg.shape[0]

    def count(mask):
        return jnp.sum(jnp.where(mask, 1.0, 0.0), axis=1, keepdims=True)

    def reaches(cand):
        return count(bits >= cand) >= cap

    thr = jnp.zeros((E, 1), jnp.int32)
    for hi in range(30, 0, -2):
        c1, c2 = thr | (1 << hi), thr | (1 << (hi - 1))
        c3 = c1 | (1 << (hi - 1))
        thr = jnp.where(reaches(c3), c3, jnp.where(reaches(c1), c1, jnp.where(reaches(c2), c2, thr)))
    thr = jnp.where(reaches(thr | 1), thr | 1, thr)
    gt = bits > thr
    eq = bits == thr
    need = cap - count(gt)
    r = lax.broadcasted_iota(jnp.int32, (LANES, LANES), 0)
    c = lax.broadcasted_iota(jnp.int32, (LANES, LANES), 1)
    tri = jnp.where(r < c, 1.0, 0.0).astype(BF16)
    eq_before = _lane_prefix(jnp.where(eq, 1.0, 0.0), tri)
    sel = gt | (eq & (eq_before < need))
    slot = _lane_prefix(jnp.where(sel, 1.0, 0.0), tri)
    return jnp.where(sel, slot, -1.0), jnp.where(sel, aff, 0.0), slot


def _route(logits, cap):
    B, E, S = logits.shape
    G = ROUTE_SEQS_PER_STEP
    tiles = S // TOKEN_TILE
    assert B % G == 0
    spec = pl.BlockSpec((G, E, S), lambda b: (b, 0, 0))
    return pl.pallas_call(
        functools.partial(_route_kernel, cap=cap),
        grid=(B // G,),
        in_specs=[spec],
        out_specs=[spec, spec, pl.BlockSpec((G, S, LANES), lambda b: (b, 0, 0)),
                   pl.BlockSpec((G, E, tiles), lambda b: (b, 0, 0)),
                   pl.BlockSpec((G, 1, tiles), lambda b: (b, 0, 0))],
        out_shape=[jax.ShapeDtypeStruct((B, E, S), jnp.int32),
                   jax.ShapeDtypeStruct((B, E, S), F32),
                   jax.ShapeDtypeStruct((B, S, LANES), jnp.int32),
                   jax.ShapeDtypeStruct((B, E, tiles), jnp.int32),
                   jax.ShapeDtypeStruct((B, 1, tiles), jnp.int32)],
        compiler_params=_cparams(("arbitrary",)),
        name="route",
    )(logits)


def _gather_rows(table, rank, cap):
    B, E, S = rank.shape
    W = table.shape[1]
    workers = SC_CORES * SC_SUBCORES
    pairs = E * B
    per = -(-pairs // workers)
    assert B & (B - 1) == 0 and cap % SC_GATHER_ROWS == 0 and S % SC_LANES == 0
    shift = B.bit_length() - 1
    mesh = plsc.VectorSubcoreMesh(core_axis_name="c", subcore_axis_name="s")

    R = SC_GATHER_ROWS
    n_chunks = cap // R

    def body(table_hbm, rank_hbm, out_hbm, rank_v, idx_v, buf_v, gsem, wsem):
        wid = lax.axis_index("s") * SC_CORES + lax.axis_index("c")

        def fetch(c):
            return pltpu.make_async_copy(table_hbm.at[idx_v.at[pl.ds(c * R, R)]], buf_v.at[c % 2], gsem.at[c % 2])

        def flush(p, c):
            return pltpu.make_async_copy(buf_v.at[c % 2], out_hbm.at[pl.ds(p * cap + c * R, R)], wsem.at[c % 2])

        def one_pair(p):
            e = lax.shift_right_logical(p, shift)
            b = p & (B - 1)
            pltpu.sync_copy(rank_hbm.at[b, e], rank_v)
            base = b * S

            @pl.loop(0, S, step=SC_LANES)
            def _(t0):
                r = rank_v[pl.ds(t0, SC_LANES)]
                tok = lax.iota(jnp.int32, SC_LANES) + (base + t0)
                plsc.store_scatter(idx_v, [r], tok, mask=r >= 0)

            fetch(0).start()
            for c in range(n_chunks):
                fetch(c).wait()
                if c >= 1:
                    flush(p, c - 1).wait()
                if c + 1 < n_chunks:
                    fetch(c + 1).start()
                flush(p, c).start()
            flush(p, n_chunks - 1).wait()

        for j in range(per):
            p = wid * per + j
            pl.when(p < pairs)(functools.partial(one_pair, p))

    return pl.kernel(
        body,
        out_type=jax.ShapeDtypeStruct((pairs * cap, W), jnp.int32),
        mesh=mesh,
        scratch_types=[pltpu.VMEM((S,), jnp.int32), pltpu.VMEM((cap,), jnp.int32),
                       pltpu.VMEM((2, R, W), jnp.int32),
                       pltpu.SemaphoreType.DMA((2,)), pltpu.SemaphoreType.DMA((2,))],
        compiler_params=pltpu.CompilerParams(needs_layout_passes=False),
    )(table, rank)


def _moe_kernel(rank_ref, gate_ref, x_ref, wg_ref, wu_ref, wd_ref, o_ref, wg_s, wu_s, wd_s, *, cap):
    r, b = pl.program_id(0), pl.program_id(1)
    n_exp = pl.num_programs(0) - 1

    @pl.when(r < n_exp)
    def _():
        slot = r % 2
        rows_in, rows_ff = wg_ref.shape[1], wd_ref.shape[1]
        wg_s[slot, pl.ds(pl.multiple_of(b * rows_in, rows_in), rows_in), :] = wg_ref[0].astype(BF16)
        wu_s[slot, pl.ds(pl.multiple_of(b * rows_in, rows_in), rows_in), :] = wu_ref[0].astype(BF16)
        wd_s[slot, pl.ds(pl.multiple_of(b * rows_ff, rows_ff), rows_ff), :] = wd_ref[0].astype(BF16)

    @pl.when(r == 0)
    def _():
        o_ref[...] = jnp.zeros_like(o_ref)

    @pl.when(r > 0)
    def _():
        slot = (r - 1) % 2
        S = rank_ref.shape[3]
        for j in range(x_ref.shape[0]):
            rank = rank_ref[j, 0]
            pick = lax.broadcasted_iota(jnp.int32, (cap, S), 0) == rank
            xin = _unpack_halves(x_ref[j])
            a = jnp.dot(xin, wg_s[slot], preferred_element_type=F32)
            u = jnp.dot(xin, wu_s[slot], preferred_element_type=F32)
            act = (a * jax.nn.sigmoid(a) * u).astype(BF16)
            y = jnp.dot(act, wd_s[slot], preferred_element_type=F32)
            g = jnp.sum(jnp.where(pick, gate_ref[j, 0], 0.0), axis=1, keepdims=True)
            o_ref[0, j] = (y * g).astype(BF16)


def _moe(rank, gate, xin, wg, wu, wd, cap, e0, n_e):
    B, E, S = rank.shape
    D, Fd = wg.shape[1], wg.shape[2]
    G = MOE_SEQS_PER_STEP
    steps = B // G
    assert steps * G == B and D % steps == 0 and Fd % steps == 0

    def cur(r):
        return e0 + jnp.maximum(r - 1, 0)

    def nxt(r):
        return e0 + jnp.minimum(r, n_e - 1)

    row = pl.BlockSpec((G, 1, 1, S), lambda r, b: (b, cur(r), 0, 0))
    return pl.pallas_call(
        functools.partial(_moe_kernel, cap=cap),
        grid=(n_e + 1, steps),
        in_specs=[row, row,
                  pl.BlockSpec((G, cap, D // 2), lambda r, b: (jnp.maximum(r - 1, 0) * steps + b, 0, 0)),
                  pl.BlockSpec((1, D // steps, Fd), lambda r, b: (nxt(r), b, 0)),
                  pl.BlockSpec((1, D // steps, Fd), lambda r, b: (nxt(r), b, 0)),
                  pl.BlockSpec((1, Fd // steps, D), lambda r, b: (nxt(r), b, 0))],
        out_specs=pl.BlockSpec((1, G, cap, D), lambda r, b: (jnp.where(r == 0, n_e, r - 1), b, 0, 0)),
        out_shape=jax.ShapeDtypeStruct((n_e + 1, B, cap, D), BF16),
        scratch_shapes=[pltpu.VMEM((2, D, Fd), BF16), pltpu.VMEM((2, D, Fd), BF16),
                        pltpu.VMEM((2, Fd, D), BF16)],
        compiler_params=_cparams(("arbitrary", "arbitrary")),
        name="moe",
    )(rank.reshape(B, E, 1, S), gate.reshape(B, E, 1, S), xin.reshape(n_e * B, cap, D // 2), wg, wu, wd)


def _scat_kernel(win_ref, ok_ref, rt_ref, *refs, cap):
    *y_refs, x1_ref, mod_ref, g_ref, o_ref = refs
    b, i = pl.program_id(0), pl.program_id(1)
    rt = rt_ref[0]
    tm, E = rt.shape[0], win_ref.shape[1]
    D = x1_ref.shape[2]
    sizes = [y.shape[0] - 1 for y in y_refs]
    home = [(g, e) for g, n in enumerate(sizes) for e in range(n)]

    def finish(moe):
        x2 = x1_ref[0] + mod_ref[0, 5:6, :] * moe
        o_ref[0] = _rms(x2, g_ref[...])

    @pl.when(ok_ref[b, 0, i] != 0)
    def _():
        slot = lax.broadcasted_iota(jnp.int32, (tm, SCAT_WINDOW), 1)
        moe = jnp.zeros((tm, D), F32)
        for e in range(0, E, 2):
            starts = [pl.multiple_of(win_ref[b, e + k, i], BF16_SUBLANES) for k in range(2)]
            place = jnp.concatenate(
                [jnp.where(rt[:, e + k:e + k + 1] - starts[k] == slot, 1.0, 0.0).astype(BF16)
                 for k in range(2)], axis=1)
            y = jnp.concatenate([y_refs[home[e + k][0]][home[e + k][1], 0, pl.ds(starts[k], SCAT_WINDOW), :]
                                 for k in range(2)], axis=0)
            moe = moe + jnp.dot(place, y, preferred_element_type=F32)
        finish(moe)

    @pl.when(ok_ref[b, 0, i] == 0)
    def _():
        slot = lax.broadcasted_iota(jnp.int32, (tm, cap), 1)
        place = jnp.concatenate(
            [jnp.where(rt[:, e:e + 1] == slot, 1.0, 0.0).astype(BF16) for e in range(E)], axis=1)
        y = jnp.concatenate([y_ref[0:n, 0].reshape(n * cap, D) for y_ref, n in zip(y_refs, sizes)], axis=0)
        finish(jnp.dot(place, y, preferred_element_type=F32))


def _scat(rank_t, win, ok, ybufs, x1, mod, g, cap):
    B, S, D = x1.shape
    tm = TOKEN_TILE
    smem = pl.BlockSpec(memory_space=pltpu.SMEM)
    return pl.pallas_call(
        functools.partial(_scat_kernel, cap=cap),
        grid=(B, S // tm),
        in_specs=[smem, smem,
                  pl.BlockSpec((1, tm, LANES), lambda b, i: (b, i, 0)),
                  *[pl.BlockSpec(y.shape[:1] + (1, cap, D), lambda b, i: (0, b, 0, 0)) for y in ybufs],
                  pl.BlockSpec((1, tm, D), lambda b, i: (b, i, 0)),
                  pl.BlockSpec((1, 6, D), lambda b, i: (b, 0, 0)),
                  _resident((1, D), lambda b, i: (0, 0))],
        out_specs=pl.BlockSpec((1, tm, D), lambda b, i: (b, i, 0)),
        out_shape=jax.ShapeDtypeStruct((B, S, D), F32),
        compiler_params=_cparams(("arbitrary", "arbitrary")),
        name="scat",
    )(win, ok, rank_t, *ybufs, x1, mod, g)


def _rot_tables(S):
    half = RET_DK // 2
    inv = 1.0 / (RET_THETA_BASE ** np.linspace(0.0, 1.0, half))
    ang = np.arange(S, dtype=np.float64)[:, None] * inv[None, :]
    cos = np.repeat(np.cos(ang), 2, axis=1)
    sin = np.stack([-np.sin(ang), np.sin(ang)], axis=2).reshape(S, RET_DK)
    sc = RET_DK ** -0.5
    return jnp.asarray(np.stack([cos, sin, cos * sc, sin * sc]), F32)


def _ret_tables():
    C = RET_CHUNK
    heads = np.arange(RET_HEADS, dtype=np.float64)
    lgf = np.log1p(-np.exp2(-RET_FWD_DECAY_OFFSET - heads))[:, None]
    lgb = np.log1p(-np.exp2(-RET_BWD_DECAY_OFFSET - heads))[:, None]
    idx = np.arange(C, dtype=np.float64)
    diff = idx[:, None] - idx[None, :]
    dmat = np.where(diff >= 0,
                    np.exp(np.maximum(diff, 0.0)[None] * lgf[:, :, None]),
                    np.exp(np.maximum(-diff, 0.0)[None] * lgb[:, :, None]))
    dec = np.stack([np.exp((idx + 1)[None, :] * lgf),
                    np.exp((C - 1 - idx)[None, :] * lgf),
                    np.exp((C - idx)[None, :] * lgb),
                    np.exp(idx[None, :] * lgb)], axis=1)
    dec = np.broadcast_to(dec[..., None], dec.shape + (LANES,))
    cd = np.concatenate([np.exp(C * lgf), np.exp(C * lgb)], axis=1)
    return jnp.asarray(cd, F32), jnp.asarray(dec, F32), jnp.asarray(dmat, F32)


def _t5_bucket(rel):
    nb = N_BUCKETS // 2
    max_exact = nb // 2
    ret = (rel > 0).astype(jnp.int32) * nb
    n = jnp.abs(rel)
    large = max_exact + (jnp.log(jnp.maximum(n, 1).astype(F32) / max_exact)
                         / math.log(MAX_DISTANCE / max_exact) * (nb - max_exact)).astype(jnp.int32)
    large = jnp.minimum(large, nb - 1)
    return ret + jnp.where(n < max_exact, n, large)


def _bias_rows(rel_bias):
    reach = BIAS_BAND - 1
    rel = jnp.concatenate([jnp.arange(reach, -reach - 1, -1, dtype=jnp.int32),
                           jnp.array([-(2 ** 20), 2 ** 20], jnp.int32)])
    f = rel_bias[_t5_bucket(rel)].astype(F32).T * LOG2E
    rows = jnp.pad(f[:, :-2], ((0, 0), (0, 2 * BIAS_BAND - (2 * reach + 1))))
    stats = jnp.stack([jnp.max(f, axis=1), f[:, -2], f[:, -1]], axis=1)
    return rows[:, None, :], stats


def kernel(x, c, w_ada, b_ada, norm_mix_g, w_in, ret_gn_g, diff_subln_g, lambda_q1, lambda_k1, lambda_q2, lambda_k2, w_ret_out, w_diff_out, w_o, rel_bias, norm_ffn_g, w_router, w_exp_gate, w_exp_up, w_exp_down, final_g):
    B, S, D = x.shape
    cap = CAPACITY_FACTOR * S // N_EXPERTS
    l = 0

    mod = _ada(c, w_ada[l], b_ada[l]).reshape(B, 6, D)
    w_in_b = w_in[l].astype(BF16)
    proj, vt = _inproj(x, norm_mix_g[l].reshape(1, D), mod, w_in_b, _rot_tables(S))

    cd, dec, dmat = _ret_tables()
    y_ret = _ret(proj, cd, dec, dmat, ret_gn_g[l].reshape(1, -1))

    lam_vecs = jnp.stack([lambda_q1[l], lambda_k1[l], lambda_q2[l], lambda_k2[l]]).astype(F32)
    brow, bstat = _bias_rows(rel_bias)
    subln = jnp.broadcast_to(diff_subln_g[l].astype(F32)[:, None], (DIFF_DV, Q_TILE))
    y_diff = _diff(proj, vt, lam_vecs, brow, bstat, subln)

    x1, h2, logits = _merge(x, y_ret, y_diff, mod, norm_mix_g[l].reshape(1, D), w_in_b[:, W_GR * LANES:],
                            w_ret_out[l].astype(BF16), w_diff_out[l].astype(BF16), w_o[l].astype(BF16),
                            norm_ffn_g[l].reshape(1, D), w_router[l].T.astype(BF16))

    rank, gate, rank_t, win, ok = _route(logits, cap)
    table, ybufs, e0 = h2.reshape(B * S, D // 2), [], 0
    for n_e in MOE_GROUP_SIZES:
        xin = _gather_rows(table, rank[:, e0:e0 + n_e], cap)
        ybufs.append(_moe(rank, gate, xin, w_exp_gate[l], w_exp_up[l], w_exp_down[l], cap, e0, n_e))
        e0 += n_e
    return _scat(rank_t, win, ok, ybufs, x1, mod, final_g.reshape(1, D), cap)
```

```python
import functools
import math

import numpy as np
import jax
import jax.numpy as jnp
from jax import lax
from jax.experimental import pallas as pl
from jax.experimental.pallas import tpu as pltpu
from jax.experimental.pallas import tpu_sc as plsc

F32 = jnp.float32
BF16 = jnp.bfloat16

RET_HEADS = 4
RET_DK = 128
RET_DV = 128
RET_FWD_DECAY_OFFSET = 5.0
RET_BWD_DECAY_OFFSET = 5.5
RET_THETA_BASE = 10000.0
DIFF_HEADS = 4
DIFF_DH = 64
DIFF_DV = 2 * DIFF_DH
N_BUCKETS = 32
MAX_DISTANCE = 128
N_EXPERTS = 16
CAPACITY_FACTOR = 2
NORM_EPS = 1e-6
LAM_INIT = 0.8 - 0.6 * math.exp(-0.3 * 0)
LOG2E = math.log2(math.e)

LANES = 128
BF16_SUBLANES = 16
VMEM_LIMIT_BYTES = 56 * 2**20

SC_CORES = 2
SC_SUBCORES = 16
SC_LANES = 16
SC_GATHER_ROWS = 64

TOKEN_TILE = 512
PROJ_TILE = 1024
RET_CHUNK = 256
Q_TILE = 256
KEY_CHUNK = 512
L_MIN = 2.0 ** -60
BIAS_BAND = Q_TILE + 2 * MAX_DISTANCE
SCAT_WINDOW = 128
MOE_SEQS_PER_STEP = 4
ROUTE_SEQS_PER_STEP = 4
MOE_GROUP_SIZES = (4, 12)

W_RQ, W_RK, W_RV, W_RG = 0, 4, 8, 12
W_DQ1, W_DQ2, W_DK1, W_DK2, W_DV = 16, 18, 20, 22, 24
W_GR, W_GD = 28, 36

COL_RQ, COL_RK, COL_RV, COL_RG = 0, 4, 8, 12
COL_DQ, COL_DK = 16, 20
PROJ_COLS = 24 * LANES


def _cparams(sem):
    return pltpu.CompilerParams(dimension_semantics=sem, vmem_limit_bytes=VMEM_LIMIT_BYTES)


def _resident(shape, index_map):
    return pl.BlockSpec(shape, index_map, pipeline_mode=pl.Buffered(1))


def _ada_kernel(c_ref, w_ref, b_ref, o_ref):
    c = c_ref[...]
    a = c * jax.nn.sigmoid(c)
    o_ref[...] = jnp.dot(a, w_ref[...], preferred_element_type=F32,
                         precision=lax.Precision.HIGHEST) + b_ref[...]


def _ada(c, w, b):
    B, D = c.shape
    n = w.shape[1] // D
    return pl.pallas_call(
        _ada_kernel,
        grid=(n,),
        in_specs=[pl.BlockSpec((B, D), lambda j: (0, 0)),
                  pl.BlockSpec((D, D), lambda j: (0, j)),
                  pl.BlockSpec((1, D), lambda j: (0, j))],
        out_specs=pl.BlockSpec((B, D), lambda j: (0, j)),
        out_shape=jax.ShapeDtypeStruct((B, n * D), F32),
        compiler_params=_cparams(("arbitrary",)),
        name="ada",
    )(c, w, b.reshape(1, -1))


def _rms(x, g):
    ms = jnp.mean(x * x, axis=-1, keepdims=True)
    return x * lax.rsqrt(ms + NORM_EPS) * g


def _rot_pairs(x, cos, sin):
    even = (lax.broadcasted_iota(jnp.int32, x.shape, 1) & 1) == 0
    partner = jnp.where(even, pltpu.roll(x, LANES - 1, axis=1), pltpu.roll(x, 1, axis=1))
    return x * cos + partner * sin


def _pair_maps(a, b, second):
    low = lax.broadcasted_iota(jnp.int32, a.shape, 1) < DIFF_DH
    if second:
        return jnp.where(low, pltpu.roll(a, DIFF_DH, axis=1), b)
    return jnp.where(low, a, pltpu.roll(b, DIFF_DH, axis=1))


def _inproj_kernel(x_ref, g_ref, mod_ref, w_ref, rot_ref, o_ref, vt_ref, wvt_s):
    @pl.when((pl.program_id(0) == 0) & (pl.program_id(1) == 0))
    def _():
        wvt_s[...] = w_ref[:, W_DV * LANES:(W_DV + 4) * LANES].astype(F32).T.astype(BF16)

    parts = 4
    step = x_ref.shape[1] // parts

    def normed(p):
        x = x_ref[0, p * step:(p + 1) * step, :]
        h = _rms(x, g_ref[...]) * (1.0 + mod_ref[0, 1:2, :]) + mod_ref[0, 0:1, :]
        return h.astype(BF16)

    def project(p, hb):
        rows = slice(p * step, (p + 1) * step)

        def mm(col, width):
            return jnp.dot(hb, w_ref[:, col * LANES:(col + width) * LANES], preferred_element_type=F32)

        def put(col, val):
            o_ref[0, rows, col * LANES:col * LANES + val.shape[1]] = val.astype(BF16)

        for src, dst, t in ((W_RQ, COL_RQ, 0), (W_RK, COL_RK, 2)):
            cos, sin = rot_ref[t, rows, :], rot_ref[t + 1, rows, :]
            r = mm(src, RET_HEADS)
            for hh in range(RET_HEADS):
                put(dst + hh, _rot_pairs(r[:, hh * LANES:(hh + 1) * LANES], cos, sin))
        put(COL_RV, mm(W_RV, 4))
        r = mm(W_RG, 4)
        put(COL_RG, r * jax.nn.sigmoid(r))
        for src1, src2, dst, scale in ((W_DQ1, W_DQ2, COL_DQ, DIFF_DH ** -0.5 * LOG2E),
                                       (W_DK1, W_DK2, COL_DK, None)):
            m1, m2 = mm(src1, 2), mm(src2, 2)
            for hh in range(DIFF_HEADS):
                blk = slice((hh // 2) * LANES, (hh // 2 + 1) * LANES)
                val = _pair_maps(m1[:, blk], m2[:, blk], hh % 2 == 1)
                put(dst + hh, val if scale is None else val * scale)
        vt_ref[0, :, rows] = lax.dot_general(wvt_s[...], hb, (((1,), (1,)), ((), ())),
                                             preferred_element_type=F32).astype(BF16)

    hb = normed(0)
    for p in range(parts):
        nxt = normed(p + 1) if p + 1 < parts else None
        project(p, hb)
        hb = nxt


def _inproj(x, g, mod, w, rot):
    B, S, D = x.shape
    tm = PROJ_TILE
    vw = DIFF_HEADS * DIFF_DV
    return pl.pallas_call(
        _inproj_kernel,
        grid=(B, S // tm),
        in_specs=[pl.BlockSpec((1, tm, D), lambda b, i: (b, i, 0)),
                  _resident((1, D), lambda b, i: (0, 0)),
                  pl.BlockSpec((1, 6, D), lambda b, i: (b, 0, 0)),
                  _resident(w.shape, lambda b, i: (0, 0)),
                  pl.BlockSpec((4, tm, LANES), lambda b, i: (0, i, 0))],
        out_specs=[pl.BlockSpec((1, tm, PROJ_COLS), lambda b, i: (b, i, 0)),
                   pl.BlockSpec((1, vw, tm), lambda b, i: (b, 0, i))],
        out_shape=[jax.ShapeDtypeStruct((B, S, PROJ_COLS), BF16),
                   jax.ShapeDtypeStruct((B, vw, S), BF16)],
        scratch_shapes=[pltpu.VMEM((vw, D), BF16)],
        compiler_params=_cparams(("arbitrary", "arbitrary")),
        name="inproj",
    )(x, g, mod, w, rot)


def _ret_kernel(cd_ref, q_ref, k_ref, v_ref, rg_ref, dec_ref, dm_ref, gn_ref, o_ref, acc_ref):
    S = q_ref.shape[1]
    H = dm_ref.shape[0]
    C = RET_CHUNK
    nc = S // C
    nt = (((1,), (1,)), ((), ()))
    tn = (((0,), (0,)), ((), ()))

    def chunk(n, h):
        sl, lanes = pl.ds(n * C, C), slice(h * LANES, (h + 1) * LANES)
        return sl, q_ref[0, sl, lanes], k_ref[0, sl, lanes], v_ref[0, sl, lanes]

    def scaled(t, dec):
        return (t.astype(F32) * dec).astype(BF16)

    def forward(heads):
        state = {h: jnp.zeros((RET_DK, RET_DV), F32) for h in heads}
        for n in range(nc):
            for h in heads:
                sl, q, k, v = chunk(n, h)
                s = lax.dot_general(q, k, nt, preferred_element_type=F32) * dm_ref[h]
                inner = jnp.dot(s.astype(BF16), v, preferred_element_type=F32)
                cross = jnp.dot(scaled(q, dec_ref[h, 0]), state[h].astype(BF16), preferred_element_type=F32)
                acc_ref[h, sl, :] = inner + cross
                kv = lax.dot_general(scaled(k, dec_ref[h, 1]), v, tn, preferred_element_type=F32)
                state[h] = cd_ref[h, 0] * state[h] + kv

    def backward(heads):
        state = {h: jnp.zeros((RET_DK, RET_DV), F32) for h in heads}
        for n in reversed(range(nc)):
            for h in heads:
                sl, q, k, v = chunk(n, h)
                cross = jnp.dot(scaled(q, dec_ref[h, 2]), state[h].astype(BF16), preferred_element_type=F32)
                acc_ref[h, sl, :] = acc_ref[h, sl, :] + cross
                kv = lax.dot_general(scaled(k, dec_ref[h, 3]), v, tn, preferred_element_type=F32)
                state[h] = cd_ref[h, 1] * state[h] + kv

    def normalise(heads):
        for h in heads:
            lanes = slice(h * LANES, (h + 1) * LANES)
            y = acc_ref[h]
            mu = jnp.mean(y, axis=-1, keepdims=True)
            yc = y - mu
            var = jnp.mean(yc * yc, axis=-1, keepdims=True)
            yn = yc * lax.rsqrt(var + NORM_EPS) * gn_ref[:, lanes]
            o_ref[0, :, lanes] = (rg_ref[0, :, lanes].astype(F32) * yn).astype(BF16)

    first, second = tuple(range(H // 2)), tuple(range(H // 2, H))
    forward(first)
    backward(first)
    forward(second)
    normalise(first)
    backward(second)
    normalise(second)


def _ret(proj, cd, dec, dmat, gn):
    B, S, _ = proj.shape
    H = RET_HEADS
    C = RET_CHUNK

    def cols(base):
        return pl.BlockSpec((1, S, H * LANES), lambda b: (b, 0, base // H))

    return pl.pallas_call(
        _ret_kernel,
        grid=(B,),
        in_specs=[pl.BlockSpec(memory_space=pltpu.SMEM),
                  cols(COL_RQ), cols(COL_RK), cols(COL_RV), cols(COL_RG),
                  _resident((H, 4, C, LANES), lambda b: (0, 0, 0, 0)),
                  _resident((H, C, C), lambda b: (0, 0, 0)),
                  _resident((1, H * LANES), lambda b: (0, 0))],
        out_specs=pl.BlockSpec((1, S, H * LANES), lambda b: (b, 0, 0)),
        out_shape=jax.ShapeDtypeStruct((B, S, H * RET_DV), BF16),
        scratch_shapes=[pltpu.VMEM((H, S, RET_DV), F32)],
        compiler_params=_cparams(("arbitrary",)),
        name="ret",
    )(cd, proj, proj, proj, proj, dec, dmat, gn)


def _diff_kernel(bstat_ref, lam_ref, q_ref, k_ref, vt_ref, brow_ref, g_ref, o_ref, p_ref, t_ref):
    hh = pl.program_id(0)
    S = k_ref.shape[1]
    TQ, KB = Q_TILE, KEY_CHUNK
    nk = S // KB
    lv = lam_ref[...]
    lam = (jnp.exp(jnp.sum(lv[0:1] * lv[1:2], axis=1, keepdims=True))
           - jnp.exp(jnp.sum(lv[2:3] * lv[3:4], axis=1, keepdims=True)) + LAM_INIT)
    first_half = lax.broadcasted_iota(jnp.int32, (TQ, LANES), 1) < DIFF_DH
    nt = (((1,), (1,)), ((), ()))
    bmax = bstat_ref[hh, 0]

    band_lo = S - TQ - MAX_DISTANCE

    @pl.when(pl.program_id(1) == 0)
    def _():
        side = band_lo
        t_ref[0:side, :] = jnp.full((side, TQ), bstat_ref[hh, 1], F32)
        rows = jnp.broadcast_to(brow_ref[0], (BIAS_BAND, brow_ref.shape[2]))
        shift = brow_ref.shape[2] - (BIAS_BAND - 1 + MAX_DISTANCE)
        t_ref[side:side + BIAS_BAND, :] = pltpu.roll(rows, shift, 1, stride=1, stride_axis=0)[:, :TQ]
        t_ref[side + BIAS_BAND:, :] = jnp.full((side, TQ), bstat_ref[hh, 2], F32)
    kf = k_ref[0].astype(F32)
    kmax = jnp.sqrt(jnp.max(jnp.sum(kf * kf, axis=1, keepdims=True), axis=0, keepdims=True))
    sel_r = lax.broadcasted_iota(jnp.int32, (8, LANES), 0)
    sel_c = lax.broadcasted_iota(jnp.int32, (8, LANES), 1)
    sel = jnp.where((sel_c < DIFF_DH) == (sel_r == 0), 1.0, 0.0)
    sel = jnp.where(sel_r < 2, sel, 0.0).astype(BF16)

    def tile_rows(qb):
        return pl.ds(qb * TQ if isinstance(qb, int) else pl.multiple_of(qb * TQ, TQ), TQ)

    def masked_q(qb):
        q = q_ref[0, tile_rows(qb), :]
        zero = jnp.zeros_like(q)
        return q, (jnp.where(first_half, q, zero), jnp.where(first_half, zero, q))

    def finish(ot, qb):
        ms = jnp.mean(ot * ot, axis=0, keepdims=True)
        y = ot * lax.rsqrt(ms + NORM_EPS) * g_ref[...] * (1.0 - LAM_INIT)
        o_ref[0, tile_rows(qb), :] = y.T.astype(BF16)

    def exponentials(qb, slot):
        w0 = (S - TQ) - qb * TQ
        q, qz = masked_q(qb)
        qsq = (q.astype(F32) * q.astype(F32)).astype(BF16)
        n2 = lax.dot_general(sel, qsq, nt, preferred_element_type=F32) * 1.01
        m = [jnp.sqrt(n2[i:i + 1]) * kmax + bmax for i in range(2)]
        m_lo = [mi - bstat_ref[hh, 1] for mi in m]
        m_hi = [mi - bstat_ref[hh, 2] for mi in m]
        l = [jnp.zeros((8, TQ), F32) for _ in range(2)]
        for c in range(nk):
            rows = slice(c * KB, (c + 1) * KB)
            kb = k_ref[0, rows, :]
            t0 = w0 + c * KB
            below, above = t0 + KB <= band_lo, t0 >= band_lo + BIAS_BAND
            bias = None if below or above else t_ref[t0:t0 + KB, :]
            for i in range(2):
                s = lax.dot_general(kb, qz[i], nt, preferred_element_type=F32)
                if below:
                    p = jnp.exp2(s - m_lo[i])
                elif above:
                    p = jnp.exp2(s - m_hi[i])
                else:
                    p = jnp.exp2((s - m[i]) + bias)
                l[i] = l[i] + jnp.sum(p.reshape(KB // 8, 8, TQ), axis=0)
                p_ref[slot, i, rows, :] = p.astype(BF16)
        return [jnp.sum(l[i], axis=0, keepdims=True) for i in range(2)]

    def values(qb, slot, lr):
        ratio = jnp.broadcast_to(lam * lr[0] / lr[1], (16, TQ)).astype(BF16)
        ot = jnp.zeros((DIFF_DV, TQ), F32)
        for c in range(nk):
            rows = slice(c * KB, (c + 1) * KB)
            p1 = p_ref[slot, 0, rows, :].reshape(KB // 16, 16, TQ)
            p2 = p_ref[slot, 1, rows, :].reshape(KB // 16, 16, TQ)
            a = (p1 - ratio * p2).reshape(KB, TQ)
            ot = ot + jnp.dot(vt_ref[0, :, rows], a, preferred_element_type=F32)
        finish(ot * (1.0 / lr[0]), qb)

    n_tiles = S // TQ
    lmin = jnp.full((1, TQ), jnp.inf, F32)
    sums = exponentials(0, 0)
    for t in range(n_tiles):
        lmin = jnp.minimum(lmin, jnp.minimum(sums[0], sums[1]))
        nxt = exponentials(t + 1, (t + 1) % 2) if t + 1 < n_tiles else None
        values(t, t % 2, sums)
        sums = nxt

    @pl.when(jnp.min(lmin) < L_MIN)
    def _():
        def exact_tile(qb, carry):
            w0 = pl.multiple_of((S - TQ) - qb * TQ, TQ)
            _, qz = masked_q(qb)
            bias = t_ref[pl.ds(w0, S), :]
            attn = None
            for i, scale in ((0, 1.0), (1, lam)):
                s = lax.dot_general(k_ref[0], qz[i], nt, preferred_element_type=F32) + bias
                p = jnp.exp2(s - jnp.max(s, axis=0, keepdims=True))
                part = p * (scale / jnp.sum(p, axis=0, keepdims=True))
                attn = part if attn is None else attn - part
            finish(jnp.dot(vt_ref[0], attn.astype(BF16), preferred_element_type=F32), qb)
            return carry

        lax.fori_loop(0, S // TQ, exact_tile, 0)


def _diff(proj, vt, lam_vecs, brow, bstat, g):
    B, S, _ = proj.shape
    H = DIFF_HEADS

    def col(base):
        return pl.BlockSpec((1, S, LANES), lambda h, b: (b, 0, base + h))

    return pl.pallas_call(
        _diff_kernel,
        grid=(H, B),
        in_specs=[pl.BlockSpec(memory_space=pltpu.SMEM),
                  pl.BlockSpec((4, DIFF_DH), lambda h, b: (0, 0)),
                  col(COL_DQ), col(COL_DK),
                  pl.BlockSpec((1, DIFF_DV, S), lambda h, b: (b, h, 0)),
                  pl.BlockSpec((1, 1, brow.shape[2]), lambda h, b: (h, 0, 0)),
                  pl.BlockSpec((DIFF_DV, Q_TILE), lambda h, b: (0, 0))],
        out_specs=pl.BlockSpec((1, S, LANES), lambda h, b: (b, 0, h)),
        out_shape=jax.ShapeDtypeStruct((B, S, H * DIFF_DV), BF16),
        scratch_shapes=[pltpu.VMEM((2, 2, S, Q_TILE), BF16), pltpu.VMEM((2 * S - Q_TILE, Q_TILE), F32)],
        compiler_params=_cparams(("arbitrary", "arbitrary")),
        name="diff",
    )(bstat, lam_vecs, proj, proj, vt, brow, g)


def _pack_halves(x):
    w = x.shape[1] // 2
    lo = pltpu.bitcast(x[:, :w].astype(BF16).astype(F32), jnp.int32)
    hi = pltpu.bitcast(x[:, w:].astype(BF16).astype(F32), jnp.int32)
    return lax.shift_right_logical(lo, jnp.full_like(lo, 16)) | (hi & jnp.int32(-65536))


def _unpack_halves(words):
    lo = pltpu.bitcast(lax.shift_left(words, jnp.full_like(words, 16)), F32)
    hi = pltpu.bitcast(words & jnp.int32(-65536), F32)
    return jnp.concatenate([lo, hi], axis=1).astype(BF16)


def _merge_kernel(x_ref, yr_ref, yd_ref, mod_ref, gmix_ref, wg_ref, wr_ref, wd_ref, wo_ref,
                  g_ref, wrt_ref, x1_ref, h2_ref, lg_ref):
    parts = 4
    gate_w = wg_ref.shape[1] // 2
    step = x_ref.shape[1] // parts

    def project(r):
        rows = slice(r * step, (r + 1) * step)
        h = _rms(x_ref[0, rows, :], gmix_ref[...]) * (1.0 + mod_ref[0, 1:2, :]) + mod_ref[0, 0:1, :]
        gates = jax.nn.sigmoid(jnp.dot(h.astype(BF16), wg_ref[...], preferred_element_type=F32))
        a = jnp.dot(yr_ref[0, rows, :], wr_ref[...], preferred_element_type=F32)
        d = jnp.dot(yd_ref[0, rows, :], wd_ref[...], preferred_element_type=F32)
        merged = gates[:, :gate_w] * a + gates[:, gate_w:] * d
        return jnp.dot(merged.astype(BF16), wo_ref[...], preferred_element_type=F32)

    def epilogue(r, o):
        rows = slice(r * step, (r + 1) * step)
        x1 = x_ref[0, rows, :] + mod_ref[0, 2:3, :] * o
        x1_ref[0, rows, :] = x1
        h2 = _rms(x1, g_ref[...]) * (1.0 + mod_ref[0, 4:5, :]) + mod_ref[0, 3:4, :]
        h2_ref[0, rows, :] = _pack_halves(h2)
        lg_ref[0, :, rows] = lax.dot_general(wrt_ref[...], h2.astype(BF16), (((1,), (1,)), ((), ())),
                                             preferred_element_type=F32)

    o = project(0)
    for r in range(parts):
        nxt = project(r + 1) if r + 1 < parts else None
        epilogue(r, o)
        o = nxt


def _merge(x, yr, yd, mod, gmix, wg, wr, wd, wo, g, wrt):
    B, S, D = x.shape
    tm = PROJ_TILE
    E = wrt.shape[0]
    return pl.pallas_call(
        _merge_kernel,
        grid=(B, S // tm),
        in_specs=[pl.BlockSpec((1, tm, D), lambda b, i: (b, i, 0)),
                  pl.BlockSpec((1, tm, yr.shape[2]), lambda b, i: (b, i, 0)),
                  pl.BlockSpec((1, tm, yd.shape[2]), lambda b, i: (b, i, 0)),
                  pl.BlockSpec((1, 6, D), lambda b, i: (b, 0, 0)),
                  _resident((1, D), lambda b, i: (0, 0)),
                  _resident(wg.shape, lambda b, i: (0, 0)),
                  _resident(wr.shape, lambda b, i: (0, 0)),
                  _resident(wd.shape, lambda b, i: (0, 0)),
                  _resident(wo.shape, lambda b, i: (0, 0)),
                  _resident((1, D), lambda b, i: (0, 0)),
                  _resident(wrt.shape, lambda b, i: (0, 0))],
        out_specs=[pl.BlockSpec((1, tm, D), lambda b, i: (b, i, 0)),
                   pl.BlockSpec((1, tm, D // 2), lambda b, i: (b, i, 0)),
                   pl.BlockSpec((1, E, tm), lambda b, i: (b, 0, i))],
        out_shape=[jax.ShapeDtypeStruct((B, S, D), F32),
                   jax.ShapeDtypeStruct((B, S, D // 2), jnp.int32),
                   jax.ShapeDtypeStruct((B, E, S), F32)],
        compiler_params=_cparams(("arbitrary", "arbitrary")),
        name="merge",
    )(x, yr, yd, mod, gmix, wg, wr, wd, wo, g, wrt)


def _lane_prefix(m, tri):
    E, S = m.shape
    off = jnp.zeros((E, 1), F32)
    parts = []
    for j in range(S // LANES):
        blk = m[:, j * LANES:(j + 1) * LANES]
        parts.append(jnp.dot(blk.astype(BF16), tri, preferred_element_type=F32) + off)
        off = off + jnp.sum(blk, axis=1, keepdims=True)
    return jnp.concatenate(parts, axis=1)


def _route_kernel(lg_ref, rank_ref, gate_ref, rank_t_ref, win_ref, ok_ref, *, cap):
    E, S = lg_ref.shape[1], lg_ref.shape[2]
    tiles = win_ref.shape[2]
    tm = S // tiles
    lane = lax.broadcasted_iota(jnp.int32, (E, LANES), 1)
    for j in range(lg_ref.shape[0]):
        rank, gate_ref[j], before = _route_one(lg_ref[j], cap)
        start = jnp.zeros((E, LANES), F32)
        end = jnp.full((E, LANES), float(cap), F32)
        for i in range(tiles):
            first = before[:, i * tm:i * tm + 1]
            start = jnp.where(lane == i, first, start)
            if i > 0:
                end = jnp.where(lane == i - 1, first, end)
        win = jnp.minimum(jnp.floor(start * (1.0 / BF16_SUBLANES)) * BF16_SUBLANES, float(cap - SCAT_WINDOW))
        fits = jnp.min(jnp.where(end - win <= SCAT_WINDOW, 1.0, 0.0), axis=0, keepdims=True)
        win_ref[j] = win[:, :tiles].astype(jnp.int32)
        ok_ref[j] = fits[:, :tiles].astype(jnp.int32)
        rank_ref[j] = rank.astype(jnp.int32)
        padded = jnp.concatenate([rank, jnp.full((LANES - E, S), -1.0, F32)], axis=0)
        rank_t_ref[j] = padded.T.astype(jnp.int32)


def _route_one(lg, cap):
    e = jnp.exp(lg - jnp.max(lg, axis=0, keepdims=True))
    aff = e / jnp.sum(e, axis=0, keepdims=True)
    bits = pltpu.bitcast(aff, jnp.int32)
    E = lg.shape[0]

    def count(mask):
        return jnp.sum(jnp.where(mask, 1.0, 0.0), axis=1, keepdims=True)

    def reaches(cand):
        return count(bits >= cand) >= cap

    thr = jnp.zeros((E, 1), jnp.int32)
    for hi in range(30, 0, -2):
        c1, c2 = thr | (1 << hi), thr | (1 << (hi - 1))
        c3 = c1 | (1 << (hi - 1))
        thr = jnp.where(reaches(c3), c3, jnp.where(reaches(c1), c1, jnp.where(reaches(c2), c2, thr)))
    thr = jnp.where(reaches(thr | 1), thr | 1, thr)
    gt = bits > thr
    eq = bits == thr
    need = cap - count(gt)
    r = lax.broadcasted_iota(jnp.int32, (LANES, LANES), 0)
    c = lax.broadcasted_iota(jnp.int32, (LANES, LANES), 1)
    tri = jnp.where(r < c, 1.0, 0.0).astype(BF16)
    eq_before = _lane_prefix(jnp.where(eq, 1.0, 0.0), tri)
    sel = gt | (eq & (eq_before < need))
    slot = _lane_prefix(jnp.where(sel, 1.0, 0.0), tri)
    return jnp.where(sel, slot, -1.0), jnp.where(sel, aff, 0.0), slot


def _route(logits, cap):
    B, E, S = logits.shape
    G = ROUTE_SEQS_PER_STEP
    tiles = S // TOKEN_TILE
    assert B % G == 0
    spec = pl.BlockSpec((G, E, S), lambda b: (b, 0, 0))
    return pl.pallas_call(
        functools.partial(_route_kernel, cap=cap),
        grid=(B // G,),
        in_specs=[spec],
        out_specs=[spec, spec, pl.BlockSpec((G, S, LANES), lambda b: (b, 0, 0)),
                   pl.BlockSpec((G, E, tiles), lambda b: (b, 0, 0)),
                   pl.BlockSpec((G, 1, tiles), lambda b: (b, 0, 0))],
        out_shape=[jax.ShapeDtypeStruct((B, E, S), jnp.int32),
                   jax.ShapeDtypeStruct((B, E, S), F32),
                   jax.ShapeDtypeStruct((B, S, LANES), jnp.int32),
                   jax.ShapeDtypeStruct((B, E, tiles), jnp.int32),
                   jax.ShapeDtypeStruct((B, 1, tiles), jnp.int32)],
        compiler_params=_cparams(("arbitrary",)),
        name="route",
    )(logits)


def _gather_rows(table, rank, cap):
    B, E, S = rank.shape
    W = table.shape[1]
    workers = SC_CORES * SC_SUBCORES
    pairs = E * B
    per = -(-pairs // workers)
    assert B & (B - 1) == 0 and cap % SC_GATHER_ROWS == 0 and S % SC_LANES == 0
    shift = B.bit_length() - 1
    mesh = plsc.VectorSubcoreMesh(core_axis_name="c", subcore_axis_name="s")

    R = SC_GATHER_ROWS
    n_chunks = cap // R

    def body(table_hbm, rank_hbm, out_hbm, rank_v, idx_v, buf_v, gsem, wsem):
        wid = lax.axis_index("s") * SC_CORES + lax.axis_index("c")

        def fetch(c):
            return pltpu.make_async_copy(table_hbm.at[idx_v.at[pl.ds(c * R, R)]], buf_v.at[c % 2], gsem.at[c % 2])

        def flush(p, c):
            return pltpu.make_async_copy(buf_v.at[c % 2], out_hbm.at[pl.ds(p * cap + c * R, R)], wsem.at[c % 2])

        def one_pair(p):
            e = lax.shift_right_logical(p, shift)
            b = p & (B - 1)
            pltpu.sync_copy(rank_hbm.at[b, e], rank_v)
            base = b * S

            @pl.loop(0, S, step=SC_LANES)
            def _(t0):
                r = rank_v[pl.ds(t0, SC_LANES)]
                tok = lax.iota(jnp.int32, SC_LANES) + (base + t0)
                plsc.store_scatter(idx_v, [r], tok, mask=r >= 0)

            fetch(0).start()
            for c in range(n_chunks):
                fetch(c).wait()
                if c >= 1:
                    flush(p, c - 1).wait()
                if c + 1 < n_chunks:
                    fetch(c + 1).start()
                flush(p, c).start()
            flush(p, n_chunks - 1).wait()

        for j in range(per):
            p = wid * per + j
            pl.when(p < pairs)(functools.partial(one_pair, p))

    return pl.kernel(
        body,
        out_type=jax.ShapeDtypeStruct((pairs * cap, W), jnp.int32),
        mesh=mesh,
        scratch_types=[pltpu.VMEM((S,), jnp.int32), pltpu.VMEM((cap,), jnp.int32),
                       pltpu.VMEM((2, R, W), jnp.int32),
                       pltpu.SemaphoreType.DMA((2,)), pltpu.SemaphoreType.DMA((2,))],
        compiler_params=pltpu.CompilerParams(needs_layout_passes=False),
    )(table, rank)


def _moe_kernel(rank_ref, gate_ref, x_ref, wg_ref, wu_ref, wd_ref, o_ref, wg_s, wu_s, wd_s, *, cap):
    r, b = pl.program_id(0), pl.program_id(1)
    n_exp = pl.num_programs(0) - 1

    @pl.when(r < n_exp)
    def _():
        slot = r % 2
        rows_in, rows_ff = wg_ref.shape[1], wd_ref.shape[1]
        wg_s[slot, pl.ds(pl.multiple_of(b * rows_in, rows_in), rows_in), :] = wg_ref[0].astype(BF16)
        wu_s[slot, pl.ds(pl.multiple_of(b * rows_in, rows_in), rows_in), :] = wu_ref[0].astype(BF16)
        wd_s[slot, pl.ds(pl.multiple_of(b * rows_ff, rows_ff), rows_ff), :] = wd_ref[0].astype(BF16)

    @pl.when(r == 0)
    def _():
        o_ref[...] = jnp.zeros_like(o_ref)

    @pl.when(r > 0)
    def _():
        slot = (r - 1) % 2
        S = rank_ref.shape[3]
        for j in range(x_ref.shape[0]):
            rank = rank_ref[j, 0]
            pick = lax.broadcasted_iota(jnp.int32, (cap, S), 0) == rank
            xin = _unpack_halves(x_ref[j])
            a = jnp.dot(xin, wg_s[slot], preferred_element_type=F32)
            u = jnp.dot(xin, wu_s[slot], preferred_element_type=F32)
            act = (a * jax.nn.sigmoid(a) * u).astype(BF16)
            y = jnp.dot(act, wd_s[slot], preferred_element_type=F32)
            g = jnp.sum(jnp.where(pick, gate_ref[j, 0], 0.0), axis=1, keepdims=True)
            o_ref[0, j] = (y * g).astype(BF16)


def _moe(rank, gate, xin, wg, wu, wd, cap, e0, n_e):
    B, E, S = rank.shape
    D, Fd = wg.shape[1], wg.shape[2]
    G = MOE_SEQS_PER_STEP
    steps = B // G
    assert steps * G == B and D % steps == 0 and Fd % steps == 0

    def cur(r):
        return e0 + jnp.maximum(r - 1, 0)

    def nxt(r):
        return e0 + jnp.minimum(r, n_e - 1)

    row = pl.BlockSpec((G, 1, 1, S), lambda r, b: (b, cur(r), 0, 0))
    return pl.pallas_call(
        functools.partial(_moe_kernel, cap=cap),
        grid=(n_e + 1, steps),
        in_specs=[row, row,
                  pl.BlockSpec((G, cap, D // 2), lambda r, b: (jnp.maximum(r - 1, 0) * steps + b, 0, 0)),
                  pl.BlockSpec((1, D // steps, Fd), lambda r, b: (nxt(r), b, 0)),
                  pl.BlockSpec((1, D // steps, Fd), lambda r, b: (nxt(r), b, 0)),
                  pl.BlockSpec((1, Fd // steps, D), lambda r, b: (nxt(r), b, 0))],
        out_specs=pl.BlockSpec((1, G, cap, D), lambda r, b: (jnp.where(r == 0, n_e, r - 1), b, 0, 0)),
        out_shape=jax.ShapeDtypeStruct((n_e + 1, B, cap, D), BF16),
        scratch_shapes=[pltpu.VMEM((2, D, Fd), BF16), pltpu.VMEM((2, D, Fd), BF16),
                        pltpu.VMEM((2, Fd, D), BF16)],
        compiler_params=_cparams(("arbitrary", "arbitrary")),
        name="moe",
    )(rank.reshape(B, E, 1, S), gate.reshape(B, E, 1, S), xin.reshape(n_e * B, cap, D // 2), wg, wu, wd)


def _scat_kernel(win_ref, ok_ref, rt_ref, *refs, cap):
    *y_refs, x1_ref, mod_ref, g_ref, o_ref = refs
    b, i = pl.program_id(0), pl.program_id(1)
    rt = rt_ref[0]
    tm, E = rt.shape[0], win_ref.shape[1]
    D = x1_ref.shape[2]
    sizes = [y.shape[0] - 1 for y in y_refs]
    home = [(g, e) for g, n in enumerate(sizes) for e in range(n)]

    def finish(moe):
        x2 = x1_ref[0] + mod_ref[0, 5:6, :] * moe
        o_ref[0] = _rms(x2, g_ref[...])

    @pl.when(ok_ref[b, 0, i] != 0)
    def _():
        slot = lax.broadcasted_iota(jnp.int32, (tm, SCAT_WINDOW), 1)
        moe = jnp.zeros((tm, D), F32)
        for e in range(0, E, 2):
            starts = [pl.multiple_of(win_ref[b, e + k, i], BF16_SUBLANES) for k in range(2)]
            place = jnp.concatenate(
                [jnp.where(rt[:, e + k:e + k + 1] - starts[k] == slot, 1.0, 0.0).astype(BF16)
                 for k in range(2)], axis=1)
            y = jnp.concatenate([y_refs[home[e + k][0]][home[e + k][1], 0, pl.ds(starts[k], SCAT_WINDOW), :]
                                 for k in range(2)], axis=0)
            moe = moe + jnp.dot(place, y, preferred_element_type=F32)
        finish(moe)

    @pl.when(ok_ref[b, 0, i] == 0)
    def _():
        slot = lax.broadcasted_iota(jnp.int32, (tm, cap), 1)
        place = jnp.concatenate(
            [jnp.where(rt[:, e:e + 1] == slot, 1.0, 0.0).astype(BF16) for e in range(E)], axis=1)
        y = jnp.concatenate([y_ref[0:n, 0].reshape(n * cap, D) for y_ref, n in zip(y_refs, sizes)], axis=0)
        finish(jnp.dot(place, y, preferred_element_type=F32))


def _scat(rank_t, win, ok, ybufs, x1, mod, g, cap):
    B, S, D = x1.shape
    tm = TOKEN_TILE
    smem = pl.BlockSpec(memory_space=pltpu.SMEM)
    return pl.pallas_call(
        functools.partial(_scat_kernel, cap=cap),
        grid=(B, S // tm),
        in_specs=[smem, smem,
                  pl.BlockSpec((1, tm, LANES), lambda b, i: (b, i, 0)),
                  *[pl.BlockSpec(y.shape[:1] + (1, cap, D), lambda b, i: (0, b, 0, 0)) for y in ybufs],
                  pl.BlockSpec((1, tm, D), lambda b, i: (b, i, 0)),
                  pl.BlockSpec((1, 6, D), lambda b, i: (b, 0, 0)),
                  _resident((1, D), lambda b, i: (0, 0))],
        out_specs=pl.BlockSpec((1, tm, D), lambda b, i: (b, i, 0)),
        out_shape=jax.ShapeDtypeStruct((B, S, D), F32),
        compiler_params=_cparams(("arbitrary", "arbitrary")),
        name="scat",
    )(win, ok, rank_t, *ybufs, x1, mod, g)


def _rot_tables(S):
    half = RET_DK // 2
    inv = 1.0 / (RET_THETA_BASE ** np.linspace(0.0, 1.0, half))
    ang = np.arange(S, dtype=np.float64)[:, None] * inv[None, :]
    cos = np.repeat(np.cos(ang), 2, axis=1)
    sin = np.stack([-np.sin(ang), np.sin(ang)], axis=2).reshape(S, RET_DK)
    sc = RET_DK ** -0.5
    return jnp.asarray(np.stack([cos, sin, cos * sc, sin * sc]), F32)


def _ret_tables():
    C = RET_CHUNK
    heads = np.arange(RET_HEADS, dtype=np.float64)
    lgf = np.log1p(-np.exp2(-RET_FWD_DECAY_OFFSET - heads))[:, None]
    lgb = np.log1p(-np.exp2(-RET_BWD_DECAY_OFFSET - heads))[:, None]
    idx = np.arange(C, dtype=np.float64)
    diff = idx[:, None] - idx[None, :]
    dmat = np.where(diff >= 0,
                    np.exp(np.maximum(diff, 0.0)[None] * lgf[:, :, None]),
                    np.exp(np.maximum(-diff, 0.0)[None] * lgb[:, :, None]))
    dec = np.stack([np.exp((idx + 1)[None, :] * lgf),
                    np.exp((C - 1 - idx)[None, :] * lgf),
                    np.exp((C - idx)[None, :] * lgb),
                    np.exp(idx[None, :] * lgb)], axis=1)
    dec = np.broadcast_to(dec[..., None], dec.shape + (LANES,))
    cd = np.concatenate([np.exp(C * lgf), np.exp(C * lgb)], axis=1)
    return jnp.asarray(cd, F32), jnp.asarray(dec, F32), jnp.asarray(dmat, F32)


def _t5_bucket(rel):
    nb = N_BUCKETS // 2
    max_exact = nb // 2
    ret = (rel > 0).astype(jnp.int32) * nb
    n = jnp.abs(rel)
    large = max_exact + (jnp.log(jnp.maximum(n, 1).astype(F32) / max_exact)
                         / math.log(MAX_DISTANCE / max_exact) * (nb - max_exact)).astype(jnp.int32)
    large = jnp.minimum(large, nb - 1)
    return ret + jnp.where(n < max_exact, n, large)


def _bias_rows(rel_bias):
    reach = BIAS_BAND - 1
    rel = jnp.concatenate([jnp.arange(reach, -reach - 1, -1, dtype=jnp.int32),
                           jnp.array([-(2 ** 20), 2 ** 20], jnp.int32)])
    f = rel_bias[_t5_bucket(rel)].astype(F32).T * LOG2E
    rows = jnp.pad(f[:, :-2], ((0, 0), (0, 2 * BIAS_BAND - (2 * reach + 1))))
    stats = jnp.stack([jnp.max(f, axis=1), f[:, -2], f[:, -1]], axis=1)
    return rows[:, None, :], stats


def kernel(x, c, w_ada, b_ada, norm_mix_g, w_in, ret_gn_g, diff_subln_g, lambda_q1, lambda_k1, lambda_q2, lambda_k2, w_ret_out, w_diff_out, w_o, rel_bias, norm_ffn_g, w_router, w_exp_gate, w_exp_up, w_exp_down, final_g):
    B, S, D = x.shape
    cap = CAPACITY_FACTOR * S // N_EXPERTS
    l = 0

    mod = _ada(c, w_ada[l], b_ada[l]).reshape(B, 6, D)
    w_in_b = w_in[l].astype(BF16)
    proj, vt = _inproj(x, norm_mix_g[l].reshape(1, D), mod, w_in_b, _rot_tables(S))

    cd, dec, dmat = _ret_tables()
    y_ret = _ret(proj, cd, dec, dmat, ret_gn_g[l].reshape(1, -1))

    lam_vecs = jnp.stack([lambda_q1[l], lambda_k1[l], lambda_q2[l], lambda_k2[l]]).astype(F32)
    brow, bstat = _bias_rows(rel_bias)
    subln = jnp.broadcast_to(diff_subln_g[l].astype(F32)[:, None], (DIFF_DV, Q_TILE))
    y_diff = _diff(proj, vt, lam_vecs, brow, bstat, subln)

    x1, h2, logits = _merge(x, y_ret, y_diff, mod, norm_mix_g[l].reshape(1, D), w_in_b[:, W_GR * LANES:],
                            w_ret_out[l].astype(BF16), w_diff_out[l].astype(BF16), w_o[l].astype(BF16),
                            norm_ffn_g[l].reshape(1, D), w_router[l].T.astype(BF16))

    rank, gate, rank_t, win, ok = _route(logits, cap)
    table, ybufs, e0 = h2.reshape(B * S, D // 2), [], 0
    for n_e in MOE_GROUP_SIZES:
        xin = _gather_rows(table, rank[:, e0:e0 + n_e], cap)
        ybufs.append(_moe(rank, gate, xin, w_exp_gate[l], w_exp_up[l], w_exp_down[l], cap, e0, n_e))
        e0 += n_e
    return _scat(rank_t, win, ok, ybufs, x1, mod, final_g.reshape(1, D), cap)
```

```python
import functools
import math

import numpy as np
import jax
import jax.numpy as jnp
from jax import lax
from jax.experimental import pallas as pl
from jax.experimental.pallas import tpu as pltpu
from jax.experimental.pallas import tpu_sc as plsc

F32 = jnp.float32
BF16 = jnp.bfloat16

RET_HEADS = 4
RET_DK = 128
RET_DV = 128
RET_FWD_DECAY_OFFSET = 5.0
RET_BWD_DECAY_OFFSET = 5.5
RET_THETA_BASE = 10000.0
DIFF_HEADS = 4
DIFF_DH = 64
DIFF_DV = 2 * DIFF_DH
N_BUCKETS = 32
MAX_DISTANCE = 128
N_EXPERTS = 16
CAPACITY_FACTOR = 2
NORM_EPS = 1e-6
LAM_INIT = 0.8 - 0.6 * math.exp(-0.3 * 0)
LOG2E = math.log2(math.e)

LANES = 128
BF16_SUBLANES = 16
VMEM_LIMIT_BYTES = 56 * 2**20

SC_CORES = 2
SC_SUBCORES = 16
SC_LANES = 16
SC_GATHER_ROWS = 64

TOKEN_TILE = 512
PROJ_TILE = 1024
RET_CHUNK = 256
Q_TILE = 256
KEY_CHUNK = 512
L_MIN = 2.0 ** -60
BIAS_BAND = Q_TILE + 2 * MAX_DISTANCE
SCAT_WINDOW = 128
MOE_SEQS_PER_STEP = 4
ROUTE_SEQS_PER_STEP = 4
MOE_GROUP_SIZES = (4, 12)

W_RQ, W_RK, W_RV, W_RG = 0, 4, 8, 12
W_DQ1, W_DQ2, W_DK1, W_DK2, W_DV = 16, 18, 20, 22, 24
W_GR, W_GD = 28, 36

COL_RQ, COL_RK, COL_RV, COL_RG = 0, 4, 8, 12
COL_DQ, COL_DK = 16, 20
PROJ_COLS = 24 * LANES


def _cparams(sem):
    return pltpu.CompilerParams(dimension_semantics=sem, vmem_limit_bytes=VMEM_LIMIT_BYTES)


def _resident(shape, index_map):
    return pl.BlockSpec(shape, index_map, pipeline_mode=pl.Buffered(1))


def _ada_kernel(c_ref, w_ref, b_ref, o_ref):
    c = c_ref[...]
    a = c * jax.nn.sigmoid(c)
    w = w_ref[...]
    a_hi, w_hi = a.astype(BF16), w.astype(BF16)
    a_lo = (a - a_hi.astype(F32)).astype(BF16)
    w_lo = (w - w_hi.astype(F32)).astype(BF16)
    p = jnp.dot(jnp.concatenate([a_hi, a_lo], axis=0), w_hi, preferred_element_type=F32)
    q = jnp.dot(a_hi, w_lo, preferred_element_type=F32)
    n = a.shape[0]
    o_ref[...] = p[:n] + p[n:] + q + b_ref[...]


def _ada(c, w, b):
    B, D = c.shape
    n = w.shape[1] // D
    return pl.pallas_call(
        _ada_kernel,
        grid=(n,),
        in_specs=[pl.BlockSpec((B, D), lambda j: (0, 0)),
                  pl.BlockSpec((D, D), lambda j: (0, j)),
                  pl.BlockSpec((1, D), lambda j: (0, j))],
        out_specs=pl.BlockSpec((B, D), lambda j: (0, j)),
        out_shape=jax.ShapeDtypeStruct((B, n * D), F32),
        compiler_params=_cparams(("arbitrary",)),
        name="ada",
    )(c, w, b.reshape(1, -1))


def _rms(x, g):
    ms = jnp.mean(x * x, axis=-1, keepdims=True)
    return x * lax.rsqrt(ms + NORM_EPS) * g


def _rot_pairs(x, cos, sin):
    even = (lax.broadcasted_iota(jnp.int32, x.shape, 1) & 1) == 0
    partner = jnp.where(even, pltpu.roll(x, LANES - 1, axis=1), pltpu.roll(x, 1, axis=1))
    return x * cos + partner * sin


def _pair_maps(a, b, second):
    low = lax.broadcasted_iota(jnp.int32, a.shape, 1) < DIFF_DH
    if second:
        return jnp.where(low, pltpu.roll(a, DIFF_DH, axis=1), b)
    return jnp.where(low, a, pltpu.roll(b, DIFF_DH, axis=1))


def _inproj_kernel(x_ref, g_ref, mod_ref, w_ref, rot_ref, o_ref, vt_ref, wvt_s):
    @pl.when((pl.program_id(0) == 0) & (pl.program_id(1) == 0))
    def _():
        wvt_s[...] = w_ref[:, W_DV * LANES:(W_DV + 4) * LANES].astype(F32).T.astype(BF16)

    parts = 4
    step = x_ref.shape[1] // parts

    def normed(p):
        x = x_ref[0, p * step:(p + 1) * step, :]
        h = _rms(x, g_ref[...]) * (1.0 + mod_ref[0, 1:2, :]) + mod_ref[0, 0:1, :]
        return h.astype(BF16)

    def project(p, hb):
        rows = slice(p * step, (p + 1) * step)

        def mm(col, width):
            return jnp.dot(hb, w_ref[:, col * LANES:(col + width) * LANES], preferred_element_type=F32)

        def put(col, val):
            o_ref[0, rows, col * LANES:col * LANES + val.shape[1]] = val.astype(BF16)

        for src, dst, t in ((W_RQ, COL_RQ, 0), (W_RK, COL_RK, 2)):
            cos, sin = rot_ref[t, rows, :], rot_ref[t + 1, rows, :]
            r = mm(src, RET_HEADS)
            for hh in range(RET_HEADS):
                put(dst + hh, _rot_pairs(r[:, hh * LANES:(hh + 1) * LANES], cos, sin))
        put(COL_RV, mm(W_RV, 4))
        r = mm(W_RG, 4)
        put(COL_RG, r * jax.nn.sigmoid(r))
        for src1, src2, dst, scale in ((W_DQ1, W_DQ2, COL_DQ, DIFF_DH ** -0.5 * LOG2E),
                                       (W_DK1, W_DK2, COL_DK, None)):
            m1, m2 = mm(src1, 2), mm(src2, 2)
            for hh in range(DIFF_HEADS):
                blk = slice((hh // 2) * LANES, (hh // 2 + 1) * LANES)
                val = _pair_maps(m1[:, blk], m2[:, blk], hh % 2 == 1)
                put(dst + hh, val if scale is None else val * scale)
        vt_ref[0, :, rows] = lax.dot_general(wvt_s[...], hb, (((1,), (1,)), ((), ())),
                                             preferred_element_type=F32).astype(BF16)

    hb = normed(0)
    for p in range(parts):
        nxt = normed(p + 1) if p + 1 < parts else None
        project(p, hb)
        hb = nxt


def _inproj(x, g, mod, w, rot):
    B, S, D = x.shape
    tm = PROJ_TILE
    vw = DIFF_HEADS * DIFF_DV
    return pl.pallas_call(
        _inproj_kernel,
        grid=(B, S // tm),
        in_specs=[pl.BlockSpec((1, tm, D), lambda b, i: (b, i, 0)),
                  _resident((1, D), lambda b, i: (0, 0)),
                  pl.BlockSpec((1, 6, D), lambda b, i: (b, 0, 0)),
                  _resident(w.shape, lambda b, i: (0, 0)),
                  pl.BlockSpec((4, tm, LANES), lambda b, i: (0, i, 0))],
        out_specs=[pl.BlockSpec((1, tm, PROJ_COLS), lambda b, i: (b, i, 0)),
                   pl.BlockSpec((1, vw, tm), lambda b, i: (b, 0, i))],
        out_shape=[jax.ShapeDtypeStruct((B, S, PROJ_COLS), BF16),
                   jax.ShapeDtypeStruct((B, vw, S), BF16)],
        scratch_shapes=[pltpu.VMEM((vw, D), BF16)],
        compiler_params=_cparams(("arbitrary", "arbitrary")),
        name="inproj",
    )(x, g, mod, w, rot)


def _ret_kernel(cd_ref, q_ref, k_ref, v_ref, rg_ref, dec_ref, dm_ref, gn_ref, o_ref, acc_ref):
    S = q_ref.shape[1]
    H = dm_ref.shape[0]
    C = RET_CHUNK
    nc = S // C
    nt = (((1,), (1,)), ((), ()))
    tn = (((0,), (0,)), ((), ()))

    def chunk(n, h):
        sl, lanes = pl.ds(n * C, C), slice(h * LANES, (h + 1) * LANES)
        return sl, q_ref[0, sl, lanes], k_ref[0, sl, lanes], v_ref[0, sl, lanes]

    def scaled(t, dec):
        return (t.astype(F32) * dec).astype(BF16)

    def forward(heads):
        state = {h: jnp.zeros((RET_DK, RET_DV), F32) for h in heads}
        for n in range(nc):
            for h in heads:
                sl, q, k, v = chunk(n, h)
                s = lax.dot_general(q, k, nt, preferred_element_type=F32) * dm_ref[h]
                inner = jnp.dot(s.astype(BF16), v, preferred_element_type=F32)
                cross = jnp.dot(scaled(q, dec_ref[h, 0]), state[h].astype(BF16), preferred_element_type=F32)
                acc_ref[h, sl, :] = inner + cross
                kv = lax.dot_general(scaled(k, dec_ref[h, 1]), v, tn, preferred_element_type=F32)
                state[h] = cd_ref[h, 0] * state[h] + kv

    def backward(heads):
        state = {h: jnp.zeros((RET_DK, RET_DV), F32) for h in heads}
        for n in reversed(range(nc)):
            for h in heads:
                sl, q, k, v = chunk(n, h)
                cross = jnp.dot(scaled(q, dec_ref[h, 2]), state[h].astype(BF16), preferred_element_type=F32)
                acc_ref[h, sl, :] = acc_ref[h, sl, :] + cross
                kv = lax.dot_general(scaled(k, dec_ref[h, 3]), v, tn, preferred_element_type=F32)
                state[h] = cd_ref[h, 1] * state[h] + kv

    def normalise(heads):
        for h in heads:
            lanes = slice(h * LANES, (h + 1) * LANES)
            y = acc_ref[h]
            mu = jnp.mean(y, axis=-1, keepdims=True)
            yc = y - mu
            var = jnp.mean(yc * yc, axis=-1, keepdims=True)
            yn = yc * lax.rsqrt(var + NORM_EPS) * gn_ref[:, lanes]
            o_ref[0, :, lanes] = (rg_ref[0, :, lanes].astype(F32) * yn).astype(BF16)

    first, second = tuple(range(H // 2)), tuple(range(H // 2, H))
    forward(first)
    backward(first)
    forward(second)
    normalise(first)
    backward(second)
    normalise(second)


def _ret(proj, cd, dec, dmat, gn):
    B, S, _ = proj.shape
    H = RET_HEADS
    C = RET_CHUNK

    def cols(base):
        return pl.BlockSpec((1, S, H * LANES), lambda b: (b, 0, base // H))

    return pl.pallas_call(
        _ret_kernel,
        grid=(B,),
        in_specs=[pl.BlockSpec(memory_space=pltpu.SMEM),
                  cols(COL_RQ), cols(COL_RK), cols(COL_RV), cols(COL_RG),
                  _resident((H, 4, C, LANES), lambda b: (0, 0, 0, 0)),
                  _resident((H, C, C), lambda b: (0, 0, 0)),
                  _resident((1, H * LANES), lambda b: (0, 0))],
        out_specs=pl.BlockSpec((1, S, H * LANES), lambda b: (b, 0, 0)),
        out_shape=jax.ShapeDtypeStruct((B, S, H * RET_DV), BF16),
        scratch_shapes=[pltpu.VMEM((H, S, RET_DV), F32)],
        compiler_params=_cparams(("arbitrary",)),
        name="ret",
    )(cd, proj, proj, proj, proj, dec, dmat, gn)


def _diff_kernel(bstat_ref, lam_ref, q_ref, k_ref, vt_ref, brow_ref, g_ref, o_ref, p_ref, t_ref):
    hh = pl.program_id(0)
    S = k_ref.shape[1]
    TQ, KB = Q_TILE, KEY_CHUNK
    nk = S // KB
    lv = lam_ref[...]
    lam = (jnp.exp(jnp.sum(lv[0:1] * lv[1:2], axis=1, keepdims=True))
           - jnp.exp(jnp.sum(lv[2:3] * lv[3:4], axis=1, keepdims=True)) + LAM_INIT)
    first_half = lax.broadcasted_iota(jnp.int32, (TQ, LANES), 1) < DIFF_DH
    nt = (((1,), (1,)), ((), ()))
    bmax = bstat_ref[hh, 0]

    band_lo = S - TQ - MAX_DISTANCE

    @pl.when(pl.program_id(1) == 0)
    def _():
        side = band_lo
        t_ref[0:side, :] = jnp.full((side, TQ), bstat_ref[hh, 1], F32)
        rows = jnp.broadcast_to(brow_ref[0], (BIAS_BAND, brow_ref.shape[2]))
        shift = brow_ref.shape[2] - (BIAS_BAND - 1 + MAX_DISTANCE)
        t_ref[side:side + BIAS_BAND, :] = pltpu.roll(rows, shift, 1, stride=1, stride_axis=0)[:, :TQ]
        t_ref[side + BIAS_BAND:, :] = jnp.full((side, TQ), bstat_ref[hh, 2], F32)
    kf = k_ref[0].astype(F32)
    kmax = jnp.sqrt(jnp.max(jnp.sum(kf * kf, axis=1, keepdims=True), axis=0, keepdims=True))
    sel_r = lax.broadcasted_iota(jnp.int32, (8, LANES), 0)
    sel_c = lax.broadcasted_iota(jnp.int32, (8, LANES), 1)
    sel = jnp.where((sel_c < DIFF_DH) == (sel_r == 0), 1.0, 0.0)
    sel = jnp.where(sel_r < 2, sel, 0.0).astype(BF16)

    def tile_rows(qb):
        return pl.ds(qb * TQ if isinstance(qb, int) else pl.multiple_of(qb * TQ, TQ), TQ)

    def masked_q(qb):
        q = q_ref[0, tile_rows(qb), :]
        zero = jnp.zeros_like(q)
        return q, (jnp.where(first_half, q, zero), jnp.where(first_half, zero, q))

    def finish(ot, qb):
        ms = jnp.mean(ot * ot, axis=0, keepdims=True)
        y = ot * lax.rsqrt(ms + NORM_EPS) * g_ref[...] * (1.0 - LAM_INIT)
        o_ref[0, tile_rows(qb), :] = y.T.astype(BF16)

    def exponentials(qb, slot):
        w0 = (S - TQ) - qb * TQ
        q, qz = masked_q(qb)
        qsq = (q.astype(F32) * q.astype(F32)).astype(BF16)
        n2 = lax.dot_general(sel, qsq, nt, preferred_element_type=F32) * 1.01
        m = [jnp.sqrt(n2[i:i + 1]) * kmax + bmax for i in range(2)]
        m_lo = [mi - bstat_ref[hh, 1] for mi in m]
        m_hi = [mi - bstat_ref[hh, 2] for mi in m]
        l = [jnp.zeros((8, TQ), F32) for _ in range(2)]
        for c in range(nk):
            rows = slice(c * KB, (c + 1) * KB)
            kb = k_ref[0, rows, :]
            t0 = w0 + c * KB
            below, above = t0 + KB <= band_lo, t0 >= band_lo + BIAS_BAND
            bias = None if below or above else t_ref[t0:t0 + KB, :]
            for i in range(2):
                s = lax.dot_general(kb, qz[i], nt, preferred_element_type=F32)
                if below:
                    p = jnp.exp2(s - m_lo[i])
                elif above:
                    p = jnp.exp2(s - m_hi[i])
                else:
                    p = jnp.exp2((s - m[i]) + bias)
                l[i] = l[i] + jnp.sum(p.reshape(KB // 8, 8, TQ), axis=0)
                p_ref[slot, i, rows, :] = p.astype(BF16)
        return [jnp.sum(l[i], axis=0, keepdims=True) for i in range(2)]

    def values(qb, slot, lr):
        ratio = jnp.broadcast_to(lam * lr[0] / lr[1], (16, TQ)).astype(BF16)
        ot = jnp.zeros((DIFF_DV, TQ), F32)
        for c in range(nk):
            rows = slice(c * KB, (c + 1) * KB)
            p1 = p_ref[slot, 0, rows, :].reshape(KB // 16, 16, TQ)
            p2 = p_ref[slot, 1, rows, :].reshape(KB // 16, 16, TQ)
            a = (p1 - ratio * p2).reshape(KB, TQ)
            ot = ot + jnp.dot(vt_ref[0, :, rows], a, preferred_element_type=F32)
        finish(ot * (1.0 / lr[0]), qb)

    n_tiles = S // TQ
    lmin = jnp.full((1, TQ), jnp.inf, F32)
    sums = exponentials(0, 0)
    for t in range(n_tiles):
        lmin = jnp.minimum(lmin, jnp.minimum(sums[0], sums[1]))
        nxt = exponentials(t + 1, (t + 1) % 2) if t + 1 < n_tiles else None
        values(t, t % 2, sums)
        sums = nxt

    @pl.when(jnp.min(lmin) < L_MIN)
    def _():
        def exact_tile(qb, carry):
            w0 = pl.multiple_of((S - TQ) - qb * TQ, TQ)
            _, qz = masked_q(qb)
            bias = t_ref[pl.ds(w0, S), :]
            attn = None
            for i, scale in ((0, 1.0), (1, lam)):
                s = lax.dot_general(k_ref[0], qz[i], nt, preferred_element_type=F32) + bias
                p = jnp.exp2(s - jnp.max(s, axis=0, keepdims=True))
                part = p * (scale / jnp.sum(p, axis=0, keepdims=True))
                attn = part if attn is None else attn - part
            finish(jnp.dot(vt_ref[0], attn.astype(BF16), preferred_element_type=F32), qb)
            return carry

        lax.fori_loop(0, S // TQ, exact_tile, 0)


def _diff(proj, vt, lam_vecs, brow, bstat, g):
    B, S, _ = proj.shape
    H = DIFF_HEADS

    def col(base):
        return pl.BlockSpec((1, S, LANES), lambda h, b: (b, 0, base + h))

    return pl.pallas_call(
        _diff_kernel,
        grid=(H, B),
        in_specs=[pl.BlockSpec(memory_space=pltpu.SMEM),
                  pl.BlockSpec((4, DIFF_DH), lambda h, b: (0, 0)),
                  col(COL_DQ), col(COL_DK),
                  pl.BlockSpec((1, DIFF_DV, S), lambda h, b: (b, h, 0)),
                  pl.BlockSpec((1, 1, brow.shape[2]), lambda h, b: (h, 0, 0)),
                  pl.BlockSpec((DIFF_DV, Q_TILE), lambda h, b: (0, 0))],
        out_specs=pl.BlockSpec((1, S, LANES), lambda h, b: (b, 0, h)),
        out_shape=jax.ShapeDtypeStruct((B, S, H * DIFF_DV), BF16),
        scratch_shapes=[pltpu.VMEM((2, 2, S, Q_TILE), BF16), pltpu.VMEM((2 * S - Q_TILE, Q_TILE), F32)],
        compiler_params=_cparams(("arbitrary", "arbitrary")),
        name="diff",
    )(bstat, lam_vecs, proj, proj, vt, brow, g)


def _pack_halves(x):
    w = x.shape[1] // 2
    lo = pltpu.bitcast(x[:, :w].astype(BF16).astype(F32), jnp.int32)
    hi = pltpu.bitcast(x[:, w:].astype(BF16).astype(F32), jnp.int32)
    return lax.shift_right_logical(lo, jnp.full_like(lo, 16)) | (hi & jnp.int32(-65536))


def _unpack_halves(words):
    lo = pltpu.bitcast(lax.shift_left(words, jnp.full_like(words, 16)), F32)
    hi = pltpu.bitcast(words & jnp.int32(-65536), F32)
    return jnp.concatenate([lo, hi], axis=1).astype(BF16)


def _merge_kernel(x_ref, yr_ref, yd_ref, mod_ref, gmix_ref, wg_ref, wr_ref, wd_ref, wo_ref,
                  g_ref, wrt_ref, x1_ref, h2_ref, lg_ref):
    parts = 4
    gate_w = wg_ref.shape[1] // 2
    step = x_ref.shape[1] // parts

    def project(r):
        rows = slice(r * step, (r + 1) * step)
        h = _rms(x_ref[0, rows, :], gmix_ref[...]) * (1.0 + mod_ref[0, 1:2, :]) + mod_ref[0, 0:1, :]
        gates = jax.nn.sigmoid(jnp.dot(h.astype(BF16), wg_ref[...], preferred_element_type=F32))
        a = jnp.dot(yr_ref[0, rows, :], wr_ref[...], preferred_element_type=F32)
        d = jnp.dot(yd_ref[0, rows, :], wd_ref[...], preferred_element_type=F32)
        merged = gates[:, :gate_w] * a + gates[:, gate_w:] * d
        return jnp.dot(merged.astype(BF16), wo_ref[...], preferred_element_type=F32)

    def epilogue(r, o):
        rows = slice(r * step, (r + 1) * step)
        x1 = x_ref[0, rows, :] + mod_ref[0, 2:3, :] * o
        x1_ref[0, rows, :] = x1
        h2 = _rms(x1, g_ref[...]) * (1.0 + mod_ref[0, 4:5, :]) + mod_ref[0, 3:4, :]
        h2_ref[0, rows, :] = _pack_halves(h2)
        lg_ref[0, :, rows] = lax.dot_general(wrt_ref[...], h2.astype(BF16), (((1,), (1,)), ((), ())),
                                             preferred_element_type=F32)

    o = project(0)
    for r in range(parts):
        nxt = project(r + 1) if r + 1 < parts else None
        epilogue(r, o)
        o = nxt


def _merge(x, yr, yd, mod, gmix, wg, wr, wd, wo, g, wrt):
    B, S, D = x.shape
    tm = PROJ_TILE
    E = wrt.shape[0]
    return pl.pallas_call(
        _merge_kernel,
        grid=(B, S // tm),
        in_specs=[pl.BlockSpec((1, tm, D), lambda b, i: (b, i, 0)),
                  pl.BlockSpec((1, tm, yr.shape[2]), lambda b, i: (b, i, 0)),
                  pl.BlockSpec((1, tm, yd.shape[2]), lambda b, i: (b, i, 0)),
                  pl.BlockSpec((1, 6, D), lambda b, i: (b, 0, 0)),
                  _resident((1, D), lambda b, i: (0, 0)),
                  _resident(wg.shape, lambda b, i: (0, 0)),
                  _resident(wr.shape, lambda b, i: (0, 0)),
                  _resident(wd.shape, lambda b, i: (0, 0)),
                  _resident(wo.shape, lambda b, i: (0, 0)),
                  _resident((1, D), lambda b, i: (0, 0)),
                  _resident(wrt.shape, lambda b, i: (0, 0))],
        out_specs=[pl.BlockSpec((1, tm, D), lambda b, i: (b, i, 0)),
                   pl.BlockSpec((1, tm, D // 2), lambda b, i: (b, i, 0)),
                   pl.BlockSpec((1, E, tm), lambda b, i: (b, 0, i))],
        out_shape=[jax.ShapeDtypeStruct((B, S, D), F32),
                   jax.ShapeDtypeStruct((B, S, D // 2), jnp.int32),
                   jax.ShapeDtypeStruct((B, E, S), F32)],
        compiler_params=_cparams(("arbitrary", "arbitrary")),
        name="merge",
    )(x, yr, yd, mod, gmix, wg, wr, wd, wo, g, wrt)


def _lane_prefix(m, tri):
    E, S = m.shape
    off = jnp.zeros((E, 1), F32)
    parts = []
    for j in range(S // LANES):
        blk = m[:, j * LANES:(j + 1) * LANES]
        parts.append(jnp.dot(blk.astype(BF16), tri, preferred_element_type=F32) + off)
        off = off + jnp.sum(blk, axis=1, keepdims=True)
    return jnp.concatenate(parts, axis=1)


def _route_kernel(lg_ref, rank_ref, gate_ref, rank_t_ref, win_ref, ok_ref, *, cap):
    E, S = lg_ref.shape[1], lg_ref.shape[2]
    tiles = win_ref.shape[2]
    tm = S // tiles
    lane = lax.broadcasted_iota(jnp.int32, (E, LANES), 1)
    for j in range(lg_ref.shape[0]):
        rank, gate_ref[j], before = _route_one(lg_ref[j], cap)
        start = jnp.zeros((E, LANES), F32)
        end = jnp.full((E, LANES), float(cap), F32)
        for i in range(tiles):
            first = before[:, i * tm:i * tm + 1]
            start = jnp.where(lane == i, first, start)
            if i > 0:
                end = jnp.where(lane == i - 1, first, end)
        win = jnp.minimum(jnp.floor(start * (1.0 / BF16_SUBLANES)) * BF16_SUBLANES, float(cap - SCAT_WINDOW))
        fits = jnp.min(jnp.where(end - win <= SCAT_WINDOW, 1.0, 0.0), axis=0, keepdims=True)
        win_ref[j] = win[:, :tiles].astype(jnp.int32)
        ok_ref[j] = fits[:, :tiles].astype(jnp.int32)
        rank_ref[j] = rank.astype(jnp.int32)
        padded = jnp.concatenate([rank, jnp.full((LANES - E, S), -1.0, F32)], axis=0)
        rank_t_ref[j] = padded.T.astype(jnp.int32)


def _route_one(lg, cap):
    e = jnp.exp(lg - jnp.max(lg, axis=0, keepdims=True))
    aff = e / jnp.sum(e, axis=0, keepdims=True)
    bits = pltpu.bitcast(aff, jnp.int32)
    E = lg.shape[0]

    def count(mask):
        return jnp.sum(jnp.where(mask, 1.0, 0.0), axis=1, keepdims=True)

    def reaches(cand):
        return count(bits >= cand) >= cap

    thr = jnp.zeros((E, 1), jnp.int32)
    for hi in range(30, 0, -2):
        c1, c2 = thr | (1 << hi), thr | (1 << (hi - 1))
        c3 = c1 | (1 << (hi - 1))
        thr = jnp.where(reaches(c3), c3, jnp.where(reaches(c1), c1, jnp.where(reaches(c2), c2, thr)))
    thr = jnp.where(reaches(thr | 1), thr | 1, thr)
    gt = bits > thr
    eq = bits == thr
    need = cap - count(gt)
    r = lax.broadcasted_iota(jnp.int32, (LANES, LANES), 0)
    c = lax.broadcasted_iota(jnp.int32, (LANES, LANES), 1)
    tri = jnp.where(r < c, 1.0, 0.0).astype(BF16)
    eq_before = _lane_prefix(jnp.where(eq, 1.0, 0.0), tri)
    sel = gt | (eq & (eq_before < need))
    slot = _lane_prefix(jnp.where(sel, 1.0, 0.0), tri)
    return jnp.where(sel, slot, -1.0), jnp.where(sel, aff, 0.0), slot


def _route(logits, cap):
    B, E, S = logits.shape
    G = ROUTE_SEQS_PER_STEP
    tiles = S // TOKEN_TILE
    assert B % G == 0
    spec = pl.BlockSpec((G, E, S), lambda b: (b, 0, 0))
    return pl.pallas_call(
        functools.partial(_route_kernel, cap=cap),
        grid=(B // G,),
        in_specs=[spec],
        out_specs=[spec, spec, pl.BlockSpec((G, S, LANES), lambda b: (b, 0, 0)),
                   pl.BlockSpec((G, E, tiles), lambda b: (b, 0, 0)),
                   pl.BlockSpec((G, 1, tiles), lambda b: (b, 0, 0))],
        out_shape=[jax.ShapeDtypeStruct((B, E, S), jnp.int32),
                   jax.ShapeDtypeStruct((B, E, S), F32),
                   jax.ShapeDtypeStruct((B, S, LANES), jnp.int32),
                   jax.ShapeDtypeStruct((B, E, tiles), jnp.int32),
                   jax.ShapeDtypeStruct((B, 1, tiles), jnp.int32)],
        compiler_params=_cparams(("arbitrary",)),
        name="route",
    )(logits)


def _gather_rows(table, rank, cap):
    B, E, S = rank.shape
    W = table.shape[1]
    workers = SC_CORES * SC_SUBCORES
    pairs = E * B
    per = -(-pairs // workers)
    assert B & (B - 1) == 0 and cap % SC_GATHER_ROWS == 0 and S % SC_LANES == 0
    shift = B.bit_length() - 1
    mesh = plsc.VectorSubcoreMesh(core_axis_name="c", subcore_axis_name="s")

    R = SC_GATHER_ROWS
    n_chunks = cap // R

    def body(table_hbm, rank_hbm, out_hbm, rank_v, idx_v, buf_v, gsem, wsem):
        wid = lax.axis_index("s") * SC_CORES + lax.axis_index("c")

        def fetch(c):
            return pltpu.make_async_copy(table_hbm.at[idx_v.at[pl.ds(c * R, R)]], buf_v.at[c % 2], gsem.at[c % 2])

        def flush(p, c):
            return pltpu.make_async_copy(buf_v.at[c % 2], out_hbm.at[pl.ds(p * cap + c * R, R)], wsem.at[c % 2])

        def one_pair(p):
            e = lax.shift_right_logical(p, shift)
            b = p & (B - 1)
            pltpu.sync_copy(rank_hbm.at[b, e], rank_v)
            base = b * S

            @pl.loop(0, S, step=SC_LANES)
            def _(t0):
                r = rank_v[pl.ds(t0, SC_LANES)]
                tok = lax.iota(jnp.int32, SC_LANES) + (base + t0)
                plsc.store_scatter(idx_v, [r], tok, mask=r >= 0)

            fetch(0).start()
            for c in range(n_chunks):
                fetch(c).wait()
                if c >= 1:
                    flush(p, c - 1).wait()
                if c + 1 < n_chunks:
                    fetch(c + 1).start()
                flush(p, c).start()
            flush(p, n_chunks - 1).wait()

        for j in range(per):
            p = wid * per + j
            pl.when(p < pairs)(functools.partial(one_pair, p))

    return pl.kernel(
        body,
        out_type=jax.ShapeDtypeStruct((pairs * cap, W), jnp.int32),
        mesh=mesh,
        scratch_types=[pltpu.VMEM((S,), jnp.int32), pltpu.VMEM((cap,), jnp.int32),
                       pltpu.VMEM((2, R, W), jnp.int32),
                       pltpu.SemaphoreType.DMA((2,)), pltpu.SemaphoreType.DMA((2,))],
        compiler_params=pltpu.CompilerParams(needs_layout_passes=False),
    )(table, rank)


def _moe_kernel(rank_ref, gate_ref, x_ref, wg_ref, wu_ref, wd_ref, o_ref, wg_s, wu_s, wd_s, *, cap):
    r, b = pl.program_id(0), pl.program_id(1)
    n_exp = pl.num_programs(0) - 1

    @pl.when(r < n_exp)
    def _():
        slot = r % 2
        rows_in, rows_ff = wg_ref.shape[1], wd_ref.shape[1]
        wg_s[slot, pl.ds(pl.multiple_of(b * rows_in, rows_in), rows_in), :] = wg_ref[0].astype(BF16)
        wu_s[slot, pl.ds(pl.multiple_of(b * rows_in, rows_in), rows_in), :] = wu_ref[0].astype(BF16)
        wd_s[slot, pl.ds(pl.multiple_of(b * rows_ff, rows_ff), rows_ff), :] = wd_ref[0].astype(BF16)

    @pl.when(r == 0)
    def _():
        o_ref[...] = jnp.zeros_like(o_ref)

    @pl.when(r > 0)
    def _():
        slot = (r - 1) % 2
        S = rank_ref.shape[3]
        for j in range(x_ref.shape[0]):
            rank = rank_ref[j, 0]
            pick = lax.broadcasted_iota(jnp.int32, (cap, S), 0) == rank
            xin = _unpack_halves(x_ref[j])
            a = jnp.dot(xin, wg_s[slot], preferred_element_type=F32)
            u = jnp.dot(xin, wu_s[slot], preferred_element_type=F32)
            act = (a * jax.nn.sigmoid(a) * u).astype(BF16)
            y = jnp.dot(act, wd_s[slot], preferred_element_type=F32)
            g = jnp.sum(jnp.where(pick, gate_ref[j, 0], 0.0), axis=1, keepdims=True)
            o_ref[0, j] = (y * g).astype(BF16)


def _moe(rank, gate, xin, wg, wu, wd, cap, e0, n_e):
    B, E, S = rank.shape
    D, Fd = wg.shape[1], wg.shape[2]
    G = MOE_SEQS_PER_STEP
    steps = B // G
    assert steps * G == B and D % steps == 0 and Fd % steps == 0

    def cur(r):
        return e0 + jnp.maximum(r - 1, 0)

    def nxt(r):
        return e0 + jnp.minimum(r, n_e - 1)

    row = pl.BlockSpec((G, 1, 1, S), lambda r, b: (b, cur(r), 0, 0))
    return pl.pallas_call(
        functools.partial(_moe_kernel, cap=cap),
        grid=(n_e + 1, steps),
        in_specs=[row, row,
                  pl.BlockSpec((G, cap, D // 2), lambda r, b: (jnp.maximum(r - 1, 0) * steps + b, 0, 0)),
                  pl.BlockSpec((1, D // steps, Fd), lambda r, b: (nxt(r), b, 0)),
                  pl.BlockSpec((1, D // steps, Fd), lambda r, b: (nxt(r), b, 0)),
                  pl.BlockSpec((1, Fd // steps, D), lambda r, b: (nxt(r), b, 0))],
        out_specs=pl.BlockSpec((1, G, cap, D), lambda r, b: (jnp.where(r == 0, n_e, r - 1), b, 0, 0)),
        out_shape=jax.ShapeDtypeStruct((n_e + 1, B, cap, D), BF16),
        scratch_shapes=[pltpu.VMEM((2, D, Fd), BF16), pltpu.VMEM((2, D, Fd), BF16),
                        pltpu.VMEM((2, Fd, D), BF16)],
        compiler_params=_cparams(("arbitrary", "arbitrary")),
        name="moe",
    )(rank.reshape(B, E, 1, S), gate.reshape(B, E, 1, S), xin.reshape(n_e * B, cap, D // 2), wg, wu, wd)


def _scat_kernel(win_ref, ok_ref, rt_ref, *refs, cap):
    *y_refs, x1_ref, mod_ref, g_ref, o_ref = refs
    b, i = pl.program_id(0), pl.program_id(1)
    rt = rt_ref[0]
    tm, E = rt.shape[0], win_ref.shape[1]
    D = x1_ref.shape[2]
    sizes = [y.shape[0] - 1 for y in y_refs]
    home = [(g, e) for g, n in enumerate(sizes) for e in range(n)]

    def finish(moe):
        x2 = x1_ref[0] + mod_ref[0, 5:6, :] * moe
        o_ref[0] = _rms(x2, g_ref[...])

    @pl.when(ok_ref[b, 0, i] != 0)
    def _():
        slot = lax.broadcasted_iota(jnp.int32, (tm, SCAT_WINDOW), 1)
        moe = jnp.zeros((tm, D), F32)
        for e in range(0, E, 2):
            starts = [pl.multiple_of(win_ref[b, e + k, i], BF16_SUBLANES) for k in range(2)]
            place = jnp.concatenate(
                [jnp.where(rt[:, e + k:e + k + 1] - starts[k] == slot, 1.0, 0.0).astype(BF16)
                 for k in range(2)], axis=1)
            y = jnp.concatenate([y_refs[home[e + k][0]][home[e + k][1], 0, pl.ds(starts[k], SCAT_WINDOW), :]
                                 for k in range(2)], axis=0)
            moe = moe + jnp.dot(place, y, preferred_element_type=F32)
        finish(moe)

    @pl.when(ok_ref[b, 0, i] == 0)
    def _():
        slot = lax.broadcasted_iota(jnp.int32, (tm, cap), 1)
        place = jnp.concatenate(
            [jnp.where(rt[:, e:e + 1] == slot, 1.0, 0.0).astype(BF16) for e in range(E)], axis=1)
        y = jnp.concatenate([y_ref[0:n, 0].reshape(n * cap, D) for y_ref, n in zip(y_refs, sizes)], axis=0)
        finish(jnp.dot(place, y, preferred_element_type=F32))


def _scat(rank_t, win, ok, ybufs, x1, mod, g, cap):
    B, S, D = x1.shape
    tm = TOKEN_TILE
    smem = pl.BlockSpec(memory_space=pltpu.SMEM)
    return pl.pallas_call(
        functools.partial(_scat_kernel, cap=cap),
        grid=(B, S // tm),
        in_specs=[smem, smem,
                  pl.BlockSpec((1, tm, LANES), lambda b, i: (b, i, 0)),
                  *[pl.BlockSpec(y.shape[:1] + (1, cap, D), lambda b, i: (0, b, 0, 0)) for y in ybufs],
                  pl.BlockSpec((1, tm, D), lambda b, i: (b, i, 0)),
                  pl.BlockSpec((1, 6, D), lambda b, i: (b, 0, 0)),
                  _resident((1, D), lambda b, i: (0, 0))],
        out_specs=pl.BlockSpec((1, tm, D), lambda b, i: (b, i, 0)),
        out_shape=jax.ShapeDtypeStruct((B, S, D), F32),
        compiler_params=_cparams(("arbitrary", "arbitrary")),
        name="scat",
    )(win, ok, rank_t, *ybufs, x1, mod, g)


def _rot_tables(S):
    half = RET_DK // 2
    inv = 1.0 / (RET_THETA_BASE ** np.linspace(0.0, 1.0, half))
    ang = np.arange(S, dtype=np.float64)[:, None] * inv[None, :]
    cos = np.repeat(np.cos(ang), 2, axis=1)
    sin = np.stack([-np.sin(ang), np.sin(ang)], axis=2).reshape(S, RET_DK)
    sc = RET_DK ** -0.5
    return jnp.asarray(np.stack([cos, sin, cos * sc, sin * sc]), F32)


def _ret_tables():
    C = RET_CHUNK
    heads = np.arange(RET_HEADS, dtype=np.float64)
    lgf = np.log1p(-np.exp2(-RET_FWD_DECAY_OFFSET - heads))[:, None]
    lgb = np.log1p(-np.exp2(-RET_BWD_DECAY_OFFSET - heads))[:, None]
    idx = np.arange(C, dtype=np.float64)
    diff = idx[:, None] - idx[None, :]
    dmat = np.where(diff >= 0,
                    np.exp(np.maximum(diff, 0.0)[None] * lgf[:, :, None]),
                    np.exp(np.maximum(-diff, 0.0)[None] * lgb[:, :, None]))
    dec = np.stack([np.exp((idx + 1)[None, :] * lgf),
                    np.exp((C - 1 - idx)[None, :] * lgf),
                    np.exp((C - idx)[None, :] * lgb),
                    np.exp(idx[None, :] * lgb)], axis=1)
    dec = np.broadcast_to(dec[..., None], dec.shape + (LANES,))
    cd = np.concatenate([np.exp(C * lgf), np.exp(C * lgb)], axis=1)
    return jnp.asarray(cd, F32), jnp.asarray(dec, F32), jnp.asarray(dmat, F32)


def _t5_bucket(rel):
    nb = N_BUCKETS // 2
    max_exact = nb // 2
    ret = (rel > 0).astype(jnp.int32) * nb
    n = jnp.abs(rel)
    large = max_exact + (jnp.log(jnp.maximum(n, 1).astype(F32) / max_exact)
                         / math.log(MAX_DISTANCE / max_exact) * (nb - max_exact)).astype(jnp.int32)
    large = jnp.minimum(large, nb - 1)
    return ret + jnp.where(n < max_exact, n, large)


def _bias_rows(rel_bias):
    reach = BIAS_BAND - 1
    rel = jnp.concatenate([jnp.arange(reach, -reach - 1, -1, dtype=jnp.int32),
                           jnp.array([-(2 ** 20), 2 ** 20], jnp.int32)])
    f = rel_bias[_t5_bucket(rel)].astype(F32).T * LOG2E
    rows = jnp.pad(f[:, :-2], ((0, 0), (0, 2 * BIAS_BAND - (2 * reach + 1))))
    stats = jnp.stack([jnp.max(f, axis=1), f[:, -2], f[:, -1]], axis=1)
    return rows[:, None, :], stats


def kernel(x, c, w_ada, b_ada, norm_mix_g, w_in, ret_gn_g, diff_subln_g, lambda_q1, lambda_k1, lambda_q2, lambda_k2, w_ret_out, w_diff_out, w_o, rel_bias, norm_ffn_g, w_router, w_exp_gate, w_exp_up, w_exp_down, final_g):
    B, S, D = x.shape
    cap = CAPACITY_FACTOR * S // N_EXPERTS
    l = 0

    mod = _ada(c, w_ada[l], b_ada[l]).reshape(B, 6, D)
    w_in_b = w_in[l].astype(BF16)
    proj, vt = _inproj(x, norm_mix_g[l].reshape(1, D), mod, w_in_b, _rot_tables(S))

    cd, dec, dmat = _ret_tables()
    y_ret = _ret(proj, cd, dec, dmat, ret_gn_g[l].reshape(1, -1))

    lam_vecs = jnp.stack([lambda_q1[l], lambda_k1[l], lambda_q2[l], lambda_k2[l]]).astype(F32)
    brow, bstat = _bias_rows(rel_bias)
    subln = jnp.broadcast_to(diff_subln_g[l].astype(F32)[:, None], (DIFF_DV, Q_TILE))
    y_diff = _diff(proj, vt, lam_vecs, brow, bstat, subln)

    x1, h2, logits = _merge(x, y_ret, y_diff, mod, norm_mix_g[l].reshape(1, D), w_in_b[:, W_GR * LANES:],
                            w_ret_out[l].astype(BF16), w_diff_out[l].astype(BF16), w_o[l].astype(BF16),
                            norm_ffn_g[l].reshape(1, D), w_router[l].T.astype(BF16))

    rank, gate, rank_t, win, ok = _route(logits, cap)
    table, ybufs, e0 = h2.reshape(B * S, D // 2), [], 0
    for n_e in MOE_GROUP_SIZES:
        xin = _gather_rows(table, rank[:, e0:e0 + n_e], cap)
        ybufs.append(_moe(rank, gate, xin, w_exp_gate[l], w_exp_up[l], w_exp_down[l], cap, e0, n_e))
        e0 += n_e
    return _scat(rank_t, win, ok, ybufs, x1, mod, final_g.reshape(1, D), cap)
```

```python
import functools
import math

import numpy as np
import jax
import jax.numpy as jnp
from jax import lax
from jax.experimental import pallas as pl
from jax.experimental.pallas import tpu as pltpu
from jax.experimental.pallas import tpu_sc as plsc

F32 = jnp.float32
BF16 = jnp.bfloat16

RET_HEADS = 4
RET_DK = 128
RET_DV = 128
RET_FWD_DECAY_OFFSET = 5.0
RET_BWD_DECAY_OFFSET = 5.5
RET_THETA_BASE = 10000.0
DIFF_HEADS = 4
DIFF_DH = 64
DIFF_DV = 2 * DIFF_DH
N_BUCKETS = 32
MAX_DISTANCE = 128
N_EXPERTS = 16
CAPACITY_FACTOR = 2
NORM_EPS = 1e-6
LAM_INIT = 0.8 - 0.6 * math.exp(-0.3 * 0)
LOG2E = math.log2(math.e)

LANES = 128
BF16_SUBLANES = 16
VMEM_LIMIT_BYTES = 56 * 2**20

SC_CORES = 2
SC_SUBCORES = 16
SC_LANES = 16
SC_GATHER_ROWS = 32
SC_GATHER_BUFS = 4

TOKEN_TILE = 256
PROJ_TILE = 1024
RET_CHUNK = 256
Q_TILE = 256
KEY_CHUNK = 512
L_MIN = 2.0 ** -60
BIAS_BAND = Q_TILE + 2 * MAX_DISTANCE
SCAT_WINDOW = 64
SCAT_DEPTH = 256
MOE_SEQS_PER_STEP = 4
ROUTE_SEQS_PER_STEP = 4
MOE_GROUP_SIZES = (4, 12)

W_RQ, W_RK, W_RV, W_RG = 0, 4, 8, 12
W_DQ1, W_DQ2, W_DK1, W_DK2, W_DV = 16, 18, 20, 22, 24
W_GR, W_GD = 28, 36

COL_RQ, COL_RK, COL_RV, COL_RG = 0, 4, 8, 12
COL_DQ, COL_DK = 16, 20
PROJ_COLS = 24 * LANES


def _cparams(sem):
    return pltpu.CompilerParams(dimension_semantics=sem, vmem_limit_bytes=VMEM_LIMIT_BYTES)


def _resident(shape, index_map):
    return pl.BlockSpec(shape, index_map, pipeline_mode=pl.Buffered(1))


def _ada_kernel(c_ref, w_ref, b_ref, o_ref):
    c = c_ref[...]
    a = c * jax.nn.sigmoid(c)
    w = w_ref[...]
    a_hi, w_hi = a.astype(BF16), w.astype(BF16)
    a_lo = (a - a_hi.astype(F32)).astype(BF16)
    w_lo = (w - w_hi.astype(F32)).astype(BF16)
    p = jnp.dot(jnp.concatenate([a_hi, a_lo], axis=0), w_hi, preferred_element_type=F32)
    q = jnp.dot(a_hi, w_lo, preferred_element_type=F32)
    n = a.shape[0]
    o_ref[...] = p[:n] + p[n:] + q + b_ref[...]


def _ada(c, w, b):
    B, D = c.shape
    n = w.shape[1] // D
    return pl.pallas_call(
        _ada_kernel,
        grid=(n,),
        in_specs=[pl.BlockSpec((B, D), lambda j: (0, 0)),
                  pl.BlockSpec((D, D), lambda j: (0, j)),
                  pl.BlockSpec((1, D), lambda j: (0, j))],
        out_specs=pl.BlockSpec((B, D), lambda j: (0, j)),
        out_shape=jax.ShapeDtypeStruct((B, n * D), F32),
        compiler_params=_cparams(("arbitrary",)),
        name="ada",
    )(c, w, b.reshape(1, -1))


def _rms(x, g):
    ms = jnp.mean(x * x, axis=-1, keepdims=True)
    return x * lax.rsqrt(ms + NORM_EPS) * g


def _rot_pairs(x, cos, sin):
    even = (lax.broadcasted_iota(jnp.int32, x.shape, 1) & 1) == 0
    partner = jnp.where(even, pltpu.roll(x, LANES - 1, axis=1), pltpu.roll(x, 1, axis=1))
    return x * cos + partner * sin


def _pair_maps(a, b, second):
    low = lax.broadcasted_iota(jnp.int32, a.shape, 1) < DIFF_DH
    if second:
        return jnp.where(low, pltpu.roll(a, DIFF_DH, axis=1), b)
    return jnp.where(low, a, pltpu.roll(b, DIFF_DH, axis=1))


def _inproj_kernel(x_ref, g_ref, mod_ref, w_ref, rot_ref, o_ref, vt_ref, wvt_s):
    @pl.when((pl.program_id(0) == 0) & (pl.program_id(1) == 0))
    def _():
        wvt_s[...] = w_ref[:, W_DV * LANES:(W_DV + 4) * LANES].astype(F32).T.astype(BF16)

    parts = 4
    step = x_ref.shape[1] // parts

    def normed(p):
        x = x_ref[0, p * step:(p + 1) * step, :]
        h = _rms(x, g_ref[...]) * (1.0 + mod_ref[0, 1:2, :]) + mod_ref[0, 0:1, :]
        return h.astype(BF16)

    def project(p, hb):
        rows = slice(p * step, (p + 1) * step)

        def mm(col, width):
            return jnp.dot(hb, w_ref[:, col * LANES:(col + width) * LANES], preferred_element_type=F32)

        def put(col, val):
            o_ref[0, rows, col * LANES:col * LANES + val.shape[1]] = val.astype(BF16)

        for src, dst, t in ((W_RQ, COL_RQ, 0), (W_RK, COL_RK, 2)):
            cos, sin = rot_ref[t, rows, :], rot_ref[t + 1, rows, :]
            r = mm(src, RET_HEADS)
            for hh in range(RET_HEADS):
                put(dst + hh, _rot_pairs(r[:, hh * LANES:(hh + 1) * LANES], cos, sin))
        put(COL_RV, mm(W_RV, 4))
        r = mm(W_RG, 4)
        put(COL_RG, r * jax.nn.sigmoid(r))
        for src1, src2, dst, scale in ((W_DQ1, W_DQ2, COL_DQ, DIFF_DH ** -0.5 * LOG2E),
                                       (W_DK1, W_DK2, COL_DK, None)):
            m1, m2 = mm(src1, 2), mm(src2, 2)
            for hh in range(DIFF_HEADS):
                blk = slice((hh // 2) * LANES, (hh // 2 + 1) * LANES)
                val = _pair_maps(m1[:, blk], m2[:, blk], hh % 2 == 1)
                put(dst + hh, val if scale is None else val * scale)
        vt_ref[0, :, rows] = lax.dot_general(wvt_s[...], hb, (((1,), (1,)), ((), ())),
                                             preferred_element_type=F32).astype(BF16)

    hb = normed(0)
    for p in range(parts):
        nxt = normed(p + 1) if p + 1 < parts else None
        project(p, hb)
        hb = nxt


def _inproj(x, g, mod, w, rot):
    B, S, D = x.shape
    tm = PROJ_TILE
    vw = DIFF_HEADS * DIFF_DV
    return pl.pallas_call(
        _inproj_kernel,
        grid=(B, S // tm),
        in_specs=[pl.BlockSpec((1, tm, D), lambda b, i: (b, i, 0)),
                  _resident((1, D), lambda b, i: (0, 0)),
                  pl.BlockSpec((1, 6, D), lambda b, i: (b, 0, 0)),
                  _resident(w.shape, lambda b, i: (0, 0)),
                  pl.BlockSpec((4, tm, LANES), lambda b, i: (0, i, 0))],
        out_specs=[pl.BlockSpec((1, tm, PROJ_COLS), lambda b, i: (b, i, 0)),
                   pl.BlockSpec((1, vw, tm), lambda b, i: (b, 0, i))],
        out_shape=[jax.ShapeDtypeStruct((B, S, PROJ_COLS), BF16),
                   jax.ShapeDtypeStruct((B, vw, S), BF16)],
        scratch_shapes=[pltpu.VMEM((vw, D), BF16)],
        compiler_params=_cparams(("arbitrary", "arbitrary")),
        name="inproj",
    )(x, g, mod, w, rot)


def _ret_kernel(cd_ref, q_ref, k_ref, v_ref, rg_ref, dec_ref, dm_ref, gn_ref, o_ref, acc_ref):
    S = q_ref.shape[1]
    H = dm_ref.shape[0]
    C = RET_CHUNK
    nc = S // C
    nt = (((1,), (1,)), ((), ()))
    tn = (((0,), (0,)), ((), ()))

    def chunk(n, h):
        sl, lanes = pl.ds(n * C, C), slice(h * LANES, (h + 1) * LANES)
        return sl, q_ref[0, sl, lanes], k_ref[0, sl, lanes], v_ref[0, sl, lanes]

    def scaled(t, dec):
        return (t.astype(F32) * dec).astype(BF16)

    def forward(heads):
        state = {h: jnp.zeros((RET_DK, RET_DV), F32) for h in heads}
        for n in range(nc):
            for h in heads:
                sl, q, k, v = chunk(n, h)
                s = lax.dot_general(q, k, nt, preferred_element_type=F32) * dm_ref[h]
                inner = jnp.dot(s.astype(BF16), v, preferred_element_type=F32)
                cross = jnp.dot(scaled(q, dec_ref[h, 0]), state[h].astype(BF16), preferred_element_type=F32)
                acc_ref[h, sl, :] = inner + cross
                kv = lax.dot_general(scaled(k, dec_ref[h, 1]), v, tn, preferred_element_type=F32)
                state[h] = cd_ref[h, 0] * state[h] + kv

    def backward(heads):
        state = {h: jnp.zeros((RET_DK, RET_DV), F32) for h in heads}
        for n in reversed(range(nc)):
            for h in heads:
                sl, q, k, v = chunk(n, h)
                cross = jnp.dot(scaled(q, dec_ref[h, 2]), state[h].astype(BF16), preferred_element_type=F32)
                acc_ref[h, sl, :] = acc_ref[h, sl, :] + cross
                kv = lax.dot_general(scaled(k, dec_ref[h, 3]), v, tn, preferred_element_type=F32)
                state[h] = cd_ref[h, 1] * state[h] + kv

    def normalise(heads):
        for h in heads:
            lanes = slice(h * LANES, (h + 1) * LANES)
            y = acc_ref[h]
            mu = jnp.mean(y, axis=-1, keepdims=True)
            yc = y - mu
            var = jnp.mean(yc * yc, axis=-1, keepdims=True)
            yn = yc * lax.rsqrt(var + NORM_EPS) * gn_ref[:, lanes]
            o_ref[0, :, lanes] = (rg_ref[0, :, lanes].astype(F32) * yn).astype(BF16)

    first, second = tuple(range(H // 2)), tuple(range(H // 2, H))
    forward(first)
    backward(first)
    forward(second)
    normalise(first)
    backward(second)
    normalise(second)


def _ret(proj, cd, dec, dmat, gn):
    B, S, _ = proj.shape
    H = RET_HEADS
    C = RET_CHUNK

    def cols(base):
        return pl.BlockSpec((1, S, H * LANES), lambda b: (b, 0, base // H))

    return pl.pallas_call(
        _ret_kernel,
        grid=(B,),
        in_specs=[pl.BlockSpec(memory_space=pltpu.SMEM),
                  cols(COL_RQ), cols(COL_RK), cols(COL_RV), cols(COL_RG),
                  _resident((H, 4, C, LANES), lambda b: (0, 0, 0, 0)),
                  _resident((H, C, C), lambda b: (0, 0, 0)),
                  _resident((1, H * LANES), lambda b: (0, 0))],
        out_specs=pl.BlockSpec((1, S, H * LANES), lambda b: (b, 0, 0)),
        out_shape=jax.ShapeDtypeStruct((B, S, H * RET_DV), BF16),
        scratch_shapes=[pltpu.VMEM((H, S, RET_DV), F32)],
        compiler_params=_cparams(("arbitrary",)),
        name="ret",
    )(cd, proj, proj, proj, proj, dec, dmat, gn)


def _diff_kernel(bstat_ref, lam_ref, q_ref, k_ref, vt_ref, brow_ref, g_ref, o_ref, p_ref, t_ref):
    hh = pl.program_id(0)
    S = k_ref.shape[1]
    TQ, KB = Q_TILE, KEY_CHUNK
    nk = S // KB
    lv = lam_ref[...]
    lam = (jnp.exp(jnp.sum(lv[0:1] * lv[1:2], axis=1, keepdims=True))
           - jnp.exp(jnp.sum(lv[2:3] * lv[3:4], axis=1, keepdims=True)) + LAM_INIT)
    first_half = lax.broadcasted_iota(jnp.int32, (TQ, LANES), 1) < DIFF_DH
    nt = (((1,), (1,)), ((), ()))
    bmax = bstat_ref[hh, 0]

    band_lo = S - TQ - MAX_DISTANCE

    @pl.when(pl.program_id(1) == 0)
    def _():
        side = band_lo
        t_ref[0:side, :] = jnp.full((side, TQ), bstat_ref[hh, 1], F32)
        rows = jnp.broadcast_to(brow_ref[0], (BIAS_BAND, brow_ref.shape[2]))
        shift = brow_ref.shape[2] - (BIAS_BAND - 1 + MAX_DISTANCE)
        t_ref[side:side + BIAS_BAND, :] = pltpu.roll(rows, shift, 1, stride=1, stride_axis=0)[:, :TQ]
        t_ref[side + BIAS_BAND:, :] = jnp.full((side, TQ), bstat_ref[hh, 2], F32)
    kf = k_ref[0].astype(F32)
    kmax = jnp.sqrt(jnp.max(jnp.sum(kf * kf, axis=1, keepdims=True), axis=0, keepdims=True))
    sel_r = lax.broadcasted_iota(jnp.int32, (8, LANES), 0)
    sel_c = lax.broadcasted_iota(jnp.int32, (8, LANES), 1)
    sel = jnp.where((sel_c < DIFF_DH) == (sel_r == 0), 1.0, 0.0)
    sel = jnp.where(sel_r < 2, sel, 0.0).astype(BF16)

    def tile_rows(qb):
        return pl.ds(qb * TQ if isinstance(qb, int) else pl.multiple_of(qb * TQ, TQ), TQ)

    def masked_q(qb):
        q = q_ref[0, tile_rows(qb), :]
        zero = jnp.zeros_like(q)
        return q, (jnp.where(first_half, q, zero), jnp.where(first_half, zero, q))

    def finish(ot, qb):
        ms = jnp.mean(ot * ot, axis=0, keepdims=True)
        y = ot * lax.rsqrt(ms + NORM_EPS) * g_ref[...] * (1.0 - LAM_INIT)
        o_ref[0, tile_rows(qb), :] = y.T.astype(BF16)

    def exponentials(qb, slot):
        w0 = (S - TQ) - qb * TQ
        q, qz = masked_q(qb)
        qsq = (q.astype(F32) * q.astype(F32)).astype(BF16)
        n2 = lax.dot_general(sel, qsq, nt, preferred_element_type=F32) * 1.01
        m = [jnp.sqrt(n2[i:i + 1]) * kmax + bmax for i in range(2)]
        m_lo = [mi - bstat_ref[hh, 1] for mi in m]
        m_hi = [mi - bstat_ref[hh, 2] for mi in m]
        l = [jnp.zeros((8, TQ), F32) for _ in range(2)]
        for c in range(nk):
            rows = slice(c * KB, (c + 1) * KB)
            kb = k_ref[0, rows, :]
            t0 = w0 + c * KB
            below, above = t0 + KB <= band_lo, t0 >= band_lo + BIAS_BAND
            bias = None if below or above else t_ref[t0:t0 + KB, :]
            for i in range(2):
                s = lax.dot_general(kb, qz[i], nt, preferred_element_type=F32)
                if below:
                    p = jnp.exp2(s - m_lo[i])
                elif above:
                    p = jnp.exp2(s - m_hi[i])
                else:
                    p = jnp.exp2((s - m[i]) + bias)
                l[i] = l[i] + jnp.sum(p.reshape(KB // 8, 8, TQ), axis=0)
                p_ref[slot, i, rows, :] = p.astype(BF16)
        return [jnp.sum(l[i], axis=0, keepdims=True) for i in range(2)]

    def values(qb, slot, lr):
        ratio = jnp.broadcast_to(lam * lr[0] / lr[1], (16, TQ)).astype(BF16)
        ot = jnp.zeros((DIFF_DV, TQ), F32)
        for c in range(nk):
            rows = slice(c * KB, (c + 1) * KB)
            p1 = p_ref[slot, 0, rows, :].reshape(KB // 16, 16, TQ)
            p2 = p_ref[slot, 1, rows, :].reshape(KB // 16, 16, TQ)
            a = (p1 - ratio * p2).reshape(KB, TQ)
            ot = ot + jnp.dot(vt_ref[0, :, rows], a, preferred_element_type=F32)
        finish(ot * (1.0 / lr[0]), qb)

    n_tiles = S // TQ
    lmin = jnp.full((1, TQ), jnp.inf, F32)
    sums = exponentials(0, 0)
    for t in range(n_tiles):
        lmin = jnp.minimum(lmin, jnp.minimum(sums[0], sums[1]))
        nxt = exponentials(t + 1, (t + 1) % 2) if t + 1 < n_tiles else None
        values(t, t % 2, sums)
        sums = nxt

    @pl.when(jnp.min(lmin) < L_MIN)
    def _():
        def exact_tile(qb, carry):
            w0 = pl.multiple_of((S - TQ) - qb * TQ, TQ)
            _, qz = masked_q(qb)
            bias = t_ref[pl.ds(w0, S), :]
            attn = None
            for i, scale in ((0, 1.0), (1, lam)):
                s = lax.dot_general(k_ref[0], qz[i], nt, preferred_element_type=F32) + bias
                p = jnp.exp2(s - jnp.max(s, axis=0, keepdims=True))
                part = p * (scale / jnp.sum(p, axis=0, keepdims=True))
                attn = part if attn is None else attn - part
            finish(jnp.dot(vt_ref[0], attn.astype(BF16), preferred_element_type=F32), qb)
            return carry

        lax.fori_loop(0, S // TQ, exact_tile, 0)


def _diff(proj, vt, lam_vecs, brow, bstat, g):
    B, S, _ = proj.shape
    H = DIFF_HEADS

    def col(base):
        return pl.BlockSpec((1, S, LANES), lambda h, b: (b, 0, base + h))

    return pl.pallas_call(
        _diff_kernel,
        grid=(H, B),
        in_specs=[pl.BlockSpec(memory_space=pltpu.SMEM),
                  pl.BlockSpec((4, DIFF_DH), lambda h, b: (0, 0)),
                  col(COL_DQ), col(COL_DK),
                  pl.BlockSpec((1, DIFF_DV, S), lambda h, b: (b, h, 0)),
                  pl.BlockSpec((1, 1, brow.shape[2]), lambda h, b: (h, 0, 0)),
                  pl.BlockSpec((DIFF_DV, Q_TILE), lambda h, b: (0, 0))],
        out_specs=pl.BlockSpec((1, S, LANES), lambda h, b: (b, 0, h)),
        out_shape=jax.ShapeDtypeStruct((B, S, H * DIFF_DV), BF16),
        scratch_shapes=[pltpu.VMEM((2, 2, S, Q_TILE), BF16), pltpu.VMEM((2 * S - Q_TILE, Q_TILE), F32)],
        compiler_params=_cparams(("arbitrary", "arbitrary")),
        name="diff",
    )(bstat, lam_vecs, proj, proj, vt, brow, g)


def _pack_halves(x):
    w = x.shape[1] // 2
    lo = pltpu.bitcast(x[:, :w].astype(BF16).astype(F32), jnp.int32)
    hi = pltpu.bitcast(x[:, w:].astype(BF16).astype(F32), jnp.int32)
    return lax.shift_right_logical(lo, jnp.full_like(lo, 16)) | (hi & jnp.int32(-65536))


def _unpack_halves(words):
    lo = pltpu.bitcast(lax.shift_left(words, jnp.full_like(words, 16)), F32)
    hi = pltpu.bitcast(words & jnp.int32(-65536), F32)
    return jnp.concatenate([lo, hi], axis=1).astype(BF16)


def _merge_kernel(x_ref, yr_ref, yd_ref, mod_ref, gmix_ref, wg_ref, wr_ref, wd_ref, wo_ref,
                  g_ref, wrt_ref, x1_ref, h2_ref, lg_ref):
    parts = 4
    gate_w = wg_ref.shape[1] // 2
    step = x_ref.shape[1] // parts

    def project(r):
        rows = slice(r * step, (r + 1) * step)
        h = _rms(x_ref[0, rows, :], gmix_ref[...]) * (1.0 + mod_ref[0, 1:2, :]) + mod_ref[0, 0:1, :]
        gates = jax.nn.sigmoid(jnp.dot(h.astype(BF16), wg_ref[...], preferred_element_type=F32))
        a = jnp.dot(yr_ref[0, rows, :], wr_ref[...], preferred_element_type=F32)
        d = jnp.dot(yd_ref[0, rows, :], wd_ref[...], preferred_element_type=F32)
        merged = gates[:, :gate_w] * a + gates[:, gate_w:] * d
        return jnp.dot(merged.astype(BF16), wo_ref[...], preferred_element_type=F32)

    def epilogue(r, o):
        rows = slice(r * step, (r + 1) * step)
        x1 = x_ref[0, rows, :] + mod_ref[0, 2:3, :] * o
        x1_ref[0, rows, :] = x1
        h2 = _rms(x1, g_ref[...]) * (1.0 + mod_ref[0, 4:5, :]) + mod_ref[0, 3:4, :]
        h2_ref[0, rows, :] = _pack_halves(h2)
        lg_ref[0, :, rows] = lax.dot_general(wrt_ref[...], h2.astype(BF16), (((1,), (1,)), ((), ())),
                                             preferred_element_type=F32)

    o = project(0)
    for r in range(parts):
        nxt = project(r + 1) if r + 1 < parts else None
        epilogue(r, o)
        o = nxt


def _merge(x, yr, yd, mod, gmix, wg, wr, wd, wo, g, wrt):
    B, S, D = x.shape
    tm = PROJ_TILE
    E = wrt.shape[0]
    return pl.pallas_call(
        _merge_kernel,
        grid=(B, S // tm),
        in_specs=[pl.BlockSpec((1, tm, D), lambda b, i: (b, i, 0)),
                  pl.BlockSpec((1, tm, yr.shape[2]), lambda b, i: (b, i, 0)),
                  pl.BlockSpec((1, tm, yd.shape[2]), lambda b, i: (b, i, 0)),
                  pl.BlockSpec((1, 6, D), lambda b, i: (b, 0, 0)),
                  _resident((1, D), lambda b, i: (0, 0)),
                  _resident(wg.shape, lambda b, i: (0, 0)),
                  _resident(wr.shape, lambda b, i: (0, 0)),
                  _resident(wd.shape, lambda b, i: (0, 0)),
                  _resident(wo.shape, lambda b, i: (0, 0)),
                  _resident((1, D), lambda b, i: (0, 0)),
                  _resident(wrt.shape, lambda b, i: (0, 0))],
        out_specs=[pl.BlockSpec((1, tm, D), lambda b, i: (b, i, 0)),
                   pl.BlockSpec((1, tm, D // 2), lambda b, i: (b, i, 0)),
                   pl.BlockSpec((1, E, tm), lambda b, i: (b, 0, i))],
        out_shape=[jax.ShapeDtypeStruct((B, S, D), F32),
                   jax.ShapeDtypeStruct((B, S, D // 2), jnp.int32),
                   jax.ShapeDtypeStruct((B, E, S), F32)],
        compiler_params=_cparams(("arbitrary", "arbitrary")),
        name="merge",
    )(x, yr, yd, mod, gmix, wg, wr, wd, wo, g, wrt)


def _lane_prefix(m, tri):
    E, S = m.shape
    off = jnp.zeros((E, 1), F32)
    parts = []
    for j in range(S // LANES):
        blk = m[:, j * LANES:(j + 1) * LANES]
        parts.append(jnp.dot(blk.astype(BF16), tri, preferred_element_type=F32) + off)
        off = off + jnp.sum(blk, axis=1, keepdims=True)
    return jnp.concatenate(parts, axis=1)


def _route_kernel(lg_ref, rank_ref, gate_ref, rank_t_ref, win_ref, ok_ref, *, cap):
    E, S = lg_ref.shape[1], lg_ref.shape[2]
    tiles = win_ref.shape[2]
    tm = S // tiles
    lane = lax.broadcasted_iota(jnp.int32, (E, LANES), 1)
    for j in range(lg_ref.shape[0]):
        rank, gate_ref[j], before = _route_one(lg_ref[j], cap)
        start = jnp.zeros((E, LANES), F32)
        end = jnp.full((E, LANES), float(cap), F32)
        for i in range(tiles):
            first = before[:, i * tm:i * tm + 1]
            start = jnp.where(lane == i, first, start)
            if i > 0:
                end = jnp.where(lane == i - 1, first, end)
        win = jnp.minimum(jnp.floor(start * (1.0 / BF16_SUBLANES)) * BF16_SUBLANES, float(cap - SCAT_WINDOW))
        fits = jnp.min(jnp.where(end - win <= SCAT_WINDOW, 1.0, 0.0), axis=0, keepdims=True)
        win_ref[j] = win[:, :tiles].astype(jnp.int32)
        ok_ref[j] = fits[:, :tiles].astype(jnp.int32)
        rank_ref[j] = rank.astype(jnp.int32)
        padded = jnp.concatenate([rank, jnp.full((LANES - E, S), -1.0, F32)], axis=0)
        rank_t_ref[j] = padded.T.astype(jnp.int32)


def _route_one(lg, cap):
    e = jnp.exp(lg - jnp.max(lg, axis=0, keepdims=True))
    aff = e / jnp.sum(e, axis=0, keepdims=True)
    bits = pltpu.bitcast(aff, jnp.int32)
    E = lg.shape[0]

    def count(mask):
        return jnp.sum(jnp.where(mask, 1.0, 0.0), axis=1, keepdims=True)

    def reaches(cand):
        return count(bits >= cand) >= cap

    thr = jnp.zeros((E, 1), jnp.int32)
    for hi in range(30, 0, -2):
        c1, c2 = thr | (1 << hi), thr | (1 << (hi - 1))
        c3 = c1 | (1 << (hi - 1))
        thr = jnp.where(reaches(c3), c3, jnp.where(reaches(c1), c1, jnp.where(reaches(c2), c2, thr)))
    thr = jnp.where(reaches(thr | 1), thr | 1, thr)
    gt = bits > thr
    eq = bits == thr
    need = cap - count(gt)
    r = lax.broadcasted_iota(jnp.int32, (LANES, LANES), 0)
    c = lax.broadcasted_iota(jnp.int32, (LANES, LANES), 1)
    tri = jnp.where(r < c, 1.0, 0.0).astype(BF16)
    eq_before = _lane_prefix(jnp.where(eq, 1.0, 0.0), tri)
    sel = gt | (eq & (eq_before < need))
    slot = _lane_prefix(jnp.where(sel, 1.0, 0.0), tri)
    return jnp.where(sel, slot, -1.0), jnp.where(sel, aff, 0.0), slot


def _route(logits, cap):
    B, E, S = logits.shape
    G = ROUTE_SEQS_PER_STEP
    tiles = S // TOKEN_TILE
    assert B % G == 0
    spec = pl.BlockSpec((G, E, S), lambda b: (b, 0, 0))
    return pl.pallas_call(
        functools.partial(_route_kernel, cap=cap),
        grid=(B // G,),
        in_specs=[spec],
        out_specs=[spec, spec, pl.BlockSpec((G, S, LANES), lambda b: (b, 0, 0)),
                   pl.BlockSpec((G, E, tiles), lambda b: (b, 0, 0)),
                   pl.BlockSpec((G, 1, tiles), lambda b: (b, 0, 0))],
        out_shape=[jax.ShapeDtypeStruct((B, E, S), jnp.int32),
                   jax.ShapeDtypeStruct((B, E, S), F32),
                   jax.ShapeDtypeStruct((B, S, LANES), jnp.int32),
                   jax.ShapeDtypeStruct((B, E, tiles), jnp.int32),
                   jax.ShapeDtypeStruct((B, 1, tiles), jnp.int32)],
        compiler_params=_cparams(("arbitrary",)),
        name="route",
    )(logits)


def _gather_rows(table, rank, e0, n_e, cap):
    B, _, S = rank.shape
    W = table.shape[1]
    workers = SC_CORES * SC_SUBCORES
    pairs = n_e * B
    per = -(-pairs // workers)
    assert B & (B - 1) == 0 and cap % SC_GATHER_ROWS == 0 and S % SC_LANES == 0
    shift = B.bit_length() - 1
    mesh = plsc.VectorSubcoreMesh(core_axis_name="c", subcore_axis_name="s")

    R, NB = SC_GATHER_ROWS, SC_GATHER_BUFS
    n_chunks = cap // R

    def body(table_hbm, rank_hbm, out_hbm, rank_v, idx_v, buf_v, gsem, wsem):
        wid = lax.axis_index("s") * SC_CORES + lax.axis_index("c")

        def fetch(c):
            return pltpu.make_async_copy(table_hbm.at[idx_v.at[pl.ds(c * R, R)]], buf_v.at[c % NB], gsem.at[c % NB])

        def flush(p, c):
            return pltpu.make_async_copy(buf_v.at[c % NB], out_hbm.at[pl.ds(p * cap + c * R, R)], wsem.at[c % NB])

        def one_pair(p):
            e = lax.shift_right_logical(p, shift)
            b = p & (B - 1)
            pltpu.sync_copy(rank_hbm.at[b, e0 + e], rank_v)
            base = b * S

            @pl.loop(0, S, step=SC_LANES)
            def _(t0):
                r = rank_v[pl.ds(t0, SC_LANES)]
                tok = lax.iota(jnp.int32, SC_LANES) + (base + t0)
                plsc.store_scatter(idx_v, [r], tok, mask=r >= 0)

            for c in range(min(NB - 1, n_chunks)):
                fetch(c).start()
            waited = 0
            for c in range(n_chunks):
                fetch(c).wait()
                flush(p, c).start()
                nxt = c + NB - 1
                if nxt < n_chunks:
                    if c >= 1:
                        flush(p, c - 1).wait()
                        waited = c
                    fetch(nxt).start()
            for c in range(waited, n_chunks):
                flush(p, c).wait()

        for j in range(per):
            p = wid * per + j
            pl.when(p < pairs)(functools.partial(one_pair, p))

    return pl.kernel(
        body,
        out_type=jax.ShapeDtypeStruct((pairs * cap, W), jnp.int32),
        mesh=mesh,
        scratch_types=[pltpu.VMEM((S,), jnp.int32), pltpu.VMEM((cap,), jnp.int32),
                       pltpu.VMEM((NB, R, W), jnp.int32),
                       pltpu.SemaphoreType.DMA((NB,)), pltpu.SemaphoreType.DMA((NB,))],
        compiler_params=pltpu.CompilerParams(needs_layout_passes=False),
    )(table, rank)


def _moe_kernel(rank_ref, gate_ref, x_ref, wg_ref, wu_ref, wd_ref, o_ref, wg_s, wu_s, wd_s, *, cap):
    r, b = pl.program_id(0), pl.program_id(1)
    n_exp = pl.num_programs(0) - 1

    @pl.when(r < n_exp)
    def _():
        slot = r % 2
        rows_in, rows_ff = wg_ref.shape[1], wd_ref.shape[1]
        wg_s[slot, pl.ds(pl.multiple_of(b * rows_in, rows_in), rows_in), :] = wg_ref[0].astype(BF16)
        wu_s[slot, pl.ds(pl.multiple_of(b * rows_in, rows_in), rows_in), :] = wu_ref[0].astype(BF16)
        wd_s[slot, pl.ds(pl.multiple_of(b * rows_ff, rows_ff), rows_ff), :] = wd_ref[0].astype(BF16)

    @pl.when(r == 0)
    def _():
        o_ref[...] = jnp.zeros_like(o_ref)

    @pl.when(r > 0)
    def _():
        slot = (r - 1) % 2
        S = rank_ref.shape[3]
        for j in range(x_ref.shape[0]):
            rank = rank_ref[j, 0]
            pick = lax.broadcasted_iota(jnp.int32, (cap, S), 0) == rank
            xin = _unpack_halves(x_ref[j])
            a = jnp.dot(xin, wg_s[slot], preferred_element_type=F32)
            u = jnp.dot(xin, wu_s[slot], preferred_element_type=F32)
            act = (a * jax.nn.sigmoid(a) * u).astype(BF16)
            y = jnp.dot(act, wd_s[slot], preferred_element_type=F32)
            g = jnp.sum(jnp.where(pick, gate_ref[j, 0], 0.0), axis=1, keepdims=True)
            o_ref[0, j] = (y * g).astype(BF16)


def _moe(rank, gate, xin, wg, wu, wd, cap, e0, n_e):
    B, E, S = rank.shape
    D, Fd = wg.shape[1], wg.shape[2]
    G = MOE_SEQS_PER_STEP
    steps = B // G
    assert steps * G == B and D % steps == 0 and Fd % steps == 0

    def cur(r):
        return e0 + jnp.maximum(r - 1, 0)

    def nxt(r):
        return e0 + jnp.minimum(r, n_e - 1)

    row = pl.BlockSpec((G, 1, 1, S), lambda r, b: (b, cur(r), 0, 0))
    return pl.pallas_call(
        functools.partial(_moe_kernel, cap=cap),
        grid=(n_e + 1, steps),
        in_specs=[row, row,
                  pl.BlockSpec((G, cap, D // 2), lambda r, b: (jnp.maximum(r - 1, 0) * steps + b, 0, 0)),
                  pl.BlockSpec((1, D // steps, Fd), lambda r, b: (nxt(r), b, 0)),
                  pl.BlockSpec((1, D // steps, Fd), lambda r, b: (nxt(r), b, 0)),
                  pl.BlockSpec((1, Fd // steps, D), lambda r, b: (nxt(r), b, 0))],
        out_specs=pl.BlockSpec((1, G, cap, D), lambda r, b: (jnp.where(r == 0, n_e, r - 1), b, 0, 0)),
        out_shape=jax.ShapeDtypeStruct((n_e + 1, B, cap, D), BF16),
        scratch_shapes=[pltpu.VMEM((2, D, Fd), BF16), pltpu.VMEM((2, D, Fd), BF16),
                        pltpu.VMEM((2, Fd, D), BF16)],
        compiler_params=_cparams(("arbitrary", "arbitrary")),
        name="moe",
    )(rank.reshape(B, E, 1, S), gate.reshape(B, E, 1, S), xin.reshape(n_e * B, cap, D // 2), wg, wu, wd)


def _scat_kernel(win_ref, ok_ref, rt_ref, *refs, cap):
    *y_refs, x1_ref, mod_ref, g_ref, o_ref = refs
    b, i = pl.program_id(0), pl.program_id(1)
    rt = rt_ref[0]
    tm, E = rt.shape[0], win_ref.shape[1]
    D = x1_ref.shape[2]
    sizes = [y.shape[0] - 1 for y in y_refs]
    home = [(g, e) for g, n in enumerate(sizes) for e in range(n)]

    def finish(moe):
        x2 = x1_ref[0] + mod_ref[0, 5:6, :] * moe
        o_ref[0] = _rms(x2, g_ref[...])

    @pl.when(ok_ref[b, 0, i] != 0)
    def _():
        pack = SCAT_DEPTH // SCAT_WINDOW
        lane = lax.broadcasted_iota(jnp.int32, (tm, SCAT_DEPTH), 1)
        moe = jnp.zeros((tm, D), F32)
        for e in range(0, E, pack):
            starts = [pl.multiple_of(win_ref[b, e + k, i], BF16_SUBLANES) for k in range(pack)]
            hit = None
            for k in range(pack):
                d = rt[:, e + k:e + k + 1] - starts[k]
                col = jnp.where((d >= 0) & (d < SCAT_WINDOW), d + k * SCAT_WINDOW, -1)
                hit = (col == lane) if hit is None else hit | (col == lane)
            place = jnp.where(hit, 1.0, 0.0).astype(BF16)
            y = jnp.concatenate([y_refs[home[e + k][0]][home[e + k][1], 0, pl.ds(starts[k], SCAT_WINDOW), :]
                                 for k in range(pack)], axis=0)
            moe = moe + jnp.dot(place, y, preferred_element_type=F32)
        finish(moe)

    @pl.when(ok_ref[b, 0, i] == 0)
    def _():
        slot = lax.broadcasted_iota(jnp.int32, (tm, cap), 1)
        place = jnp.concatenate(
            [jnp.where(rt[:, e:e + 1] == slot, 1.0, 0.0).astype(BF16) for e in range(E)], axis=1)
        y = jnp.concatenate([y_ref[0:n, 0].reshape(n * cap, D) for y_ref, n in zip(y_refs, sizes)], axis=0)
        finish(jnp.dot(place, y, preferred_element_type=F32))


def _scat(rank_t, win, ok, ybufs, x1, mod, g, cap):
    B, S, D = x1.shape
    tm = TOKEN_TILE
    smem = pl.BlockSpec(memory_space=pltpu.SMEM)
    return pl.pallas_call(
        functools.partial(_scat_kernel, cap=cap),
        grid=(B, S // tm),
        in_specs=[smem, smem,
                  pl.BlockSpec((1, tm, LANES), lambda b, i: (b, i, 0)),
                  *[pl.BlockSpec(y.shape[:1] + (1, cap, D), lambda b, i: (0, b, 0, 0)) for y in ybufs],
                  pl.BlockSpec((1, tm, D), lambda b, i: (b, i, 0)),
                  pl.BlockSpec((1, 6, D), lambda b, i: (b, 0, 0)),
                  _resident((1, D), lambda b, i: (0, 0))],
        out_specs=pl.BlockSpec((1, tm, D), lambda b, i: (b, i, 0)),
        out_shape=jax.ShapeDtypeStruct((B, S, D), F32),
        compiler_params=_cparams(("arbitrary", "arbitrary")),
        name="scat",
    )(win, ok, rank_t, *ybufs, x1, mod, g)


def _rot_tables(S):
    half = RET_DK // 2
    inv = 1.0 / (RET_THETA_BASE ** np.linspace(0.0, 1.0, half))
    ang = np.arange(S, dtype=np.float64)[:, None] * inv[None, :]
    cos = np.repeat(np.cos(ang), 2, axis=1)
    sin = np.stack([-np.sin(ang), np.sin(ang)], axis=2).reshape(S, RET_DK)
    sc = RET_DK ** -0.5
    return jnp.asarray(np.stack([cos, sin, cos * sc, sin * sc]), F32)


def _ret_tables():
    C = RET_CHUNK
    heads = np.arange(RET_HEADS, dtype=np.float64)
    lgf = np.log1p(-np.exp2(-RET_FWD_DECAY_OFFSET - heads))[:, None]
    lgb = np.log1p(-np.exp2(-RET_BWD_DECAY_OFFSET - heads))[:, None]
    idx = np.arange(C, dtype=np.float64)
    diff = idx[:, None] - idx[None, :]
    dmat = np.where(diff >= 0,
                    np.exp(np.maximum(diff, 0.0)[None] * lgf[:, :, None]),
                    np.exp(np.maximum(-diff, 0.0)[None] * lgb[:, :, None]))
    dec = np.stack([np.exp((idx + 1)[None, :] * lgf),
                    np.exp((C - 1 - idx)[None, :] * lgf),
                    np.exp((C - idx)[None, :] * lgb),
                    np.exp(idx[None, :] * lgb)], axis=1)
    dec = np.broadcast_to(dec[..., None], dec.shape + (LANES,))
    cd = np.concatenate([np.exp(C * lgf), np.exp(C * lgb)], axis=1)
    return jnp.asarray(cd, F32), jnp.asarray(dec, F32), jnp.asarray(dmat, F32)


def _t5_bucket(rel):
    nb = N_BUCKETS // 2
    max_exact = nb // 2
    ret = (rel > 0).astype(jnp.int32) * nb
    n = jnp.abs(rel)
    large = max_exact + (jnp.log(jnp.maximum(n, 1).astype(F32) / max_exact)
                         / math.log(MAX_DISTANCE / max_exact) * (nb - max_exact)).astype(jnp.int32)
    large = jnp.minimum(large, nb - 1)
    return ret + jnp.where(n < max_exact, n, large)


def _bias_rows(rel_bias):
    reach = BIAS_BAND - 1
    rel = jnp.concatenate([jnp.arange(reach, -reach - 1, -1, dtype=jnp.int32),
                           jnp.array([-(2 ** 20), 2 ** 20], jnp.int32)])
    f = rel_bias[_t5_bucket(rel)].astype(F32).T * LOG2E
    rows = jnp.pad(f[:, :-2], ((0, 0), (0, 2 * BIAS_BAND - (2 * reach + 1))))
    stats = jnp.stack([jnp.max(f, axis=1), f[:, -2], f[:, -1]], axis=1)
    return rows[:, None, :], stats


def kernel(x, c, w_ada, b_ada, norm_mix_g, w_in, ret_gn_g, diff_subln_g, lambda_q1, lambda_k1, lambda_q2, lambda_k2, w_ret_out, w_diff_out, w_o, rel_bias, norm_ffn_g, w_router, w_exp_gate, w_exp_up, w_exp_down, final_g):
    B, S, D = x.shape
    cap = CAPACITY_FACTOR * S // N_EXPERTS
    l = 0

    mod = _ada(c, w_ada[l], b_ada[l]).reshape(B, 6, D)
    w_in_b = w_in[l].astype(BF16)
    proj, vt = _inproj(x, norm_mix_g[l].reshape(1, D), mod, w_in_b, _rot_tables(S))

    cd, dec, dmat = _ret_tables()
    y_ret = _ret(proj, cd, dec, dmat, ret_gn_g[l].reshape(1, -1))

    lam_vecs = jnp.stack([lambda_q1[l], lambda_k1[l], lambda_q2[l], lambda_k2[l]]).astype(F32)
    brow, bstat = _bias_rows(rel_bias)
    subln = jnp.broadcast_to(diff_subln_g[l].astype(F32)[:, None], (DIFF_DV, Q_TILE))
    y_diff = _diff(proj, vt, lam_vecs, brow, bstat, subln)

    x1, h2, logits = _merge(x, y_ret, y_diff, mod, norm_mix_g[l].reshape(1, D), w_in_b[:, W_GR * LANES:],
                            w_ret_out[l].astype(BF16), w_diff_out[l].astype(BF16), w_o[l].astype(BF16),
                            norm_ffn_g[l].reshape(1, D), w_router[l].T.astype(BF16))

    rank, gate, rank_t, win, ok = _route(logits, cap)
    table, ybufs, e0 = h2.reshape(B * S, D // 2), [], 0
    for n_e in MOE_GROUP_SIZES:
        xin = _gather_rows(table, rank, e0, n_e, cap)
        ybufs.append(_moe(rank, gate, xin, w_exp_gate[l], w_exp_up[l], w_exp_down[l], cap, e0, n_e))
        e0 += n_e
    return _scat(rank_t, win, ok, ybufs, x1, mod, final_g.reshape(1, D), cap)
```
